```python
import math
import jax
import jax.numpy as jnp
from jax import lax
import numpy as np

D_MODEL = 1024
BATCH = 4
SEQ = 8192
DEPTH = 2

GRID_W = 64
CTX_LEN = 256
HEAD_DIM = 64
S5_GROUP_CH = 16
S5_GROUPS = 16
S5_WIDTH = S5_GROUPS * S5_GROUP_CH
S5_STATE = 64
RWKV_HEADS = 8
RWKV_WIDTH = RWKV_HEADS * HEAD_DIM
RWKV_LORA_W = 64
RWKV_LORA_A = 64
RWKV_LORA_G = 128
RWKV_IN = 3 * RWKV_WIDTH + 2 * RWKV_LORA_W + 2 * RWKV_LORA_A + RWKV_LORA_G
RWKV_SPLITS = (RWKV_WIDTH, 2 * RWKV_WIDTH, 3 * RWKV_WIDTH,
               3 * RWKV_WIDTH + 2 * RWKV_LORA_W,
               3 * RWKV_WIDTH + 2 * RWKV_LORA_W + 2 * RWKV_LORA_A)
EVEN_IN = S5_WIDTH + RWKV_IN
EVEN_MIX = S5_WIDTH + RWKV_WIDTH
WIN_HEADS = 8
WIN_KV_HEADS = 2
WINDOW = 128
WIN_BLOCK = 128
NA_HEADS = 8
NA_ROWS = 8
NA_COLS = 16
ODD_SPLITS = (WIN_HEADS * HEAD_DIM, (WIN_HEADS + WIN_KV_HEADS) * HEAD_DIM,
              (WIN_HEADS + 2 * WIN_KV_HEADS) * HEAD_DIM,
              (WIN_HEADS + 2 * WIN_KV_HEADS + NA_HEADS) * HEAD_DIM,
              (WIN_HEADS + 2 * WIN_KV_HEADS + 2 * NA_HEADS) * HEAD_DIM)
ODD_IN = (WIN_HEADS + 2 * WIN_KV_HEADS + 3 * NA_HEADS) * HEAD_DIM
ODD_MIX = (WIN_HEADS + NA_HEADS) * HEAD_DIM
ROPE_BASE = 10000.0
PEER_HEADS = 8
PEER_KEYS = 128
PEER_EXPERTS = PEER_KEYS * PEER_KEYS
PEER_DK = 128
PEER_TOPK = 16
PEER_BLOCK = 128
ALPHA = (2.0 * DEPTH) ** 0.25
BETA = (8.0 * DEPTH) ** -0.25
LN_EPS = 1e-5
GN_EPS = 64e-5

kernel_name = 'hybrid_s5_rwkv7_swa_natten_peer_dit'

F32 = jnp.float32


def _layer_norm(z, g, b):
    zf = z.astype(F32)
    mu = jnp.mean(zf, -1, keepdims=True)
    var = jnp.mean(jnp.square(zf - mu), -1, keepdims=True)
    return ((zf - mu) * lax.rsqrt(var + LN_EPS) * g + b).astype(z.dtype)


def _modulation(cond, w_mod, b_mod):
    m = (jax.nn.silu(cond) @ w_mod + b_mod)[:, None, :]
    return jnp.split(m, 6, axis=-1)


def _modulate(z, shift, scale):
    return z * (1.0 + scale) + shift


def _dir_view(z, n_ctx, d):
    if d == 0:
        return z
    return jnp.concatenate([jnp.flip(z[:, :n_ctx], 1), jnp.flip(z[:, n_ctx:], 1)], 1)


def _axial_rope(z, row, col):
    half = HEAD_DIM // 2
    quarter = half // 2
    freqs = ROPE_BASE ** (-jnp.arange(quarter, dtype=F32) / quarter)

    def rot(za, pos):
        ang = pos[:, None] * freqs[None, :]
        cos = jnp.cos(ang)[None, :, None, :].astype(z.dtype)
        sin = jnp.sin(ang)[None, :, None, :].astype(z.dtype)
        z1, z2 = za[..., :quarter], za[..., quarter:]
        return jnp.concatenate([z1 * cos - z2 * sin, z2 * cos + z1 * sin], -1)

    return jnp.concatenate([rot(z[..., :half], row), rot(z[..., half:], col)], -1)


def _complex_diag_scan(a_re, a_im, b_re, b_im):
    def combine(e1, e2):
        a1r, a1i, b1r, b1i = e1
        a2r, a2i, b2r, b2i = e2
        return (a1r * a2r - a1i * a2i, a1r * a2i + a1i * a2r,
                a2r * b1r - a2i * b1i + b2r, a2r * b1i + a2i * b1r + b2i)

    a_re = jnp.broadcast_to(a_re, b_re.shape)
    a_im = jnp.broadcast_to(a_im, b_re.shape)
    _, _, h_re, h_im = lax.associative_scan(combine, (a_re, a_im, b_re, b_im), axis=1)
    return h_re, h_im


def _s5_mixer(ux, uc, lam_re, lam_im, log_dt, b_re, b_im, c_re, c_im, d_skip, w_glu, b_glu):
    Bt, T, _ = ux.shape
    L = uc.shape[1]
    u = jnp.concatenate([uc, ux], 1).astype(F32).reshape(Bt, L + T, S5_GROUPS, S5_GROUP_CH)
    y = u * d_skip.astype(F32).reshape(S5_GROUPS, S5_GROUP_CH)
    for d in range(2):
        lr, li = lam_re[d].astype(F32), lam_im[d].astype(F32)
        dt = jnp.exp(log_dt[d].astype(F32))[:, None]
        mag = jnp.exp(lr * dt)
        ab_re, ab_im = mag * jnp.cos(li * dt), mag * jnp.sin(li * dt)
        den = lr * lr + li * li
        nr = ab_re - 1.0
        coef_re = (nr * lr + ab_im * li) / den
        coef_im = (ab_im * lr - nr * li) / den
        br, bi = b_re[d].astype(F32), b_im[d].astype(F32)
        bb_re = coef_re[..., None] * br - coef_im[..., None] * bi
        bb_im = coef_re[..., None] * bi + coef_im[..., None] * br
        seq = _dir_view(u, L, d)
        bu_re = jnp.einsum('bsgi,gpi->bsgp', seq, bb_re)
        bu_im = jnp.einsum('bsgi,gpi->bsgp', seq, bb_im)
        h_re, h_im = _complex_diag_scan(ab_re, ab_im, bu_re, bu_im)
        yd = (jnp.einsum('bsgp,gip->bsgi', h_re, c_re[d].astype(F32))
              - jnp.einsum('bsgp,gip->bsgi', h_im, c_im[d].astype(F32)))
        y = y + _dir_view(yd, L, d)
    y = jax.nn.gelu(y.reshape(Bt, L + T, S5_WIDTH), approximate=False)
    y = (y * jax.nn.sigmoid(y @ w_glu.astype(F32) + b_glu.astype(F32))).astype(ux.dtype)
    return y[:, L:], y[:, :L]


def _centred_shift(z, mu):
    prev = jnp.pad(z[:, :-1], ((0, 0), (1, 0), (0, 0)))
    nxt = jnp.pad(z[:, 1:], ((0, 0), (0, 1), (0, 0)))
    return z + mu[0] * (prev - z) + mu[1] * (nxt - z)


def _rwkv_scan(r, w, k, v, a, b):
    Bt, S, H, N = r.shape

    def step(state, inp):
        r_t, w_t, k_t, v_t, a_t, b_t = inp
        sa = jnp.einsum('bhvk,bhk->bhv', state, a_t)
        state = (state * w_t[:, :, None, :] + sa[..., None] * b_t[:, :, None, :]
                 + v_t[..., None] * k_t[:, :, None, :])
        return state, jnp.einsum('bhvk,bhk->bhv', state, r_t)

    xs = tuple(jnp.moveaxis(t, 1, 0) for t in (r, w, k, v, a, b))
    _, ys = lax.scan(step, jnp.zeros((Bt, H, N, N), F32), xs)
    return jnp.moveaxis(ys, 0, 1)


def _rwkv_mixer(zx, zc, shift_mu, w0, w2, a0, a2, g2, k_k, k_a, r_k, gn_w, gn_b):
    Bt, T, _ = zx.shape
    L = zc.shape[1]
    S = L + T
    z = jnp.concatenate([_centred_shift(zc, shift_mu), _centred_shift(zx, shift_mu)], 1).astype(F32)
    r, k, v, w_lo, a_lo, g_lo = jnp.split(z, RWKV_SPLITS, axis=-1)
    w_lo = w_lo.reshape(Bt, S, 2, RWKV_LORA_W)
    a_lo = a_lo.reshape(Bt, S, 2, RWKV_LORA_A)
    g = jax.nn.sigmoid(g_lo) @ g2

    def heads(t):
        return t.reshape(Bt, S, RWKV_HEADS, HEAD_DIM)

    kk = heads(k * k_k)
    kk = kk * lax.rsqrt(jnp.sum(kk * kk, -1, keepdims=True) + 1e-12)
    rh, vh = heads(r), heads(v)
    y = 0.0
    bonus = 0.0
    for d in range(2):
        w_log = -jax.nn.softplus(-(w0[d] + jnp.tanh(w_lo[:, :, d]) @ w2[d])) - 0.5
        decay = jnp.exp(-jnp.exp(w_log))
        a = jax.nn.sigmoid(a0[d] + a_lo[:, :, d] @ a2[d])
        kd = heads(k * (1.0 + (a - 1.0) * k_a))
        ah = heads(a)
        ins = [_dir_view(t, L, d) for t in (rh, heads(decay), kd, vh, -kk, kk * ah)]
        y = y + _dir_view(_rwkv_scan(*ins), L, d)
        bonus = bonus + jnp.sum(rh * kd * r_k, -1, keepdims=True) * vh
    mu = jnp.mean(y, -1, keepdims=True)
    var = jnp.mean(jnp.square(y - mu), -1, keepdims=True)
    y = ((y - mu) * lax.rsqrt(var + GN_EPS)).reshape(Bt, S, RWKV_WIDTH) * gn_w + gn_b
    y = ((y + bonus.reshape(Bt, S, RWKV_WIDTH)) * g).astype(zx.dtype)
    return y[:, L:], y[:, :L]


def _mixer_a_b(hx, hc, p, ctx_out):
    zx, zc = hx @ p['w_in'], hc @ p['w_in']
    ya_x, ya_c = _s5_mixer(zx[..., :S5_WIDTH], zc[..., :S5_WIDTH], p['s5_lam_re'], p['s5_lam_im'],
                           p['s5_log_dt'], p['s5_b_re'], p['s5_b_im'], p['s5_c_re'], p['s5_c_im'],
                           p['s5_d'], p['w_glu'], p['b_glu'])
    yb_x, yb_c = _rwkv_mixer(zx[..., S5_WIDTH:], zc[..., S5_WIDTH:], p['rwkv_shift'], p['rwkv_w0'],
                             p['rwkv_w2'], p['rwkv_a0'], p['rwkv_a2'], p['rwkv_g2'], p['rwkv_k_k'],
                             p['rwkv_k_a'], p['rwkv_r_k'], p['rwkv_gn_w'], p['rwkv_gn_b'])
    out_x = jnp.concatenate([ya_x, yb_x], -1) @ p['w_out']
    out_c = jnp.concatenate([ya_c, yb_c], -1) @ p['w_out'] if ctx_out else None
    return out_x, out_c


def _window_gqa(q, k, v, k_ctx, v_ctx, sink):
    Bt, T, HQ, hd = q.shape
    HK = k.shape[2]
    G = HQ // HK
    nb = T // WIN_BLOCK
    scale = hd ** -0.5

    def band(z):
        zp = jnp.pad(z, ((0, 0), (WIN_BLOCK, WIN_BLOCK), (0, 0), (0, 0))).reshape(Bt, nb + 2, WIN_BLOCK, HK, hd)
        return jnp.concatenate([zp[:, :-2], zp[:, 1:-1], zp[:, 2:]], axis=2)

    kb, vb = band(k), band(v)
    qb = q.reshape(Bt, nb, WIN_BLOCK, HK, G, hd)
    qi = jnp.arange(WIN_BLOCK)[:, None]
    kj = jnp.arange(3 * WIN_BLOCK)[None, :] - WIN_BLOCK
    kabs = jnp.arange(nb)[:, None] * WIN_BLOCK + kj
    valid = (jnp.abs(kj - qi) <= WINDOW)[None] & ((kabs >= 0) & (kabs < T))[:, None, :]
    sink_col = sink.astype(F32).reshape(1, HK, G, 1, 1)
    n_loc = 3 * WIN_BLOCK

    def one_block(args):
        q_i, k_i, v_i, m_i = args
        s_loc = jnp.einsum('bqkgd,bskd->bkgqs', q_i, k_i).astype(F32) * scale
        s_loc = jnp.where(m_i, s_loc, -1e30)
        s_ctx = jnp.einsum('bqkgd,bskd->bkgqs', q_i, k_ctx).astype(F32) * scale
        s_snk = jnp.broadcast_to(sink_col, (Bt, HK, G, WIN_BLOCK, 1))
        p = jax.nn.softmax(jnp.concatenate([s_loc, s_ctx, s_snk], -1), -1).astype(v.dtype)
        return (jnp.einsum('bkgqs,bskd->bqkgd', p[..., :n_loc], v_i)
                + jnp.einsum('bkgqs,bskd->bqkgd', p[..., n_loc:-1], v_ctx))

    o = lax.map(one_block, (jnp.moveaxis(qb, 1, 0), jnp.moveaxis(kb, 1, 0), jnp.moveaxis(vb, 1, 0), valid))
    return jnp.moveaxis(o, 0, 1).reshape(Bt, T, HQ * hd)


def _context_attention(q, k, v, sink):
    Bt, L, HQ, hd = q.shape
    HK = k.shape[2]
    G = HQ // HK
    s = jnp.einsum('bqkgd,bskd->bkgqs', q.reshape(Bt, L, HK, G, hd), k).astype(F32) * hd ** -0.5
    if sink is not None:
        s = jnp.concatenate([s, jnp.broadcast_to(sink.astype(F32).reshape(1, HK, G, 1, 1), (Bt, HK, G, L, 1))], -1)
    p = jax.nn.softmax(s, -1)[..., :L].astype(v.dtype)
    return jnp.einsum('bkgqs,bskd->bqkgd', p, v).reshape(Bt, L, HQ * hd)


def _neighbourhood_attention(q, k, v, k_ctx, v_ctx, rpb):
    Bt, T, H, hd = q.shape
    rows = T // GRID_W
    kh = min(NA_ROWS, rows)
    kw = min(NA_COLS, GRID_W)
    scale = hd ** -0.5

    def grid(z):
        return z.reshape(Bt, rows, GRID_W, H, hd)

    kg, vg = grid(k), grid(v)
    q_rows = jnp.moveaxis(grid(q), 1, 0)
    r_idx = jnp.arange(rows)
    row_start = jnp.clip(r_idx - kh // 2, 0, rows - kh)
    c_idx = jnp.arange(GRID_W)
    col_win = jnp.clip(c_idx - kw // 2, 0, GRID_W - kw)[:, None] + jnp.arange(kw)[None, :]
    col_rel = col_win - c_idx[:, None] + (NA_COLS - 1)
    n_loc = kh * kw

    def one_row(args):
        r, rs, q_r = args
        k_win = lax.dynamic_slice_in_dim(kg, rs, kh, axis=1)[:, :, col_win]
        v_win = lax.dynamic_slice_in_dim(vg, rs, kh, axis=1)[:, :, col_win]
        row_rel = rs + jnp.arange(kh) - r + (NA_ROWS - 1)
        bias = rpb[:, row_rel[:, None, None], col_rel[None, :, :]]
        s_loc = (jnp.einsum('bqhd,brqjhd->bhqrj', q_r, k_win).astype(F32) * scale
                 + jnp.transpose(bias, (0, 2, 1, 3))[None]).reshape(Bt, H, GRID_W, n_loc)
        s_ctx = jnp.einsum('bqhd,bshd->bhqs', q_r, k_ctx).astype(F32) * scale
        p = jax.nn.softmax(jnp.concatenate([s_loc, s_ctx], -1), -1).astype(v.dtype)
        p_loc = p[..., :n_loc].reshape(Bt, H, GRID_W, kh, kw)
        return (jnp.einsum('bhqrj,brqjhd->bqhd', p_loc, v_win)
                + jnp.einsum('bhqs,bshd->bqhd', p[..., n_loc:], v_ctx))

    o = lax.map(one_row, (r_idx, row_start, q_rows))
    return jnp.moveaxis(o, 0, 1).reshape(Bt, T, H * hd)


def _mixer_c_d(hx, hc, p, row, col, ctx_out):
    def heads(z, n):
        return z.reshape(z.shape[0], z.shape[1], n, HEAD_DIM)

    qcx, kcx, vcx, qdx, kdx, vdx = jnp.split(hx @ p['w_in'], ODD_SPLITS, axis=-1)
    qcc, kcc, vcc, qdc, kdc, vdc = jnp.split(hc @ p['w_in'], ODD_SPLITS, axis=-1)
    kcc, vcc = heads(kcc, WIN_KV_HEADS), heads(vcc, WIN_KV_HEADS)
    kdc, vdc = heads(kdc, NA_HEADS), heads(vdc, NA_HEADS)
    oc = _window_gqa(_axial_rope(heads(qcx, WIN_HEADS), row, col), _axial_rope(heads(kcx, WIN_KV_HEADS), row, col),
                     heads(vcx, WIN_KV_HEADS), kcc, vcc, p['sink'])
    od = _neighbourhood_attention(heads(qdx, NA_HEADS), heads(kdx, NA_HEADS), heads(vdx, NA_HEADS), kdc, vdc, p['rpb'])
    out_x = jnp.concatenate([oc, od], -1) @ p['w_out']
    out_c = None
    if ctx_out:
        occ = _context_attention(heads(qcc, WIN_HEADS), kcc, vcc, p['sink'])
        odc = _context_attention(heads(qdc, NA_HEADS), kdc, vdc, None)
        out_c = jnp.concatenate([occ, odc], -1) @ p['w_out']
    return out_x, out_c


def _peer(h, wq, sub_keys, u_tab, v_tab):
    Bt, T, Dm = h.shape
    half = PEER_DK // 2
    n_cand = PEER_TOPK * PEER_TOPK

    def one_block(hb):
        q = (hb @ wq).reshape(PEER_BLOCK, PEER_HEADS, 2, half)
        s = jnp.einsum('nhcd,hckd->nhck', q, sub_keys).astype(F32)
        s1, i1 = lax.top_k(s[:, :, 0], PEER_TOPK)
        s2, i2 = lax.top_k(s[:, :, 1], PEER_TOPK)
        cand = (s1[..., :, None] + s2[..., None, :]).reshape(PEER_BLOCK, PEER_HEADS, n_cand)
        cand_idx = (i1[..., :, None] * PEER_KEYS + i2[..., None, :]).reshape(PEER_BLOCK, PEER_HEADS, n_cand)
        top, pos = lax.top_k(cand, PEER_TOPK)
        idx = jnp.take_along_axis(cand_idx, pos, -1).reshape(PEER_BLOCK, PEER_HEADS * PEER_TOPK)
        gate = jax.nn.softmax(top, -1).reshape(PEER_BLOCK, PEER_HEADS * PEER_TOPK)
        u_e, v_e = u_tab[idx], v_tab[idx]
        act = jax.nn.gelu(jnp.einsum('nd,ned->ne', hb, u_e).astype(F32), approximate=False)
        return jnp.einsum('ne,ned->nd', (gate * act).astype(v_e.dtype), v_e)

    out = lax.map(one_block, h.reshape((Bt * T) // PEER_BLOCK, PEER_BLOCK, Dm))
    return out.reshape(Bt, T, Dm)


def setup_inputs(seed: int = 0) -> dict:
    key = jax.random.key(seed)
    keys = iter(jax.random.split(key, 96))

    def nrm(shape, scale):
        return jax.random.normal(next(keys), shape, F32) * scale

    def unif(shape, lo, hi):
        return jax.random.uniform(next(keys), shape, F32, lo, hi)

    D = D_MODEL
    lam_n = math.pi * jnp.arange(S5_STATE, dtype=F32)
    return {
        'x': nrm((BATCH, SEQ, D), 1.0),
        'c': nrm((BATCH, D), 1.0),
        'ctx': nrm((BATCH, CTX_LEN, D), 1.0),
        'c_ctx': nrm((D,), 1.0),
        'l0_w_mod': nrm((D, 6 * D), 0.5 * D ** -0.5),
        'l0_b_mod': nrm((6 * D,), 0.02),
        'l0_w_in': nrm((D, EVEN_IN), D ** -0.5),
        'l0_s5_lam_re': -0.5 + nrm((2, S5_GROUPS, S5_STATE), 0.01),
        'l0_s5_lam_im': lam_n + nrm((2, S5_GROUPS, S5_STATE), 0.01),
        'l0_s5_log_dt': unif((2, S5_GROUPS), math.log(1e-3), math.log(1e-1)),
        'l0_s5_b_re': nrm((2, S5_GROUPS, S5_STATE, S5_GROUP_CH), (2 * S5_GROUP_CH) ** -0.5),
        'l0_s5_b_im': nrm((2, S5_GROUPS, S5_STATE, S5_GROUP_CH), (2 * S5_GROUP_CH) ** -0.5),
        'l0_s5_c_re': nrm((2, S5_GROUPS, S5_GROUP_CH, S5_STATE), S5_STATE ** -0.5),
        'l0_s5_c_im': nrm((2, S5_GROUPS, S5_GROUP_CH, S5_STATE), S5_STATE ** -0.5),
        'l0_s5_d': nrm((S5_WIDTH,), 1.0),
        'l0_w_glu': nrm((S5_WIDTH, S5_WIDTH), S5_WIDTH ** -0.5),
        'l0_b_glu': nrm((S5_WIDTH,), 0.02),
        'l0_rwkv_shift': unif((2, RWKV_IN), 0.0, 0.5),
        'l0_rwkv_w0': unif((2, RWKV_WIDTH), -6.0, -1.0),
        'l0_rwkv_w2': nrm((2, RWKV_LORA_W, RWKV_WIDTH), 0.1 * RWKV_LORA_W ** -0.5),
        'l0_rwkv_a0': nrm((2, RWKV_WIDTH), 0.1),
        'l0_rwkv_a2': nrm((2, RWKV_LORA_A, RWKV_WIDTH), 0.1 * RWKV_LORA_A ** -0.5),
        'l0_rwkv_g2': nrm((RWKV_LORA_G, RWKV_WIDTH), RWKV_LORA_G ** -0.5),
        'l0_rwkv_k_k': 0.85 + nrm((RWKV_WIDTH,), 0.02),
        'l0_rwkv_k_a': 1.0 + nrm((RWKV_WIDTH,), 0.02),
        'l0_rwkv_r_k': nrm((RWKV_HEADS, HEAD_DIM), 0.1),
        'l0_rwkv_gn_w': 1.0 + nrm((RWKV_WIDTH,), 0.02),
        'l0_rwkv_gn_b': nrm((RWKV_WIDTH,), 0.02),
        'l0_w_out': nrm((EVEN_MIX, D), BETA * EVEN_MIX ** -0.5),
        'l0_ln1_g': 1.0 + nrm((D,), 0.02),
        'l0_ln1_b': nrm((D,), 0.02),
        'l0_peer_wq': nrm((D, PEER_HEADS * PEER_DK), D ** -0.5),
        'l0_peer_keys': nrm((PEER_HEADS, 2, PEER_KEYS, PEER_DK // 2), (PEER_DK // 2) ** -0.5),
        'l0_peer_u': nrm((PEER_EXPERTS, D), D ** -0.5),
        'l0_peer_v': nrm((PEER_EXPERTS, D), BETA),
        'l0_ln2_g': 1.0 + nrm((D,), 0.02),
        'l0_ln2_b': nrm((D,), 0.02),
        'l1_w_mod': nrm((D, 6 * D), 0.5 * D ** -0.5),
        'l1_b_mod': nrm((6 * D,), 0.02),
        'l1_w_in': nrm((D, ODD_IN), D ** -0.5),
        'l1_sink': nrm((WIN_HEADS,), 0.5),
        'l1_rpb': nrm((NA_HEADS, 2 * NA_ROWS - 1, 2 * NA_COLS - 1), 0.1),
        'l1_w_out': nrm((ODD_MIX, D), BETA * ODD_MIX ** -0.5),
        'l1_ln1_g': 1.0 + nrm((D,), 0.02),
        'l1_ln1_b': nrm((D,), 0.02),
        'l1_peer_wq': nrm((D, PEER_HEADS * PEER_DK), D ** -0.5),
        'l1_peer_keys': nrm((PEER_HEADS, 2, PEER_KEYS, PEER_DK // 2), (PEER_DK // 2) ** -0.5),
        'l1_peer_u': nrm((PEER_EXPERTS, D), D ** -0.5),
        'l1_peer_v': nrm((PEER_EXPERTS, D), BETA),
        'l1_ln2_g': 1.0 + nrm((D,), 0.02),
        'l1_ln2_b': nrm((D,), 0.02),
    }


def reference(x, c, ctx, c_ctx,
              l0_w_mod, l0_b_mod, l0_w_in, l0_s5_lam_re, l0_s5_lam_im, l0_s5_log_dt, l0_s5_b_re, l0_s5_b_im,
              l0_s5_c_re, l0_s5_c_im, l0_s5_d, l0_w_glu, l0_b_glu, l0_rwkv_shift, l0_rwkv_w0, l0_rwkv_w2,
              l0_rwkv_a0, l0_rwkv_a2, l0_rwkv_g2, l0_rwkv_k_k, l0_rwkv_k_a, l0_rwkv_r_k, l0_rwkv_gn_w,
              l0_rwkv_gn_b, l0_w_out, l0_ln1_g, l0_ln1_b, l0_peer_wq, l0_peer_keys, l0_peer_u, l0_peer_v,
              l0_ln2_g, l0_ln2_b,
              l1_w_mod, l1_b_mod, l1_w_in, l1_sink, l1_rpb, l1_w_out, l1_ln1_g, l1_ln1_b, l1_peer_wq,
              l1_peer_keys, l1_peer_u, l1_peer_v, l1_ln2_g, l1_ln2_b):
    T = x.shape[1]
    t = jnp.arange(T)
    row = (t // GRID_W).astype(F32)
    col = (t % GRID_W).astype(F32)
    layers = (
        dict(w_mod=l0_w_mod, b_mod=l0_b_mod, w_in=l0_w_in, s5_lam_re=l0_s5_lam_re, s5_lam_im=l0_s5_lam_im,
             s5_log_dt=l0_s5_log_dt, s5_b_re=l0_s5_b_re, s5_b_im=l0_s5_b_im, s5_c_re=l0_s5_c_re,
             s5_c_im=l0_s5_c_im, s5_d=l0_s5_d, w_glu=l0_w_glu, b_glu=l0_b_glu, rwkv_shift=l0_rwkv_shift,
             rwkv_w0=l0_rwkv_w0, rwkv_w2=l0_rwkv_w2, rwkv_a0=l0_rwkv_a0, rwkv_a2=l0_rwkv_a2,
             rwkv_g2=l0_rwkv_g2, rwkv_k_k=l0_rwkv_k_k, rwkv_k_a=l0_rwkv_k_a, rwkv_r_k=l0_rwkv_r_k,
             rwkv_gn_w=l0_rwkv_gn_w, rwkv_gn_b=l0_rwkv_gn_b, w_out=l0_w_out, ln1_g=l0_ln1_g, ln1_b=l0_ln1_b,
             peer_wq=l0_peer_wq, peer_keys=l0_peer_keys, peer_u=l0_peer_u, peer_v=l0_peer_v,
             ln2_g=l0_ln2_g, ln2_b=l0_ln2_b),
        dict(w_mod=l1_w_mod, b_mod=l1_b_mod, w_in=l1_w_in, sink=l1_sink, rpb=l1_rpb, w_out=l1_w_out,
             ln1_g=l1_ln1_g, ln1_b=l1_ln1_b, peer_wq=l1_peer_wq, peer_keys=l1_peer_keys, peer_u=l1_peer_u,
             peer_v=l1_peer_v, ln2_g=l1_ln2_g, ln2_b=l1_ln2_b),
    )
    for i in range(DEPTH):
        p = layers[i]
        last = i == DEPTH - 1
        sx1, cx1, gx1, sx2, cx2, gx2 = _modulation(c, p['w_mod'], p['b_mod'])
        sc1, cc1, gc1, sc2, cc2, gc2 = _modulation(c_ctx[None, :], p['w_mod'], p['b_mod'])
        hx = _modulate(x, sx1, cx1)
        hc = _modulate(ctx, sc1, cc1)
        if i % 2 == 0:
            yx, yc = _mixer_a_b(hx, hc, p, not last)
        else:
            yx, yc = _mixer_c_d(hx, hc, p, row, col, not last)
        x = _layer_norm(ALPHA * x + gx1 * yx, p['ln1_g'], p['ln1_b'])
        x = _layer_norm(ALPHA * x + gx2 * _peer(_modulate(x, sx2, cx2), p['peer_wq'], p['peer_keys'],
                                                p['peer_u'], p['peer_v']), p['ln2_g'], p['ln2_b'])
        if not last:
            ctx = _layer_norm(ALPHA * ctx + gc1 * yc, p['ln1_g'], p['ln1_b'])
            ctx = _layer_norm(ALPHA * ctx + gc2 * _peer(_modulate(ctx, sc2, cc2), p['peer_wq'], p['peer_keys'],
                                                        p['peer_u'], p['peer_v']), p['ln2_g'], p['ln2_b'])
    return x
```

```python
import functools
import math

import jax
import jax.numpy as jnp
import numpy as np
from jax import lax
from jax.experimental import pallas as pl
from jax.experimental.pallas import tpu as pltpu

F32 = jnp.float32
BF16 = jnp.bfloat16

D_MODEL = 1024
DEPTH = 2
GRID_W = 64
CTX_LEN = 256
HEAD_DIM = 64
S5_GROUP_CH = 16
S5_GROUPS = 16
S5_WIDTH = S5_GROUPS * S5_GROUP_CH
S5_STATE = 64
RWKV_HEADS = 8
RWKV_WIDTH = RWKV_HEADS * HEAD_DIM
RWKV_LORA_W = 64
RWKV_LORA_A = 64
RWKV_LORA_G = 128
RWKV_IN = 3 * RWKV_WIDTH + 2 * RWKV_LORA_W + 2 * RWKV_LORA_A + RWKV_LORA_G
RWKV_SPLITS = (RWKV_WIDTH, 2 * RWKV_WIDTH, 3 * RWKV_WIDTH,
               3 * RWKV_WIDTH + 2 * RWKV_LORA_W,
               3 * RWKV_WIDTH + 2 * RWKV_LORA_W + 2 * RWKV_LORA_A)
WIN_HEADS = 8
WIN_KV_HEADS = 2
WINDOW = 128
WIN_BLOCK = 128
NA_HEADS = 8
NA_ROWS = 8
NA_COLS = 16
ODD_SPLITS = (WIN_HEADS * HEAD_DIM, (WIN_HEADS + WIN_KV_HEADS) * HEAD_DIM,
              (WIN_HEADS + 2 * WIN_KV_HEADS) * HEAD_DIM,
              (WIN_HEADS + 2 * WIN_KV_HEADS + NA_HEADS) * HEAD_DIM,
              (WIN_HEADS + 2 * WIN_KV_HEADS + 2 * NA_HEADS) * HEAD_DIM)
ROPE_BASE = 10000.0
PEER_HEADS = 8
PEER_KEYS = 128
PEER_DK = 128
PEER_TOPK = 16
PEER_BLOCK = 128
ALPHA = (2.0 * DEPTH) ** 0.25
LN_EPS = 1e-5
GN_EPS = 64e-5

VMEM_LIMIT_BYTES = 56 * 1024 * 1024
ROW_BLOCK = 256


def _cparams(*sem):
    return pltpu.CompilerParams(dimension_semantics=sem, vmem_limit_bytes=VMEM_LIMIT_BYTES)


def _modulation_kernel(c_ref, w_ref, b_ref, o_ref):
    c = c_ref[...]
    s = c * jax.nn.sigmoid(c)
    o_ref[...] = jnp.dot(s.astype(BF16), w_ref[...].astype(BF16), preferred_element_type=F32) + b_ref[...]


def _modulation(cond, w_mod, b_mod):
    n, d = cond.shape
    nout = w_mod.shape[1]
    tn = 1536
    return pl.pallas_call(
        _modulation_kernel,
        grid=(nout // tn,),
        in_specs=[pl.BlockSpec((n, d), lambda j: (0, 0)),
                  pl.BlockSpec((d, tn), lambda j: (0, j)),
                  pl.BlockSpec((1, tn), lambda j: (0, j))],
        out_specs=pl.BlockSpec((n, tn), lambda j: (0, j)),
        out_shape=jax.ShapeDtypeStruct((n, nout), F32),
        compiler_params=_cparams("arbitrary"),
        name="modulation",
    )(cond, w_mod, b_mod.reshape(1, nout))


def _mod_tables(c, c_ctx, w_mod, b_mod):
    bsz = c.shape[0]
    cond = jnp.concatenate([c, c_ctx[None, :], jnp.zeros((8 - bsz - 1, c.shape[1]), F32)], 0)
    m = _modulation(cond, w_mod, b_mod)
    mx = m[:bsz].reshape(bsz, 1, 6, 1, D_MODEL)
    mc = jnp.broadcast_to(m[bsz].reshape(1, 1, 6, 1, D_MODEL), (bsz, 1, 6, 1, D_MODEL))
    return jnp.concatenate([mc, mx], 1)


def _tab_spec(which):
    return pl.BlockSpec((None, None, None, 1, D_MODEL),
                        lambda b, i: (b, jnp.minimum(i, 1), which, 0, 0))


def _inproj_kernel(x_ref, shift_ref, scale_ref, w_ref, o_ref):
    hm = x_ref[...] * (1.0 + scale_ref[...]) + shift_ref[...]
    o_ref[...] = jnp.dot(hm.astype(BF16), w_ref[...], preferred_element_type=F32)


def _inproj(h, tab, which_shift, which_scale, w_bf16):
    bsz, s, d = h.shape
    n = w_bf16.shape[1]
    return pl.pallas_call(
        _inproj_kernel,
        grid=(bsz, s // ROW_BLOCK),
        in_specs=[pl.BlockSpec((None, ROW_BLOCK, d), lambda b, i: (b, i, 0)),
                  _tab_spec(which_shift), _tab_spec(which_scale),
                  pl.BlockSpec((d, n), lambda b, i: (0, 0))],
        out_specs=pl.BlockSpec((None, ROW_BLOCK, n), lambda b, i: (b, i, 0)),
        out_shape=jax.ShapeDtypeStruct((bsz, s, n), F32),
        compiler_params=_cparams("parallel", "parallel"),
        name="inproj",
    )(h, tab, tab, w_bf16)


def _layer_norm_rows(z, g, b):
    mu = jnp.mean(z, -1, keepdims=True)
    zc = z - mu
    var = jnp.mean(zc * zc, -1, keepdims=True)
    return zc * lax.rsqrt(var + LN_EPS) * g + b


def _outproj_ln_kernel(y_ref, h_ref, gate_ref, w_ref, g_ref, b_ref, o_ref):
    proj = jnp.dot(y_ref[...].astype(BF16), w_ref[...], preferred_element_type=F32)
    z = ALPHA * h_ref[...] + gate_ref[...] * proj
    o_ref[...] = _layer_norm_rows(z, g_ref[...], b_ref[...])


def _outproj_ln(y, h, tab, which_gate, w_bf16, ln_g, ln_b):
    bsz, s, d = h.shape
    k = y.shape[-1]
    return pl.pallas_call(
        _outproj_ln_kernel,
        grid=(bsz, s // ROW_BLOCK),
        in_specs=[pl.BlockSpec((None, ROW_BLOCK, k), lambda b, i: (b, i, 0)),
                  pl.BlockSpec((None, ROW_BLOCK, d), lambda b, i: (b, i, 0)),
                  _tab_spec(which_gate),
                  pl.BlockSpec((k, d), lambda b, i: (0, 0)),
                  pl.BlockSpec((1, d), lambda b, i: (0, 0)),
                  pl.BlockSpec((1, d), lambda b, i: (0, 0))],
        out_specs=pl.BlockSpec((None, ROW_BLOCK, d), lambda b, i: (b, i, 0)),
        out_shape=jax.ShapeDtypeStruct((bsz, s, d), F32),
        compiler_params=_cparams("parallel", "parallel"),
        name="outproj_ln",
    )(y, h, tab, w_bf16, ln_g.reshape(1, d), ln_b.reshape(1, d))


def _dir_view(z, n_ctx, d):
    if d == 0:
        return z
    return jnp.concatenate([jnp.flip(z[:, :n_ctx], 1), jnp.flip(z[:, n_ctx:], 1)], 1)


def _axial_rope(z, row, col):
    half = HEAD_DIM // 2
    quarter = half // 2
    freqs = ROPE_BASE ** (-jnp.arange(quarter, dtype=F32) / quarter)

    def rot(za, pos):
        ang = pos[:, None] * freqs[None, :]
        cos = jnp.cos(ang)[None, :, None, :].astype(z.dtype)
        sin = jnp.sin(ang)[None, :, None, :].astype(z.dtype)
        z1, z2 = za[..., :quarter], za[..., quarter:]
        return jnp.concatenate([z1 * cos - z2 * sin, z2 * cos + z1 * sin], -1)

    return jnp.concatenate([rot(z[..., :half], row), rot(z[..., half:], col)], -1)


def _complex_diag_scan(a_re, a_im, b_re, b_im):
    def combine(e1, e2):
        a1r, a1i, b1r, b1i = e1
        a2r, a2i, b2r, b2i = e2
        return (a1r * a2r - a1i * a2i, a1r * a2i + a1i * a2r,
                a2r * b1r - a2i * b1i + b2r, a2r * b1i + a2i * b1r + b2i)

    a_re = jnp.broadcast_to(a_re, b_re.shape)
    a_im = jnp.broadcast_to(a_im, b_re.shape)
    _, _, h_re, h_im = lax.associative_scan(combine, (a_re, a_im, b_re, b_im), axis=1)
    return h_re, h_im


def _s5_mixer(u_all, L, lam_re, lam_im, log_dt, b_re, b_im, c_re, c_im, d_skip, w_glu, b_glu):
    Bt, S, _ = u_all.shape
    u = u_all.reshape(Bt, S, S5_GROUPS, S5_GROUP_CH)
    y = u * d_skip.reshape(S5_GROUPS, S5_GROUP_CH)
    for d in range(2):
        lr, li = lam_re[d], lam_im[d]
        dt = jnp.exp(log_dt[d])[:, None]
        mag = jnp.exp(lr * dt)
        ab_re, ab_im = mag * jnp.cos(li * dt), mag * jnp.sin(li * dt)
        den = lr * lr + li * li
        nr = ab_re - 1.0
        coef_re = (nr * lr + ab_im * li) / den
        coef_im = (ab_im * lr - nr * li) / den
        br, bi = b_re[d], b_im[d]
        bb_re = coef_re[..., None] * br - coef_im[..., None] * bi
        bb_im = coef_re[..., None] * bi + coef_im[..., None] * br
        seq = _dir_view(u, L, d)
        bu_re = jnp.einsum('bsgi,gpi->bsgp', seq, bb_re)
        bu_im = jnp.einsum('bsgi,gpi->bsgp', seq, bb_im)
        h_re, h_im = _complex_diag_scan(ab_re, ab_im, bu_re, bu_im)
        yd = (jnp.einsum('bsgp,gip->bsgi', h_re, c_re[d]) - jnp.einsum('bsgp,gip->bsgi', h_im, c_im[d]))
        y = y + _dir_view(yd, L, d)
    y = jax.nn.gelu(y.reshape(Bt, S, S5_WIDTH), approximate=False)
    return y * jax.nn.sigmoid(y @ w_glu + b_glu)


def _centred_shift(z, mu):
    prev = jnp.pad(z[:, :-1], ((0, 0), (1, 0), (0, 0)))
    nxt = jnp.pad(z[:, 1:], ((0, 0), (0, 1), (0, 0)))
    return z + mu[0] * (prev - z) + mu[1] * (nxt - z)


def _rwkv_scan(r, w, k, v, a, b):
    Bt, S, H, N = r.shape

    def step(state, inp):
        r_t, w_t, k_t, v_t, a_t, b_t = inp
        sa = jnp.einsum('bhvk,bhk->bhv', state, a_t)
        state = (state * w_t[:, :, None, :] + sa[..., None] * b_t[:, :, None, :]
                 + v_t[..., None] * k_t[:, :, None, :])
        return state, jnp.einsum('bhvk,bhk->bhv', state, r_t)

    xs = tuple(jnp.moveaxis(t, 1, 0) for t in (r, w, k, v, a, b))
    _, ys = lax.scan(step, jnp.zeros((Bt, H, N, N), F32), xs)
    return jnp.moveaxis(ys, 0, 1)


def _rwkv_mixer(z_all, L, shift_mu, w0, w2, a0, a2, g2, k_k, k_a, r_k, gn_w, gn_b):
    Bt, S, _ = z_all.shape
    z = jnp.concatenate([_centred_shift(z_all[:, :L], shift_mu), _centred_shift(z_all[:, L:], shift_mu)], 1)
    r, k, v, w_lo, a_lo, g_lo = jnp.split(z, RWKV_SPLITS, axis=-1)
    w_lo = w_lo.reshape(Bt, S, 2, RWKV_LORA_W)
    a_lo = a_lo.reshape(Bt, S, 2, RWKV_LORA_A)
    g = jax.nn.sigmoid(g_lo) @ g2

    def heads(t):
        return t.reshape(Bt, S, RWKV_HEADS, HEAD_DIM)

    kk = heads(k * k_k)
    kk = kk * lax.rsqrt(jnp.sum(kk * kk, -1, keepdims=True) + 1e-12)
    rh, vh = heads(r), heads(v)
    y = 0.0
    bonus = 0.0
    for d in range(2):
        w_log = -jax.nn.softplus(-(w0[d] + jnp.tanh(w_lo[:, :, d]) @ w2[d])) - 0.5
        decay = jnp.exp(-jnp.exp(w_log))
        a = jax.nn.sigmoid(a0[d] + a_lo[:, :, d] @ a2[d])
        kd = heads(k * (1.0 + (a - 1.0) * k_a))
        ah = heads(a)
        ins = [_dir_view(t, L, d) for t in (rh, heads(decay), kd, vh, -kk, kk * ah)]
        y = y + _dir_view(_rwkv_scan(*ins), L, d)
        bonus = bonus + jnp.sum(rh * kd * r_k, -1, keepdims=True) * vh
    mu = jnp.mean(y, -1, keepdims=True)
    var = jnp.mean(jnp.square(y - mu), -1, keepdims=True)
    y = ((y - mu) * lax.rsqrt(var + GN_EPS)).reshape(Bt, S, RWKV_WIDTH) * gn_w + gn_b
    return (y + bonus.reshape(Bt, S, RWKV_WIDTH)) * g


def _window_gqa(q, k, v, k_ctx, v_ctx, sink):
    Bt, T, HQ, hd = q.shape
    HK = k.shape[2]
    G = HQ // HK
    nb = T // WIN_BLOCK
    scale = hd ** -0.5

    def band(z):
        zp = jnp.pad(z, ((0, 0), (WIN_BLOCK, WIN_BLOCK), (0, 0), (0, 0))).reshape(Bt, nb + 2, WIN_BLOCK, HK, hd)
        return jnp.concatenate([zp[:, :-2], zp[:, 1:-1], zp[:, 2:]], axis=2)

    kb, vb = band(k), band(v)
    qb = q.reshape(Bt, nb, WIN_BLOCK, HK, G, hd)
    qi = jnp.arange(WIN_BLOCK)[:, None]
    kj = jnp.arange(3 * WIN_BLOCK)[None, :] - WIN_BLOCK
    kabs = jnp.arange(nb)[:, None] * WIN_BLOCK + kj
    valid = (jnp.abs(kj - qi) <= WINDOW)[None] & ((kabs >= 0) & (kabs < T))[:, None, :]
    sink_col = sink.reshape(1, HK, G, 1, 1)
    n_loc = 3 * WIN_BLOCK

    def one_block(args):
        q_i, k_i, v_i, m_i = args
        s_loc = jnp.einsum('bqkgd,bskd->bkgqs', q_i, k_i) * scale
        s_loc = jnp.where(m_i, s_loc, -1e30)
        s_ctx = jnp.einsum('bqkgd,bskd->bkgqs', q_i, k_ctx) * scale
        s_snk = jnp.broadcast_to(sink_col, (Bt, HK, G, WIN_BLOCK, 1))
        p = jax.nn.softmax(jnp.concatenate([s_loc, s_ctx, s_snk], -1), -1)
        return (jnp.einsum('bkgqs,bskd->bqkgd', p[..., :n_loc], v_i)
                + jnp.einsum('bkgqs,bskd->bqkgd', p[..., n_loc:-1], v_ctx))

    o = lax.map(one_block, (jnp.moveaxis(qb, 1, 0), jnp.moveaxis(kb, 1, 0), jnp.moveaxis(vb, 1, 0), valid))
    return jnp.moveaxis(o, 0, 1).reshape(Bt, T, HQ * hd)


def _neighbourhood_attention(q, k, v, k_ctx, v_ctx, rpb):
    Bt, T, H, hd = q.shape
    rows = T // GRID_W
    kh = min(NA_ROWS, rows)
    kw = min(NA_COLS, GRID_W)
    scale = hd ** -0.5

    def grid(z):
        return z.reshape(Bt, rows, GRID_W, H, hd)

    kg, vg = grid(k), grid(v)
    q_rows = jnp.moveaxis(grid(q), 1, 0)
    r_idx = jnp.arange(rows)
    row_start = jnp.clip(r_idx - kh // 2, 0, rows - kh)
    c_idx = jnp.arange(GRID_W)
    col_win = jnp.clip(c_idx - kw // 2, 0, GRID_W - kw)[:, None] + jnp.arange(kw)[None, :]
    col_rel = col_win - c_idx[:, None] + (NA_COLS - 1)
    n_loc = kh * kw

    def one_row(args):
        r, rs, q_r = args
        k_win = lax.dynamic_slice_in_dim(kg, rs, kh, axis=1)[:, :, col_win]
        v_win = lax.dynamic_slice_in_dim(vg, rs, kh, axis=1)[:, :, col_win]
        row_rel = rs + jnp.arange(kh) - r + (NA_ROWS - 1)
        bias = rpb[:, row_rel[:, None, None], col_rel[None, :, :]]
        s_loc = (jnp.einsum('bqhd,brqjhd->bhqrj', q_r, k_win) * scale
                 + jnp.transpose(bias, (0, 2, 1, 3))[None]).reshape(Bt, H, GRID_W, n_loc)
        s_ctx = jnp.einsum('bqhd,bshd->bhqs', q_r, k_ctx) * scale
        p = jax.nn.softmax(jnp.concatenate([s_loc, s_ctx], -1), -1)
        p_loc = p[..., :n_loc].reshape(Bt, H, GRID_W, kh, kw)
        return (jnp.einsum('bhqrj,brqjhd->bqhd', p_loc, v_win)
                + jnp.einsum('bhqs,bshd->bqhd', p[..., n_loc:], v_ctx))

    o = lax.map(one_row, (r_idx, row_start, q_rows))
    return jnp.moveaxis(o, 0, 1).reshape(Bt, T, H * hd)


def _mixer_c_d(zx, zc, sink, rpb, row, col):
    def heads(z, n):
        return z.reshape(z.shape[0], z.shape[1], n, HEAD_DIM)

    qcx, kcx, vcx, qdx, kdx, vdx = jnp.split(zx, ODD_SPLITS, axis=-1)
    _, kcc, vcc, _, kdc, vdc = jnp.split(zc, ODD_SPLITS, axis=-1)
    kcc, vcc = heads(kcc, WIN_KV_HEADS), heads(vcc, WIN_KV_HEADS)
    kdc, vdc = heads(kdc, NA_HEADS), heads(vdc, NA_HEADS)
    oc = _window_gqa(_axial_rope(heads(qcx, WIN_HEADS), row, col), _axial_rope(heads(kcx, WIN_KV_HEADS), row, col),
                     heads(vcx, WIN_KV_HEADS), kcc, vcc, sink)
    od = _neighbourhood_attention(heads(qdx, NA_HEADS), heads(kdx, NA_HEADS), heads(vdx, NA_HEADS), kdc, vdc, rpb)
    return jnp.concatenate([oc, od], -1)


def _peer(h, wq, sub_keys, u_tab, v_tab):
    Bt, T, Dm = h.shape
    half = PEER_DK // 2
    n_cand = PEER_TOPK * PEER_TOPK

    def one_block(hb):
        q = (hb @ wq).reshape(PEER_BLOCK, PEER_HEADS, 2, half)
        s = jnp.einsum('nhcd,hckd->nhck', q, sub_keys)
        s1, i1 = lax.top_k(s[:, :, 0], PEER_TOPK)
        s2, i2 = lax.top_k(s[:, :, 1], PEER_TOPK)
        cand = (s1[..., :, None] + s2[..., None, :]).reshape(PEER_BLOCK, PEER_HEADS, n_cand)
        cand_idx = (i1[..., :, None] * PEER_KEYS + i2[..., None, :]).reshape(PEER_BLOCK, PEER_HEADS, n_cand)
        top, pos = lax.top_k(cand, PEER_TOPK)
        idx = jnp.take_along_axis(cand_idx, pos, -1).reshape(PEER_BLOCK, PEER_HEADS * PEER_TOPK)
        gate = jax.nn.softmax(top, -1).reshape(PEER_BLOCK, PEER_HEADS * PEER_TOPK)
        u_e, v_e = u_tab[idx], v_tab[idx]
        act = jax.nn.gelu(jnp.einsum('nd,ned->ne', hb, u_e), approximate=False)
        return jnp.einsum('ne,ned->nd', gate * act, v_e)

    out = lax.map(one_block, h.reshape((Bt * T) // PEER_BLOCK, PEER_BLOCK, Dm))
    return out.reshape(Bt, T, Dm)


def _ln(z, g, b):
    mu = jnp.mean(z, -1, keepdims=True)
    var = jnp.mean(jnp.square(z - mu), -1, keepdims=True)
    return (z - mu) * lax.rsqrt(var + LN_EPS) * g + b


def kernel(x, c, ctx, c_ctx,
           l0_w_mod, l0_b_mod, l0_w_in, l0_s5_lam_re, l0_s5_lam_im, l0_s5_log_dt, l0_s5_b_re, l0_s5_b_im,
           l0_s5_c_re, l0_s5_c_im, l0_s5_d, l0_w_glu, l0_b_glu, l0_rwkv_shift, l0_rwkv_w0, l0_rwkv_w2,
           l0_rwkv_a0, l0_rwkv_a2, l0_rwkv_g2, l0_rwkv_k_k, l0_rwkv_k_a, l0_rwkv_r_k, l0_rwkv_gn_w,
           l0_rwkv_gn_b, l0_w_out, l0_ln1_g, l0_ln1_b, l0_peer_wq, l0_peer_keys, l0_peer_u, l0_peer_v,
           l0_ln2_g, l0_ln2_b,
           l1_w_mod, l1_b_mod, l1_w_in, l1_sink, l1_rpb, l1_w_out, l1_ln1_g, l1_ln1_b, l1_peer_wq,
           l1_peer_keys, l1_peer_u, l1_peer_v, l1_ln2_g, l1_ln2_b):
    L = ctx.shape[1]
    T = x.shape[1]
    t = jnp.arange(T)
    row = (t // GRID_W).astype(F32)
    col = (t % GRID_W).astype(F32)
    h = jnp.concatenate([ctx, x], 1)

    tab = _mod_tables(c, c_ctx, l0_w_mod, l0_b_mod)
    z = _inproj(h, tab, 0, 1, l0_w_in.astype(BF16))
    ya = _s5_mixer(z[..., :S5_WIDTH], L, l0_s5_lam_re, l0_s5_lam_im, l0_s5_log_dt, l0_s5_b_re, l0_s5_b_im,
                   l0_s5_c_re, l0_s5_c_im, l0_s5_d, l0_w_glu, l0_b_glu)
    yb = _rwkv_mixer(z[..., S5_WIDTH:], L, l0_rwkv_shift, l0_rwkv_w0, l0_rwkv_w2, l0_rwkv_a0, l0_rwkv_a2,
                     l0_rwkv_g2, l0_rwkv_k_k, l0_rwkv_k_a, l0_rwkv_r_k, l0_rwkv_gn_w, l0_rwkv_gn_b)
    h = _outproj_ln(jnp.concatenate([ya, yb], -1), h, tab, 2, l0_w_out.astype(BF16), l0_ln1_g, l0_ln1_b)
    is_x = (jnp.arange(L + T) >= L)[None, :, None]

    def seg_vec(which):
        return jnp.where(is_x, tab[:, 1, which], tab[:, 0, which])

    pin = h * (1.0 + seg_vec(4)) + seg_vec(3)
    h = _ln(ALPHA * h + seg_vec(5) * _peer(pin, l0_peer_wq, l0_peer_keys, l0_peer_u, l0_peer_v), l0_ln2_g, l0_ln2_b)

    tab = _mod_tables(c, c_ctx, l1_w_mod, l1_b_mod)
    z = _inproj(h, tab, 0, 1, l1_w_in.astype(BF16))
    y = _mixer_c_d(z[:, L:], z[:, :L], l1_sink, l1_rpb, row, col)
    y = jnp.concatenate([jnp.zeros((y.shape[0], L, y.shape[2]), F32), y], 1)
    h = _outproj_ln(y, h, tab, 2, l1_w_out.astype(BF16), l1_ln1_g, l1_ln1_b)
    hx = h[:, L:]
    mx = tab[:, 1, :, 0, :]
    pin = hx * (1.0 + mx[:, None, 4]) + mx[:, None, 3]
    return _ln(ALPHA * hx + mx[:, None, 5] * _peer(pin, l1_peer_wq, l1_peer_keys, l1_peer_u, l1_peer_v),
               l1_ln2_g, l1_ln2_b)
```

```python
import functools
import math

import jax
import jax.numpy as jnp
import numpy as np
from jax import lax
from jax.experimental import pallas as pl
from jax.experimental.pallas import tpu as pltpu

F32 = jnp.float32
BF16 = jnp.bfloat16

D_MODEL = 1024
DEPTH = 2
GRID_W = 64
CTX_LEN = 256
HEAD_DIM = 64
S5_GROUP_CH = 16
S5_GROUPS = 16
S5_WIDTH = S5_GROUPS * S5_GROUP_CH
S5_STATE = 64
RWKV_HEADS = 8
RWKV_WIDTH = RWKV_HEADS * HEAD_DIM
RWKV_LORA_W = 64
RWKV_LORA_A = 64
RWKV_LORA_G = 128
RWKV_IN = 3 * RWKV_WIDTH + 2 * RWKV_LORA_W + 2 * RWKV_LORA_A + RWKV_LORA_G
RWKV_SPLITS = (RWKV_WIDTH, 2 * RWKV_WIDTH, 3 * RWKV_WIDTH,
               3 * RWKV_WIDTH + 2 * RWKV_LORA_W,
               3 * RWKV_WIDTH + 2 * RWKV_LORA_W + 2 * RWKV_LORA_A)
WIN_HEADS = 8
WIN_KV_HEADS = 2
WINDOW = 128
WIN_BLOCK = 128
NA_HEADS = 8
NA_ROWS = 8
NA_COLS = 16
ODD_SPLITS = (WIN_HEADS * HEAD_DIM, (WIN_HEADS + WIN_KV_HEADS) * HEAD_DIM,
              (WIN_HEADS + 2 * WIN_KV_HEADS) * HEAD_DIM,
              (WIN_HEADS + 2 * WIN_KV_HEADS + NA_HEADS) * HEAD_DIM,
              (WIN_HEADS + 2 * WIN_KV_HEADS + 2 * NA_HEADS) * HEAD_DIM)
ROPE_BASE = 10000.0
PEER_HEADS = 8
PEER_KEYS = 128
PEER_DK = 128
PEER_TOPK = 16
PEER_BLOCK = 128
ALPHA = (2.0 * DEPTH) ** 0.25
LN_EPS = 1e-5
GN_EPS = 64e-5

VMEM_LIMIT_BYTES = 56 * 1024 * 1024
ROW_BLOCK = 256


def _cparams(*sem):
    return pltpu.CompilerParams(dimension_semantics=sem, vmem_limit_bytes=VMEM_LIMIT_BYTES)


def _modulation_kernel(c_ref, w_ref, b_ref, o_ref):
    c = c_ref[...]
    s = c * jax.nn.sigmoid(c)
    o_ref[...] = jnp.dot(s.astype(BF16), w_ref[...].astype(BF16), preferred_element_type=F32) + b_ref[...]


def _modulation(cond, w_mod, b_mod):
    n, d = cond.shape
    nout = w_mod.shape[1]
    tn = 1536
    return pl.pallas_call(
        _modulation_kernel,
        grid=(nout // tn,),
        in_specs=[pl.BlockSpec((n, d), lambda j: (0, 0)),
                  pl.BlockSpec((d, tn), lambda j: (0, j)),
                  pl.BlockSpec((1, tn), lambda j: (0, j))],
        out_specs=pl.BlockSpec((n, tn), lambda j: (0, j)),
        out_shape=jax.ShapeDtypeStruct((n, nout), F32),
        compiler_params=_cparams("arbitrary"),
        name="modulation",
    )(cond, w_mod, b_mod.reshape(1, nout))


def _mod_tables(c, c_ctx, w_mod, b_mod):
    bsz = c.shape[0]
    cond = jnp.concatenate([c, c_ctx[None, :], jnp.zeros((8 - bsz - 1, c.shape[1]), F32)], 0)
    m = _modulation(cond, w_mod, b_mod)
    mx = m[:bsz].reshape(bsz, 1, 6, 1, D_MODEL)
    mc = jnp.broadcast_to(m[bsz].reshape(1, 1, 6, 1, D_MODEL), (bsz, 1, 6, 1, D_MODEL))
    return jnp.concatenate([mc, mx], 1)


def _tab_spec(which):
    return pl.BlockSpec((None, None, None, 1, D_MODEL),
                        lambda b, i: (b, jnp.minimum(i, 1), which, 0, 0))


def _inproj_kernel(x_ref, shift_ref, scale_ref, w_ref, o_ref):
    hm = x_ref[...] * (1.0 + scale_ref[...]) + shift_ref[...]
    o_ref[...] = jnp.dot(hm.astype(BF16), w_ref[...], preferred_element_type=F32)


def _inproj(h, tab, which_shift, which_scale, w_bf16):
    bsz, s, d = h.shape
    n = w_bf16.shape[1]
    return pl.pallas_call(
        _inproj_kernel,
        grid=(bsz, s // ROW_BLOCK),
        in_specs=[pl.BlockSpec((None, ROW_BLOCK, d), lambda b, i: (b, i, 0)),
                  _tab_spec(which_shift), _tab_spec(which_scale),
                  pl.BlockSpec((d, n), lambda b, i: (0, 0))],
        out_specs=pl.BlockSpec((None, ROW_BLOCK, n), lambda b, i: (b, i, 0)),
        out_shape=jax.ShapeDtypeStruct((bsz, s, n), F32),
        compiler_params=_cparams("parallel", "parallel"),
        name="inproj",
    )(h, tab, tab, w_bf16)


def _layer_norm_rows(z, g, b):
    mu = jnp.mean(z, -1, keepdims=True)
    zc = z - mu
    var = jnp.mean(zc * zc, -1, keepdims=True)
    return zc * lax.rsqrt(var + LN_EPS) * g + b


def _outproj_ln_kernel(y_ref, h_ref, gate_ref, w_ref, g_ref, b_ref, o_ref):
    proj = jnp.dot(y_ref[...].astype(BF16), w_ref[...], preferred_element_type=F32)
    z = ALPHA * h_ref[...] + gate_ref[...] * proj
    o_ref[...] = _layer_norm_rows(z, g_ref[...], b_ref[...])


def _outproj_ln(y, h, tab, which_gate, w_bf16, ln_g, ln_b):
    bsz, s, d = h.shape
    k = y.shape[-1]
    return pl.pallas_call(
        _outproj_ln_kernel,
        grid=(bsz, s // ROW_BLOCK),
        in_specs=[pl.BlockSpec((None, ROW_BLOCK, k), lambda b, i: (b, i, 0)),
                  pl.BlockSpec((None, ROW_BLOCK, d), lambda b, i: (b, i, 0)),
                  _tab_spec(which_gate),
                  pl.BlockSpec((k, d), lambda b, i: (0, 0)),
                  pl.BlockSpec((1, d), lambda b, i: (0, 0)),
                  pl.BlockSpec((1, d), lambda b, i: (0, 0))],
        out_specs=pl.BlockSpec((None, ROW_BLOCK, d), lambda b, i: (b, i, 0)),
        out_shape=jax.ShapeDtypeStruct((bsz, s, d), F32),
        compiler_params=_cparams("parallel", "parallel"),
        name="outproj_ln",
    )(y, h, tab, w_bf16, ln_g.reshape(1, d), ln_b.reshape(1, d))


PEER_NEG = -3.0e38
PEER_ETILE = 1024
SQRT_HALF = 0.7071067811865476


def _top_values(s, n):
    vals = []
    for _ in range(n):
        m = jnp.max(s, axis=0, keepdims=True)
        vals.append(m)
        s = jnp.where(s == m, PEER_NEG, s)
    return vals


def _peer_route_kernel(h_ref, shift_ref, scale_ref, wq_ref, keys_ref,
                       pin_ref, thr_ref, e1_ref, s2_ref, e2_ref):
    pin = (h_ref[...] * (1.0 + scale_ref[...]) + shift_ref[...]).astype(BF16)
    pin_ref[...] = pin
    q = jnp.dot(pin, wq_ref[...], preferred_element_type=F32).astype(BF16)
    st = lax.dot_general(keys_ref[...], q, (((1,), (1,)), ((), ())), preferred_element_type=F32)
    for hd in range(PEER_HEADS):
        base = hd * 2 * PEER_KEYS
        s1 = st[base:base + PEER_KEYS]
        s2 = st[base + PEER_KEYS:base + 2 * PEER_KEYS]
        v1 = _top_values(s1, PEER_TOPK)
        v2 = _top_values(s2, PEER_TOPK)
        v2all = jnp.concatenate(v2, axis=0)
        cand = jnp.concatenate([v1[a] + v2all for a in range(PEER_TOPK)], axis=0)
        top = _top_values(cand, PEER_TOPK + 1)
        tau = 0.5 * (top[PEER_TOPK - 1] + top[PEER_TOPK])
        zsum = top[0] * 0.0
        for kk in range(PEER_TOPK):
            zsum = zsum + jnp.exp(top[kk] - top[0])
        thr_ref[hd] = tau - s1
        e1_ref[hd] = jnp.exp(s1 - v1[0])
        s2_ref[hd] = s2
        e2_ref[hd] = jnp.exp(s2 - v2[0]) / zsum


def _peer_dense_kernel(pin_ref, thr_ref, e1_ref, s2_ref, e2_ref, u_ref, vt_ref,
                       h_ref, gate_ref, g_ref, b_ref, o_ref, acc_ref):
    j = pl.program_id(2)

    @pl.when(j == 0)
    def _():
        acc_ref[...] = jnp.zeros_like(acc_ref)

    at = lax.dot_general(u_ref[...], pin_ref[...], (((1,), (1,)), ((), ())), preferred_element_type=F32)
    ws = []
    for ii in range(PEER_ETILE // PEER_KEYS):
        a = at[ii * PEER_KEYS:(ii + 1) * PEER_KEYS]
        act = 0.5 * a * (1.0 + lax.erf(a * SQRT_HALF))
        g = jnp.zeros_like(a)
        for hd in range(PEER_HEADS):
            thr_row = thr_ref[hd, pl.ds(ii, 1), :]
            c_row = e1_ref[hd, pl.ds(ii, 1), :]
            g = g + jnp.where(s2_ref[hd] >= thr_row, e2_ref[hd], 0.0) * c_row
        ws.append((g * act).astype(BF16))
    w = jnp.concatenate(ws, axis=0)
    acc_ref[...] += jnp.dot(vt_ref[...], w, preferred_element_type=F32)

    @pl.when(j == pl.num_programs(2) - 1)
    def _():
        z = ALPHA * h_ref[...] + gate_ref[...] * acc_ref[...].T
        o_ref[...] = _layer_norm_rows(z, g_ref[...], b_ref[...])


def _peer_block_diag_keys(sub_keys):
    hh, two, nk, dh = sub_keys.shape
    eye = jnp.eye(hh * two, dtype=F32)
    kb = sub_keys.reshape(hh * two, nk, dh)
    return jnp.einsum('akd,ab->akbd', kb, eye).reshape(hh * two * nk, hh * two * dh)


def _peer_ln(h, tab, first_block, wq, sub_keys, u_tab, v_tab, ln_g, ln_b):
    bsz, s, d = h.shape
    tm = ROW_BLOCK
    nblk = s // tm - first_block
    hk = PEER_HEADS * 2 * PEER_KEYS
    keys_bd = _peer_block_diag_keys(sub_keys).astype(BF16)

    def tab_spec3(which):
        return pl.BlockSpec((None, None, None, 1, d),
                            lambda b, i, *_: (b, jnp.minimum(i + first_block, 1), which, 0, 0))

    route_shape = jax.ShapeDtypeStruct((bsz, nblk, PEER_HEADS, PEER_KEYS, tm), F32)
    route_spec = pl.BlockSpec((None, None, PEER_HEADS, PEER_KEYS, tm), lambda b, i: (b, i, 0, 0, 0))
    pin, thr, e1, s2, e2 = pl.pallas_call(
        _peer_route_kernel,
        grid=(bsz, nblk),
        in_specs=[pl.BlockSpec((None, tm, d), lambda b, i: (b, i + first_block, 0)),
                  tab_spec3(3), tab_spec3(4),
                  pl.BlockSpec((d, PEER_HEADS * PEER_DK), lambda b, i: (0, 0)),
                  pl.BlockSpec((hk, PEER_HEADS * PEER_DK), lambda b, i: (0, 0))],
        out_specs=[pl.BlockSpec((None, tm, d), lambda b, i: (b, i, 0)),
                   route_spec, route_spec, route_spec, route_spec],
        out_shape=[jax.ShapeDtypeStruct((bsz, nblk * tm, d), BF16),
                   route_shape, route_shape, route_shape, route_shape],
        compiler_params=_cparams("parallel", "parallel"),
        name="peer_route",
    )(h, tab, tab, wq.astype(BF16), keys_bd)

    n_exp = u_tab.shape[0]
    rows_per_step = PEER_ETILE // PEER_KEYS
    full_spec = pl.BlockSpec((None, None, PEER_HEADS, PEER_KEYS, tm), lambda b, i, j: (b, i, 0, 0, 0))
    row_spec = pl.BlockSpec((None, None, PEER_HEADS, rows_per_step, tm), lambda b, i, j: (b, i, 0, j, 0))
    return pl.pallas_call(
        _peer_dense_kernel,
        grid=(bsz, nblk, n_exp // PEER_ETILE),
        in_specs=[pl.BlockSpec((None, tm, d), lambda b, i, j: (b, i, 0)),
                  row_spec, row_spec, full_spec, full_spec,
                  pl.BlockSpec((PEER_ETILE, d), lambda b, i, j: (j, 0)),
                  pl.BlockSpec((d, PEER_ETILE), lambda b, i, j: (0, j)),
                  pl.BlockSpec((None, tm, d), lambda b, i, j: (b, i + first_block, 0)),
                  tab_spec3(5),
                  pl.BlockSpec((1, d), lambda b, i, j: (0, 0)),
                  pl.BlockSpec((1, d), lambda b, i, j: (0, 0))],
        out_specs=pl.BlockSpec((None, tm, d), lambda b, i, j: (b, i, 0)),
        out_shape=jax.ShapeDtypeStruct((bsz, nblk * tm, d), F32),
        scratch_shapes=[pltpu.VMEM((d, tm), F32)],
        compiler_params=_cparams("parallel", "parallel", "arbitrary"),
        name="peer_dense",
    )(pin, thr, e1, s2, e2, u_tab.astype(BF16), v_tab.T.astype(BF16), h, tab,
      ln_g.reshape(1, d), ln_b.reshape(1, d))


S5_CHAINS = 8
S5_TCHUNK = 128
S5_NSTATE = S5_GROUPS * S5_STATE


def _s5_scan_kernel(u_ref, wb_ref, wc_ref, a_ref, y_ref, bre_ref, bim_ref, h_ref):
    @pl.when(pl.program_id(0) == 0)
    def _():
        h_ref[...] = jnp.zeros_like(h_ref)

    rows = u_ref.shape[0]
    u = u_ref[...].astype(BF16)
    is_bwd = (lax.broadcasted_iota(jnp.int32, (rows, 1), 0) % S5_CHAINS) >= (S5_CHAINS // 2)

    def proj(w_fwd, w_bwd, x):
        return jnp.where(is_bwd, jnp.dot(x, w_bwd, preferred_element_type=F32),
                         jnp.dot(x, w_fwd, preferred_element_type=F32))

    bre_ref[...] = proj(wb_ref[0], wb_ref[2], u)
    bim_ref[...] = proj(wb_ref[1], wb_ref[3], u)
    a_re = a_ref[0]
    a_im = a_ref[1]

    def step(t, carry):
        h_re, h_im = carry
        off = pl.multiple_of(t * S5_CHAINS, S5_CHAINS)
        n_re = a_re * h_re - a_im * h_im + bre_ref[pl.ds(off, S5_CHAINS), :]
        n_im = a_re * h_im + a_im * h_re + bim_ref[pl.ds(off, S5_CHAINS), :]
        bre_ref[pl.ds(off, S5_CHAINS), :] = n_re
        bim_ref[pl.ds(off, S5_CHAINS), :] = n_im
        return n_re, n_im

    h_re, h_im = lax.fori_loop(0, rows // S5_CHAINS, step, (h_ref[0], h_ref[1]), unroll=2)
    h_ref[0] = h_re
    h_ref[1] = h_im
    hre = bre_ref[...].astype(BF16)
    him = bim_ref[...].astype(BF16)
    y_ref[...] = proj(wc_ref[0], wc_ref[2], hre) - proj(wc_ref[1], wc_ref[3], him)


def _s5_glu_kernel(u_ref, y0_ref, y1_ref, d_ref, w_ref, b_ref, o_ref):
    y = u_ref[...] * d_ref[...] + y0_ref[...] + y1_ref[...]
    y = 0.5 * y * (1.0 + lax.erf(y * SQRT_HALF))
    gate = jnp.dot(y.astype(BF16), w_ref[...], preferred_element_type=F32) + b_ref[...]
    o_ref[...] = y * jax.nn.sigmoid(gate)


def _dir_view(z, n_ctx, d):
    if d == 0:
        return z
    return jnp.concatenate([jnp.flip(z[:, :n_ctx], 1), jnp.flip(z[:, n_ctx:], 1)], 1)


def _s5_params(lam_re, lam_im, log_dt, b_re, b_im, c_re, c_im):
    eye = jnp.eye(S5_GROUPS, dtype=F32)
    wb, wc, a_rows = [], [], [[], []]
    for d in range(2):
        lr, li = lam_re[d], lam_im[d]
        dt = jnp.exp(log_dt[d])[:, None]
        mag = jnp.exp(lr * dt)
        ab_re, ab_im = mag * jnp.cos(li * dt), mag * jnp.sin(li * dt)
        den = lr * lr + li * li
        nr = ab_re - 1.0
        coef_re = (nr * lr + ab_im * li) / den
        coef_im = (ab_im * lr - nr * li) / den
        bb_re = coef_re[..., None] * b_re[d] - coef_im[..., None] * b_im[d]
        bb_im = coef_re[..., None] * b_im[d] + coef_im[..., None] * b_re[d]
        for bb in (bb_re, bb_im):
            wb.append(jnp.einsum('gpi,gh->gihp', bb, eye).reshape(S5_WIDTH, S5_NSTATE))
        for cc in (c_re[d], c_im[d]):
            wc.append(jnp.einsum('gip,gh->gphi', cc, eye).reshape(S5_NSTATE, S5_WIDTH))
        for k, ab in enumerate((ab_re, ab_im)):
            a_rows[k].append(jnp.broadcast_to(ab.reshape(1, S5_NSTATE), (S5_CHAINS // 2, S5_NSTATE)))
    a_tiles = jnp.stack([jnp.concatenate(a_rows[0], 0), jnp.concatenate(a_rows[1], 0)])
    return jnp.stack(wb).astype(BF16), jnp.stack(wc).astype(BF16), a_tiles


def _s5_mixer(u, n_ctx, lam_re, lam_im, log_dt, b_re, b_im, c_re, c_im, d_skip, w_glu, b_glu):
    bsz, s, w = u.shape
    assert 2 * bsz == S5_CHAINS
    wb, wc, a_tiles = _s5_params(lam_re, lam_im, log_dt, b_re, b_im, c_re, c_im)
    chain = jnp.concatenate([jnp.transpose(u, (1, 0, 2)), jnp.transpose(_dir_view(u, n_ctx, 1), (1, 0, 2))], 1)
    rows = S5_TCHUNK * S5_CHAINS
    y = pl.pallas_call(
        _s5_scan_kernel,
        grid=(s // S5_TCHUNK,),
        in_specs=[pl.BlockSpec((rows, w), lambda i: (i, 0)),
                  pl.BlockSpec((4, w, S5_NSTATE), lambda i: (0, 0, 0)),
                  pl.BlockSpec((4, S5_NSTATE, w), lambda i: (0, 0, 0)),
                  pl.BlockSpec((2, S5_CHAINS, S5_NSTATE), lambda i: (0, 0, 0))],
        out_specs=pl.BlockSpec((rows, w), lambda i: (i, 0)),
        out_shape=jax.ShapeDtypeStruct((s * S5_CHAINS, w), F32),
        scratch_shapes=[pltpu.VMEM((rows, S5_NSTATE), F32), pltpu.VMEM((rows, S5_NSTATE), F32),
                        pltpu.VMEM((2, S5_CHAINS, S5_NSTATE), F32)],
        compiler_params=_cparams("arbitrary"),
        name="s5_scan",
    )(chain.reshape(s * S5_CHAINS, w), wb, wc, a_tiles)
    y = y.reshape(s, S5_CHAINS, w)
    y_fwd = jnp.transpose(y[:, :bsz], (1, 0, 2))
    y_bwd = _dir_view(jnp.transpose(y[:, bsz:], (1, 0, 2)), n_ctx, 1)
    tok = pl.BlockSpec((None, ROW_BLOCK, w), lambda b, i: (b, i, 0))
    vec = pl.BlockSpec((1, w), lambda b, i: (0, 0))
    return pl.pallas_call(
        _s5_glu_kernel,
        grid=(bsz, s // ROW_BLOCK),
        in_specs=[tok, tok, tok, vec, pl.BlockSpec((w, w), lambda b, i: (0, 0)), vec],
        out_specs=tok,
        out_shape=jax.ShapeDtypeStruct((bsz, s, w), F32),
        compiler_params=_cparams("parallel", "parallel"),
        name="s5_glu",
    )(u, y_fwd, y_bwd, d_skip.reshape(1, w), w_glu.astype(BF16), b_glu.reshape(1, w))


ATT_SCALE = HEAD_DIM ** -0.5
ATT_NEG = -1e30
ROPE_QUARTER = HEAD_DIM // 4
QC_W = WIN_HEADS * HEAD_DIM
KC_W = WIN_KV_HEADS * HEAD_DIM
QD_W = NA_HEADS * HEAD_DIM


def _rope_tables(n_ctx, t):
    pos = np.arange(t)
    rowcol = np.stack([pos // GRID_W, pos % GRID_W], 1).astype(np.float32)
    lane = np.arange(QC_W)
    j = lane % HEAD_DIM
    freqs = (ROPE_BASE ** (-np.arange(ROPE_QUARTER, dtype=np.float32) / ROPE_QUARTER)).astype(np.float32)
    ang = rowcol[:, j // (HEAD_DIM // 2)] * freqs[j % ROPE_QUARTER][None, :]
    ang = jnp.asarray(ang, F32)
    sign = np.where((j % (HEAD_DIM // 2)) < ROPE_QUARTER, -1.0, 1.0).astype(np.float32)
    cos = jnp.concatenate([jnp.ones((n_ctx, QC_W), F32), jnp.cos(ang)], 0)
    sin = jnp.concatenate([jnp.zeros((n_ctx, QC_W), F32), jnp.sin(ang) * sign[None, :]], 0)
    return cos, sin


def _rope(x, cos, sin):
    n = x.shape[-1]
    lane = lax.broadcasted_iota(jnp.int32, (1, n), 1)
    first = (lane % (HEAD_DIM // 2)) < ROPE_QUARTER
    partner = jnp.where(first, pltpu.roll(x, n - ROPE_QUARTER, 1), pltpu.roll(x, ROPE_QUARTER, 1))
    return x * cos + partner * sin


def _attn_prep_kernel(z_ref, cos_ref, sin_ref, qc_ref, kc_ref, vc_ref, qd_ref, kd_ref, vd_ref):
    c0, c1, c2, c3, c4 = ODD_SPLITS
    cos = cos_ref[...]
    sin = sin_ref[...]
    qc_ref[...] = (_rope(z_ref[:, 0:c0], cos, sin) * ATT_SCALE).astype(BF16)
    kc_ref[...] = _rope(z_ref[:, c0:c1], cos[:, :KC_W], sin[:, :KC_W]).astype(BF16)
    vc_ref[...] = z_ref[:, c1:c2].astype(BF16)
    qd_ref[...] = (z_ref[:, c2:c3] * ATT_SCALE).astype(BF16)
    kd_ref[...] = z_ref[:, c3:c4].astype(BF16)
    vd_ref[...] = z_ref[:, c4:].astype(BF16)


def _attn_prep(z, n_ctx):
    bsz, s, n = z.shape
    cos, sin = _rope_tables(n_ctx, s - n_ctx)
    widths = (QC_W, KC_W, KC_W, QD_W, QD_W, QD_W)
    return pl.pallas_call(
        _attn_prep_kernel,
        grid=(bsz, s // ROW_BLOCK),
        in_specs=[pl.BlockSpec((None, ROW_BLOCK, n), lambda b, i: (b, i, 0)),
                  pl.BlockSpec((ROW_BLOCK, QC_W), lambda b, i: (i, 0)),
                  pl.BlockSpec((ROW_BLOCK, QC_W), lambda b, i: (i, 0))],
        out_specs=[pl.BlockSpec((None, ROW_BLOCK, w), lambda b, i: (b, i, 0)) for w in widths],
        out_shape=[jax.ShapeDtypeStruct((bsz, s, w), BF16) for w in widths],
        compiler_params=_cparams("parallel", "parallel"),
        name="attn_prep",
    )(z, cos, sin)


def _softmax_pv(s, v, extra_logit=None):
    m = jnp.max(s, axis=1, keepdims=True)
    if extra_logit is not None:
        m = jnp.maximum(m, extra_logit)
    p = jnp.exp(s - m)
    den = jnp.sum(p, axis=1, keepdims=True)
    if extra_logit is not None:
        den = den + jnp.exp(extra_logit - m)
    return jnp.dot(p.astype(BF16), v, preferred_element_type=F32) / den


def _window_kernel(q_ref, kp_ref, kc_ref, kn_ref, kx_ref, vp_ref, vc_ref, vn_ref, vx_ref, sink_ref, o_ref, *, t_len):
    i = pl.program_id(1)
    wb = WIN_BLOCK
    qi = lax.broadcasted_iota(jnp.int32, (wb, 1), 0)
    kj = lax.broadcasted_iota(jnp.int32, (1, 3 * wb), 1) - wb
    kabs = i * wb + kj
    valid = (jnp.abs(kj - qi) <= WINDOW) & (kabs >= 0) & (kabs < t_len)
    n_ctx = kx_ref.shape[0]
    bias = jnp.concatenate([jnp.where(valid, 0.0, ATT_NEG), jnp.zeros((wb, n_ctx), F32)], axis=1)
    k_all = jnp.concatenate([kp_ref[...], kc_ref[...], kn_ref[...], kx_ref[...]], axis=0)
    v_all = jnp.concatenate([vp_ref[...], vc_ref[...], vn_ref[...], vx_ref[...]], axis=0)
    group = WIN_HEADS // WIN_KV_HEADS
    for hq in range(WIN_HEADS):
        g = hq // group
        kg = k_all[:, g * HEAD_DIM:(g + 1) * HEAD_DIM]
        vg = v_all[:, g * HEAD_DIM:(g + 1) * HEAD_DIM]
        q = q_ref[:, hq * HEAD_DIM:(hq + 1) * HEAD_DIM]
        s = lax.dot_general(q, kg, (((1,), (1,)), ((), ())), preferred_element_type=F32) + bias
        o_ref[:, hq * HEAD_DIM:(hq + 1) * HEAD_DIM] = _softmax_pv(s, vg, sink_ref[hq:hq + 1, 0:1])


def _window_gqa(qc, kc, vc, sink, n_ctx):
    bsz, s, _ = qc.shape
    t_len = s - n_ctx
    nb = t_len // WIN_BLOCK
    off = n_ctx // WIN_BLOCK

    def kv_spec(delta):
        return pl.BlockSpec((None, WIN_BLOCK, KC_W), lambda b, i: (b, jnp.clip(i + delta, 0, nb - 1) + off, 0))

    ctx_spec = pl.BlockSpec((None, n_ctx, KC_W), lambda b, i: (b, 0, 0))
    sink_tile = jnp.broadcast_to(sink.reshape(WIN_HEADS, 1), (WIN_HEADS, 128))
    return pl.pallas_call(
        functools.partial(_window_kernel, t_len=t_len),
        grid=(bsz, nb),
        in_specs=[pl.BlockSpec((None, WIN_BLOCK, QC_W), lambda b, i: (b, i + off, 0)),
                  kv_spec(-1), kv_spec(0), kv_spec(1), ctx_spec,
                  kv_spec(-1), kv_spec(0), kv_spec(1), ctx_spec,
                  pl.BlockSpec((WIN_HEADS, 128), lambda b, i: (0, 0))],
        out_specs=pl.BlockSpec((None, WIN_BLOCK, QC_W), lambda b, i: (b, i, 0)),
        out_shape=jax.ShapeDtypeStruct((bsz, t_len, QC_W), F32),
        compiler_params=_cparams("parallel", "parallel"),
        name="window_gqa",
    )(qc, kc, kc, kc, kc, vc, vc, vc, vc, sink_tile)


NA_QROWS = 2
NA_KROWS = NA_QROWS + NA_ROWS - 1


def _na_block_plan(rows):
    kh = min(NA_ROWS, rows)
    kw = min(NA_COLS, GRID_W)
    c = np.arange(GRID_W)
    cs = np.clip(c - kw // 2, 0, GRID_W - kw)
    kc = np.arange(GRID_W)
    col_valid = (kc[None, :] >= cs[:, None]) & (kc[None, :] < cs[:, None] + kw)
    dc = np.clip(kc[None, :] - c[:, None] + (NA_COLS - 1), 0, 2 * NA_COLS - 2)
    patterns, types, kbs = {}, [], []
    for r0 in range(0, rows, NA_QROWS):
        kb = int(np.clip(r0 - kh // 2, 0, rows - NA_KROWS))
        r = r0 + np.arange(NA_QROWS)
        rs = np.clip(r - kh // 2, 0, rows - kh)
        kr = kb + np.arange(NA_KROWS)
        row_valid = (kr[None, :] >= rs[:, None]) & (kr[None, :] < rs[:, None] + kh)
        dr = np.clip(kr[None, :] - r[:, None] + (NA_ROWS - 1), 0, 2 * NA_ROWS - 2)
        key = (row_valid.tobytes(), dr.tobytes())
        if key not in patterns:
            valid = row_valid[:, None, :, None] & col_valid[None, :, None, :]
            shape = valid.shape
            patterns[key] = (len(patterns), valid.reshape(NA_QROWS * GRID_W, NA_KROWS * GRID_W),
                             np.broadcast_to(dr[:, None, :, None], shape).reshape(valid.shape[0] * shape[1], -1),
                             np.broadcast_to(dc[None, :, None, :], shape).reshape(valid.shape[0] * shape[1], -1))
        types.append(patterns[key][0])
        kbs.append(kb)
    ordered = sorted(patterns.values(), key=lambda p: p[0])
    valid = np.stack([p[1] for p in ordered])
    dr = np.stack([p[2] for p in ordered])
    dc = np.stack([p[3] for p in ordered])
    return np.asarray(types, np.int32), np.asarray(kbs, np.int32), valid, dr, dc


def _na_kernel(types_ref, kbs_ref, q_ref, *refs):
    del types_ref, kbs_ref
    k_refs = refs[:NA_KROWS]
    v_refs = refs[NA_KROWS:2 * NA_KROWS]
    kx_ref, vx_ref, bias_ref, o_ref = refs[2 * NA_KROWS:]
    k_all = jnp.concatenate([r[...] for r in k_refs] + [kx_ref[...]], axis=0)
    v_all = jnp.concatenate([r[...] for r in v_refs] + [vx_ref[...]], axis=0)
    n_q = q_ref.shape[0]
    n_ctx = kx_ref.shape[0]
    zeros = jnp.zeros((n_q, n_ctx), F32)
    for hd in range(NA_HEADS):
        sl = slice(hd * HEAD_DIM, (hd + 1) * HEAD_DIM)
        s = lax.dot_general(q_ref[:, sl], k_all[:, sl], (((1,), (1,)), ((), ())), preferred_element_type=F32)
        s = s + jnp.concatenate([bias_ref[hd], zeros], axis=1)
        o_ref[:, sl] = _softmax_pv(s, v_all[:, sl])


def _neighbourhood_attention(qd, kd, vd, rpb, n_ctx):
    bsz, s, _ = qd.shape
    t_len = s - n_ctx
    rows = t_len // GRID_W
    types, kbs, valid, dr, dc = _na_block_plan(rows)
    bias = jnp.where(valid[None], rpb[:, dr, dc], ATT_NEG)
    bias = jnp.transpose(bias, (1, 0, 2, 3))
    n_q = NA_QROWS * GRID_W
    off_q = n_ctx // n_q
    off_k = n_ctx // GRID_W

    def kv_spec(m):
        return pl.BlockSpec((None, GRID_W, QD_W), lambda b, i, types, kbs: (b, kbs[i] + m + off_k, 0))

    ctx_spec = pl.BlockSpec((None, n_ctx, QD_W), lambda b, i, types, kbs: (b, 0, 0))
    grid_spec = pltpu.PrefetchScalarGridSpec(
        num_scalar_prefetch=2,
        grid=(bsz, rows // NA_QROWS),
        in_specs=([pl.BlockSpec((None, n_q, QD_W), lambda b, i, types, kbs: (b, i + off_q, 0))]
                  + [kv_spec(m) for m in range(NA_KROWS)] + [kv_spec(m) for m in range(NA_KROWS)]
                  + [ctx_spec, ctx_spec,
                     pl.BlockSpec((None, NA_HEADS, n_q, NA_KROWS * GRID_W),
                                  lambda b, i, types, kbs: (types[i], 0, 0, 0))]),
        out_specs=pl.BlockSpec((None, n_q, QD_W), lambda b, i, types, kbs: (b, i, 0)),
    )
    return pl.pallas_call(
        _na_kernel,
        grid_spec=grid_spec,
        out_shape=jax.ShapeDtypeStruct((bsz, t_len, QD_W), F32),
        compiler_params=_cparams("parallel", "arbitrary"),
        name="neighbourhood_attention",
    )(jnp.asarray(types), jnp.asarray(kbs), qd, *([kd] * NA_KROWS), *([vd] * NA_KROWS), kd, vd, bias)


def _centred_shift(z, mu):
    prev = jnp.pad(z[:, :-1], ((0, 0), (1, 0), (0, 0)))
    nxt = jnp.pad(z[:, 1:], ((0, 0), (0, 1), (0, 0)))
    return z + mu[0] * (prev - z) + mu[1] * (nxt - z)


def _rwkv_scan(r, w, k, v, a, b):
    Bt, S, H, N = r.shape

    def step(state, inp):
        r_t, w_t, k_t, v_t, a_t, b_t = inp
        sa = jnp.einsum('bhvk,bhk->bhv', state, a_t)
        state = (state * w_t[:, :, None, :] + sa[..., None] * b_t[:, :, None, :]
                 + v_t[..., None] * k_t[:, :, None, :])
        return state, jnp.einsum('bhvk,bhk->bhv', state, r_t)

    xs = tuple(jnp.moveaxis(t, 1, 0) for t in (r, w, k, v, a, b))
    _, ys = lax.scan(step, jnp.zeros((Bt, H, N, N), F32), xs)
    return jnp.moveaxis(ys, 0, 1)


def _rwkv_mixer(z_all, L, shift_mu, w0, w2, a0, a2, g2, k_k, k_a, r_k, gn_w, gn_b):
    Bt, S, _ = z_all.shape
    z = jnp.concatenate([_centred_shift(z_all[:, :L], shift_mu), _centred_shift(z_all[:, L:], shift_mu)], 1)
    r, k, v, w_lo, a_lo, g_lo = jnp.split(z, RWKV_SPLITS, axis=-1)
    w_lo = w_lo.reshape(Bt, S, 2, RWKV_LORA_W)
    a_lo = a_lo.reshape(Bt, S, 2, RWKV_LORA_A)
    g = jax.nn.sigmoid(g_lo) @ g2

    def heads(t):
        return t.reshape(Bt, S, RWKV_HEADS, HEAD_DIM)

    kk = heads(k * k_k)
    kk = kk * lax.rsqrt(jnp.sum(kk * kk, -1, keepdims=True) + 1e-12)
    rh, vh = heads(r), heads(v)
    y = 0.0
    bonus = 0.0
    for d in range(2):
        w_log = -jax.nn.softplus(-(w0[d] + jnp.tanh(w_lo[:, :, d]) @ w2[d])) - 0.5
        decay = jnp.exp(-jnp.exp(w_log))
        a = jax.nn.sigmoid(a0[d] + a_lo[:, :, d] @ a2[d])
        kd = heads(k * (1.0 + (a - 1.0) * k_a))
        ah = heads(a)
        ins = [_dir_view(t, L, d) for t in (rh, heads(decay), kd, vh, -kk, kk * ah)]
        y = y + _dir_view(_rwkv_scan(*ins), L, d)
        bonus = bonus + jnp.sum(rh * kd * r_k, -1, keepdims=True) * vh
    mu = jnp.mean(y, -1, keepdims=True)
    var = jnp.mean(jnp.square(y - mu), -1, keepdims=True)
    y = ((y - mu) * lax.rsqrt(var + GN_EPS)).reshape(Bt, S, RWKV_WIDTH) * gn_w + gn_b
    return (y + bonus.reshape(Bt, S, RWKV_WIDTH)) * g


def kernel(x, c, ctx, c_ctx,
           l0_w_mod, l0_b_mod, l0_w_in, l0_s5_lam_re, l0_s5_lam_im, l0_s5_log_dt, l0_s5_b_re, l0_s5_b_im,
           l0_s5_c_re, l0_s5_c_im, l0_s5_d, l0_w_glu, l0_b_glu, l0_rwkv_shift, l0_rwkv_w0, l0_rwkv_w2,
           l0_rwkv_a0, l0_rwkv_a2, l0_rwkv_g2, l0_rwkv_k_k, l0_rwkv_k_a, l0_rwkv_r_k, l0_rwkv_gn_w,
           l0_rwkv_gn_b, l0_w_out, l0_ln1_g, l0_ln1_b, l0_peer_wq, l0_peer_keys, l0_peer_u, l0_peer_v,
           l0_ln2_g, l0_ln2_b,
           l1_w_mod, l1_b_mod, l1_w_in, l1_sink, l1_rpb, l1_w_out, l1_ln1_g, l1_ln1_b, l1_peer_wq,
           l1_peer_keys, l1_peer_u, l1_peer_v, l1_ln2_g, l1_ln2_b):
    L = ctx.shape[1]
    h = jnp.concatenate([ctx, x], 1)

    tab = _mod_tables(c, c_ctx, l0_w_mod, l0_b_mod)
    z = _inproj(h, tab, 0, 1, l0_w_in.astype(BF16))
    ya = _s5_mixer(z[..., :S5_WIDTH], L, l0_s5_lam_re, l0_s5_lam_im, l0_s5_log_dt, l0_s5_b_re, l0_s5_b_im,
                   l0_s5_c_re, l0_s5_c_im, l0_s5_d, l0_w_glu, l0_b_glu)
    yb = _rwkv_mixer(z[..., S5_WIDTH:], L, l0_rwkv_shift, l0_rwkv_w0, l0_rwkv_w2, l0_rwkv_a0, l0_rwkv_a2,
                     l0_rwkv_g2, l0_rwkv_k_k, l0_rwkv_k_a, l0_rwkv_r_k, l0_rwkv_gn_w, l0_rwkv_gn_b)
    h = _outproj_ln(jnp.concatenate([ya, yb], -1), h, tab, 2, l0_w_out.astype(BF16), l0_ln1_g, l0_ln1_b)
    h = _peer_ln(h, tab, 0, l0_peer_wq, l0_peer_keys, l0_peer_u, l0_peer_v, l0_ln2_g, l0_ln2_b)

    tab = _mod_tables(c, c_ctx, l1_w_mod, l1_b_mod)
    z = _inproj(h, tab, 0, 1, l1_w_in.astype(BF16))
    qc, kc, vc, qd, kd, vd = _attn_prep(z, L)
    y = jnp.concatenate([_window_gqa(qc, kc, vc, l1_sink, L), _neighbourhood_attention(qd, kd, vd, l1_rpb, L)], -1)
    y = jnp.concatenate([jnp.zeros((y.shape[0], L, y.shape[2]), F32), y], 1)
    h = _outproj_ln(y, h, tab, 2, l1_w_out.astype(BF16), l1_ln1_g, l1_ln1_b)
    return _peer_ln(h, tab, L // ROW_BLOCK, l1_peer_wq, l1_peer_keys, l1_peer_u, l1_peer_v, l1_ln2_g, l1_ln2_b)
```

```python
import functools
import math

import jax
import jax.numpy as jnp
import numpy as np
from jax import lax
from jax.experimental import pallas as pl
from jax.experimental.pallas import tpu as pltpu

F32 = jnp.float32
BF16 = jnp.bfloat16

D_MODEL = 1024
DEPTH = 2
GRID_W = 64
CTX_LEN = 256
HEAD_DIM = 64
S5_GROUP_CH = 16
S5_GROUPS = 16
S5_WIDTH = S5_GROUPS * S5_GROUP_CH
S5_STATE = 64
RWKV_HEADS = 8
RWKV_WIDTH = RWKV_HEADS * HEAD_DIM
RWKV_LORA_W = 64
RWKV_LORA_A = 64
RWKV_LORA_G = 128
RWKV_IN = 3 * RWKV_WIDTH + 2 * RWKV_LORA_W + 2 * RWKV_LORA_A + RWKV_LORA_G
RWKV_SPLITS = (RWKV_WIDTH, 2 * RWKV_WIDTH, 3 * RWKV_WIDTH,
               3 * RWKV_WIDTH + 2 * RWKV_LORA_W,
               3 * RWKV_WIDTH + 2 * RWKV_LORA_W + 2 * RWKV_LORA_A)
WIN_HEADS = 8
WIN_KV_HEADS = 2
WINDOW = 128
WIN_BLOCK = 128
NA_HEADS = 8
NA_ROWS = 8
NA_COLS = 16
ODD_SPLITS = (WIN_HEADS * HEAD_DIM, (WIN_HEADS + WIN_KV_HEADS) * HEAD_DIM,
              (WIN_HEADS + 2 * WIN_KV_HEADS) * HEAD_DIM,
              (WIN_HEADS + 2 * WIN_KV_HEADS + NA_HEADS) * HEAD_DIM,
              (WIN_HEADS + 2 * WIN_KV_HEADS + 2 * NA_HEADS) * HEAD_DIM)
ROPE_BASE = 10000.0
PEER_HEADS = 8
PEER_KEYS = 128
PEER_DK = 128
PEER_TOPK = 16
PEER_BLOCK = 128
ALPHA = (2.0 * DEPTH) ** 0.25
LN_EPS = 1e-5
GN_EPS = 64e-5

VMEM_LIMIT_BYTES = 56 * 1024 * 1024
ROW_BLOCK = 256


def _cparams(*sem):
    return pltpu.CompilerParams(dimension_semantics=sem, vmem_limit_bytes=VMEM_LIMIT_BYTES)


def _modulation_kernel(c_ref, w_ref, b_ref, o_ref):
    c = c_ref[...]
    s = c * jax.nn.sigmoid(c)
    o_ref[...] = jnp.dot(s.astype(BF16), w_ref[...].astype(BF16), preferred_element_type=F32) + b_ref[...]


def _modulation(cond, w_mod, b_mod):
    n, d = cond.shape
    nout = w_mod.shape[1]
    tn = 1536
    return pl.pallas_call(
        _modulation_kernel,
        grid=(nout // tn,),
        in_specs=[pl.BlockSpec((n, d), lambda j: (0, 0)),
                  pl.BlockSpec((d, tn), lambda j: (0, j)),
                  pl.BlockSpec((1, tn), lambda j: (0, j))],
        out_specs=pl.BlockSpec((n, tn), lambda j: (0, j)),
        out_shape=jax.ShapeDtypeStruct((n, nout), F32),
        compiler_params=_cparams("arbitrary"),
        name="modulation",
    )(cond, w_mod, b_mod.reshape(1, nout))


def _mod_tables(c, c_ctx, w_mod, b_mod):
    bsz = c.shape[0]
    cond = jnp.concatenate([c, c_ctx[None, :], jnp.zeros((8 - bsz - 1, c.shape[1]), F32)], 0)
    m = _modulation(cond, w_mod, b_mod)
    mx = m[:bsz].reshape(bsz, 1, 6, 1, D_MODEL)
    mc = jnp.broadcast_to(m[bsz].reshape(1, 1, 6, 1, D_MODEL), (bsz, 1, 6, 1, D_MODEL))
    return jnp.concatenate([mc, mx], 1)


def _tab_spec(which):
    return pl.BlockSpec((None, None, None, 1, D_MODEL),
                        lambda b, i: (b, jnp.minimum(i, 1), which, 0, 0))


def _inproj_kernel(x_ref, shift_ref, scale_ref, w_ref, o_ref):
    hm = x_ref[...] * (1.0 + scale_ref[...]) + shift_ref[...]
    o_ref[...] = jnp.dot(hm.astype(BF16), w_ref[...], preferred_element_type=F32)


def _inproj(h, tab, which_shift, which_scale, w_bf16):
    bsz, s, d = h.shape
    n = w_bf16.shape[1]
    return pl.pallas_call(
        _inproj_kernel,
        grid=(bsz, s // ROW_BLOCK),
        in_specs=[pl.BlockSpec((None, ROW_BLOCK, d), lambda b, i: (b, i, 0)),
                  _tab_spec(which_shift), _tab_spec(which_scale),
                  pl.BlockSpec((d, n), lambda b, i: (0, 0))],
        out_specs=pl.BlockSpec((None, ROW_BLOCK, n), lambda b, i: (b, i, 0)),
        out_shape=jax.ShapeDtypeStruct((bsz, s, n), F32),
        compiler_params=_cparams("parallel", "parallel"),
        name="inproj",
    )(h, tab, tab, w_bf16)


def _layer_norm_rows(z, g, b):
    mu = jnp.mean(z, -1, keepdims=True)
    zc = z - mu
    var = jnp.mean(zc * zc, -1, keepdims=True)
    return zc * lax.rsqrt(var + LN_EPS) * g + b


def _outproj_ln_kernel(y_ref, h_ref, gate_ref, w_ref, g_ref, b_ref, o_ref):
    proj = jnp.dot(y_ref[...].astype(BF16), w_ref[...], preferred_element_type=F32)
    z = ALPHA * h_ref[...] + gate_ref[...] * proj
    o_ref[...] = _layer_norm_rows(z, g_ref[...], b_ref[...])


def _outproj_ln(y, h, tab, which_gate, w_bf16, ln_g, ln_b):
    bsz, s, d = h.shape
    k = y.shape[-1]
    return pl.pallas_call(
        _outproj_ln_kernel,
        grid=(bsz, s // ROW_BLOCK),
        in_specs=[pl.BlockSpec((None, ROW_BLOCK, k), lambda b, i: (b, i, 0)),
                  pl.BlockSpec((None, ROW_BLOCK, d), lambda b, i: (b, i, 0)),
                  _tab_spec(which_gate),
                  pl.BlockSpec((k, d), lambda b, i: (0, 0)),
                  pl.BlockSpec((1, d), lambda b, i: (0, 0)),
                  pl.BlockSpec((1, d), lambda b, i: (0, 0))],
        out_specs=pl.BlockSpec((None, ROW_BLOCK, d), lambda b, i: (b, i, 0)),
        out_shape=jax.ShapeDtypeStruct((bsz, s, d), F32),
        compiler_params=_cparams("parallel", "parallel"),
        name="outproj_ln",
    )(y, h, tab, w_bf16, ln_g.reshape(1, d), ln_b.reshape(1, d))


PEER_NEG = -3.0e38
PEER_ETILE = 1024
SQRT_HALF = 0.7071067811865476


def _top_values(s, n):
    vals = []
    for _ in range(n):
        m = jnp.max(s, axis=0, keepdims=True)
        vals.append(m)
        s = jnp.where(s == m, PEER_NEG, s)
    return vals


def _peer_route_kernel(h_ref, shift_ref, scale_ref, wq_ref, keys_ref,
                       pin_ref, thr_ref, e1_ref, s2_ref, e2_ref):
    pin = (h_ref[...] * (1.0 + scale_ref[...]) + shift_ref[...]).astype(BF16)
    pin_ref[...] = pin
    q = jnp.dot(pin, wq_ref[...], preferred_element_type=F32).astype(BF16)
    st = lax.dot_general(keys_ref[...], q, (((1,), (1,)), ((), ())), preferred_element_type=F32)
    for hd in range(PEER_HEADS):
        base = hd * 2 * PEER_KEYS
        s1 = st[base:base + PEER_KEYS]
        s2 = st[base + PEER_KEYS:base + 2 * PEER_KEYS]
        v1 = _top_values(s1, PEER_TOPK)
        v2 = _top_values(s2, PEER_TOPK)
        v2all = jnp.concatenate(v2, axis=0)
        cand = jnp.concatenate([v1[a] + v2all for a in range(PEER_TOPK)], axis=0)
        top = _top_values(cand, PEER_TOPK + 1)
        tau = 0.5 * (top[PEER_TOPK - 1] + top[PEER_TOPK])
        zsum = top[0] * 0.0
        for kk in range(PEER_TOPK):
            zsum = zsum + jnp.exp(top[kk] - top[0])
        thr_ref[hd] = tau - s1
        e1_ref[hd] = jnp.exp(s1 - v1[0])
        s2_ref[hd] = s2
        e2_ref[hd] = jnp.exp(s2 - v2[0]) / zsum


def _peer_dense_kernel(pin_ref, thr_ref, e1_ref, s2_ref, e2_ref, u_ref, vt_ref,
                       h_ref, gate_ref, g_ref, b_ref, o_ref, acc_ref):
    j = pl.program_id(2)

    @pl.when(j == 0)
    def _():
        acc_ref[...] = jnp.zeros_like(acc_ref)

    at = lax.dot_general(u_ref[...], pin_ref[...], (((1,), (1,)), ((), ())), preferred_element_type=F32)
    ws = []
    for ii in range(PEER_ETILE // PEER_KEYS):
        a = at[ii * PEER_KEYS:(ii + 1) * PEER_KEYS]
        act = 0.5 * a * (1.0 + lax.erf(a * SQRT_HALF))
        g = jnp.zeros_like(a)
        for hd in range(PEER_HEADS):
            thr_row = thr_ref[hd, pl.ds(ii, 1), :]
            c_row = e1_ref[hd, pl.ds(ii, 1), :]
            g = g + jnp.where(s2_ref[hd] >= thr_row, e2_ref[hd], 0.0) * c_row
        ws.append((g * act).astype(BF16))
    w = jnp.concatenate(ws, axis=0)
    acc_ref[...] += jnp.dot(vt_ref[...], w, preferred_element_type=F32)

    @pl.when(j == pl.num_programs(2) - 1)
    def _():
        z = ALPHA * h_ref[...] + gate_ref[...] * acc_ref[...].T
        o_ref[...] = _layer_norm_rows(z, g_ref[...], b_ref[...])


def _peer_block_diag_keys(sub_keys):
    hh, two, nk, dh = sub_keys.shape
    eye = jnp.eye(hh * two, dtype=F32)
    kb = sub_keys.reshape(hh * two, nk, dh)
    return jnp.einsum('akd,ab->akbd', kb, eye).reshape(hh * two * nk, hh * two * dh)


def _peer_ln(h, tab, first_block, wq, sub_keys, u_tab, v_tab, ln_g, ln_b):
    bsz, s, d = h.shape
    tm = ROW_BLOCK
    nblk = s // tm - first_block
    hk = PEER_HEADS * 2 * PEER_KEYS
    keys_bd = _peer_block_diag_keys(sub_keys).astype(BF16)

    def tab_spec3(which):
        return pl.BlockSpec((None, None, None, 1, d),
                            lambda b, i, *_: (b, jnp.minimum(i + first_block, 1), which, 0, 0))

    route_shape = jax.ShapeDtypeStruct((bsz, nblk, PEER_HEADS, PEER_KEYS, tm), F32)
    route_spec = pl.BlockSpec((None, None, PEER_HEADS, PEER_KEYS, tm), lambda b, i: (b, i, 0, 0, 0))
    pin, thr, e1, s2, e2 = pl.pallas_call(
        _peer_route_kernel,
        grid=(bsz, nblk),
        in_specs=[pl.BlockSpec((None, tm, d), lambda b, i: (b, i + first_block, 0)),
                  tab_spec3(3), tab_spec3(4),
                  pl.BlockSpec((d, PEER_HEADS * PEER_DK), lambda b, i: (0, 0)),
                  pl.BlockSpec((hk, PEER_HEADS * PEER_DK), lambda b, i: (0, 0))],
        out_specs=[pl.BlockSpec((None, tm, d), lambda b, i: (b, i, 0)),
                   route_spec, route_spec, route_spec, route_spec],
        out_shape=[jax.ShapeDtypeStruct((bsz, nblk * tm, d), BF16),
                   route_shape, route_shape, route_shape, route_shape],
        compiler_params=_cparams("parallel", "parallel"),
        name="peer_route",
    )(h, tab, tab, wq.astype(BF16), keys_bd)

    n_exp = u_tab.shape[0]
    rows_per_step = PEER_ETILE // PEER_KEYS
    full_spec = pl.BlockSpec((None, None, PEER_HEADS, PEER_KEYS, tm), lambda b, i, j: (b, i, 0, 0, 0))
    row_spec = pl.BlockSpec((None, None, PEER_HEADS, rows_per_step, tm), lambda b, i, j: (b, i, 0, j, 0))
    return pl.pallas_call(
        _peer_dense_kernel,
        grid=(bsz, nblk, n_exp // PEER_ETILE),
        in_specs=[pl.BlockSpec((None, tm, d), lambda b, i, j: (b, i, 0)),
                  row_spec, row_spec, full_spec, full_spec,
                  pl.BlockSpec((PEER_ETILE, d), lambda b, i, j: (j, 0)),
                  pl.BlockSpec((d, PEER_ETILE), lambda b, i, j: (0, j)),
                  pl.BlockSpec((None, tm, d), lambda b, i, j: (b, i + first_block, 0)),
                  tab_spec3(5),
                  pl.BlockSpec((1, d), lambda b, i, j: (0, 0)),
                  pl.BlockSpec((1, d), lambda b, i, j: (0, 0))],
        out_specs=pl.BlockSpec((None, tm, d), lambda b, i, j: (b, i, 0)),
        out_shape=jax.ShapeDtypeStruct((bsz, nblk * tm, d), F32),
        scratch_shapes=[pltpu.VMEM((d, tm), F32)],
        compiler_params=_cparams("parallel", "parallel", "arbitrary"),
        name="peer_dense",
    )(pin, thr, e1, s2, e2, u_tab.astype(BF16), v_tab.T.astype(BF16), h, tab,
      ln_g.reshape(1, d), ln_b.reshape(1, d))


S5_CHAINS = 8
S5_TCHUNK = 128
S5_NSTATE = S5_GROUPS * S5_STATE


def _s5_scan_kernel(u_ref, wb_ref, wc_ref, a_ref, y_ref, bre_ref, bim_ref, h_ref):
    @pl.when(pl.program_id(0) == 0)
    def _():
        h_ref[...] = jnp.zeros_like(h_ref)

    rows = u_ref.shape[0]
    u = u_ref[...].astype(BF16)
    is_bwd = (lax.broadcasted_iota(jnp.int32, (rows, 1), 0) % S5_CHAINS) >= (S5_CHAINS // 2)

    def proj(w_fwd, w_bwd, x):
        return jnp.where(is_bwd, jnp.dot(x, w_bwd, preferred_element_type=F32),
                         jnp.dot(x, w_fwd, preferred_element_type=F32))

    bre_ref[...] = proj(wb_ref[0], wb_ref[2], u)
    bim_ref[...] = proj(wb_ref[1], wb_ref[3], u)
    a_re = a_ref[0]
    a_im = a_ref[1]

    def step(t, carry):
        h_re, h_im = carry
        off = pl.multiple_of(t * S5_CHAINS, S5_CHAINS)
        n_re = a_re * h_re - a_im * h_im + bre_ref[pl.ds(off, S5_CHAINS), :]
        n_im = a_re * h_im + a_im * h_re + bim_ref[pl.ds(off, S5_CHAINS), :]
        bre_ref[pl.ds(off, S5_CHAINS), :] = n_re
        bim_ref[pl.ds(off, S5_CHAINS), :] = n_im
        return n_re, n_im

    h_re, h_im = lax.fori_loop(0, rows // S5_CHAINS, step, (h_ref[0], h_ref[1]), unroll=2)
    h_ref[0] = h_re
    h_ref[1] = h_im
    hre = bre_ref[...].astype(BF16)
    him = bim_ref[...].astype(BF16)
    y_ref[...] = proj(wc_ref[0], wc_ref[2], hre) - proj(wc_ref[1], wc_ref[3], him)


def _s5_glu_kernel(u_ref, y0_ref, y1_ref, d_ref, w_ref, b_ref, o_ref):
    y = u_ref[...] * d_ref[...] + y0_ref[...] + y1_ref[...]
    y = 0.5 * y * (1.0 + lax.erf(y * SQRT_HALF))
    gate = jnp.dot(y.astype(BF16), w_ref[...], preferred_element_type=F32) + b_ref[...]
    o_ref[...] = y * jax.nn.sigmoid(gate)


def _dir_view(z, n_ctx, d):
    if d == 0:
        return z
    return jnp.concatenate([jnp.flip(z[:, :n_ctx], 1), jnp.flip(z[:, n_ctx:], 1)], 1)


def _s5_params(lam_re, lam_im, log_dt, b_re, b_im, c_re, c_im):
    eye = jnp.eye(S5_GROUPS, dtype=F32)
    wb, wc, a_rows = [], [], [[], []]
    for d in range(2):
        lr, li = lam_re[d], lam_im[d]
        dt = jnp.exp(log_dt[d])[:, None]
        mag = jnp.exp(lr * dt)
        ab_re, ab_im = mag * jnp.cos(li * dt), mag * jnp.sin(li * dt)
        den = lr * lr + li * li
        nr = ab_re - 1.0
        coef_re = (nr * lr + ab_im * li) / den
        coef_im = (ab_im * lr - nr * li) / den
        bb_re = coef_re[..., None] * b_re[d] - coef_im[..., None] * b_im[d]
        bb_im = coef_re[..., None] * b_im[d] + coef_im[..., None] * b_re[d]
        for bb in (bb_re, bb_im):
            wb.append(jnp.einsum('gpi,gh->gihp', bb, eye).reshape(S5_WIDTH, S5_NSTATE))
        for cc in (c_re[d], c_im[d]):
            wc.append(jnp.einsum('gip,gh->gphi', cc, eye).reshape(S5_NSTATE, S5_WIDTH))
        for k, ab in enumerate((ab_re, ab_im)):
            a_rows[k].append(jnp.broadcast_to(ab.reshape(1, S5_NSTATE), (S5_CHAINS // 2, S5_NSTATE)))
    a_tiles = jnp.stack([jnp.concatenate(a_rows[0], 0), jnp.concatenate(a_rows[1], 0)])
    return jnp.stack(wb).astype(BF16), jnp.stack(wc).astype(BF16), a_tiles


def _s5_mixer(u, n_ctx, lam_re, lam_im, log_dt, b_re, b_im, c_re, c_im, d_skip, w_glu, b_glu):
    bsz, s, w = u.shape
    assert 2 * bsz == S5_CHAINS
    wb, wc, a_tiles = _s5_params(lam_re, lam_im, log_dt, b_re, b_im, c_re, c_im)
    chain = jnp.concatenate([jnp.transpose(u, (1, 0, 2)), jnp.transpose(_dir_view(u, n_ctx, 1), (1, 0, 2))], 1)
    rows = S5_TCHUNK * S5_CHAINS
    y = pl.pallas_call(
        _s5_scan_kernel,
        grid=(s // S5_TCHUNK,),
        in_specs=[pl.BlockSpec((rows, w), lambda i: (i, 0)),
                  pl.BlockSpec((4, w, S5_NSTATE), lambda i: (0, 0, 0)),
                  pl.BlockSpec((4, S5_NSTATE, w), lambda i: (0, 0, 0)),
                  pl.BlockSpec((2, S5_CHAINS, S5_NSTATE), lambda i: (0, 0, 0))],
        out_specs=pl.BlockSpec((rows, w), lambda i: (i, 0)),
        out_shape=jax.ShapeDtypeStruct((s * S5_CHAINS, w), F32),
        scratch_shapes=[pltpu.VMEM((rows, S5_NSTATE), F32), pltpu.VMEM((rows, S5_NSTATE), F32),
                        pltpu.VMEM((2, S5_CHAINS, S5_NSTATE), F32)],
        compiler_params=_cparams("arbitrary"),
        name="s5_scan",
    )(chain.reshape(s * S5_CHAINS, w), wb, wc, a_tiles)
    y = y.reshape(s, S5_CHAINS, w)
    y_fwd = jnp.transpose(y[:, :bsz], (1, 0, 2))
    y_bwd = _dir_view(jnp.transpose(y[:, bsz:], (1, 0, 2)), n_ctx, 1)
    tok = pl.BlockSpec((None, ROW_BLOCK, w), lambda b, i: (b, i, 0))
    vec = pl.BlockSpec((1, w), lambda b, i: (0, 0))
    return pl.pallas_call(
        _s5_glu_kernel,
        grid=(bsz, s // ROW_BLOCK),
        in_specs=[tok, tok, tok, vec, pl.BlockSpec((w, w), lambda b, i: (0, 0)), vec],
        out_specs=tok,
        out_shape=jax.ShapeDtypeStruct((bsz, s, w), F32),
        compiler_params=_cparams("parallel", "parallel"),
        name="s5_glu",
    )(u, y_fwd, y_bwd, d_skip.reshape(1, w), w_glu.astype(BF16), b_glu.reshape(1, w))


ATT_SCALE = HEAD_DIM ** -0.5
ATT_NEG = -1e30
ROPE_QUARTER = HEAD_DIM // 4
QC_W = WIN_HEADS * HEAD_DIM
KC_W = WIN_KV_HEADS * HEAD_DIM
QD_W = NA_HEADS * HEAD_DIM


def _rope_tables(n_ctx, t):
    pos = np.arange(t)
    rowcol = np.stack([pos // GRID_W, pos % GRID_W], 1).astype(np.float32)
    lane = np.arange(QC_W)
    j = lane % HEAD_DIM
    freqs = (ROPE_BASE ** (-np.arange(ROPE_QUARTER, dtype=np.float32) / ROPE_QUARTER)).astype(np.float32)
    ang = rowcol[:, j // (HEAD_DIM // 2)] * freqs[j % ROPE_QUARTER][None, :]
    ang = jnp.asarray(ang, F32)
    sign = np.where((j % (HEAD_DIM // 2)) < ROPE_QUARTER, -1.0, 1.0).astype(np.float32)
    cos = jnp.concatenate([jnp.ones((n_ctx, QC_W), F32), jnp.cos(ang)], 0)
    sin = jnp.concatenate([jnp.zeros((n_ctx, QC_W), F32), jnp.sin(ang) * sign[None, :]], 0)
    return cos, sin


def _rope(x, cos, sin):
    n = x.shape[-1]
    lane = lax.broadcasted_iota(jnp.int32, (1, n), 1)
    first = (lane % (HEAD_DIM // 2)) < ROPE_QUARTER
    partner = jnp.where(first, pltpu.roll(x, n - ROPE_QUARTER, 1), pltpu.roll(x, ROPE_QUARTER, 1))
    return x * cos + partner * sin


def _attn_prep_kernel(z_ref, cos_ref, sin_ref, qc_ref, kc_ref, vc_ref, qd_ref, kd_ref, vd_ref):
    c0, c1, c2, c3, c4 = ODD_SPLITS
    cos = cos_ref[...]
    sin = sin_ref[...]
    qc_ref[...] = (_rope(z_ref[:, 0:c0], cos, sin) * ATT_SCALE).astype(BF16)
    kc_ref[...] = _rope(z_ref[:, c0:c1], cos[:, :KC_W], sin[:, :KC_W]).astype(BF16)
    vc_ref[...] = z_ref[:, c1:c2].astype(BF16)
    qd_ref[...] = (z_ref[:, c2:c3] * ATT_SCALE).astype(BF16)
    kd_ref[...] = z_ref[:, c3:c4].astype(BF16)
    vd_ref[...] = z_ref[:, c4:].astype(BF16)


def _attn_prep(z, n_ctx):
    bsz, s, n = z.shape
    cos, sin = _rope_tables(n_ctx, s - n_ctx)
    widths = (QC_W, KC_W, KC_W, QD_W, QD_W, QD_W)
    return pl.pallas_call(
        _attn_prep_kernel,
        grid=(bsz, s // ROW_BLOCK),
        in_specs=[pl.BlockSpec((None, ROW_BLOCK, n), lambda b, i: (b, i, 0)),
                  pl.BlockSpec((ROW_BLOCK, QC_W), lambda b, i: (i, 0)),
                  pl.BlockSpec((ROW_BLOCK, QC_W), lambda b, i: (i, 0))],
        out_specs=[pl.BlockSpec((None, ROW_BLOCK, w), lambda b, i: (b, i, 0)) for w in widths],
        out_shape=[jax.ShapeDtypeStruct((bsz, s, w), BF16) for w in widths],
        compiler_params=_cparams("parallel", "parallel"),
        name="attn_prep",
    )(z, cos, sin)


def _softmax_pv(s, v, extra_logit=None):
    m = jnp.max(s, axis=1, keepdims=True)
    if extra_logit is not None:
        m = jnp.maximum(m, extra_logit)
    p = jnp.exp(s - m)
    den = jnp.sum(p, axis=1, keepdims=True)
    if extra_logit is not None:
        den = den + jnp.exp(extra_logit - m)
    return jnp.dot(p.astype(BF16), v, preferred_element_type=F32) / den


def _window_kernel(q_ref, kp_ref, kc_ref, kn_ref, kx_ref, vp_ref, vc_ref, vn_ref, vx_ref, sink_ref, o_ref, *, t_len):
    i = pl.program_id(1)
    wb = WIN_BLOCK
    qi = lax.broadcasted_iota(jnp.int32, (wb, 1), 0)
    kj = lax.broadcasted_iota(jnp.int32, (1, 3 * wb), 1) - wb
    kabs = i * wb + kj
    valid = (jnp.abs(kj - qi) <= WINDOW) & (kabs >= 0) & (kabs < t_len)
    n_ctx = kx_ref.shape[0]
    bias = jnp.concatenate([jnp.where(valid, 0.0, ATT_NEG), jnp.zeros((wb, n_ctx), F32)], axis=1)
    k_all = jnp.concatenate([kp_ref[...], kc_ref[...], kn_ref[...], kx_ref[...]], axis=0)
    v_all = jnp.concatenate([vp_ref[...], vc_ref[...], vn_ref[...], vx_ref[...]], axis=0)
    group = WIN_HEADS // WIN_KV_HEADS
    for hq in range(WIN_HEADS):
        g = hq // group
        kg = k_all[:, g * HEAD_DIM:(g + 1) * HEAD_DIM]
        vg = v_all[:, g * HEAD_DIM:(g + 1) * HEAD_DIM]
        q = q_ref[:, hq * HEAD_DIM:(hq + 1) * HEAD_DIM]
        s = lax.dot_general(q, kg, (((1,), (1,)), ((), ())), preferred_element_type=F32) + bias
        o_ref[:, hq * HEAD_DIM:(hq + 1) * HEAD_DIM] = _softmax_pv(s, vg, sink_ref[hq:hq + 1, 0:1])


def _window_gqa(qc, kc, vc, sink, n_ctx):
    bsz, s, _ = qc.shape
    t_len = s - n_ctx
    nb = t_len // WIN_BLOCK
    off = n_ctx // WIN_BLOCK

    def kv_spec(delta):
        return pl.BlockSpec((None, WIN_BLOCK, KC_W), lambda b, i: (b, jnp.clip(i + delta, 0, nb - 1) + off, 0))

    ctx_spec = pl.BlockSpec((None, n_ctx, KC_W), lambda b, i: (b, 0, 0))
    sink_tile = jnp.broadcast_to(sink.reshape(WIN_HEADS, 1), (WIN_HEADS, 128))
    return pl.pallas_call(
        functools.partial(_window_kernel, t_len=t_len),
        grid=(bsz, nb),
        in_specs=[pl.BlockSpec((None, WIN_BLOCK, QC_W), lambda b, i: (b, i + off, 0)),
                  kv_spec(-1), kv_spec(0), kv_spec(1), ctx_spec,
                  kv_spec(-1), kv_spec(0), kv_spec(1), ctx_spec,
                  pl.BlockSpec((WIN_HEADS, 128), lambda b, i: (0, 0))],
        out_specs=pl.BlockSpec((None, WIN_BLOCK, QC_W), lambda b, i: (b, i, 0)),
        out_shape=jax.ShapeDtypeStruct((bsz, t_len, QC_W), F32),
        compiler_params=_cparams("parallel", "parallel"),
        name="window_gqa",
    )(qc, kc, kc, kc, kc, vc, vc, vc, vc, sink_tile)


NA_QROWS = 2
NA_KROWS = NA_QROWS + NA_ROWS - 1


def _na_block_plan(rows):
    kh = min(NA_ROWS, rows)
    kw = min(NA_COLS, GRID_W)
    c = np.arange(GRID_W)
    cs = np.clip(c - kw // 2, 0, GRID_W - kw)
    kc = np.arange(GRID_W)
    col_valid = (kc[None, :] >= cs[:, None]) & (kc[None, :] < cs[:, None] + kw)
    dc = np.clip(kc[None, :] - c[:, None] + (NA_COLS - 1), 0, 2 * NA_COLS - 2)
    patterns, types, kbs = {}, [], []
    for r0 in range(0, rows, NA_QROWS):
        kb = int(np.clip(r0 - kh // 2, 0, rows - NA_KROWS))
        r = r0 + np.arange(NA_QROWS)
        rs = np.clip(r - kh // 2, 0, rows - kh)
        kr = kb + np.arange(NA_KROWS)
        row_valid = (kr[None, :] >= rs[:, None]) & (kr[None, :] < rs[:, None] + kh)
        dr = np.clip(kr[None, :] - r[:, None] + (NA_ROWS - 1), 0, 2 * NA_ROWS - 2)
        key = (row_valid.tobytes(), dr.tobytes())
        if key not in patterns:
            valid = row_valid[:, None, :, None] & col_valid[None, :, None, :]
            shape = valid.shape
            patterns[key] = (len(patterns), valid.reshape(NA_QROWS * GRID_W, NA_KROWS * GRID_W),
                             np.broadcast_to(dr[:, None, :, None], shape).reshape(valid.shape[0] * shape[1], -1),
                             np.broadcast_to(dc[None, :, None, :], shape).reshape(valid.shape[0] * shape[1], -1))
        types.append(patterns[key][0])
        kbs.append(kb)
    ordered = sorted(patterns.values(), key=lambda p: p[0])
    valid = np.stack([p[1] for p in ordered])
    dr = np.stack([p[2] for p in ordered])
    dc = np.stack([p[3] for p in ordered])
    return np.asarray(types, np.int32), np.asarray(kbs, np.int32), valid, dr, dc


def _na_kernel(types_ref, kbs_ref, q_ref, *refs):
    del types_ref, kbs_ref
    k_refs = refs[:NA_KROWS]
    v_refs = refs[NA_KROWS:2 * NA_KROWS]
    kx_ref, vx_ref, bias_ref, o_ref = refs[2 * NA_KROWS:]
    k_all = jnp.concatenate([r[...] for r in k_refs] + [kx_ref[...]], axis=0)
    v_all = jnp.concatenate([r[...] for r in v_refs] + [vx_ref[...]], axis=0)
    n_q = q_ref.shape[0]
    n_ctx = kx_ref.shape[0]
    zeros = jnp.zeros((n_q, n_ctx), F32)
    for hd in range(NA_HEADS):
        sl = slice(hd * HEAD_DIM, (hd + 1) * HEAD_DIM)
        s = lax.dot_general(q_ref[:, sl], k_all[:, sl], (((1,), (1,)), ((), ())), preferred_element_type=F32)
        s = s + jnp.concatenate([bias_ref[hd], zeros], axis=1)
        o_ref[:, sl] = _softmax_pv(s, v_all[:, sl])


def _neighbourhood_attention(qd, kd, vd, rpb, n_ctx):
    bsz, s, _ = qd.shape
    t_len = s - n_ctx
    rows = t_len // GRID_W
    types, kbs, valid, dr, dc = _na_block_plan(rows)
    n_types = valid.shape[0]
    dr_pat = dr.reshape(n_types, NA_QROWS, GRID_W, NA_KROWS, GRID_W)[:, :, 0, :, 0]
    dc_pat = dc.reshape(n_types, NA_QROWS, GRID_W, NA_KROWS, GRID_W)[0, 0, :, 0, :]
    oh_r = jnp.asarray(dr_pat[..., None] == np.arange(2 * NA_ROWS - 1), F32)
    oh_c = jnp.asarray(dc_pat[..., None] == np.arange(2 * NA_COLS - 1), F32)
    bias = jnp.einsum('tamr,hrs,cks->thacmk', oh_r, rpb, oh_c, precision=HI)
    bias = bias.reshape(n_types, NA_HEADS, NA_QROWS * GRID_W, NA_KROWS * GRID_W)
    bias = jnp.where(valid[:, None], bias, ATT_NEG)
    n_q = NA_QROWS * GRID_W
    off_q = n_ctx // n_q
    off_k = n_ctx // GRID_W

    def kv_spec(m):
        return pl.BlockSpec((None, GRID_W, QD_W), lambda b, i, types, kbs: (b, kbs[i] + m + off_k, 0))

    ctx_spec = pl.BlockSpec((None, n_ctx, QD_W), lambda b, i, types, kbs: (b, 0, 0))
    grid_spec = pltpu.PrefetchScalarGridSpec(
        num_scalar_prefetch=2,
        grid=(bsz, rows // NA_QROWS),
        in_specs=([pl.BlockSpec((None, n_q, QD_W), lambda b, i, types, kbs: (b, i + off_q, 0))]
                  + [kv_spec(m) for m in range(NA_KROWS)] + [kv_spec(m) for m in range(NA_KROWS)]
                  + [ctx_spec, ctx_spec,
                     pl.BlockSpec((None, NA_HEADS, n_q, NA_KROWS * GRID_W),
                                  lambda b, i, types, kbs: (types[i], 0, 0, 0))]),
        out_specs=pl.BlockSpec((None, n_q, QD_W), lambda b, i, types, kbs: (b, i, 0)),
    )
    return pl.pallas_call(
        _na_kernel,
        grid_spec=grid_spec,
        out_shape=jax.ShapeDtypeStruct((bsz, t_len, QD_W), F32),
        compiler_params=_cparams("parallel", "arbitrary"),
        name="neighbourhood_attention",
    )(jnp.asarray(types), jnp.asarray(kbs), qd, *([kd] * NA_KROWS), *([vd] * NA_KROWS), kd, vd, bias)


RWKV_CHUNK = 64
RWKV_SUPER = ROW_BLOCK // RWKV_CHUNK
HI = lax.Precision.HIGHEST


def _split_dot(x, e_bf16):
    hi = x.astype(BF16)
    lo = (x - hi.astype(F32)).astype(BF16)
    return jnp.dot(hi, e_bf16, preferred_element_type=F32) + jnp.dot(lo, e_bf16, preferred_element_type=F32)


def _rwkv_prep_kernel(z_ref, zp_ref, zn_ref, mu_ref, w0_ref, w2_ref, a0_ref, a2_ref, g2_ref, kk_ref, ka_ref,
                      rk_ref, e_ref, r_o, v_o, kk_o, g_o, bonus_o, logw_o, kd_o, bb_o):
    i = pl.program_id(1)
    last = pl.num_programs(1) - 1
    tm = z_ref.shape[0]
    hw = RWKV_WIDTH
    z = z_ref[:, S5_WIDTH:]
    row = lax.broadcasted_iota(jnp.int32, (tm, 1), 0)
    prev_halo = jnp.where(i <= 1, 0.0, zp_ref[7:8, S5_WIDTH:])
    next_halo = jnp.where((i == 0) | (i == last), 0.0, zn_ref[0:1, S5_WIDTH:])
    prev = jnp.where(row == 0, prev_halo, pltpu.roll(z, 1, 0))
    nxt = jnp.where(row == tm - 1, next_halo, pltpu.roll(z, tm - 1, 0))
    zs = z + mu_ref[0:1, :] * (prev - z) + mu_ref[1:2, :] * (nxt - z)
    c0, c1, c2, c3, c4 = RWKV_SPLITS
    r = zs[:, :c0]
    k = zs[:, c0:c1]
    v = zs[:, c1:c2]
    e = e_ref[...]
    g_o[...] = jnp.dot(jax.nn.sigmoid(zs[:, c4:]).astype(BF16), g2_ref[...], preferred_element_type=F32)
    kk = k * kk_ref[...]
    kk = kk * lax.rsqrt(_split_dot(kk * kk, e) + 1e-12)
    w_lora = jnp.dot(jnp.tanh(zs[:, c2:c3]).astype(BF16), w2_ref[...], preferred_element_type=F32)
    a_lora = jnp.dot(zs[:, c3:c4].astype(BF16), a2_ref[...], preferred_element_type=F32)
    bonus = jnp.zeros((tm, hw), F32)
    for d in range(2):
        x = w0_ref[d:d + 1, :] + w_lora[:, d * hw:(d + 1) * hw]
        softplus_neg = jnp.maximum(-x, 0.0) + jnp.log(1.0 + jnp.exp(-jnp.abs(x)))
        logw_o[d] = -jnp.exp(-softplus_neg - 0.5)
        a = jax.nn.sigmoid(a0_ref[d:d + 1, :] + a_lora[:, d * hw:(d + 1) * hw])
        kd = k * (1.0 + (a - 1.0) * ka_ref[...])
        kd_o[d] = kd
        bb_o[d] = kk * a
        bonus = bonus + _split_dot(r * kd * rk_ref[...], e) * v
    r_o[...] = r
    v_o[...] = v
    kk_o[...] = kk
    bonus_o[...] = bonus


def _rwkv_chunk_kernel(r_ref, v_ref, kk_ref, logw_ref, kd_ref, bb_ref, mn_ref, ry_ref):
    fwd = pl.program_id(1) == 0
    cn = RWKV_CHUNK
    hd = HEAD_DIM
    rowi = lax.broadcasted_iota(jnp.int32, (cn, cn), 0)
    coli = lax.broadcasted_iota(jnp.int32, (cn, cn), 1)
    lag = (rowi - coli) * jnp.where(fwd, 1, -1)
    before_incl = lag >= 0
    before = lag > 0
    eye = (rowi == coli).astype(F32)
    logw = logw_ref[...]
    lg = jnp.dot(before_incl.astype(F32), logw, precision=HI, preferred_element_type=F32)
    lg_tot = jnp.sum(logw, axis=0, keepdims=True)
    a_t = -kk_ref[...] * jnp.exp(lg - logw)
    r_t = r_ref[...] * jnp.exp(lg)
    inv = jnp.exp(-lg)
    b_t = bb_ref[...] * inv
    k_t = kd_ref[...] * inv
    to_end = jnp.exp(lg_tot - lg)
    b_h = bb_ref[...] * to_end
    k_h = kd_ref[...] * to_end
    g_end = jnp.exp(lg_tot)
    nt = (((1,), (1,)), ((), ()))
    tn = (((0,), (0,)), ((), ()))

    def mm(x, y):
        return jnp.dot(x, y, precision=HI, preferred_element_type=F32)

    for h in range(RWKV_HEADS):
        sl = slice(h * hd, (h + 1) * hd)
        v = v_ref[:, sl]
        p = lax.dot_general(jnp.concatenate([a_t[:, sl], r_t[:, sl]], axis=0),
                            jnp.concatenate([b_t[:, sl], k_t[:, sl]], axis=0), nt,
                            precision=HI, preferred_element_type=F32)
        n_ab = jnp.where(before, p[:cn, :cn], 0.0)
        n_ak = jnp.where(before, p[:cn, cn:], 0.0)
        n_rb = jnp.where(before_incl, p[cn:, :cn], 0.0)
        n_rk = jnp.where(before_incl, p[cn:, cn:], 0.0)
        tinv = eye + n_ab
        pw = n_ab
        for _ in range(int(math.log2(cn)) - 1):
            pw = mm(pw, pw)
            tinv = tinv + mm(tinv, pw)
        zz = mm(tinv, jnp.concatenate([a_t[:, sl], mm(n_ak, v)], axis=1))
        ry_ref[h] = mm(n_rb, zz) + jnp.concatenate([r_t[:, sl], mm(n_rk, v)], axis=1)
        mn_ref[h] = (lax.dot_general(b_h[:, sl], zz, tn, precision=HI, preferred_element_type=F32)
                     + jnp.concatenate([eye * g_end[:, sl],
                                        lax.dot_general(k_h[:, sl], v, tn, precision=HI,
                                                        preferred_element_type=F32)], axis=1))


def _rwkv_seq_kernel(mn_ref, ry_ref, y_ref, h_ref):
    @pl.when(pl.program_id(2) == 0)
    def _():
        h_ref[...] = jnp.zeros_like(h_ref)

    fwd = pl.program_id(1) == 0
    hd = HEAD_DIM
    for cc in range(RWKV_SUPER):
        c = jnp.where(fwd, cc, RWKV_SUPER - 1 - cc)
        row0 = pl.multiple_of(c * RWKV_CHUNK, RWKV_CHUNK)
        for h in range(RWKV_HEADS):
            mn = mn_ref[c, h]
            ry = ry_ref[c, h]
            state = h_ref[h]
            y_ref[pl.ds(row0, RWKV_CHUNK), h * hd:(h + 1) * hd] = (
                jnp.dot(ry[:, :hd], state, precision=HI, preferred_element_type=F32) + ry[:, hd:])
            h_ref[h] = jnp.dot(mn[:, :hd], state, precision=HI, preferred_element_type=F32) + mn[:, hd:]


def _rwkv_post_kernel(y0_ref, y1_ref, bonus_ref, g_ref, gnw_ref, gnb_ref, e_ref, o_ref):
    y = y0_ref[...] + y1_ref[...]
    e = e_ref[...]
    mu = _split_dot(y, e) * (1.0 / HEAD_DIM)
    yc = y - mu
    var = _split_dot(yc * yc, e) * (1.0 / HEAD_DIM)
    yn = yc * lax.rsqrt(var + GN_EPS) * gnw_ref[...] + gnb_ref[...]
    o_ref[...] = (yn + bonus_ref[...]) * g_ref[...]


def _block_diag2(w):
    z = jnp.zeros_like(w[0])
    return jnp.concatenate([jnp.concatenate([w[0], z], 1), jnp.concatenate([z, w[1]], 1)], 0)


def _rwkv_mixer(z, shift_mu, w0, w2, a0, a2, g2, k_k, k_a, r_k, gn_w, gn_b):
    bsz, s, zw = z.shape
    hw = RWKV_WIDTH
    tm = ROW_BLOCK
    nblk = s // tm
    head_of = np.arange(hw) // HEAD_DIM
    e = jnp.asarray(head_of[:, None] == head_of[None, :], BF16)
    tok = pl.BlockSpec((None, tm, hw), lambda b, i: (b, i, 0))
    tok2 = pl.BlockSpec((None, 2, tm, hw), lambda b, i: (b, 0, i, 0))
    halo = tm // 8

    def full(shape):
        return pl.BlockSpec(shape, lambda b, i: (0,) * len(shape))

    r, v, kk, g, bonus, logw, kd, bb = pl.pallas_call(
        _rwkv_prep_kernel,
        grid=(bsz, nblk),
        in_specs=[pl.BlockSpec((None, tm, zw), lambda b, i: (b, i, 0)),
                  pl.BlockSpec((None, 8, zw), lambda b, i: (b, jnp.maximum(i * halo - 1, 0), 0)),
                  pl.BlockSpec((None, 8, zw), lambda b, i: (b, jnp.minimum((i + 1) * halo, s // 8 - 1), 0)),
                  full((2, RWKV_IN)), full((2, hw)), full((2 * RWKV_LORA_W, 2 * hw)), full((2, hw)),
                  full((2 * RWKV_LORA_A, 2 * hw)), full((RWKV_LORA_G, hw)), full((1, hw)), full((1, hw)),
                  full((1, hw)), full((hw, hw))],
        out_specs=[tok, tok, tok, tok, tok, tok2, tok2, tok2],
        out_shape=[jax.ShapeDtypeStruct((bsz, s, hw), F32)] * 5 + [jax.ShapeDtypeStruct((bsz, 2, s, hw), F32)] * 3,
        compiler_params=_cparams("parallel", "parallel"),
        name="rwkv_prep",
    )(z, z, z, shift_mu, w0, _block_diag2(w2).astype(BF16), a0, _block_diag2(a2).astype(BF16), g2.astype(BF16),
      k_k.reshape(1, hw), k_a.reshape(1, hw), r_k.reshape(1, hw), e)

    nch = s // RWKV_CHUNK
    ctok = pl.BlockSpec((None, RWKV_CHUNK, hw), lambda b, d, c: (b, c, 0))
    ctok2 = pl.BlockSpec((None, None, RWKV_CHUNK, hw), lambda b, d, c: (b, d, c, 0))
    mat_shape = jax.ShapeDtypeStruct((bsz, 2, nch, RWKV_HEADS, RWKV_CHUNK, 2 * HEAD_DIM), F32)
    mat_spec = pl.BlockSpec((None, None, None, RWKV_HEADS, RWKV_CHUNK, 2 * HEAD_DIM),
                            lambda b, d, c: (b, d, c, 0, 0, 0))
    mn, ry = pl.pallas_call(
        _rwkv_chunk_kernel,
        grid=(bsz, 2, nch),
        in_specs=[ctok, ctok, ctok, ctok2, ctok2, ctok2],
        out_specs=[mat_spec, mat_spec],
        out_shape=[mat_shape, mat_shape],
        compiler_params=_cparams("parallel", "parallel", "parallel"),
        name="rwkv_chunk",
    )(r, v, kk, logw, kd, bb)

    nsb = nch // RWKV_SUPER

    def sb_of(d, j):
        return jnp.where(d == 0, j, jnp.where(j == 0, 0, nsb - j))

    sup_shape = (bsz, 2, nsb, RWKV_SUPER, RWKV_HEADS, RWKV_CHUNK, 2 * HEAD_DIM)
    sup_spec = pl.BlockSpec((None, None, None, RWKV_SUPER, RWKV_HEADS, RWKV_CHUNK, 2 * HEAD_DIM),
                            lambda b, d, j: (b, d, sb_of(d, j), 0, 0, 0, 0))
    y = pl.pallas_call(
        _rwkv_seq_kernel,
        grid=(bsz, 2, nsb),
        in_specs=[sup_spec, sup_spec],
        out_specs=pl.BlockSpec((None, None, tm, hw), lambda b, d, j: (b, d, sb_of(d, j), 0)),
        out_shape=jax.ShapeDtypeStruct((bsz, 2, s, hw), F32),
        scratch_shapes=[pltpu.VMEM((RWKV_HEADS, HEAD_DIM, HEAD_DIM), F32)],
        compiler_params=_cparams("parallel", "parallel", "arbitrary"),
        name="rwkv_seq",
    )(mn.reshape(sup_shape), ry.reshape(sup_shape))

    vec = pl.BlockSpec((1, hw), lambda b, i: (0, 0))
    return pl.pallas_call(
        _rwkv_post_kernel,
        grid=(bsz, nblk),
        in_specs=[pl.BlockSpec((None, None, tm, hw), lambda b, i: (b, 0, i, 0)),
                  pl.BlockSpec((None, None, tm, hw), lambda b, i: (b, 1, i, 0)),
                  tok, tok, vec, vec, pl.BlockSpec((hw, hw), lambda b, i: (0, 0))],
        out_specs=tok,
        out_shape=jax.ShapeDtypeStruct((bsz, s, hw), F32),
        compiler_params=_cparams("parallel", "parallel"),
        name="rwkv_post",
    )(y, y, bonus, g, gn_w.reshape(1, hw), gn_b.reshape(1, hw), e)


def kernel(x, c, ctx, c_ctx,
           l0_w_mod, l0_b_mod, l0_w_in, l0_s5_lam_re, l0_s5_lam_im, l0_s5_log_dt, l0_s5_b_re, l0_s5_b_im,
           l0_s5_c_re, l0_s5_c_im, l0_s5_d, l0_w_glu, l0_b_glu, l0_rwkv_shift, l0_rwkv_w0, l0_rwkv_w2,
           l0_rwkv_a0, l0_rwkv_a2, l0_rwkv_g2, l0_rwkv_k_k, l0_rwkv_k_a, l0_rwkv_r_k, l0_rwkv_gn_w,
           l0_rwkv_gn_b, l0_w_out, l0_ln1_g, l0_ln1_b, l0_peer_wq, l0_peer_keys, l0_peer_u, l0_peer_v,
           l0_ln2_g, l0_ln2_b,
           l1_w_mod, l1_b_mod, l1_w_in, l1_sink, l1_rpb, l1_w_out, l1_ln1_g, l1_ln1_b, l1_peer_wq,
           l1_peer_keys, l1_peer_u, l1_peer_v, l1_ln2_g, l1_ln2_b):
    L = ctx.shape[1]
    h = jnp.concatenate([ctx, x], 1)

    tab = _mod_tables(c, c_ctx, l0_w_mod, l0_b_mod)
    z = _inproj(h, tab, 0, 1, l0_w_in.astype(BF16))
    ya = _s5_mixer(z[..., :S5_WIDTH], L, l0_s5_lam_re, l0_s5_lam_im, l0_s5_log_dt, l0_s5_b_re, l0_s5_b_im,
                   l0_s5_c_re, l0_s5_c_im, l0_s5_d, l0_w_glu, l0_b_glu)
    yb = _rwkv_mixer(z, l0_rwkv_shift, l0_rwkv_w0, l0_rwkv_w2, l0_rwkv_a0, l0_rwkv_a2,
                     l0_rwkv_g2, l0_rwkv_k_k, l0_rwkv_k_a, l0_rwkv_r_k, l0_rwkv_gn_w, l0_rwkv_gn_b)
    h = _outproj_ln(jnp.concatenate([ya, yb], -1), h, tab, 2, l0_w_out.astype(BF16), l0_ln1_g, l0_ln1_b)
    h = _peer_ln(h, tab, 0, l0_peer_wq, l0_peer_keys, l0_peer_u, l0_peer_v, l0_ln2_g, l0_ln2_b)

    tab = _mod_tables(c, c_ctx, l1_w_mod, l1_b_mod)
    z = _inproj(h, tab, 0, 1, l1_w_in.astype(BF16))
    qc, kc, vc, qd, kd, vd = _attn_prep(z, L)
    y = jnp.concatenate([_window_gqa(qc, kc, vc, l1_sink, L), _neighbourhood_attention(qd, kd, vd, l1_rpb, L)], -1)
    y = jnp.concatenate([jnp.zeros((y.shape[0], L, y.shape[2]), F32), y], 1)
    h = _outproj_ln(y, h, tab, 2, l1_w_out.astype(BF16), l1_ln1_g, l1_ln1_b)
    return _peer_ln(h, tab, L // ROW_BLOCK, l1_peer_wq, l1_peer_keys, l1_peer_u, l1_peer_v, l1_ln2_g, l1_ln2_b)
```

```python
import functools
import math

import jax
import jax.numpy as jnp
import numpy as np
from jax import lax
from jax.experimental import pallas as pl
from jax.experimental.pallas import tpu as pltpu

F32 = jnp.float32
BF16 = jnp.bfloat16

D_MODEL = 1024
DEPTH = 2
GRID_W = 64
CTX_LEN = 256
HEAD_DIM = 64
S5_GROUP_CH = 16
S5_GROUPS = 16
S5_WIDTH = S5_GROUPS * S5_GROUP_CH
S5_STATE = 64
RWKV_HEADS = 8
RWKV_WIDTH = RWKV_HEADS * HEAD_DIM
RWKV_LORA_W = 64
RWKV_LORA_A = 64
RWKV_LORA_G = 128
RWKV_IN = 3 * RWKV_WIDTH + 2 * RWKV_LORA_W + 2 * RWKV_LORA_A + RWKV_LORA_G
RWKV_SPLITS = (RWKV_WIDTH, 2 * RWKV_WIDTH, 3 * RWKV_WIDTH,
               3 * RWKV_WIDTH + 2 * RWKV_LORA_W,
               3 * RWKV_WIDTH + 2 * RWKV_LORA_W + 2 * RWKV_LORA_A)
WIN_HEADS = 8
WIN_KV_HEADS = 2
WINDOW = 128
WIN_BLOCK = 128
NA_HEADS = 8
NA_ROWS = 8
NA_COLS = 16
ODD_SPLITS = (WIN_HEADS * HEAD_DIM, (WIN_HEADS + WIN_KV_HEADS) * HEAD_DIM,
              (WIN_HEADS + 2 * WIN_KV_HEADS) * HEAD_DIM,
              (WIN_HEADS + 2 * WIN_KV_HEADS + NA_HEADS) * HEAD_DIM,
              (WIN_HEADS + 2 * WIN_KV_HEADS + 2 * NA_HEADS) * HEAD_DIM)
ROPE_BASE = 10000.0
PEER_HEADS = 8
PEER_KEYS = 128
PEER_DK = 128
PEER_TOPK = 16
PEER_BLOCK = 128
ALPHA = (2.0 * DEPTH) ** 0.25
LN_EPS = 1e-5
GN_EPS = 64e-5

VMEM_LIMIT_BYTES = 56 * 1024 * 1024
ROW_BLOCK = 256


def _cparams(*sem):
    return pltpu.CompilerParams(dimension_semantics=sem, vmem_limit_bytes=VMEM_LIMIT_BYTES)


def _modulation_kernel(c_ref, w_ref, b_ref, o_ref):
    c = c_ref[...]
    s = c * jax.nn.sigmoid(c)
    o_ref[...] = jnp.dot(s.astype(BF16), w_ref[...].astype(BF16), preferred_element_type=F32) + b_ref[...]


def _modulation(cond, w_mod, b_mod):
    n, d = cond.shape
    nout = w_mod.shape[1]
    tn = 1536
    return pl.pallas_call(
        _modulation_kernel,
        grid=(nout // tn,),
        in_specs=[pl.BlockSpec((n, d), lambda j: (0, 0)),
                  pl.BlockSpec((d, tn), lambda j: (0, j)),
                  pl.BlockSpec((1, tn), lambda j: (0, j))],
        out_specs=pl.BlockSpec((n, tn), lambda j: (0, j)),
        out_shape=jax.ShapeDtypeStruct((n, nout), F32),
        compiler_params=_cparams("arbitrary"),
        name="modulation",
    )(cond, w_mod, b_mod.reshape(1, nout))


def _mod_tables(c, c_ctx, w_mod, b_mod):
    bsz = c.shape[0]
    cond = jnp.concatenate([c, c_ctx[None, :], jnp.zeros((8 - bsz - 1, c.shape[1]), F32)], 0)
    m = _modulation(cond, w_mod, b_mod)
    mx = m[:bsz].reshape(bsz, 1, 6, 1, D_MODEL)
    mc = jnp.broadcast_to(m[bsz].reshape(1, 1, 6, 1, D_MODEL), (bsz, 1, 6, 1, D_MODEL))
    return jnp.concatenate([mc, mx], 1)


def _tab_spec(which):
    return pl.BlockSpec((None, None, None, 1, D_MODEL),
                        lambda b, i: (b, jnp.minimum(i, 1), which, 0, 0))


def _inproj_kernel(x_ref, shift_ref, scale_ref, w_ref, o_ref):
    hm = x_ref[...] * (1.0 + scale_ref[...]) + shift_ref[...]
    o_ref[...] = jnp.dot(hm.astype(BF16), w_ref[...], preferred_element_type=F32)


def _inproj(h, tab, which_shift, which_scale, w_bf16):
    bsz, s, d = h.shape
    n = w_bf16.shape[1]
    return pl.pallas_call(
        _inproj_kernel,
        grid=(bsz, s // ROW_BLOCK),
        in_specs=[pl.BlockSpec((None, ROW_BLOCK, d), lambda b, i: (b, i, 0)),
                  _tab_spec(which_shift), _tab_spec(which_scale),
                  pl.BlockSpec((d, n), lambda b, i: (0, 0))],
        out_specs=pl.BlockSpec((None, ROW_BLOCK, n), lambda b, i: (b, i, 0)),
        out_shape=jax.ShapeDtypeStruct((bsz, s, n), F32),
        compiler_params=_cparams("parallel", "parallel"),
        name="inproj",
    )(h, tab, tab, w_bf16)


def _layer_norm_rows(z, g, b):
    mu = jnp.mean(z, -1, keepdims=True)
    zc = z - mu
    var = jnp.mean(zc * zc, -1, keepdims=True)
    return zc * lax.rsqrt(var + LN_EPS) * g + b


def _outproj_ln_kernel(y_ref, h_ref, gate_ref, w_ref, g_ref, b_ref, o_ref):
    proj = jnp.dot(y_ref[...].astype(BF16), w_ref[...], preferred_element_type=F32)
    z = ALPHA * h_ref[...] + gate_ref[...] * proj
    o_ref[...] = _layer_norm_rows(z, g_ref[...], b_ref[...])


def _outproj_ln(y, h, tab, which_gate, w_bf16, ln_g, ln_b):
    bsz, s, d = h.shape
    k = y.shape[-1]
    return pl.pallas_call(
        _outproj_ln_kernel,
        grid=(bsz, s // ROW_BLOCK),
        in_specs=[pl.BlockSpec((None, ROW_BLOCK, k), lambda b, i: (b, i, 0)),
                  pl.BlockSpec((None, ROW_BLOCK, d), lambda b, i: (b, i, 0)),
                  _tab_spec(which_gate),
                  pl.BlockSpec((k, d), lambda b, i: (0, 0)),
                  pl.BlockSpec((1, d), lambda b, i: (0, 0)),
                  pl.BlockSpec((1, d), lambda b, i: (0, 0))],
        out_specs=pl.BlockSpec((None, ROW_BLOCK, d), lambda b, i: (b, i, 0)),
        out_shape=jax.ShapeDtypeStruct((bsz, s, d), F32),
        compiler_params=_cparams("parallel", "parallel"),
        name="outproj_ln",
    )(y, h, tab, w_bf16, ln_g.reshape(1, d), ln_b.reshape(1, d))


PEER_NEG = -3.0e38
PEER_ETILE = 1024
PEER_SUBTILE = 256
SQRT_HALF = 0.7071067811865476


def _top_values(s, n):
    vals = []
    for _ in range(n):
        m = jnp.max(s, axis=0, keepdims=True)
        vals.append(m)
        s = jnp.where(s == m, PEER_NEG, s)
    return vals


def _peer_route_kernel(h_ref, shift_ref, scale_ref, wq_ref, keys_ref,
                       pin_ref, thr_ref, e1_ref, s2_ref, e2_ref):
    pin = (h_ref[...] * (1.0 + scale_ref[...]) + shift_ref[...]).astype(BF16)
    pin_ref[...] = pin
    q = jnp.dot(pin, wq_ref[...], preferred_element_type=F32).astype(BF16)
    st = lax.dot_general(keys_ref[...], q, (((1,), (1,)), ((), ())), preferred_element_type=F32)
    for hd in range(PEER_HEADS):
        base = hd * 2 * PEER_KEYS
        s1 = st[base:base + PEER_KEYS]
        s2 = st[base + PEER_KEYS:base + 2 * PEER_KEYS]
        v1 = _top_values(s1, PEER_TOPK)
        v2 = _top_values(s2, PEER_TOPK)
        v2all = jnp.concatenate(v2, axis=0)
        cand = jnp.concatenate([v1[a] + v2all for a in range(PEER_TOPK)], axis=0)
        top = _top_values(cand, PEER_TOPK + 1)
        tau = 0.5 * (top[PEER_TOPK - 1] + top[PEER_TOPK])
        zsum = top[0] * 0.0
        for kk in range(PEER_TOPK):
            zsum = zsum + jnp.exp(top[kk] - top[0])
        thr_ref[hd] = tau - s1
        e1_ref[hd] = jnp.exp(s1 - v1[0])
        s2_ref[hd] = s2
        e2_ref[hd] = jnp.exp(s2 - v2[0]) / zsum


def _peer_dense_kernel(pin_ref, thr_ref, e1_ref, s2_ref, e2_ref, u_ref, vt_ref,
                       h_ref, gate_ref, g_ref, b_ref, o_ref, acc_ref):
    j = pl.program_id(2)

    @pl.when(j == 0)
    def _():
        acc_ref[...] = jnp.zeros_like(acc_ref)

    pin = pin_ref[...]
    parts = []
    for sb in range(PEER_ETILE // PEER_SUBTILE):
        es = slice(sb * PEER_SUBTILE, (sb + 1) * PEER_SUBTILE)
        at = lax.dot_general(u_ref[es, :], pin, (((1,), (1,)), ((), ())), preferred_element_type=F32)
        ws = []
        for i2 in range(PEER_SUBTILE // PEER_KEYS):
            ii = sb * (PEER_SUBTILE // PEER_KEYS) + i2
            a = at[i2 * PEER_KEYS:(i2 + 1) * PEER_KEYS]
            act = 0.5 * a * (1.0 + lax.erf(a * SQRT_HALF))
            g = jnp.zeros_like(a)
            for hd in range(PEER_HEADS):
                thr_row = thr_ref[hd, pl.ds(ii, 1), :]
                c_row = e1_ref[hd, pl.ds(ii, 1), :]
                g = g + jnp.where(s2_ref[hd] >= thr_row, e2_ref[hd], 0.0) * c_row
            ws.append((g * act).astype(BF16))
        parts.append(jnp.dot(vt_ref[:, es], jnp.concatenate(ws, axis=0), preferred_element_type=F32))
    acc_ref[...] += functools.reduce(lambda x, y: x + y, parts)

    @pl.when(j == pl.num_programs(2) - 1)
    def _():
        z = ALPHA * h_ref[...] + gate_ref[...] * acc_ref[...].T
        o_ref[...] = _layer_norm_rows(z, g_ref[...], b_ref[...])


def _peer_block_diag_keys(sub_keys):
    hh, two, nk, dh = sub_keys.shape
    eye = jnp.eye(hh * two, dtype=F32)
    kb = sub_keys.reshape(hh * two, nk, dh)
    return jnp.einsum('akd,ab->akbd', kb, eye).reshape(hh * two * nk, hh * two * dh)


def _peer_ln(h, tab, first_block, wq, sub_keys, u_tab, v_tab, ln_g, ln_b):
    bsz, s, d = h.shape
    tm = ROW_BLOCK
    nblk = s // tm - first_block
    hk = PEER_HEADS * 2 * PEER_KEYS
    keys_bd = _peer_block_diag_keys(sub_keys).astype(BF16)

    def tab_spec3(which):
        return pl.BlockSpec((None, None, None, 1, d),
                            lambda b, i, *_: (b, jnp.minimum(i + first_block, 1), which, 0, 0))

    route_shape = jax.ShapeDtypeStruct((bsz, nblk, PEER_HEADS, PEER_KEYS, tm), F32)
    route_spec = pl.BlockSpec((None, None, PEER_HEADS, PEER_KEYS, tm), lambda b, i: (b, i, 0, 0, 0))
    pin, thr, e1, s2, e2 = pl.pallas_call(
        _peer_route_kernel,
        grid=(bsz, nblk),
        in_specs=[pl.BlockSpec((None, tm, d), lambda b, i: (b, i + first_block, 0)),
                  tab_spec3(3), tab_spec3(4),
                  pl.BlockSpec((d, PEER_HEADS * PEER_DK), lambda b, i: (0, 0)),
                  pl.BlockSpec((hk, PEER_HEADS * PEER_DK), lambda b, i: (0, 0))],
        out_specs=[pl.BlockSpec((None, tm, d), lambda b, i: (b, i, 0)),
                   route_spec, route_spec, route_spec, route_spec],
        out_shape=[jax.ShapeDtypeStruct((bsz, nblk * tm, d), BF16),
                   route_shape, route_shape, route_shape, route_shape],
        compiler_params=_cparams("parallel", "parallel"),
        name="peer_route",
    )(h, tab, tab, wq.astype(BF16), keys_bd)

    n_exp = u_tab.shape[0]
    rows_per_step = PEER_ETILE // PEER_KEYS
    full_spec = pl.BlockSpec((None, None, PEER_HEADS, PEER_KEYS, tm), lambda b, i, j: (b, i, 0, 0, 0))
    row_spec = pl.BlockSpec((None, None, PEER_HEADS, rows_per_step, tm), lambda b, i, j: (b, i, 0, j, 0))
    return pl.pallas_call(
        _peer_dense_kernel,
        grid=(bsz, nblk, n_exp // PEER_ETILE),
        in_specs=[pl.BlockSpec((None, tm, d), lambda b, i, j: (b, i, 0)),
                  row_spec, row_spec, full_spec, full_spec,
                  pl.BlockSpec((PEER_ETILE, d), lambda b, i, j: (j, 0)),
                  pl.BlockSpec((d, PEER_ETILE), lambda b, i, j: (0, j)),
                  pl.BlockSpec((None, tm, d), lambda b, i, j: (b, i + first_block, 0)),
                  tab_spec3(5),
                  pl.BlockSpec((1, d), lambda b, i, j: (0, 0)),
                  pl.BlockSpec((1, d), lambda b, i, j: (0, 0))],
        out_specs=pl.BlockSpec((None, tm, d), lambda b, i, j: (b, i, 0)),
        out_shape=jax.ShapeDtypeStruct((bsz, nblk * tm, d), F32),
        scratch_shapes=[pltpu.VMEM((d, tm), F32)],
        compiler_params=_cparams("parallel", "parallel", "arbitrary"),
        name="peer_dense",
    )(pin, thr, e1, s2, e2, u_tab.astype(BF16), v_tab.T.astype(BF16), h, tab,
      ln_g.reshape(1, d), ln_b.reshape(1, d))


S5_CHAINS = 8
S5_TCHUNK = 128
S5_NSTATE = S5_GROUPS * S5_STATE


def _s5_scan_kernel(u_ref, wb_ref, wc_ref, a_ref, y_ref, bre_ref, bim_ref, h_ref):
    @pl.when(pl.program_id(0) == 0)
    def _():
        h_ref[...] = jnp.zeros_like(h_ref)

    rows = u_ref.shape[0]
    u = u_ref[...].astype(BF16)
    is_bwd = (lax.broadcasted_iota(jnp.int32, (rows, 1), 0) % S5_CHAINS) >= (S5_CHAINS // 2)

    def proj(w_fwd, w_bwd, x):
        return jnp.where(is_bwd, jnp.dot(x, w_bwd, preferred_element_type=F32),
                         jnp.dot(x, w_fwd, preferred_element_type=F32))

    bre_ref[...] = proj(wb_ref[0], wb_ref[2], u)
    bim_ref[...] = proj(wb_ref[1], wb_ref[3], u)
    a_re = a_ref[0]
    a_im = a_ref[1]

    def step(t, carry):
        h_re, h_im = carry
        off = pl.multiple_of(t * S5_CHAINS, S5_CHAINS)
        n_re = a_re * h_re - a_im * h_im + bre_ref[pl.ds(off, S5_CHAINS), :]
        n_im = a_re * h_im + a_im * h_re + bim_ref[pl.ds(off, S5_CHAINS), :]
        bre_ref[pl.ds(off, S5_CHAINS), :] = n_re
        bim_ref[pl.ds(off, S5_CHAINS), :] = n_im
        return n_re, n_im

    h_re, h_im = lax.fori_loop(0, rows // S5_CHAINS, step, (h_ref[0], h_ref[1]), unroll=2)
    h_ref[0] = h_re
    h_ref[1] = h_im
    hre = bre_ref[...].astype(BF16)
    him = bim_ref[...].astype(BF16)
    y_ref[...] = proj(wc_ref[0], wc_ref[2], hre) - proj(wc_ref[1], wc_ref[3], him)


def _s5_glu_kernel(u_ref, y0_ref, y1_ref, d_ref, w_ref, b_ref, o_ref):
    y = u_ref[...] * d_ref[...] + y0_ref[...] + y1_ref[...]
    y = 0.5 * y * (1.0 + lax.erf(y * SQRT_HALF))
    gate = jnp.dot(y.astype(BF16), w_ref[...], preferred_element_type=F32) + b_ref[...]
    o_ref[...] = y * jax.nn.sigmoid(gate)


def _dir_view(z, n_ctx, d):
    if d == 0:
        return z
    return jnp.concatenate([jnp.flip(z[:, :n_ctx], 1), jnp.flip(z[:, n_ctx:], 1)], 1)


def _s5_params(lam_re, lam_im, log_dt, b_re, b_im, c_re, c_im):
    eye = jnp.eye(S5_GROUPS, dtype=F32)
    wb, wc, a_rows = [], [], [[], []]
    for d in range(2):
        lr, li = lam_re[d], lam_im[d]
        dt = jnp.exp(log_dt[d])[:, None]
        mag = jnp.exp(lr * dt)
        ab_re, ab_im = mag * jnp.cos(li * dt), mag * jnp.sin(li * dt)
        den = lr * lr + li * li
        nr = ab_re - 1.0
        coef_re = (nr * lr + ab_im * li) / den
        coef_im = (ab_im * lr - nr * li) / den
        bb_re = coef_re[..., None] * b_re[d] - coef_im[..., None] * b_im[d]
        bb_im = coef_re[..., None] * b_im[d] + coef_im[..., None] * b_re[d]
        for bb in (bb_re, bb_im):
            wb.append(jnp.einsum('gpi,gh->gihp', bb, eye).reshape(S5_WIDTH, S5_NSTATE))
        for cc in (c_re[d], c_im[d]):
            wc.append(jnp.einsum('gip,gh->gphi', cc, eye).reshape(S5_NSTATE, S5_WIDTH))
        for k, ab in enumerate((ab_re, ab_im)):
            a_rows[k].append(jnp.broadcast_to(ab.reshape(1, S5_NSTATE), (S5_CHAINS // 2, S5_NSTATE)))
    a_tiles = jnp.stack([jnp.concatenate(a_rows[0], 0), jnp.concatenate(a_rows[1], 0)])
    return jnp.stack(wb).astype(BF16), jnp.stack(wc).astype(BF16), a_tiles


def _s5_mixer(u, n_ctx, lam_re, lam_im, log_dt, b_re, b_im, c_re, c_im, d_skip, w_glu, b_glu):
    bsz, s, w = u.shape
    assert 2 * bsz == S5_CHAINS
    wb, wc, a_tiles = _s5_params(lam_re, lam_im, log_dt, b_re, b_im, c_re, c_im)
    chain = jnp.concatenate([jnp.transpose(u, (1, 0, 2)), jnp.transpose(_dir_view(u, n_ctx, 1), (1, 0, 2))], 1)
    rows = S5_TCHUNK * S5_CHAINS
    y = pl.pallas_call(
        _s5_scan_kernel,
        grid=(s // S5_TCHUNK,),
        in_specs=[pl.BlockSpec((rows, w), lambda i: (i, 0)),
                  pl.BlockSpec((4, w, S5_NSTATE), lambda i: (0, 0, 0)),
                  pl.BlockSpec((4, S5_NSTATE, w), lambda i: (0, 0, 0)),
                  pl.BlockSpec((2, S5_CHAINS, S5_NSTATE), lambda i: (0, 0, 0))],
        out_specs=pl.BlockSpec((rows, w), lambda i: (i, 0)),
        out_shape=jax.ShapeDtypeStruct((s * S5_CHAINS, w), F32),
        scratch_shapes=[pltpu.VMEM((rows, S5_NSTATE), F32), pltpu.VMEM((rows, S5_NSTATE), F32),
                        pltpu.VMEM((2, S5_CHAINS, S5_NSTATE), F32)],
        compiler_params=_cparams("arbitrary"),
        name="s5_scan",
    )(chain.reshape(s * S5_CHAINS, w), wb, wc, a_tiles)
    y = y.reshape(s, S5_CHAINS, w)
    y_fwd = jnp.transpose(y[:, :bsz], (1, 0, 2))
    y_bwd = _dir_view(jnp.transpose(y[:, bsz:], (1, 0, 2)), n_ctx, 1)
    tok = pl.BlockSpec((None, ROW_BLOCK, w), lambda b, i: (b, i, 0))
    vec = pl.BlockSpec((1, w), lambda b, i: (0, 0))
    return pl.pallas_call(
        _s5_glu_kernel,
        grid=(bsz, s // ROW_BLOCK),
        in_specs=[tok, tok, tok, vec, pl.BlockSpec((w, w), lambda b, i: (0, 0)), vec],
        out_specs=tok,
        out_shape=jax.ShapeDtypeStruct((bsz, s, w), F32),
        compiler_params=_cparams("parallel", "parallel"),
        name="s5_glu",
    )(u, y_fwd, y_bwd, d_skip.reshape(1, w), w_glu.astype(BF16), b_glu.reshape(1, w))


ATT_SCALE = HEAD_DIM ** -0.5
ATT_NEG = -1e30
ROPE_QUARTER = HEAD_DIM // 4
QC_W = WIN_HEADS * HEAD_DIM
KC_W = WIN_KV_HEADS * HEAD_DIM
QD_W = NA_HEADS * HEAD_DIM


def _rope_tables(n_ctx, t):
    pos = np.arange(t)
    rowcol = np.stack([pos // GRID_W, pos % GRID_W], 1).astype(np.float32)
    lane = np.arange(QC_W)
    j = lane % HEAD_DIM
    freqs = (ROPE_BASE ** (-np.arange(ROPE_QUARTER, dtype=np.float32) / ROPE_QUARTER)).astype(np.float32)
    ang = rowcol[:, j // (HEAD_DIM // 2)] * freqs[j % ROPE_QUARTER][None, :]
    ang = jnp.asarray(ang, F32)
    sign = np.where((j % (HEAD_DIM // 2)) < ROPE_QUARTER, -1.0, 1.0).astype(np.float32)
    cos = jnp.concatenate([jnp.ones((n_ctx, QC_W), F32), jnp.cos(ang)], 0)
    sin = jnp.concatenate([jnp.zeros((n_ctx, QC_W), F32), jnp.sin(ang) * sign[None, :]], 0)
    return cos, sin


def _rope(x, cos, sin):
    n = x.shape[-1]
    lane = lax.broadcasted_iota(jnp.int32, (1, n), 1)
    first = (lane % (HEAD_DIM // 2)) < ROPE_QUARTER
    partner = jnp.where(first, pltpu.roll(x, n - ROPE_QUARTER, 1), pltpu.roll(x, ROPE_QUARTER, 1))
    return x * cos + partner * sin


def _attn_prep_kernel(z_ref, cos_ref, sin_ref, qc_ref, kc_ref, vc_ref, qd_ref, kd_ref, vd_ref):
    c0, c1, c2, c3, c4 = ODD_SPLITS
    cos = cos_ref[...]
    sin = sin_ref[...]
    qc_ref[...] = (_rope(z_ref[:, 0:c0], cos, sin) * ATT_SCALE).astype(BF16)
    kc_ref[...] = _rope(z_ref[:, c0:c1], cos[:, :KC_W], sin[:, :KC_W]).astype(BF16)
    vc_ref[...] = z_ref[:, c1:c2].astype(BF16)
    qd_ref[...] = (z_ref[:, c2:c3] * ATT_SCALE).astype(BF16)
    kd_ref[...] = z_ref[:, c3:c4].astype(BF16)
    vd_ref[...] = z_ref[:, c4:].astype(BF16)


def _attn_prep(z, n_ctx):
    bsz, s, n = z.shape
    cos, sin = _rope_tables(n_ctx, s - n_ctx)
    widths = (QC_W, KC_W, KC_W, QD_W, QD_W, QD_W)
    return pl.pallas_call(
        _attn_prep_kernel,
        grid=(bsz, s // ROW_BLOCK),
        in_specs=[pl.BlockSpec((None, ROW_BLOCK, n), lambda b, i: (b, i, 0)),
                  pl.BlockSpec((ROW_BLOCK, QC_W), lambda b, i: (i, 0)),
                  pl.BlockSpec((ROW_BLOCK, QC_W), lambda b, i: (i, 0))],
        out_specs=[pl.BlockSpec((None, ROW_BLOCK, w), lambda b, i: (b, i, 0)) for w in widths],
        out_shape=[jax.ShapeDtypeStruct((bsz, s, w), BF16) for w in widths],
        compiler_params=_cparams("parallel", "parallel"),
        name="attn_prep",
    )(z, cos, sin)


def _softmax_pv(s, v, extra_logit=None):
    m = jnp.max(s, axis=1, keepdims=True)
    if extra_logit is not None:
        m = jnp.maximum(m, extra_logit)
    p = jnp.exp(s - m)
    den = jnp.sum(p, axis=1, keepdims=True)
    if extra_logit is not None:
        den = den + jnp.exp(extra_logit - m)
    return jnp.dot(p.astype(BF16), v, preferred_element_type=F32) / den


def _window_kernel(q_ref, kp_ref, kc_ref, kn_ref, kx_ref, vp_ref, vc_ref, vn_ref, vx_ref, sink_ref, o_ref, *, t_len):
    i = pl.program_id(1)
    wb = WIN_BLOCK
    qi = lax.broadcasted_iota(jnp.int32, (wb, 1), 0)
    kj = lax.broadcasted_iota(jnp.int32, (1, 3 * wb), 1) - wb
    kabs = i * wb + kj
    valid = (jnp.abs(kj - qi) <= WINDOW) & (kabs >= 0) & (kabs < t_len)
    n_ctx = kx_ref.shape[0]
    bias = jnp.concatenate([jnp.where(valid, 0.0, ATT_NEG), jnp.zeros((wb, n_ctx), F32)], axis=1)
    k_all = jnp.concatenate([kp_ref[...], kc_ref[...], kn_ref[...], kx_ref[...]], axis=0)
    v_all = jnp.concatenate([vp_ref[...], vc_ref[...], vn_ref[...], vx_ref[...]], axis=0)
    group = WIN_HEADS // WIN_KV_HEADS
    for hq in range(WIN_HEADS):
        g = hq // group
        kg = k_all[:, g * HEAD_DIM:(g + 1) * HEAD_DIM]
        vg = v_all[:, g * HEAD_DIM:(g + 1) * HEAD_DIM]
        q = q_ref[:, hq * HEAD_DIM:(hq + 1) * HEAD_DIM]
        s = lax.dot_general(q, kg, (((1,), (1,)), ((), ())), preferred_element_type=F32) + bias
        o_ref[:, hq * HEAD_DIM:(hq + 1) * HEAD_DIM] = _softmax_pv(s, vg, sink_ref[hq:hq + 1, 0:1])


def _window_gqa(qc, kc, vc, sink, n_ctx):
    bsz, s, _ = qc.shape
    t_len = s - n_ctx
    nb = t_len // WIN_BLOCK
    off = n_ctx // WIN_BLOCK

    def kv_spec(delta):
        return pl.BlockSpec((None, WIN_BLOCK, KC_W), lambda b, i: (b, jnp.clip(i + delta, 0, nb - 1) + off, 0))

    ctx_spec = pl.BlockSpec((None, n_ctx, KC_W), lambda b, i: (b, 0, 0))
    sink_tile = jnp.broadcast_to(sink.reshape(WIN_HEADS, 1), (WIN_HEADS, 128))
    return pl.pallas_call(
        functools.partial(_window_kernel, t_len=t_len),
        grid=(bsz, nb),
        in_specs=[pl.BlockSpec((None, WIN_BLOCK, QC_W), lambda b, i: (b, i + off, 0)),
                  kv_spec(-1), kv_spec(0), kv_spec(1), ctx_spec,
                  kv_spec(-1), kv_spec(0), kv_spec(1), ctx_spec,
                  pl.BlockSpec((WIN_HEADS, 128), lambda b, i: (0, 0))],
        out_specs=pl.BlockSpec((None, WIN_BLOCK, QC_W), lambda b, i: (b, i, 0)),
        out_shape=jax.ShapeDtypeStruct((bsz, t_len, QC_W), F32),
        compiler_params=_cparams("parallel", "parallel"),
        name="window_gqa",
    )(qc, kc, kc, kc, kc, vc, vc, vc, vc, sink_tile)


NA_QROWS = 2
NA_KROWS = NA_QROWS + NA_ROWS - 1


def _na_block_plan(rows):
    kh = min(NA_ROWS, rows)
    kw = min(NA_COLS, GRID_W)
    c = np.arange(GRID_W)
    cs = np.clip(c - kw // 2, 0, GRID_W - kw)
    kc = np.arange(GRID_W)
    col_valid = (kc[None, :] >= cs[:, None]) & (kc[None, :] < cs[:, None] + kw)
    dc = np.clip(kc[None, :] - c[:, None] + (NA_COLS - 1), 0, 2 * NA_COLS - 2)
    patterns, types, kbs = {}, [], []
    for r0 in range(0, rows, NA_QROWS):
        kb = int(np.clip(r0 - kh // 2, 0, rows - NA_KROWS))
        r = r0 + np.arange(NA_QROWS)
        rs = np.clip(r - kh // 2, 0, rows - kh)
        kr = kb + np.arange(NA_KROWS)
        row_valid = (kr[None, :] >= rs[:, None]) & (kr[None, :] < rs[:, None] + kh)
        dr = np.clip(kr[None, :] - r[:, None] + (NA_ROWS - 1), 0, 2 * NA_ROWS - 2)
        key = (row_valid.tobytes(), dr.tobytes())
        if key not in patterns:
            valid = row_valid[:, None, :, None] & col_valid[None, :, None, :]
            shape = valid.shape
            patterns[key] = (len(patterns), valid.reshape(NA_QROWS * GRID_W, NA_KROWS * GRID_W),
                             np.broadcast_to(dr[:, None, :, None], shape).reshape(valid.shape[0] * shape[1], -1),
                             np.broadcast_to(dc[None, :, None, :], shape).reshape(valid.shape[0] * shape[1], -1))
        types.append(patterns[key][0])
        kbs.append(kb)
    ordered = sorted(patterns.values(), key=lambda p: p[0])
    valid = np.stack([p[1] for p in ordered])
    dr = np.stack([p[2] for p in ordered])
    dc = np.stack([p[3] for p in ordered])
    return np.asarray(types, np.int32), np.asarray(kbs, np.int32), valid, dr, dc


def _na_kernel(types_ref, kbs_ref, q_ref, *refs):
    del types_ref, kbs_ref
    k_refs = refs[:NA_KROWS]
    v_refs = refs[NA_KROWS:2 * NA_KROWS]
    kx_ref, vx_ref, bias_ref, o_ref = refs[2 * NA_KROWS:]
    k_all = jnp.concatenate([r[...] for r in k_refs] + [kx_ref[...]], axis=0)
    v_all = jnp.concatenate([r[...] for r in v_refs] + [vx_ref[...]], axis=0)
    n_q = q_ref.shape[0]
    n_ctx = kx_ref.shape[0]
    zeros = jnp.zeros((n_q, n_ctx), F32)
    for hd in range(NA_HEADS):
        sl = slice(hd * HEAD_DIM, (hd + 1) * HEAD_DIM)
        s = lax.dot_general(q_ref[:, sl], k_all[:, sl], (((1,), (1,)), ((), ())), preferred_element_type=F32)
        s = s + jnp.concatenate([bias_ref[hd], zeros], axis=1)
        o_ref[:, sl] = _softmax_pv(s, v_all[:, sl])


def _neighbourhood_attention(qd, kd, vd, rpb, n_ctx):
    bsz, s, _ = qd.shape
    t_len = s - n_ctx
    rows = t_len // GRID_W
    types, kbs, valid, dr, dc = _na_block_plan(rows)
    n_types = valid.shape[0]
    dr_pat = dr.reshape(n_types, NA_QROWS, GRID_W, NA_KROWS, GRID_W)[:, :, 0, :, 0]
    dc_pat = dc.reshape(n_types, NA_QROWS, GRID_W, NA_KROWS, GRID_W)[0, 0, :, 0, :]
    oh_r = jnp.asarray(dr_pat[..., None] == np.arange(2 * NA_ROWS - 1), F32)
    oh_c = jnp.asarray(dc_pat[..., None] == np.arange(2 * NA_COLS - 1), F32)
    bias = jnp.einsum('tamr,hrs,cks->thacmk', oh_r, rpb, oh_c, precision=HI)
    bias = bias.reshape(n_types, NA_HEADS, NA_QROWS * GRID_W, NA_KROWS * GRID_W)
    bias = jnp.where(valid[:, None], bias, ATT_NEG)
    n_q = NA_QROWS * GRID_W
    off_q = n_ctx // n_q
    off_k = n_ctx // GRID_W

    def kv_spec(m):
        return pl.BlockSpec((None, GRID_W, QD_W), lambda b, i, types, kbs: (b, kbs[i] + m + off_k, 0))

    ctx_spec = pl.BlockSpec((None, n_ctx, QD_W), lambda b, i, types, kbs: (b, 0, 0))
    grid_spec = pltpu.PrefetchScalarGridSpec(
        num_scalar_prefetch=2,
        grid=(bsz, rows // NA_QROWS),
        in_specs=([pl.BlockSpec((None, n_q, QD_W), lambda b, i, types, kbs: (b, i + off_q, 0))]
                  + [kv_spec(m) for m in range(NA_KROWS)] + [kv_spec(m) for m in range(NA_KROWS)]
                  + [ctx_spec, ctx_spec,
                     pl.BlockSpec((None, NA_HEADS, n_q, NA_KROWS * GRID_W),
                                  lambda b, i, types, kbs: (types[i], 0, 0, 0))]),
        out_specs=pl.BlockSpec((None, n_q, QD_W), lambda b, i, types, kbs: (b, i, 0)),
    )
    return pl.pallas_call(
        _na_kernel,
        grid_spec=grid_spec,
        out_shape=jax.ShapeDtypeStruct((bsz, t_len, QD_W), F32),
        compiler_params=_cparams("parallel", "arbitrary"),
        name="neighbourhood_attention",
    )(jnp.asarray(types), jnp.asarray(kbs), qd, *([kd] * NA_KROWS), *([vd] * NA_KROWS), kd, vd, bias)


RWKV_CHUNK = 64
RWKV_SUPER = ROW_BLOCK // RWKV_CHUNK
HI = lax.Precision.HIGHEST


RWKV_PREC_PAIR = "bf16"
RWKV_PREC_INV = "bf16"
RWKV_PREC_OUT = "bf16"


def _pdot(x, y, dims, mode):
    if mode == "f32":
        return lax.dot_general(x, y, dims, precision=HI, preferred_element_type=F32)
    xh = x.astype(BF16)
    yh = y.astype(BF16)
    out = lax.dot_general(xh, yh, dims, preferred_element_type=F32)
    if mode == "bf16x3":
        xl = (x - xh.astype(F32)).astype(BF16)
        yl = (y - yh.astype(F32)).astype(BF16)
        out = (out + lax.dot_general(xh, yl, dims, preferred_element_type=F32)
               + lax.dot_general(xl, yh, dims, preferred_element_type=F32))
    return out


def _split_dot(x, e_bf16):
    hi = x.astype(BF16)
    lo = (x - hi.astype(F32)).astype(BF16)
    return jnp.dot(hi, e_bf16, preferred_element_type=F32) + jnp.dot(lo, e_bf16, preferred_element_type=F32)


def _rwkv_prep_kernel(z_ref, zp_ref, zn_ref, mu_ref, w0_ref, w2_ref, a0_ref, a2_ref, g2_ref, kk_ref, ka_ref,
                      rk_ref, e_ref, r_o, v_o, kk_o, g_o, bonus_o, logw_o, kd_o, bb_o):
    i = pl.program_id(1)
    last = pl.num_programs(1) - 1
    tm = z_ref.shape[0]
    hw = RWKV_WIDTH
    z = z_ref[:, S5_WIDTH:]
    row = lax.broadcasted_iota(jnp.int32, (tm, 1), 0)
    prev_halo = jnp.where(i <= 1, 0.0, zp_ref[7:8, S5_WIDTH:])
    next_halo = jnp.where((i == 0) | (i == last), 0.0, zn_ref[0:1, S5_WIDTH:])
    prev = jnp.where(row == 0, prev_halo, pltpu.roll(z, 1, 0))
    nxt = jnp.where(row == tm - 1, next_halo, pltpu.roll(z, tm - 1, 0))
    zs = z + mu_ref[0:1, :] * (prev - z) + mu_ref[1:2, :] * (nxt - z)
    c0, c1, c2, c3, c4 = RWKV_SPLITS
    r = zs[:, :c0]
    k = zs[:, c0:c1]
    v = zs[:, c1:c2]
    e = e_ref[...]
    g_o[...] = jnp.dot(jax.nn.sigmoid(zs[:, c4:]).astype(BF16), g2_ref[...], preferred_element_type=F32)
    kk = k * kk_ref[...]
    kk = kk * lax.rsqrt(_split_dot(kk * kk, e) + 1e-12)
    w_lora = jnp.dot(jnp.tanh(zs[:, c2:c3]).astype(BF16), w2_ref[...], preferred_element_type=F32)
    a_lora = jnp.dot(zs[:, c3:c4].astype(BF16), a2_ref[...], preferred_element_type=F32)
    bonus = jnp.zeros((tm, hw), F32)
    for d in range(2):
        x = w0_ref[d:d + 1, :] + w_lora[:, d * hw:(d + 1) * hw]
        softplus_neg = jnp.maximum(-x, 0.0) + jnp.log(1.0 + jnp.exp(-jnp.abs(x)))
        logw_o[d] = -jnp.exp(-softplus_neg - 0.5)
        a = jax.nn.sigmoid(a0_ref[d:d + 1, :] + a_lora[:, d * hw:(d + 1) * hw])
        kd = k * (1.0 + (a - 1.0) * ka_ref[...])
        kd_o[d] = kd
        bb_o[d] = kk * a
        bonus = bonus + _split_dot(r * kd * rk_ref[...], e) * v
    r_o[...] = r
    v_o[...] = v
    kk_o[...] = kk
    bonus_o[...] = bonus


def _rwkv_chunk_kernel(r_ref, v_ref, kk_ref, logw_ref, kd_ref, bb_ref, mn_ref, ry_ref):
    fwd = pl.program_id(1) == 0
    cn = RWKV_CHUNK
    hd = HEAD_DIM
    rowi = lax.broadcasted_iota(jnp.int32, (cn, cn), 0)
    coli = lax.broadcasted_iota(jnp.int32, (cn, cn), 1)
    lag = (rowi - coli) * jnp.where(fwd, 1, -1)
    before_incl = lag >= 0
    before = lag > 0
    eye = (rowi == coli).astype(F32)
    logw = logw_ref[...]
    lg = jnp.dot(before_incl.astype(F32), logw, precision=HI, preferred_element_type=F32)
    lg_tot = jnp.sum(logw, axis=0, keepdims=True)
    a_t = -kk_ref[...] * jnp.exp(lg - logw)
    r_t = r_ref[...] * jnp.exp(lg)
    inv = jnp.exp(-lg)
    b_t = bb_ref[...] * inv
    k_t = kd_ref[...] * inv
    to_end = jnp.exp(lg_tot - lg)
    b_h = bb_ref[...] * to_end
    k_h = kd_ref[...] * to_end
    g_end = jnp.exp(lg_tot)
    nn = (((1,), (0,)), ((), ()))
    nt = (((1,), (1,)), ((), ()))
    tn = (((0,), (0,)), ((), ()))

    heads = range(RWKV_HEADS)
    sls = [slice(h * hd, (h + 1) * hd) for h in heads]
    vs = [v_ref[:, sl] for sl in sls]
    ps = [_pdot(jnp.concatenate([a_t[:, sl], r_t[:, sl]], axis=0),
                jnp.concatenate([b_t[:, sl], k_t[:, sl]], axis=0), nt, RWKV_PREC_PAIR) for sl in sls]
    n_ab = [jnp.where(before, p[:cn, :cn], 0.0) for p in ps]
    n_ak = [jnp.where(before, p[:cn, cn:], 0.0) for p in ps]
    n_rb = [jnp.where(before_incl, p[cn:, :cn], 0.0) for p in ps]
    n_rk = [jnp.where(before_incl, p[cn:, cn:], 0.0) for p in ps]
    tinv = [eye + n for n in n_ab]
    pw = n_ab
    for _ in range(int(math.log2(cn)) - 1):
        pw = [_pdot(x, x, nn, RWKV_PREC_INV) for x in pw]
        tinv = [t + _pdot(t, x, nn, RWKV_PREC_INV) for t, x in zip(tinv, pw)]
    akv = [_pdot(n, v, nn, RWKV_PREC_OUT) for n, v in zip(n_ak, vs)]
    rkv = [_pdot(n, v, nn, RWKV_PREC_OUT) for n, v in zip(n_rk, vs)]
    kv = [_pdot(k_h[:, sl], v, tn, RWKV_PREC_OUT) for sl, v in zip(sls, vs)]
    zz = [_pdot(t, jnp.concatenate([a_t[:, sl], w], axis=1), nn, RWKV_PREC_OUT)
          for t, sl, w in zip(tinv, sls, akv)]
    for h in heads:
        sl = sls[h]
        ry_ref[h] = _pdot(n_rb[h], zz[h], nn, RWKV_PREC_OUT) + jnp.concatenate([r_t[:, sl], rkv[h]], axis=1)
        mn_ref[h] = (_pdot(b_h[:, sl], zz[h], tn, RWKV_PREC_OUT)
                     + jnp.concatenate([eye * g_end[:, sl], kv[h]], axis=1))


def _rwkv_seq_kernel(mn_ref, ry_ref, y_ref, h_ref):
    @pl.when(pl.program_id(2) == 0)
    def _():
        h_ref[...] = jnp.zeros_like(h_ref)

    fwd = pl.program_id(1) == 0
    hd = HEAD_DIM
    for cc in range(RWKV_SUPER):
        c = jnp.where(fwd, cc, RWKV_SUPER - 1 - cc)
        row0 = pl.multiple_of(c * RWKV_CHUNK, RWKV_CHUNK)
        for h in range(RWKV_HEADS):
            mn = mn_ref[c, h]
            ry = ry_ref[c, h]
            state = h_ref[h]
            y_ref[pl.ds(row0, RWKV_CHUNK), h * hd:(h + 1) * hd] = (
                jnp.dot(ry[:, :hd], state, precision=HI, preferred_element_type=F32) + ry[:, hd:])
            h_ref[h] = jnp.dot(mn[:, :hd], state, precision=HI, preferred_element_type=F32) + mn[:, hd:]


def _rwkv_post_kernel(y0_ref, y1_ref, bonus_ref, g_ref, gnw_ref, gnb_ref, e_ref, o_ref):
    y = y0_ref[...] + y1_ref[...]
    e = e_ref[...]
    mu = _split_dot(y, e) * (1.0 / HEAD_DIM)
    yc = y - mu
    var = _split_dot(yc * yc, e) * (1.0 / HEAD_DIM)
    yn = yc * lax.rsqrt(var + GN_EPS) * gnw_ref[...] + gnb_ref[...]
    o_ref[...] = (yn + bonus_ref[...]) * g_ref[...]


def _block_diag2(w):
    z = jnp.zeros_like(w[0])
    return jnp.concatenate([jnp.concatenate([w[0], z], 1), jnp.concatenate([z, w[1]], 1)], 0)


def _rwkv_mixer(z, shift_mu, w0, w2, a0, a2, g2, k_k, k_a, r_k, gn_w, gn_b):
    bsz, s, zw = z.shape
    hw = RWKV_WIDTH
    tm = ROW_BLOCK
    nblk = s // tm
    head_of = np.arange(hw) // HEAD_DIM
    e = jnp.asarray(head_of[:, None] == head_of[None, :], BF16)
    tok = pl.BlockSpec((None, tm, hw), lambda b, i: (b, i, 0))
    tok2 = pl.BlockSpec((None, 2, tm, hw), lambda b, i: (b, 0, i, 0))
    halo = tm // 8

    def full(shape):
        return pl.BlockSpec(shape, lambda b, i: (0,) * len(shape))

    r, v, kk, g, bonus, logw, kd, bb = pl.pallas_call(
        _rwkv_prep_kernel,
        grid=(bsz, nblk),
        in_specs=[pl.BlockSpec((None, tm, zw), lambda b, i: (b, i, 0)),
                  pl.BlockSpec((None, 8, zw), lambda b, i: (b, jnp.maximum(i * halo - 1, 0), 0)),
                  pl.BlockSpec((None, 8, zw), lambda b, i: (b, jnp.minimum((i + 1) * halo, s // 8 - 1), 0)),
                  full((2, RWKV_IN)), full((2, hw)), full((2 * RWKV_LORA_W, 2 * hw)), full((2, hw)),
                  full((2 * RWKV_LORA_A, 2 * hw)), full((RWKV_LORA_G, hw)), full((1, hw)), full((1, hw)),
                  full((1, hw)), full((hw, hw))],
        out_specs=[tok, tok, tok, tok, tok, tok2, tok2, tok2],
        out_shape=[jax.ShapeDtypeStruct((bsz, s, hw), F32)] * 5 + [jax.ShapeDtypeStruct((bsz, 2, s, hw), F32)] * 3,
        compiler_params=_cparams("parallel", "parallel"),
        name="rwkv_prep",
    )(z, z, z, shift_mu, w0, _block_diag2(w2).astype(BF16), a0, _block_diag2(a2).astype(BF16), g2.astype(BF16),
      k_k.reshape(1, hw), k_a.reshape(1, hw), r_k.reshape(1, hw), e)

    nch = s // RWKV_CHUNK
    ctok = pl.BlockSpec((None, RWKV_CHUNK, hw), lambda b, d, c: (b, c, 0))
    ctok2 = pl.BlockSpec((None, None, RWKV_CHUNK, hw), lambda b, d, c: (b, d, c, 0))
    mat_shape = jax.ShapeDtypeStruct((bsz, 2, nch, RWKV_HEADS, RWKV_CHUNK, 2 * HEAD_DIM), F32)
    mat_spec = pl.BlockSpec((None, None, None, RWKV_HEADS, RWKV_CHUNK, 2 * HEAD_DIM),
                            lambda b, d, c: (b, d, c, 0, 0, 0))
    mn, ry = pl.pallas_call(
        _rwkv_chunk_kernel,
        grid=(bsz, 2, nch),
        in_specs=[ctok, ctok, ctok, ctok2, ctok2, ctok2],
        out_specs=[mat_spec, mat_spec],
        out_shape=[mat_shape, mat_shape],
        compiler_params=_cparams("parallel", "parallel", "parallel"),
        name="rwkv_chunk",
    )(r, v, kk, logw, kd, bb)

    nsb = nch // RWKV_SUPER

    def sb_of(d, j):
        return jnp.where(d == 0, j, jnp.where(j == 0, 0, nsb - j))

    sup_shape = (bsz, 2, nsb, RWKV_SUPER, RWKV_HEADS, RWKV_CHUNK, 2 * HEAD_DIM)
    sup_spec = pl.BlockSpec((None, None, None, RWKV_SUPER, RWKV_HEADS, RWKV_CHUNK, 2 * HEAD_DIM),
                            lambda b, d, j: (b, d, sb_of(d, j), 0, 0, 0, 0))
    y = pl.pallas_call(
        _rwkv_seq_kernel,
        grid=(bsz, 2, nsb),
        in_specs=[sup_spec, sup_spec],
        out_specs=pl.BlockSpec((None, None, tm, hw), lambda b, d, j: (b, d, sb_of(d, j), 0)),
        out_shape=jax.ShapeDtypeStruct((bsz, 2, s, hw), F32),
        scratch_shapes=[pltpu.VMEM((RWKV_HEADS, HEAD_DIM, HEAD_DIM), F32)],
        compiler_params=_cparams("parallel", "parallel", "arbitrary"),
        name="rwkv_seq",
    )(mn.reshape(sup_shape), ry.reshape(sup_shape))

    vec = pl.BlockSpec((1, hw), lambda b, i: (0, 0))
    return pl.pallas_call(
        _rwkv_post_kernel,
        grid=(bsz, nblk),
        in_specs=[pl.BlockSpec((None, None, tm, hw), lambda b, i: (b, 0, i, 0)),
                  pl.BlockSpec((None, None, tm, hw), lambda b, i: (b, 1, i, 0)),
                  tok, tok, vec, vec, pl.BlockSpec((hw, hw), lambda b, i: (0, 0))],
        out_specs=tok,
        out_shape=jax.ShapeDtypeStruct((bsz, s, hw), F32),
        compiler_params=_cparams("parallel", "parallel"),
        name="rwkv_post",
    )(y, y, bonus, g, gn_w.reshape(1, hw), gn_b.reshape(1, hw), e)


def kernel(x, c, ctx, c_ctx,
           l0_w_mod, l0_b_mod, l0_w_in, l0_s5_lam_re, l0_s5_lam_im, l0_s5_log_dt, l0_s5_b_re, l0_s5_b_im,
           l0_s5_c_re, l0_s5_c_im, l0_s5_d, l0_w_glu, l0_b_glu, l0_rwkv_shift, l0_rwkv_w0, l0_rwkv_w2,
           l0_rwkv_a0, l0_rwkv_a2, l0_rwkv_g2, l0_rwkv_k_k, l0_rwkv_k_a, l0_rwkv_r_k, l0_rwkv_gn_w,
           l0_rwkv_gn_b, l0_w_out, l0_ln1_g, l0_ln1_b, l0_peer_wq, l0_peer_keys, l0_peer_u, l0_peer_v,
           l0_ln2_g, l0_ln2_b,
           l1_w_mod, l1_b_mod, l1_w_in, l1_sink, l1_rpb, l1_w_out, l1_ln1_g, l1_ln1_b, l1_peer_wq,
           l1_peer_keys, l1_peer_u, l1_peer_v, l1_ln2_g, l1_ln2_b):
    L = ctx.shape[1]
    h = jnp.concatenate([ctx, x], 1)

    tab = _mod_tables(c, c_ctx, l0_w_mod, l0_b_mod)
    z = _inproj(h, tab, 0, 1, l0_w_in.astype(BF16))
    ya = _s5_mixer(z[..., :S5_WIDTH], L, l0_s5_lam_re, l0_s5_lam_im, l0_s5_log_dt, l0_s5_b_re, l0_s5_b_im,
                   l0_s5_c_re, l0_s5_c_im, l0_s5_d, l0_w_glu, l0_b_glu)
    yb = _rwkv_mixer(z, l0_rwkv_shift, l0_rwkv_w0, l0_rwkv_w2, l0_rwkv_a0, l0_rwkv_a2,
                     l0_rwkv_g2, l0_rwkv_k_k, l0_rwkv_k_a, l0_rwkv_r_k, l0_rwkv_gn_w, l0_rwkv_gn_b)
    h = _outproj_ln(jnp.concatenate([ya, yb], -1), h, tab, 2, l0_w_out.astype(BF16), l0_ln1_g, l0_ln1_b)
    h = _peer_ln(h, tab, 0, l0_peer_wq, l0_peer_keys, l0_peer_u, l0_peer_v, l0_ln2_g, l0_ln2_b)

    tab = _mod_tables(c, c_ctx, l1_w_mod, l1_b_mod)
    z = _inproj(h, tab, 0, 1, l1_w_in.astype(BF16))
    qc, kc, vc, qd, kd, vd = _attn_prep(z, L)
    y = jnp.concatenate([_window_gqa(qc, kc, vc, l1_sink, L), _neighbourhood_attention(qd, kd, vd, l1_rpb, L)], -1)
    y = jnp.concatenate([jnp.zeros((y.shape[0], L, y.shape[2]), F32), y], 1)
    h = _outproj_ln(y, h, tab, 2, l1_w_out.astype(BF16), l1_ln1_g, l1_ln1_b)
    return _peer_ln(h, tab, L // ROW_BLOCK, l1_peer_wq, l1_peer_keys, l1_peer_u, l1_peer_v, l1_ln2_g, l1_ln2_b)
```

```python
import functools
import math

import jax
import jax.numpy as jnp
import numpy as np
from jax import lax
from jax.experimental import pallas as pl
from jax.experimental.pallas import tpu as pltpu

F32 = jnp.float32
BF16 = jnp.bfloat16

D_MODEL = 1024
DEPTH = 2
GRID_W = 64
CTX_LEN = 256
HEAD_DIM = 64
S5_GROUP_CH = 16
S5_GROUPS = 16
S5_WIDTH = S5_GROUPS * S5_GROUP_CH
S5_STATE = 64
RWKV_HEADS = 8
RWKV_WIDTH = RWKV_HEADS * HEAD_DIM
RWKV_LORA_W = 64
RWKV_LORA_A = 64
RWKV_LORA_G = 128
RWKV_IN = 3 * RWKV_WIDTH + 2 * RWKV_LORA_W + 2 * RWKV_LORA_A + RWKV_LORA_G
RWKV_SPLITS = (RWKV_WIDTH, 2 * RWKV_WIDTH, 3 * RWKV_WIDTH,
               3 * RWKV_WIDTH + 2 * RWKV_LORA_W,
               3 * RWKV_WIDTH + 2 * RWKV_LORA_W + 2 * RWKV_LORA_A)
WIN_HEADS = 8
WIN_KV_HEADS = 2
WINDOW = 128
WIN_BLOCK = 128
NA_HEADS = 8
NA_ROWS = 8
NA_COLS = 16
ODD_SPLITS = (WIN_HEADS * HEAD_DIM, (WIN_HEADS + WIN_KV_HEADS) * HEAD_DIM,
              (WIN_HEADS + 2 * WIN_KV_HEADS) * HEAD_DIM,
              (WIN_HEADS + 2 * WIN_KV_HEADS + NA_HEADS) * HEAD_DIM,
              (WIN_HEADS + 2 * WIN_KV_HEADS + 2 * NA_HEADS) * HEAD_DIM)
ROPE_BASE = 10000.0
PEER_HEADS = 8
PEER_KEYS = 128
PEER_DK = 128
PEER_TOPK = 16
PEER_BLOCK = 128
ALPHA = (2.0 * DEPTH) ** 0.25
LN_EPS = 1e-5
GN_EPS = 64e-5

VMEM_LIMIT_BYTES = 56 * 1024 * 1024
ROW_BLOCK = 256


def _cparams(*sem):
    return pltpu.CompilerParams(dimension_semantics=sem, vmem_limit_bytes=VMEM_LIMIT_BYTES)


def _modulation_kernel(c_ref, w_ref, b_ref, o_ref):
    c = c_ref[...]
    s = c * jax.nn.sigmoid(c)
    o_ref[...] = jnp.dot(s.astype(BF16), w_ref[...].astype(BF16), preferred_element_type=F32) + b_ref[...]


def _modulation(cond, w_mod, b_mod):
    n, d = cond.shape
    nout = w_mod.shape[1]
    tn = 1536
    return pl.pallas_call(
        _modulation_kernel,
        grid=(nout // tn,),
        in_specs=[pl.BlockSpec((n, d), lambda j: (0, 0)),
                  pl.BlockSpec((d, tn), lambda j: (0, j)),
                  pl.BlockSpec((1, tn), lambda j: (0, j))],
        out_specs=pl.BlockSpec((n, tn), lambda j: (0, j)),
        out_shape=jax.ShapeDtypeStruct((n, nout), F32),
        compiler_params=_cparams("arbitrary"),
        name="modulation",
    )(cond, w_mod, b_mod.reshape(1, nout))


def _mod_tables(c, c_ctx, w_mod, b_mod):
    bsz = c.shape[0]
    cond = jnp.concatenate([c, c_ctx[None, :], jnp.zeros((8 - bsz - 1, c.shape[1]), F32)], 0)
    m = _modulation(cond, w_mod, b_mod)
    mx = m[:bsz].reshape(bsz, 1, 6, 1, D_MODEL)
    mc = jnp.broadcast_to(m[bsz].reshape(1, 1, 6, 1, D_MODEL), (bsz, 1, 6, 1, D_MODEL))
    return jnp.concatenate([mc, mx], 1)


def _tab_spec(which):
    return pl.BlockSpec((None, None, None, 1, D_MODEL),
                        lambda b, i: (b, jnp.minimum(i, 1), which, 0, 0))


def _inproj_kernel(x_ref, shift_ref, scale_ref, w_ref, o_ref):
    hm = x_ref[...] * (1.0 + scale_ref[...]) + shift_ref[...]
    o_ref[...] = jnp.dot(hm.astype(BF16), w_ref[...], preferred_element_type=F32)


def _inproj(h, tab, which_shift, which_scale, w_bf16):
    bsz, s, d = h.shape
    n = w_bf16.shape[1]
    return pl.pallas_call(
        _inproj_kernel,
        grid=(bsz, s // ROW_BLOCK),
        in_specs=[pl.BlockSpec((None, ROW_BLOCK, d), lambda b, i: (b, i, 0)),
                  _tab_spec(which_shift), _tab_spec(which_scale),
                  pl.BlockSpec((d, n), lambda b, i: (0, 0))],
        out_specs=pl.BlockSpec((None, ROW_BLOCK, n), lambda b, i: (b, i, 0)),
        out_shape=jax.ShapeDtypeStruct((bsz, s, n), F32),
        compiler_params=_cparams("parallel", "parallel"),
        name="inproj",
    )(h, tab, tab, w_bf16)


def _layer_norm_rows(z, g, b):
    mu = jnp.mean(z, -1, keepdims=True)
    zc = z - mu
    var = jnp.mean(zc * zc, -1, keepdims=True)
    return zc * lax.rsqrt(var + LN_EPS) * g + b


def _outproj_ln_kernel(y_ref, h_ref, gate_ref, w_ref, g_ref, b_ref, o_ref):
    proj = jnp.dot(y_ref[...].astype(BF16), w_ref[...], preferred_element_type=F32)
    z = ALPHA * h_ref[...] + gate_ref[...] * proj
    o_ref[...] = _layer_norm_rows(z, g_ref[...], b_ref[...])


def _outproj_ln(y, h, tab, which_gate, w_bf16, ln_g, ln_b):
    bsz, s, d = h.shape
    k = y.shape[-1]
    return pl.pallas_call(
        _outproj_ln_kernel,
        grid=(bsz, s // ROW_BLOCK),
        in_specs=[pl.BlockSpec((None, ROW_BLOCK, k), lambda b, i: (b, i, 0)),
                  pl.BlockSpec((None, ROW_BLOCK, d), lambda b, i: (b, i, 0)),
                  _tab_spec(which_gate),
                  pl.BlockSpec((k, d), lambda b, i: (0, 0)),
                  pl.BlockSpec((1, d), lambda b, i: (0, 0)),
                  pl.BlockSpec((1, d), lambda b, i: (0, 0))],
        out_specs=pl.BlockSpec((None, ROW_BLOCK, d), lambda b, i: (b, i, 0)),
        out_shape=jax.ShapeDtypeStruct((bsz, s, d), F32),
        compiler_params=_cparams("parallel", "parallel"),
        name="outproj_ln",
    )(y, h, tab, w_bf16, ln_g.reshape(1, d), ln_b.reshape(1, d))


PEER_NEG = -3.0e38
PEER_ETILE = 1024
PEER_SUBTILE = 256
SQRT_HALF = 0.7071067811865476


PEER_NO_RANK = 255.0


def _top_values(s, n, with_rank=False):
    vals = []
    rank = jnp.full(s.shape, PEER_NO_RANK, F32) if with_rank else None
    for k in range(n):
        m = jnp.max(s, axis=0, keepdims=True)
        vals.append(m)
        hit = s == m
        if with_rank:
            rank = jnp.where(hit, float(k), rank)
        s = jnp.where(hit, PEER_NEG, s)
    return (vals, rank) if with_rank else vals


def _peer_route_kernel(h_ref, shift_ref, scale_ref, wq_ref, keys_ref,
                       pin_ref, cnt_ref, e1_ref, r2_ref, e2_ref):
    pin = (h_ref[...] * (1.0 + scale_ref[...]) + shift_ref[...]).astype(BF16)
    pin_ref[...] = pin
    q = jnp.dot(pin, wq_ref[...], preferred_element_type=F32).astype(BF16)
    st = lax.dot_general(keys_ref[...], q, (((1,), (1,)), ((), ())), preferred_element_type=F32)
    for hd in range(PEER_HEADS):
        base = hd * 2 * PEER_KEYS
        s1 = st[base:base + PEER_KEYS]
        s2 = st[base + PEER_KEYS:base + 2 * PEER_KEYS]
        v1, rank1 = _top_values(s1, PEER_TOPK, with_rank=True)
        v2, rank2 = _top_values(s2, PEER_TOPK, with_rank=True)
        v2all = jnp.concatenate(v2, axis=0)
        sums = [v1[a] + v2all for a in range(PEER_TOPK)]
        frontier = [sums[a][:(PEER_TOPK + 1) // (a + 1)] for a in range(PEER_TOPK)]
        top = _top_values(jnp.concatenate(frontier, axis=0), PEER_TOPK + 1)
        tau = 0.5 * (top[PEER_TOPK - 1] + top[PEER_TOPK])
        zsum = top[0] * 0.0
        for kk in range(PEER_TOPK):
            zsum = zsum + jnp.exp(top[kk] - top[0])
        cnt = jnp.zeros_like(s1)
        for a in range(PEER_TOPK):
            n_ok = jnp.sum(jnp.where(sums[a] >= tau, 1.0, 0.0), axis=0, keepdims=True)
            cnt = jnp.where(rank1 == float(a), n_ok, cnt)
        cnt_ref[hd] = cnt
        e1_ref[hd] = jnp.exp(s1 - v1[0])
        r2_ref[hd] = rank2.astype(BF16)
        e2_ref[hd] = (jnp.exp(s2 - v2[0]) / zsum).astype(BF16)


def _peer_dense_kernel(pin_ref, cnt_ref, e1_ref, r2_ref, e2_ref, u_ref, vt_ref,
                       h_ref, gate_ref, g_ref, b_ref, o_ref, acc_ref, w_ref):
    j = pl.program_id(2)

    @pl.when(j == 0)
    def _():
        acc_ref[...] = jnp.zeros_like(acc_ref)
        w_ref[1] = jnp.zeros(w_ref.shape[1:], BF16)

    acc_ref[...] += jnp.dot(vt_ref[...], w_ref[(j + 1) % 2], preferred_element_type=F32)

    pin = pin_ref[...]
    cur = j % 2
    for sb in range(PEER_ETILE // PEER_SUBTILE):
        es = slice(sb * PEER_SUBTILE, (sb + 1) * PEER_SUBTILE)
        at = lax.dot_general(u_ref[es, :], pin, (((1,), (1,)), ((), ())), preferred_element_type=F32)
        for i2 in range(PEER_SUBTILE // PEER_KEYS):
            ii = sb * (PEER_SUBTILE // PEER_KEYS) + i2
            a = at[i2 * PEER_KEYS:(i2 + 1) * PEER_KEYS]
            act = (0.5 * a * (1.0 + lax.erf(a * SQRT_HALF))).astype(BF16)
            g = jnp.zeros(a.shape, BF16)
            for hd in range(PEER_HEADS):
                cnt_row = cnt_ref[hd, pl.ds(ii, 1), :].astype(BF16)
                c_row = e1_ref[hd, pl.ds(ii, 1), :].astype(BF16)
                g = g + jnp.where(r2_ref[hd] < cnt_row, e2_ref[hd] * c_row, jnp.zeros((), BF16))
            w_ref[cur, ii * PEER_KEYS:(ii + 1) * PEER_KEYS, :] = g * act

    @pl.when(j == pl.num_programs(2) - 1)
    def _():
        z = ALPHA * h_ref[...] + gate_ref[...] * acc_ref[...].T
        o_ref[...] = _layer_norm_rows(z, g_ref[...], b_ref[...])


def _peer_block_diag_keys(sub_keys):
    hh, two, nk, dh = sub_keys.shape
    eye = jnp.eye(hh * two, dtype=F32)
    kb = sub_keys.reshape(hh * two, nk, dh)
    return jnp.einsum('akd,ab->akbd', kb, eye).reshape(hh * two * nk, hh * two * dh)


def _peer_ln(h, tab, first_block, wq, sub_keys, u_tab, v_tab, ln_g, ln_b):
    bsz, s, d = h.shape
    tm = ROW_BLOCK
    nblk = s // tm - first_block
    hk = PEER_HEADS * 2 * PEER_KEYS
    keys_bd = _peer_block_diag_keys(sub_keys).astype(BF16)

    def tab_spec3(which):
        return pl.BlockSpec((None, None, None, 1, d),
                            lambda b, i, *_: (b, jnp.minimum(i + first_block, 1), which, 0, 0))

    route_dims = (bsz, nblk, PEER_HEADS, PEER_KEYS, tm)
    route_spec = pl.BlockSpec((None, None, PEER_HEADS, PEER_KEYS, tm), lambda b, i: (b, i, 0, 0, 0))
    pin, cnt, e1, r2, e2 = pl.pallas_call(
        _peer_route_kernel,
        grid=(bsz, nblk),
        in_specs=[pl.BlockSpec((None, tm, d), lambda b, i: (b, i + first_block, 0)),
                  tab_spec3(3), tab_spec3(4),
                  pl.BlockSpec((d, PEER_HEADS * PEER_DK), lambda b, i: (0, 0)),
                  pl.BlockSpec((hk, PEER_HEADS * PEER_DK), lambda b, i: (0, 0))],
        out_specs=[pl.BlockSpec((None, tm, d), lambda b, i: (b, i, 0)),
                   route_spec, route_spec, route_spec, route_spec],
        out_shape=[jax.ShapeDtypeStruct((bsz, nblk * tm, d), BF16),
                   jax.ShapeDtypeStruct(route_dims, F32), jax.ShapeDtypeStruct(route_dims, F32),
                   jax.ShapeDtypeStruct(route_dims, BF16), jax.ShapeDtypeStruct(route_dims, BF16)],
        compiler_params=_cparams("parallel", "parallel"),
        name="peer_route",
    )(h, tab, tab, wq.astype(BF16), keys_bd)

    n_exp = u_tab.shape[0]
    rows_per_step = PEER_ETILE // PEER_KEYS
    full_spec = pl.BlockSpec((None, None, PEER_HEADS, PEER_KEYS, tm), lambda b, i, j: (b, i, 0, 0, 0))
    n_tiles = n_exp // PEER_ETILE
    row_spec = pl.BlockSpec((None, None, PEER_HEADS, rows_per_step, tm),
                            lambda b, i, j: (b, i, 0, jnp.minimum(j, n_tiles - 1), 0))
    return pl.pallas_call(
        _peer_dense_kernel,
        grid=(bsz, nblk, n_tiles + 1),
        in_specs=[pl.BlockSpec((None, tm, d), lambda b, i, j: (b, i, 0)),
                  row_spec, row_spec, full_spec, full_spec,
                  pl.BlockSpec((PEER_ETILE, d), lambda b, i, j: (jnp.minimum(j, n_tiles - 1), 0)),
                  pl.BlockSpec((d, PEER_ETILE), lambda b, i, j: (0, jnp.maximum(j - 1, 0))),
                  pl.BlockSpec((None, tm, d), lambda b, i, j: (b, i + first_block, 0)),
                  tab_spec3(5),
                  pl.BlockSpec((1, d), lambda b, i, j: (0, 0)),
                  pl.BlockSpec((1, d), lambda b, i, j: (0, 0))],
        out_specs=pl.BlockSpec((None, tm, d), lambda b, i, j: (b, i, 0)),
        out_shape=jax.ShapeDtypeStruct((bsz, nblk * tm, d), F32),
        scratch_shapes=[pltpu.VMEM((d, tm), F32), pltpu.VMEM((2, PEER_ETILE, tm), BF16)],
        compiler_params=_cparams("parallel", "parallel", "arbitrary"),
        name="peer_dense",
    )(pin, cnt, e1, r2, e2, u_tab.astype(BF16), v_tab.T.astype(BF16), h, tab,
      ln_g.reshape(1, d), ln_b.reshape(1, d))


S5_CHAINS = 8
S5_TCHUNK = 128
S5_NSTATE = S5_GROUPS * S5_STATE


def _s5_scan_kernel(u_ref, wb_ref, wc_ref, a_ref, y_ref, bre_ref, bim_ref, h_ref):
    @pl.when(pl.program_id(0) == 0)
    def _():
        h_ref[...] = jnp.zeros_like(h_ref)

    rows = u_ref.shape[0]
    u = u_ref[...].astype(BF16)
    is_bwd = (lax.broadcasted_iota(jnp.int32, (rows, 1), 0) % S5_CHAINS) >= (S5_CHAINS // 2)

    def proj(w_fwd, w_bwd, x):
        return jnp.where(is_bwd, jnp.dot(x, w_bwd, preferred_element_type=F32),
                         jnp.dot(x, w_fwd, preferred_element_type=F32))

    bre_ref[...] = proj(wb_ref[0], wb_ref[2], u)
    bim_ref[...] = proj(wb_ref[1], wb_ref[3], u)
    a_re = a_ref[0]
    a_im = a_ref[1]

    def step(t, carry):
        h_re, h_im = carry
        off = pl.multiple_of(t * S5_CHAINS, S5_CHAINS)
        n_re = a_re * h_re - a_im * h_im + bre_ref[pl.ds(off, S5_CHAINS), :]
        n_im = a_re * h_im + a_im * h_re + bim_ref[pl.ds(off, S5_CHAINS), :]
        bre_ref[pl.ds(off, S5_CHAINS), :] = n_re
        bim_ref[pl.ds(off, S5_CHAINS), :] = n_im
        return n_re, n_im

    h_re, h_im = lax.fori_loop(0, rows // S5_CHAINS, step, (h_ref[0], h_ref[1]), unroll=2)
    h_ref[0] = h_re
    h_ref[1] = h_im
    hre = bre_ref[...].astype(BF16)
    him = bim_ref[...].astype(BF16)
    y_ref[...] = proj(wc_ref[0], wc_ref[2], hre) - proj(wc_ref[1], wc_ref[3], him)


def _s5_glu_kernel(u_ref, y0_ref, y1_ref, d_ref, w_ref, b_ref, o_ref):
    y = u_ref[...] * d_ref[...] + y0_ref[...] + y1_ref[...]
    y = 0.5 * y * (1.0 + lax.erf(y * SQRT_HALF))
    gate = jnp.dot(y.astype(BF16), w_ref[...], preferred_element_type=F32) + b_ref[...]
    o_ref[...] = y * jax.nn.sigmoid(gate)


def _dir_view(z, n_ctx, d):
    if d == 0:
        return z
    return jnp.concatenate([jnp.flip(z[:, :n_ctx], 1), jnp.flip(z[:, n_ctx:], 1)], 1)


def _s5_params(lam_re, lam_im, log_dt, b_re, b_im, c_re, c_im):
    eye = jnp.eye(S5_GROUPS, dtype=F32)
    wb, wc, a_rows = [], [], [[], []]
    for d in range(2):
        lr, li = lam_re[d], lam_im[d]
        dt = jnp.exp(log_dt[d])[:, None]
        mag = jnp.exp(lr * dt)
        ab_re, ab_im = mag * jnp.cos(li * dt), mag * jnp.sin(li * dt)
        den = lr * lr + li * li
        nr = ab_re - 1.0
        coef_re = (nr * lr + ab_im * li) / den
        coef_im = (ab_im * lr - nr * li) / den
        bb_re = coef_re[..., None] * b_re[d] - coef_im[..., None] * b_im[d]
        bb_im = coef_re[..., None] * b_im[d] + coef_im[..., None] * b_re[d]
        for bb in (bb_re, bb_im):
            wb.append(jnp.einsum('gpi,gh->gihp', bb, eye).reshape(S5_WIDTH, S5_NSTATE))
        for cc in (c_re[d], c_im[d]):
            wc.append(jnp.einsum('gip,gh->gphi', cc, eye).reshape(S5_NSTATE, S5_WIDTH))
        for k, ab in enumerate((ab_re, ab_im)):
            a_rows[k].append(jnp.broadcast_to(ab.reshape(1, S5_NSTATE), (S5_CHAINS // 2, S5_NSTATE)))
    a_tiles = jnp.stack([jnp.concatenate(a_rows[0], 0), jnp.concatenate(a_rows[1], 0)])
    return jnp.stack(wb).astype(BF16), jnp.stack(wc).astype(BF16), a_tiles


def _s5_mixer(u, n_ctx, lam_re, lam_im, log_dt, b_re, b_im, c_re, c_im, d_skip, w_glu, b_glu):
    bsz, s, w = u.shape
    assert 2 * bsz == S5_CHAINS
    wb, wc, a_tiles = _s5_params(lam_re, lam_im, log_dt, b_re, b_im, c_re, c_im)
    chain = jnp.concatenate([jnp.transpose(u, (1, 0, 2)), jnp.transpose(_dir_view(u, n_ctx, 1), (1, 0, 2))], 1)
    rows = S5_TCHUNK * S5_CHAINS
    y = pl.pallas_call(
        _s5_scan_kernel,
        grid=(s // S5_TCHUNK,),
        in_specs=[pl.BlockSpec((rows, w), lambda i: (i, 0)),
                  pl.BlockSpec((4, w, S5_NSTATE), lambda i: (0, 0, 0)),
                  pl.BlockSpec((4, S5_NSTATE, w), lambda i: (0, 0, 0)),
                  pl.BlockSpec((2, S5_CHAINS, S5_NSTATE), lambda i: (0, 0, 0))],
        out_specs=pl.BlockSpec((rows, w), lambda i: (i, 0)),
        out_shape=jax.ShapeDtypeStruct((s * S5_CHAINS, w), F32),
        scratch_shapes=[pltpu.VMEM((rows, S5_NSTATE), F32), pltpu.VMEM((rows, S5_NSTATE), F32),
                        pltpu.VMEM((2, S5_CHAINS, S5_NSTATE), F32)],
        compiler_params=_cparams("arbitrary"),
        name="s5_scan",
    )(chain.reshape(s * S5_CHAINS, w), wb, wc, a_tiles)
    y = y.reshape(s, S5_CHAINS, w)
    y_fwd = jnp.transpose(y[:, :bsz], (1, 0, 2))
    y_bwd = _dir_view(jnp.transpose(y[:, bsz:], (1, 0, 2)), n_ctx, 1)
    tok = pl.BlockSpec((None, ROW_BLOCK, w), lambda b, i: (b, i, 0))
    vec = pl.BlockSpec((1, w), lambda b, i: (0, 0))
    return pl.pallas_call(
        _s5_glu_kernel,
        grid=(bsz, s // ROW_BLOCK),
        in_specs=[tok, tok, tok, vec, pl.BlockSpec((w, w), lambda b, i: (0, 0)), vec],
        out_specs=tok,
        out_shape=jax.ShapeDtypeStruct((bsz, s, w), F32),
        compiler_params=_cparams("parallel", "parallel"),
        name="s5_glu",
    )(u, y_fwd, y_bwd, d_skip.reshape(1, w), w_glu.astype(BF16), b_glu.reshape(1, w))


ATT_SCALE = HEAD_DIM ** -0.5
ATT_NEG = -1e30
ROPE_QUARTER = HEAD_DIM // 4
QC_W = WIN_HEADS * HEAD_DIM
KC_W = WIN_KV_HEADS * HEAD_DIM
QD_W = NA_HEADS * HEAD_DIM


def _rope_tables(n_ctx, t):
    pos = np.arange(t)
    rowcol = np.stack([pos // GRID_W, pos % GRID_W], 1).astype(np.float32)
    lane = np.arange(QC_W)
    j = lane % HEAD_DIM
    freqs = (ROPE_BASE ** (-np.arange(ROPE_QUARTER, dtype=np.float32) / ROPE_QUARTER)).astype(np.float32)
    ang = rowcol[:, j // (HEAD_DIM // 2)] * freqs[j % ROPE_QUARTER][None, :]
    ang = jnp.asarray(ang, F32)
    sign = np.where((j % (HEAD_DIM // 2)) < ROPE_QUARTER, -1.0, 1.0).astype(np.float32)
    cos = jnp.concatenate([jnp.ones((n_ctx, QC_W), F32), jnp.cos(ang)], 0)
    sin = jnp.concatenate([jnp.zeros((n_ctx, QC_W), F32), jnp.sin(ang) * sign[None, :]], 0)
    return cos, sin


def _rope(x, cos, sin):
    n = x.shape[-1]
    lane = lax.broadcasted_iota(jnp.int32, (1, n), 1)
    first = (lane % (HEAD_DIM // 2)) < ROPE_QUARTER
    partner = jnp.where(first, pltpu.roll(x, n - ROPE_QUARTER, 1), pltpu.roll(x, ROPE_QUARTER, 1))
    return x * cos + partner * sin


def _attn_prep_kernel(z_ref, cos_ref, sin_ref, qc_ref, kc_ref, vc_ref, qd_ref, kd_ref, vd_ref):
    c0, c1, c2, c3, c4 = ODD_SPLITS
    cos = cos_ref[...]
    sin = sin_ref[...]
    qc_ref[...] = (_rope(z_ref[:, 0:c0], cos, sin) * ATT_SCALE).astype(BF16)
    kc_ref[...] = _rope(z_ref[:, c0:c1], cos[:, :KC_W], sin[:, :KC_W]).astype(BF16)
    vc_ref[...] = z_ref[:, c1:c2].astype(BF16)
    qd_ref[...] = (z_ref[:, c2:c3] * ATT_SCALE).astype(BF16)
    kd_ref[...] = z_ref[:, c3:c4].astype(BF16)
    vd_ref[...] = z_ref[:, c4:].astype(BF16)


def _attn_prep(z, n_ctx):
    bsz, s, n = z.shape
    cos, sin = _rope_tables(n_ctx, s - n_ctx)
    widths = (QC_W, KC_W, KC_W, QD_W, QD_W, QD_W)
    return pl.pallas_call(
        _attn_prep_kernel,
        grid=(bsz, s // ROW_BLOCK),
        in_specs=[pl.BlockSpec((None, ROW_BLOCK, n), lambda b, i: (b, i, 0)),
                  pl.BlockSpec((ROW_BLOCK, QC_W), lambda b, i: (i, 0)),
                  pl.BlockSpec((ROW_BLOCK, QC_W), lambda b, i: (i, 0))],
        out_specs=[pl.BlockSpec((None, ROW_BLOCK, w), lambda b, i: (b, i, 0)) for w in widths],
        out_shape=[jax.ShapeDtypeStruct((bsz, s, w), BF16) for w in widths],
        compiler_params=_cparams("parallel", "parallel"),
        name="attn_prep",
    )(z, cos, sin)


def _softmax_pv(s, v, extra_logit=None):
    m = jnp.max(s, axis=1, keepdims=True)
    if extra_logit is not None:
        m = jnp.maximum(m, extra_logit)
    p = jnp.exp(s - m)
    den = jnp.sum(p, axis=1, keepdims=True)
    if extra_logit is not None:
        den = den + jnp.exp(extra_logit - m)
    return jnp.dot(p.astype(BF16), v, preferred_element_type=F32) / den


def _window_kernel(q_ref, kp_ref, kc_ref, kn_ref, kx_ref, vp_ref, vc_ref, vn_ref, vx_ref, sink_ref, o_ref, *, t_len):
    i = pl.program_id(1)
    wb = WIN_BLOCK
    qi = lax.broadcasted_iota(jnp.int32, (wb, 1), 0)
    kj = lax.broadcasted_iota(jnp.int32, (1, 3 * wb), 1) - wb
    kabs = i * wb + kj
    valid = (jnp.abs(kj - qi) <= WINDOW) & (kabs >= 0) & (kabs < t_len)
    n_ctx = kx_ref.shape[0]
    bias = jnp.concatenate([jnp.where(valid, 0.0, ATT_NEG), jnp.zeros((wb, n_ctx), F32)], axis=1)
    k_all = jnp.concatenate([kp_ref[...], kc_ref[...], kn_ref[...], kx_ref[...]], axis=0)
    v_all = jnp.concatenate([vp_ref[...], vc_ref[...], vn_ref[...], vx_ref[...]], axis=0)
    group = WIN_HEADS // WIN_KV_HEADS
    for hq in range(WIN_HEADS):
        g = hq // group
        kg = k_all[:, g * HEAD_DIM:(g + 1) * HEAD_DIM]
        vg = v_all[:, g * HEAD_DIM:(g + 1) * HEAD_DIM]
        q = q_ref[:, hq * HEAD_DIM:(hq + 1) * HEAD_DIM]
        s = lax.dot_general(q, kg, (((1,), (1,)), ((), ())), preferred_element_type=F32) + bias
        o_ref[:, hq * HEAD_DIM:(hq + 1) * HEAD_DIM] = _softmax_pv(s, vg, sink_ref[hq:hq + 1, 0:1])


def _window_gqa(qc, kc, vc, sink, n_ctx):
    bsz, s, _ = qc.shape
    t_len = s - n_ctx
    nb = t_len // WIN_BLOCK
    off = n_ctx // WIN_BLOCK

    def kv_spec(delta):
        return pl.BlockSpec((None, WIN_BLOCK, KC_W), lambda b, i: (b, jnp.clip(i + delta, 0, nb - 1) + off, 0))

    ctx_spec = pl.BlockSpec((None, n_ctx, KC_W), lambda b, i: (b, 0, 0))
    sink_tile = jnp.broadcast_to(sink.reshape(WIN_HEADS, 1), (WIN_HEADS, 128))
    return pl.pallas_call(
        functools.partial(_window_kernel, t_len=t_len),
        grid=(bsz, nb),
        in_specs=[pl.BlockSpec((None, WIN_BLOCK, QC_W), lambda b, i: (b, i + off, 0)),
                  kv_spec(-1), kv_spec(0), kv_spec(1), ctx_spec,
                  kv_spec(-1), kv_spec(0), kv_spec(1), ctx_spec,
                  pl.BlockSpec((WIN_HEADS, 128), lambda b, i: (0, 0))],
        out_specs=pl.BlockSpec((None, WIN_BLOCK, QC_W), lambda b, i: (b, i, 0)),
        out_shape=jax.ShapeDtypeStruct((bsz, t_len, QC_W), F32),
        compiler_params=_cparams("parallel", "parallel"),
        name="window_gqa",
    )(qc, kc, kc, kc, kc, vc, vc, vc, vc, sink_tile)


NA_QROWS = 2
NA_KROWS = NA_QROWS + NA_ROWS - 1


def _na_block_plan(rows):
    kh = min(NA_ROWS, rows)
    kw = min(NA_COLS, GRID_W)
    c = np.arange(GRID_W)
    cs = np.clip(c - kw // 2, 0, GRID_W - kw)
    kc = np.arange(GRID_W)
    col_valid = (kc[None, :] >= cs[:, None]) & (kc[None, :] < cs[:, None] + kw)
    dc = np.clip(kc[None, :] - c[:, None] + (NA_COLS - 1), 0, 2 * NA_COLS - 2)
    patterns, types, kbs = {}, [], []
    for r0 in range(0, rows, NA_QROWS):
        kb = int(np.clip(r0 - kh // 2, 0, rows - NA_KROWS))
        r = r0 + np.arange(NA_QROWS)
        rs = np.clip(r - kh // 2, 0, rows - kh)
        kr = kb + np.arange(NA_KROWS)
        row_valid = (kr[None, :] >= rs[:, None]) & (kr[None, :] < rs[:, None] + kh)
        dr = np.clip(kr[None, :] - r[:, None] + (NA_ROWS - 1), 0, 2 * NA_ROWS - 2)
        key = (row_valid.tobytes(), dr.tobytes())
        if key not in patterns:
            valid = row_valid[:, None, :, None] & col_valid[None, :, None, :]
            shape = valid.shape
            patterns[key] = (len(patterns), valid.reshape(NA_QROWS * GRID_W, NA_KROWS * GRID_W),
                             np.broadcast_to(dr[:, None, :, None], shape).reshape(valid.shape[0] * shape[1], -1),
                             np.broadcast_to(dc[None, :, None, :], shape).reshape(valid.shape[0] * shape[1], -1))
        types.append(patterns[key][0])
        kbs.append(kb)
    ordered = sorted(patterns.values(), key=lambda p: p[0])
    valid = np.stack([p[1] for p in ordered])
    dr = np.stack([p[2] for p in ordered])
    dc = np.stack([p[3] for p in ordered])
    return np.asarray(types, np.int32), np.asarray(kbs, np.int32), valid, dr, dc


def _na_kernel(types_ref, kbs_ref, q_ref, *refs):
    del types_ref, kbs_ref
    k_refs = refs[:NA_KROWS]
    v_refs = refs[NA_KROWS:2 * NA_KROWS]
    kx_ref, vx_ref, bias_ref, o_ref = refs[2 * NA_KROWS:]
    k_all = jnp.concatenate([r[...] for r in k_refs] + [kx_ref[...]], axis=0)
    v_all = jnp.concatenate([r[...] for r in v_refs] + [vx_ref[...]], axis=0)
    n_q = q_ref.shape[0]
    n_ctx = kx_ref.shape[0]
    zeros = jnp.zeros((n_q, n_ctx), F32)
    for hd in range(NA_HEADS):
        sl = slice(hd * HEAD_DIM, (hd + 1) * HEAD_DIM)
        s = lax.dot_general(q_ref[:, sl], k_all[:, sl], (((1,), (1,)), ((), ())), preferred_element_type=F32)
        s = s + jnp.concatenate([bias_ref[hd], zeros], axis=1)
        o_ref[:, sl] = _softmax_pv(s, v_all[:, sl])


def _neighbourhood_attention(qd, kd, vd, rpb, n_ctx):
    bsz, s, _ = qd.shape
    t_len = s - n_ctx
    rows = t_len // GRID_W
    types, kbs, valid, dr, dc = _na_block_plan(rows)
    n_types = valid.shape[0]
    dr_pat = dr.reshape(n_types, NA_QROWS, GRID_W, NA_KROWS, GRID_W)[:, :, 0, :, 0]
    dc_pat = dc.reshape(n_types, NA_QROWS, GRID_W, NA_KROWS, GRID_W)[0, 0, :, 0, :]
    oh_r = jnp.asarray(dr_pat[..., None] == np.arange(2 * NA_ROWS - 1), F32)
    oh_c = jnp.asarray(dc_pat[..., None] == np.arange(2 * NA_COLS - 1), F32)
    bias = jnp.einsum('tamr,hrs,cks->thacmk', oh_r, rpb, oh_c, precision=HI)
    bias = bias.reshape(n_types, NA_HEADS, NA_QROWS * GRID_W, NA_KROWS * GRID_W)
    bias = jnp.where(valid[:, None], bias, ATT_NEG)
    n_q = NA_QROWS * GRID_W
    off_q = n_ctx // n_q
    off_k = n_ctx // GRID_W

    def kv_spec(m):
        return pl.BlockSpec((None, GRID_W, QD_W), lambda b, i, types, kbs: (b, kbs[i] + m + off_k, 0))

    ctx_spec = pl.BlockSpec((None, n_ctx, QD_W), lambda b, i, types, kbs: (b, 0, 0))
    grid_spec = pltpu.PrefetchScalarGridSpec(
        num_scalar_prefetch=2,
        grid=(bsz, rows // NA_QROWS),
        in_specs=([pl.BlockSpec((None, n_q, QD_W), lambda b, i, types, kbs: (b, i + off_q, 0))]
                  + [kv_spec(m) for m in range(NA_KROWS)] + [kv_spec(m) for m in range(NA_KROWS)]
                  + [ctx_spec, ctx_spec,
                     pl.BlockSpec((None, NA_HEADS, n_q, NA_KROWS * GRID_W),
                                  lambda b, i, types, kbs: (types[i], 0, 0, 0))]),
        out_specs=pl.BlockSpec((None, n_q, QD_W), lambda b, i, types, kbs: (b, i, 0)),
    )
    return pl.pallas_call(
        _na_kernel,
        grid_spec=grid_spec,
        out_shape=jax.ShapeDtypeStruct((bsz, t_len, QD_W), F32),
        compiler_params=_cparams("parallel", "arbitrary"),
        name="neighbourhood_attention",
    )(jnp.asarray(types), jnp.asarray(kbs), qd, *([kd] * NA_KROWS), *([vd] * NA_KROWS), kd, vd, bias)


RWKV_CHUNK = 64
RWKV_SUPER = ROW_BLOCK // RWKV_CHUNK
HI = lax.Precision.HIGHEST


RWKV_PREC_PAIR = "bf16"
RWKV_PREC_INV = "bf16"
RWKV_PREC_OUT = "bf16"


def _pdot(x, y, dims, mode):
    if mode == "f32":
        return lax.dot_general(x, y, dims, precision=HI, preferred_element_type=F32)
    xh = x.astype(BF16)
    yh = y.astype(BF16)
    out = lax.dot_general(xh, yh, dims, preferred_element_type=F32)
    if mode == "bf16x3":
        xl = (x - xh.astype(F32)).astype(BF16)
        yl = (y - yh.astype(F32)).astype(BF16)
        out = (out + lax.dot_general(xh, yl, dims, preferred_element_type=F32)
               + lax.dot_general(xl, yh, dims, preferred_element_type=F32))
    return out


def _split_dot(x, e_bf16):
    hi = x.astype(BF16)
    lo = (x - hi.astype(F32)).astype(BF16)
    return jnp.dot(hi, e_bf16, preferred_element_type=F32) + jnp.dot(lo, e_bf16, preferred_element_type=F32)


def _rwkv_prep_kernel(z_ref, zp_ref, zn_ref, mu_ref, w0_ref, w2_ref, a0_ref, a2_ref, g2_ref, kk_ref, ka_ref,
                      rk_ref, e_ref, r_o, v_o, kk_o, g_o, bonus_o, logw_o, kd_o, bb_o):
    i = pl.program_id(1)
    last = pl.num_programs(1) - 1
    tm = z_ref.shape[0]
    hw = RWKV_WIDTH
    z = z_ref[:, S5_WIDTH:]
    row = lax.broadcasted_iota(jnp.int32, (tm, 1), 0)
    prev_halo = jnp.where(i <= 1, 0.0, zp_ref[7:8, S5_WIDTH:])
    next_halo = jnp.where((i == 0) | (i == last), 0.0, zn_ref[0:1, S5_WIDTH:])
    prev = jnp.where(row == 0, prev_halo, pltpu.roll(z, 1, 0))
    nxt = jnp.where(row == tm - 1, next_halo, pltpu.roll(z, tm - 1, 0))
    zs = z + mu_ref[0:1, :] * (prev - z) + mu_ref[1:2, :] * (nxt - z)
    c0, c1, c2, c3, c4 = RWKV_SPLITS
    r = zs[:, :c0]
    k = zs[:, c0:c1]
    v = zs[:, c1:c2]
    e = e_ref[...]
    g_o[...] = jnp.dot(jax.nn.sigmoid(zs[:, c4:]).astype(BF16), g2_ref[...], preferred_element_type=F32)
    kk = k * kk_ref[...]
    kk = kk * lax.rsqrt(_split_dot(kk * kk, e) + 1e-12)
    w_lora = jnp.dot(jnp.tanh(zs[:, c2:c3]).astype(BF16), w2_ref[...], preferred_element_type=F32)
    a_lora = jnp.dot(zs[:, c3:c4].astype(BF16), a2_ref[...], preferred_element_type=F32)
    bonus = jnp.zeros((tm, hw), F32)
    for d in range(2):
        x = w0_ref[d:d + 1, :] + w_lora[:, d * hw:(d + 1) * hw]
        softplus_neg = jnp.maximum(-x, 0.0) + jnp.log(1.0 + jnp.exp(-jnp.abs(x)))
        logw_o[d] = -jnp.exp(-softplus_neg - 0.5)
        a = jax.nn.sigmoid(a0_ref[d:d + 1, :] + a_lora[:, d * hw:(d + 1) * hw])
        kd = k * (1.0 + (a - 1.0) * ka_ref[...])
        kd_o[d] = kd
        bb_o[d] = kk * a
        bonus = bonus + _split_dot(r * kd * rk_ref[...], e) * v
    r_o[...] = r
    v_o[...] = v
    kk_o[...] = kk
    bonus_o[...] = bonus


def _rwkv_chunk_kernel(r_ref, v_ref, kk_ref, logw_ref, kd_ref, bb_ref, mn_ref, ry_ref):
    fwd = pl.program_id(1) == 0
    cn = RWKV_CHUNK
    hd = HEAD_DIM
    rowi = lax.broadcasted_iota(jnp.int32, (cn, cn), 0)
    coli = lax.broadcasted_iota(jnp.int32, (cn, cn), 1)
    lag = (rowi - coli) * jnp.where(fwd, 1, -1)
    before_incl = lag >= 0
    before = lag > 0
    eye = (rowi == coli).astype(F32)
    logw = logw_ref[...]
    lg = jnp.dot(before_incl.astype(F32), logw, precision=HI, preferred_element_type=F32)
    lg_tot = jnp.sum(logw, axis=0, keepdims=True)
    a_t = -kk_ref[...] * jnp.exp(lg - logw)
    r_t = r_ref[...] * jnp.exp(lg)
    inv = jnp.exp(-lg)
    b_t = bb_ref[...] * inv
    k_t = kd_ref[...] * inv
    to_end = jnp.exp(lg_tot - lg)
    b_h = bb_ref[...] * to_end
    k_h = kd_ref[...] * to_end
    g_end = jnp.exp(lg_tot)
    nn = (((1,), (0,)), ((), ()))
    nt = (((1,), (1,)), ((), ()))
    tn = (((0,), (0,)), ((), ()))

    heads = range(RWKV_HEADS)
    sls = [slice(h * hd, (h + 1) * hd) for h in heads]
    vs = [v_ref[:, sl] for sl in sls]
    ps = [_pdot(jnp.concatenate([a_t[:, sl], r_t[:, sl]], axis=0),
                jnp.concatenate([b_t[:, sl], k_t[:, sl]], axis=0), nt, RWKV_PREC_PAIR) for sl in sls]
    n_ab = [jnp.where(before, p[:cn, :cn], 0.0) for p in ps]
    n_ak = [jnp.where(before, p[:cn, cn:], 0.0) for p in ps]
    n_rb = [jnp.where(before_incl, p[cn:, :cn], 0.0) for p in ps]
    n_rk = [jnp.where(before_incl, p[cn:, cn:], 0.0) for p in ps]
    tinv = [eye + n for n in n_ab]
    pw = n_ab
    for _ in range(int(math.log2(cn)) - 1):
        pw = [_pdot(x, x, nn, RWKV_PREC_INV) for x in pw]
        tinv = [t + _pdot(t, x, nn, RWKV_PREC_INV) for t, x in zip(tinv, pw)]
    akv = [_pdot(n, v, nn, RWKV_PREC_OUT) for n, v in zip(n_ak, vs)]
    rkv = [_pdot(n, v, nn, RWKV_PREC_OUT) for n, v in zip(n_rk, vs)]
    kv = [_pdot(k_h[:, sl], v, tn, RWKV_PREC_OUT) for sl, v in zip(sls, vs)]
    zz = [_pdot(t, jnp.concatenate([a_t[:, sl], w], axis=1), nn, RWKV_PREC_OUT)
          for t, sl, w in zip(tinv, sls, akv)]
    for h in heads:
        sl = sls[h]
        ry_ref[h] = _pdot(n_rb[h], zz[h], nn, RWKV_PREC_OUT) + jnp.concatenate([r_t[:, sl], rkv[h]], axis=1)
        mn_ref[h] = (_pdot(b_h[:, sl], zz[h], tn, RWKV_PREC_OUT)
                     + jnp.concatenate([eye * g_end[:, sl], kv[h]], axis=1))


def _rwkv_seq_kernel(mn_ref, ry_ref, y_ref, h_ref):
    @pl.when(pl.program_id(2) == 0)
    def _():
        h_ref[...] = jnp.zeros_like(h_ref)

    fwd = pl.program_id(1) == 0
    hd = HEAD_DIM
    for cc in range(RWKV_SUPER):
        c = jnp.where(fwd, cc, RWKV_SUPER - 1 - cc)
        row0 = pl.multiple_of(c * RWKV_CHUNK, RWKV_CHUNK)
        for h in range(RWKV_HEADS):
            mn = mn_ref[c, h]
            ry = ry_ref[c, h]
            state = h_ref[h]
            y_ref[pl.ds(row0, RWKV_CHUNK), h * hd:(h + 1) * hd] = (
                jnp.dot(ry[:, :hd], state, precision=HI, preferred_element_type=F32) + ry[:, hd:])
            h_ref[h] = jnp.dot(mn[:, :hd], state, precision=HI, preferred_element_type=F32) + mn[:, hd:]


def _rwkv_post_kernel(y0_ref, y1_ref, bonus_ref, g_ref, gnw_ref, gnb_ref, e_ref, o_ref):
    y = y0_ref[...] + y1_ref[...]
    e = e_ref[...]
    mu = _split_dot(y, e) * (1.0 / HEAD_DIM)
    yc = y - mu
    var = _split_dot(yc * yc, e) * (1.0 / HEAD_DIM)
    yn = yc * lax.rsqrt(var + GN_EPS) * gnw_ref[...] + gnb_ref[...]
    o_ref[...] = (yn + bonus_ref[...]) * g_ref[...]


def _block_diag2(w):
    z = jnp.zeros_like(w[0])
    return jnp.concatenate([jnp.concatenate([w[0], z], 1), jnp.concatenate([z, w[1]], 1)], 0)


def _rwkv_mixer(z, shift_mu, w0, w2, a0, a2, g2, k_k, k_a, r_k, gn_w, gn_b):
    bsz, s, zw = z.shape
    hw = RWKV_WIDTH
    tm = ROW_BLOCK
    nblk = s // tm
    head_of = np.arange(hw) // HEAD_DIM
    e = jnp.asarray(head_of[:, None] == head_of[None, :], BF16)
    tok = pl.BlockSpec((None, tm, hw), lambda b, i: (b, i, 0))
    tok2 = pl.BlockSpec((None, 2, tm, hw), lambda b, i: (b, 0, i, 0))
    halo = tm // 8

    def full(shape):
        return pl.BlockSpec(shape, lambda b, i: (0,) * len(shape))

    r, v, kk, g, bonus, logw, kd, bb = pl.pallas_call(
        _rwkv_prep_kernel,
        grid=(bsz, nblk),
        in_specs=[pl.BlockSpec((None, tm, zw), lambda b, i: (b, i, 0)),
                  pl.BlockSpec((None, 8, zw), lambda b, i: (b, jnp.maximum(i * halo - 1, 0), 0)),
                  pl.BlockSpec((None, 8, zw), lambda b, i: (b, jnp.minimum((i + 1) * halo, s // 8 - 1), 0)),
                  full((2, RWKV_IN)), full((2, hw)), full((2 * RWKV_LORA_W, 2 * hw)), full((2, hw)),
                  full((2 * RWKV_LORA_A, 2 * hw)), full((RWKV_LORA_G, hw)), full((1, hw)), full((1, hw)),
                  full((1, hw)), full((hw, hw))],
        out_specs=[tok, tok, tok, tok, tok, tok2, tok2, tok2],
        out_shape=[jax.ShapeDtypeStruct((bsz, s, hw), F32)] * 5 + [jax.ShapeDtypeStruct((bsz, 2, s, hw), F32)] * 3,
        compiler_params=_cparams("parallel", "parallel"),
        name="rwkv_prep",
    )(z, z, z, shift_mu, w0, _block_diag2(w2).astype(BF16), a0, _block_diag2(a2).astype(BF16), g2.astype(BF16),
      k_k.reshape(1, hw), k_a.reshape(1, hw), r_k.reshape(1, hw), e)

    nch = s // RWKV_CHUNK
    ctok = pl.BlockSpec((None, RWKV_CHUNK, hw), lambda b, d, c: (b, c, 0))
    ctok2 = pl.BlockSpec((None, None, RWKV_CHUNK, hw), lambda b, d, c: (b, d, c, 0))
    mat_shape = jax.ShapeDtypeStruct((bsz, 2, nch, RWKV_HEADS, RWKV_CHUNK, 2 * HEAD_DIM), F32)
    mat_spec = pl.BlockSpec((None, None, None, RWKV_HEADS, RWKV_CHUNK, 2 * HEAD_DIM),
                            lambda b, d, c: (b, d, c, 0, 0, 0))
    mn, ry = pl.pallas_call(
        _rwkv_chunk_kernel,
        grid=(bsz, 2, nch),
        in_specs=[ctok, ctok, ctok, ctok2, ctok2, ctok2],
        out_specs=[mat_spec, mat_spec],
        out_shape=[mat_shape, mat_shape],
        compiler_params=_cparams("parallel", "parallel", "parallel"),
        name="rwkv_chunk",
    )(r, v, kk, logw, kd, bb)

    nsb = nch // RWKV_SUPER

    def sb_of(d, j):
        return jnp.where(d == 0, j, jnp.where(j == 0, 0, nsb - j))

    sup_shape = (bsz, 2, nsb, RWKV_SUPER, RWKV_HEADS, RWKV_CHUNK, 2 * HEAD_DIM)
    sup_spec = pl.BlockSpec((None, None, None, RWKV_SUPER, RWKV_HEADS, RWKV_CHUNK, 2 * HEAD_DIM),
                            lambda b, d, j: (b, d, sb_of(d, j), 0, 0, 0, 0))
    y = pl.pallas_call(
        _rwkv_seq_kernel,
        grid=(bsz, 2, nsb),
        in_specs=[sup_spec, sup_spec],
        out_specs=pl.BlockSpec((None, None, tm, hw), lambda b, d, j: (b, d, sb_of(d, j), 0)),
        out_shape=jax.ShapeDtypeStruct((bsz, 2, s, hw), F32),
        scratch_shapes=[pltpu.VMEM((RWKV_HEADS, HEAD_DIM, HEAD_DIM), F32)],
        compiler_params=_cparams("parallel", "parallel", "arbitrary"),
        name="rwkv_seq",
    )(mn.reshape(sup_shape), ry.reshape(sup_shape))

    vec = pl.BlockSpec((1, hw), lambda b, i: (0, 0))
    return pl.pallas_call(
        _rwkv_post_kernel,
        grid=(bsz, nblk),
        in_specs=[pl.BlockSpec((None, None, tm, hw), lambda b, i: (b, 0, i, 0)),
                  pl.BlockSpec((None, None, tm, hw), lambda b, i: (b, 1, i, 0)),
                  tok, tok, vec, vec, pl.BlockSpec((hw, hw), lambda b, i: (0, 0))],
        out_specs=tok,
        out_shape=jax.ShapeDtypeStruct((bsz, s, hw), F32),
        compiler_params=_cparams("parallel", "parallel"),
        name="rwkv_post",
    )(y, y, bonus, g, gn_w.reshape(1, hw), gn_b.reshape(1, hw), e)


def kernel(x, c, ctx, c_ctx,
           l0_w_mod, l0_b_mod, l0_w_in, l0_s5_lam_re, l0_s5_lam_im, l0_s5_log_dt, l0_s5_b_re, l0_s5_b_im,
           l0_s5_c_re, l0_s5_c_im, l0_s5_d, l0_w_glu, l0_b_glu, l0_rwkv_shift, l0_rwkv_w0, l0_rwkv_w2,
           l0_rwkv_a0, l0_rwkv_a2, l0_rwkv_g2, l0_rwkv_k_k, l0_rwkv_k_a, l0_rwkv_r_k, l0_rwkv_gn_w,
           l0_rwkv_gn_b, l0_w_out, l0_ln1_g, l0_ln1_b, l0_peer_wq, l0_peer_keys, l0_peer_u, l0_peer_v,
           l0_ln2_g, l0_ln2_b,
           l1_w_mod, l1_b_mod, l1_w_in, l1_sink, l1_rpb, l1_w_out, l1_ln1_g, l1_ln1_b, l1_peer_wq,
           l1_peer_keys, l1_peer_u, l1_peer_v, l1_ln2_g, l1_ln2_b):
    L = ctx.shape[1]
    h = jnp.concatenate([ctx, x], 1)

    tab = _mod_tables(c, c_ctx, l0_w_mod, l0_b_mod)
    z = _inproj(h, tab, 0, 1, l0_w_in.astype(BF16))
    ya = _s5_mixer(z[..., :S5_WIDTH], L, l0_s5_lam_re, l0_s5_lam_im, l0_s5_log_dt, l0_s5_b_re, l0_s5_b_im,
                   l0_s5_c_re, l0_s5_c_im, l0_s5_d, l0_w_glu, l0_b_glu)
    yb = _rwkv_mixer(z, l0_rwkv_shift, l0_rwkv_w0, l0_rwkv_w2, l0_rwkv_a0, l0_rwkv_a2,
                     l0_rwkv_g2, l0_rwkv_k_k, l0_rwkv_k_a, l0_rwkv_r_k, l0_rwkv_gn_w, l0_rwkv_gn_b)
    h = _outproj_ln(jnp.concatenate([ya, yb], -1), h, tab, 2, l0_w_out.astype(BF16), l0_ln1_g, l0_ln1_b)
    h = _peer_ln(h, tab, 0, l0_peer_wq, l0_peer_keys, l0_peer_u, l0_peer_v, l0_ln2_g, l0_ln2_b)

    tab = _mod_tables(c, c_ctx, l1_w_mod, l1_b_mod)
    z = _inproj(h, tab, 0, 1, l1_w_in.astype(BF16))
    qc, kc, vc, qd, kd, vd = _attn_prep(z, L)
    y = jnp.concatenate([_window_gqa(qc, kc, vc, l1_sink, L), _neighbourhood_attention(qd, kd, vd, l1_rpb, L)], -1)
    y = jnp.concatenate([jnp.zeros((y.shape[0], L, y.shape[2]), F32), y], 1)
    h = _outproj_ln(y, h, tab, 2, l1_w_out.astype(BF16), l1_ln1_g, l1_ln1_b)
    return _peer_ln(h, tab, L // ROW_BLOCK, l1_peer_wq, l1_peer_keys, l1_peer_u, l1_peer_v, l1_ln2_g, l1_ln2_b)
```

```python
import functools
import math

import jax
import jax.numpy as jnp
import numpy as np
from jax import lax
from jax.experimental import pallas as pl
from jax.experimental.pallas import tpu as pltpu

F32 = jnp.float32
BF16 = jnp.bfloat16

D_MODEL = 1024
DEPTH = 2
GRID_W = 64
CTX_LEN = 256
HEAD_DIM = 64
S5_GROUP_CH = 16
S5_GROUPS = 16
S5_WIDTH = S5_GROUPS * S5_GROUP_CH
S5_STATE = 64
RWKV_HEADS = 8
RWKV_WIDTH = RWKV_HEADS * HEAD_DIM
RWKV_LORA_W = 64
RWKV_LORA_A = 64
RWKV_LORA_G = 128
RWKV_IN = 3 * RWKV_WIDTH + 2 * RWKV_LORA_W + 2 * RWKV_LORA_A + RWKV_LORA_G
RWKV_SPLITS = (RWKV_WIDTH, 2 * RWKV_WIDTH, 3 * RWKV_WIDTH,
               3 * RWKV_WIDTH + 2 * RWKV_LORA_W,
               3 * RWKV_WIDTH + 2 * RWKV_LORA_W + 2 * RWKV_LORA_A)
WIN_HEADS = 8
WIN_KV_HEADS = 2
WINDOW = 128
WIN_BLOCK = 128
NA_HEADS = 8
NA_ROWS = 8
NA_COLS = 16
ODD_SPLITS = (WIN_HEADS * HEAD_DIM, (WIN_HEADS + WIN_KV_HEADS) * HEAD_DIM,
              (WIN_HEADS + 2 * WIN_KV_HEADS) * HEAD_DIM,
              (WIN_HEADS + 2 * WIN_KV_HEADS + NA_HEADS) * HEAD_DIM,
              (WIN_HEADS + 2 * WIN_KV_HEADS + 2 * NA_HEADS) * HEAD_DIM)
ROPE_BASE = 10000.0
PEER_HEADS = 8
PEER_KEYS = 128
PEER_DK = 128
PEER_TOPK = 16
PEER_BLOCK = 128
ALPHA = (2.0 * DEPTH) ** 0.25
LN_EPS = 1e-5
GN_EPS = 64e-5

VMEM_LIMIT_BYTES = 56 * 1024 * 1024
ROW_BLOCK = 256


def _cparams(*sem):
    return pltpu.CompilerParams(dimension_semantics=sem, vmem_limit_bytes=VMEM_LIMIT_BYTES)


def _modulation_kernel(c_ref, w_ref, b_ref, o_ref):
    c = c_ref[...]
    s = c * jax.nn.sigmoid(c)
    o_ref[...] = jnp.dot(s.astype(BF16), w_ref[...].astype(BF16), preferred_element_type=F32) + b_ref[...]


def _modulation(cond, w_mod, b_mod):
    n, d = cond.shape
    nout = w_mod.shape[1]
    tn = 1536
    return pl.pallas_call(
        _modulation_kernel,
        grid=(nout // tn,),
        in_specs=[pl.BlockSpec((n, d), lambda j: (0, 0)),
                  pl.BlockSpec((d, tn), lambda j: (0, j)),
                  pl.BlockSpec((1, tn), lambda j: (0, j))],
        out_specs=pl.BlockSpec((n, tn), lambda j: (0, j)),
        out_shape=jax.ShapeDtypeStruct((n, nout), F32),
        compiler_params=_cparams("arbitrary"),
        name="modulation",
    )(cond, w_mod, b_mod.reshape(1, nout))


def _mod_tables(c, c_ctx, w_mod, b_mod):
    bsz = c.shape[0]
    cond = jnp.concatenate([c, c_ctx[None, :], jnp.zeros((8 - bsz - 1, c.shape[1]), F32)], 0)
    m = _modulation(cond, w_mod, b_mod)
    mx = m[:bsz].reshape(bsz, 1, 6, 1, D_MODEL)
    mc = jnp.broadcast_to(m[bsz].reshape(1, 1, 6, 1, D_MODEL), (bsz, 1, 6, 1, D_MODEL))
    return jnp.concatenate([mc, mx], 1)


def _tab_spec(which):
    return pl.BlockSpec((None, None, None, 1, D_MODEL),
                        lambda b, i: (b, jnp.minimum(i, 1), which, 0, 0))


def _inproj_kernel(x_ref, shift_ref, scale_ref, w_ref, o_ref):
    hm = x_ref[...] * (1.0 + scale_ref[...]) + shift_ref[...]
    o_ref[...] = jnp.dot(hm.astype(BF16), w_ref[...], preferred_element_type=F32)


def _inproj(h, tab, which_shift, which_scale, w_bf16):
    bsz, s, d = h.shape
    n = w_bf16.shape[1]
    return pl.pallas_call(
        _inproj_kernel,
        grid=(bsz, s // ROW_BLOCK),
        in_specs=[pl.BlockSpec((None, ROW_BLOCK, d), lambda b, i: (b, i, 0)),
                  _tab_spec(which_shift), _tab_spec(which_scale),
                  pl.BlockSpec((d, n), lambda b, i: (0, 0))],
        out_specs=pl.BlockSpec((None, ROW_BLOCK, n), lambda b, i: (b, i, 0)),
        out_shape=jax.ShapeDtypeStruct((bsz, s, n), F32),
        compiler_params=_cparams("parallel", "parallel"),
        name="inproj",
    )(h, tab, tab, w_bf16)


def _layer_norm_rows(z, g, b):
    mu = jnp.mean(z, -1, keepdims=True)
    zc = z - mu
    var = jnp.mean(zc * zc, -1, keepdims=True)
    return zc * lax.rsqrt(var + LN_EPS) * g + b


def _outproj_ln_kernel(ya_ref, yb_ref, h_ref, gate_ref, wa_ref, wb_ref, g_ref, b_ref, o_ref):
    proj = (jnp.dot(ya_ref[...].astype(BF16), wa_ref[...], preferred_element_type=F32)
            + jnp.dot(yb_ref[...].astype(BF16), wb_ref[...], preferred_element_type=F32))
    z = ALPHA * h_ref[...] + gate_ref[...] * proj
    o_ref[...] = _layer_norm_rows(z, g_ref[...], b_ref[...])


def _outproj_ln(ya, yb, h, tab, which_gate, w_out, ln_g, ln_b, row_off):
    bsz, rows, ka = ya.shape
    kb = yb.shape[-1]
    d = h.shape[-1]
    return pl.pallas_call(
        _outproj_ln_kernel,
        grid=(bsz, rows // ROW_BLOCK),
        in_specs=[pl.BlockSpec((None, ROW_BLOCK, ka), lambda b, i: (b, i, 0)),
                  pl.BlockSpec((None, ROW_BLOCK, kb), lambda b, i: (b, i, 0)),
                  pl.BlockSpec((None, ROW_BLOCK, d), lambda b, i: (b, i + row_off, 0)),
                  pl.BlockSpec((None, None, None, 1, d), lambda b, i: (b, jnp.minimum(i + row_off, 1), which_gate, 0, 0)),
                  pl.BlockSpec((ka, d), lambda b, i: (0, 0)),
                  pl.BlockSpec((kb, d), lambda b, i: (0, 0)),
                  pl.BlockSpec((1, d), lambda b, i: (0, 0)),
                  pl.BlockSpec((1, d), lambda b, i: (0, 0))],
        out_specs=pl.BlockSpec((None, ROW_BLOCK, d), lambda b, i: (b, i, 0)),
        out_shape=jax.ShapeDtypeStruct((bsz, rows, d), F32),
        compiler_params=_cparams("parallel", "parallel"),
        name="outproj_ln",
    )(ya, yb, h, tab, w_out[:ka].astype(BF16), w_out[ka:].astype(BF16), ln_g.reshape(1, d), ln_b.reshape(1, d))


PEER_NEG = -3.0e38
PEER_ETILE = 1024
PEER_SUBTILE = 256
SQRT_HALF = 0.7071067811865476


PEER_NO_RANK = 255.0


def _top_values(s, n, with_rank=False):
    vals = []
    rank = jnp.full(s.shape, PEER_NO_RANK, F32) if with_rank else None
    for k in range(n):
        m = jnp.max(s, axis=0, keepdims=True)
        vals.append(m)
        hit = s == m
        if with_rank:
            rank = jnp.where(hit, float(k), rank)
        s = jnp.where(hit, PEER_NEG, s)
    return (vals, rank) if with_rank else vals


def _peer_route_kernel(h_ref, shift_ref, scale_ref, wq_ref, keys_ref,
                       pin_ref, cnt_ref, e1_ref, r2_ref, e2_ref):
    pin = (h_ref[...] * (1.0 + scale_ref[...]) + shift_ref[...]).astype(BF16)
    pin_ref[...] = pin
    q = jnp.dot(pin, wq_ref[...], preferred_element_type=F32).astype(BF16)
    st = lax.dot_general(keys_ref[...], q, (((1,), (1,)), ((), ())), preferred_element_type=F32)
    for hd in range(PEER_HEADS):
        base = hd * 2 * PEER_KEYS
        s1 = st[base:base + PEER_KEYS]
        s2 = st[base + PEER_KEYS:base + 2 * PEER_KEYS]
        v1, rank1 = _top_values(s1, PEER_TOPK, with_rank=True)
        v2, rank2 = _top_values(s2, PEER_TOPK, with_rank=True)
        v2all = jnp.concatenate(v2, axis=0)
        sums = [v1[a] + v2all for a in range(PEER_TOPK)]
        frontier = [sums[a][:(PEER_TOPK + 1) // (a + 1)] for a in range(PEER_TOPK)]
        top = _top_values(jnp.concatenate(frontier, axis=0), PEER_TOPK + 1)
        tau = 0.5 * (top[PEER_TOPK - 1] + top[PEER_TOPK])
        zsum = top[0] * 0.0
        for kk in range(PEER_TOPK):
            zsum = zsum + jnp.exp(top[kk] - top[0])
        cnt = jnp.zeros_like(s1)
        for a in range(PEER_TOPK):
            n_ok = jnp.sum(jnp.where(sums[a] >= tau, 1.0, 0.0), axis=0, keepdims=True)
            cnt = jnp.where(rank1 == float(a), n_ok, cnt)
        cnt_ref[hd] = cnt
        e1_ref[hd] = jnp.exp(s1 - v1[0])
        r2_ref[hd] = rank2.astype(BF16)
        e2_ref[hd] = (jnp.exp(s2 - v2[0]) / zsum).astype(BF16)


def _peer_dense_kernel(pin_ref, cnt_ref, e1_ref, r2_ref, e2_ref, u_ref, vt_ref,
                       h_ref, gate_ref, g_ref, b_ref, o_ref, acc_ref, w_ref):
    j = pl.program_id(2)

    last = pl.num_programs(2) - 1

    def apply_values(first):
        part = jnp.dot(vt_ref[...], w_ref[(j + 1) % 2], preferred_element_type=F32)
        if first:
            acc_ref[...] = part
        else:
            acc_ref[...] += part

    def score_and_gate():
        pin = pin_ref[...]
        cur = j % 2
        for sb in range(PEER_ETILE // PEER_SUBTILE):
            es = slice(sb * PEER_SUBTILE, (sb + 1) * PEER_SUBTILE)
            at = lax.dot_general(u_ref[es, :], pin, (((1,), (1,)), ((), ())), preferred_element_type=F32)
            for i2 in range(PEER_SUBTILE // PEER_KEYS):
                ii = sb * (PEER_SUBTILE // PEER_KEYS) + i2
                a = at[i2 * PEER_KEYS:(i2 + 1) * PEER_KEYS]
                act = (0.5 * a * (1.0 + lax.erf(a * SQRT_HALF))).astype(BF16)
                g = jnp.zeros(a.shape, BF16)
                for hd in range(PEER_HEADS):
                    cnt_row = cnt_ref[hd, pl.ds(ii, 1), :].astype(BF16)
                    c_row = e1_ref[hd, pl.ds(ii, 1), :].astype(BF16)
                    g = g + jnp.where(r2_ref[hd] < cnt_row, e2_ref[hd] * c_row, jnp.zeros((), BF16))
                w_ref[cur, ii * PEER_KEYS:(ii + 1) * PEER_KEYS, :] = g * act

    @pl.when(j == 0)
    def _():
        score_and_gate()

    @pl.when(j == 1)
    def _():
        apply_values(first=True)
        score_and_gate()

    @pl.when((j > 1) & (j < last))
    def _():
        apply_values(first=False)
        score_and_gate()

    @pl.when(j == last)
    def _():
        apply_values(first=False)
        z = ALPHA * h_ref[...] + gate_ref[...] * acc_ref[...].T
        o_ref[...] = _layer_norm_rows(z, g_ref[...], b_ref[...])


def _peer_block_diag_keys(sub_keys):
    hh, two, nk, dh = sub_keys.shape
    eye = jnp.eye(hh * two, dtype=F32)
    kb = sub_keys.reshape(hh * two, nk, dh)
    return jnp.einsum('akd,ab->akbd', kb, eye).reshape(hh * two * nk, hh * two * dh)


def _peer_ln(h, tab, seg_off, wq, sub_keys, u_tab, v_tab, ln_g, ln_b):
    bsz, s, d = h.shape
    tm = ROW_BLOCK
    nblk = s // tm
    hk = PEER_HEADS * 2 * PEER_KEYS
    keys_bd = _peer_block_diag_keys(sub_keys).astype(BF16)

    def tab_spec3(which):
        return pl.BlockSpec((None, None, None, 1, d),
                            lambda b, i, *_: (b, jnp.minimum(i + seg_off, 1), which, 0, 0))

    route_dims = (bsz, nblk, PEER_HEADS, PEER_KEYS, tm)
    route_spec = pl.BlockSpec((None, None, PEER_HEADS, PEER_KEYS, tm), lambda b, i: (b, i, 0, 0, 0))
    pin, cnt, e1, r2, e2 = pl.pallas_call(
        _peer_route_kernel,
        grid=(bsz, nblk),
        in_specs=[pl.BlockSpec((None, tm, d), lambda b, i: (b, i, 0)),
                  tab_spec3(3), tab_spec3(4),
                  pl.BlockSpec((d, PEER_HEADS * PEER_DK), lambda b, i: (0, 0)),
                  pl.BlockSpec((hk, PEER_HEADS * PEER_DK), lambda b, i: (0, 0))],
        out_specs=[pl.BlockSpec((None, tm, d), lambda b, i: (b, i, 0)),
                   route_spec, route_spec, route_spec, route_spec],
        out_shape=[jax.ShapeDtypeStruct((bsz, nblk * tm, d), BF16),
                   jax.ShapeDtypeStruct(route_dims, F32), jax.ShapeDtypeStruct(route_dims, F32),
                   jax.ShapeDtypeStruct(route_dims, BF16), jax.ShapeDtypeStruct(route_dims, BF16)],
        compiler_params=_cparams("parallel", "parallel"),
        name="peer_route",
    )(h, tab, tab, wq.astype(BF16), keys_bd)

    n_exp = u_tab.shape[0]
    rows_per_step = PEER_ETILE // PEER_KEYS
    full_spec = pl.BlockSpec((None, None, PEER_HEADS, PEER_KEYS, tm), lambda b, i, j: (b, i, 0, 0, 0))
    n_tiles = n_exp // PEER_ETILE
    row_spec = pl.BlockSpec((None, None, PEER_HEADS, rows_per_step, tm),
                            lambda b, i, j: (b, i, 0, jnp.minimum(j, n_tiles - 1), 0))
    return pl.pallas_call(
        _peer_dense_kernel,
        grid=(bsz, nblk, n_tiles + 1),
        in_specs=[pl.BlockSpec((None, tm, d), lambda b, i, j: (b, i, 0)),
                  row_spec, row_spec, full_spec, full_spec,
                  pl.BlockSpec((PEER_ETILE, d), lambda b, i, j: (jnp.minimum(j, n_tiles - 1), 0)),
                  pl.BlockSpec((d, PEER_ETILE), lambda b, i, j: (0, jnp.maximum(j - 1, 0))),
                  pl.BlockSpec((None, tm, d), lambda b, i, j: (b, i, 0)),
                  tab_spec3(5),
                  pl.BlockSpec((1, d), lambda b, i, j: (0, 0)),
                  pl.BlockSpec((1, d), lambda b, i, j: (0, 0))],
        out_specs=pl.BlockSpec((None, tm, d), lambda b, i, j: (b, i, 0)),
        out_shape=jax.ShapeDtypeStruct((bsz, nblk * tm, d), F32),
        scratch_shapes=[pltpu.VMEM((d, tm), F32), pltpu.VMEM((2, PEER_ETILE, tm), BF16)],
        compiler_params=_cparams("parallel", "parallel", "arbitrary"),
        name="peer_dense",
    )(pin, cnt, e1, r2, e2, u_tab.astype(BF16), v_tab.T.astype(BF16), h, tab,
      ln_g.reshape(1, d), ln_b.reshape(1, d))


S5_CHAINS = 8
S5_TCHUNK = 128
S5_NSTATE = S5_GROUPS * S5_STATE


def _s5_scan_kernel(u_ref, wb_ref, wc_ref, a_ref, y_ref, bre_ref, bim_ref, h_ref):
    @pl.when(pl.program_id(0) == 0)
    def _():
        h_ref[...] = jnp.zeros_like(h_ref)

    rows = u_ref.shape[0]
    u = u_ref[...].astype(BF16)
    is_bwd = (lax.broadcasted_iota(jnp.int32, (rows, 1), 0) % S5_CHAINS) >= (S5_CHAINS // 2)

    def proj(w_fwd, w_bwd, x):
        return jnp.where(is_bwd, jnp.dot(x, w_bwd, preferred_element_type=F32),
                         jnp.dot(x, w_fwd, preferred_element_type=F32))

    bre_ref[...] = proj(wb_ref[0], wb_ref[2], u)
    bim_ref[...] = proj(wb_ref[1], wb_ref[3], u)
    a_re = a_ref[0]
    a_im = a_ref[1]

    def step(t, carry):
        h_re, h_im = carry
        off = pl.multiple_of(t * S5_CHAINS, S5_CHAINS)
        n_re = a_re * h_re - a_im * h_im + bre_ref[pl.ds(off, S5_CHAINS), :]
        n_im = a_re * h_im + a_im * h_re + bim_ref[pl.ds(off, S5_CHAINS), :]
        bre_ref[pl.ds(off, S5_CHAINS), :] = n_re
        bim_ref[pl.ds(off, S5_CHAINS), :] = n_im
        return n_re, n_im

    h_re, h_im = lax.fori_loop(0, rows // S5_CHAINS, step, (h_ref[0], h_ref[1]), unroll=2)
    h_ref[0] = h_re
    h_ref[1] = h_im
    hre = bre_ref[...].astype(BF16)
    him = bim_ref[...].astype(BF16)
    y_ref[...] = proj(wc_ref[0], wc_ref[2], hre) - proj(wc_ref[1], wc_ref[3], him)


def _s5_glu_kernel(u_ref, y0_ref, y1_ref, d_ref, w_ref, b_ref, o_ref):
    y = u_ref[...] * d_ref[...] + y0_ref[...] + y1_ref[...]
    y = 0.5 * y * (1.0 + lax.erf(y * SQRT_HALF))
    gate = jnp.dot(y.astype(BF16), w_ref[...], preferred_element_type=F32) + b_ref[...]
    o_ref[...] = y * jax.nn.sigmoid(gate)


def _dir_view(z, n_ctx, d):
    if d == 0:
        return z
    return jnp.concatenate([jnp.flip(z[:, :n_ctx], 1), jnp.flip(z[:, n_ctx:], 1)], 1)


def _s5_params(lam_re, lam_im, log_dt, b_re, b_im, c_re, c_im):
    eye = jnp.eye(S5_GROUPS, dtype=F32)
    wb, wc, a_rows = [], [], [[], []]
    for d in range(2):
        lr, li = lam_re[d], lam_im[d]
        dt = jnp.exp(log_dt[d])[:, None]
        mag = jnp.exp(lr * dt)
        ab_re, ab_im = mag * jnp.cos(li * dt), mag * jnp.sin(li * dt)
        den = lr * lr + li * li
        nr = ab_re - 1.0
        coef_re = (nr * lr + ab_im * li) / den
        coef_im = (ab_im * lr - nr * li) / den
        bb_re = coef_re[..., None] * b_re[d] - coef_im[..., None] * b_im[d]
        bb_im = coef_re[..., None] * b_im[d] + coef_im[..., None] * b_re[d]
        for bb in (bb_re, bb_im):
            wb.append(jnp.einsum('gpi,gh->gihp', bb, eye).reshape(S5_WIDTH, S5_NSTATE))
        for cc in (c_re[d], c_im[d]):
            wc.append(jnp.einsum('gip,gh->gphi', cc, eye).reshape(S5_NSTATE, S5_WIDTH))
        for k, ab in enumerate((ab_re, ab_im)):
            a_rows[k].append(jnp.broadcast_to(ab.reshape(1, S5_NSTATE), (S5_CHAINS // 2, S5_NSTATE)))
    a_tiles = jnp.stack([jnp.concatenate(a_rows[0], 0), jnp.concatenate(a_rows[1], 0)])
    return jnp.stack(wb).astype(BF16), jnp.stack(wc).astype(BF16), a_tiles


def _s5_mixer(u, n_ctx, lam_re, lam_im, log_dt, b_re, b_im, c_re, c_im, d_skip, w_glu, b_glu):
    bsz, s, w = u.shape
    assert 2 * bsz == S5_CHAINS
    wb, wc, a_tiles = _s5_params(lam_re, lam_im, log_dt, b_re, b_im, c_re, c_im)
    chain = jnp.concatenate([jnp.transpose(u, (1, 0, 2)), jnp.transpose(_dir_view(u, n_ctx, 1), (1, 0, 2))], 1)
    rows = S5_TCHUNK * S5_CHAINS
    y = pl.pallas_call(
        _s5_scan_kernel,
        grid=(s // S5_TCHUNK,),
        in_specs=[pl.BlockSpec((rows, w), lambda i: (i, 0)),
                  pl.BlockSpec((4, w, S5_NSTATE), lambda i: (0, 0, 0)),
                  pl.BlockSpec((4, S5_NSTATE, w), lambda i: (0, 0, 0)),
                  pl.BlockSpec((2, S5_CHAINS, S5_NSTATE), lambda i: (0, 0, 0))],
        out_specs=pl.BlockSpec((rows, w), lambda i: (i, 0)),
        out_shape=jax.ShapeDtypeStruct((s * S5_CHAINS, w), F32),
        scratch_shapes=[pltpu.VMEM((rows, S5_NSTATE), F32), pltpu.VMEM((rows, S5_NSTATE), F32),
                        pltpu.VMEM((2, S5_CHAINS, S5_NSTATE), F32)],
        compiler_params=_cparams("arbitrary"),
        name="s5_scan",
    )(chain.reshape(s * S5_CHAINS, w), wb, wc, a_tiles)
    y = y.reshape(s, S5_CHAINS, w)
    y_fwd = jnp.transpose(y[:, :bsz], (1, 0, 2))
    y_bwd = _dir_view(jnp.transpose(y[:, bsz:], (1, 0, 2)), n_ctx, 1)
    tok = pl.BlockSpec((None, ROW_BLOCK, w), lambda b, i: (b, i, 0))
    vec = pl.BlockSpec((1, w), lambda b, i: (0, 0))
    return pl.pallas_call(
        _s5_glu_kernel,
        grid=(bsz, s // ROW_BLOCK),
        in_specs=[tok, tok, tok, vec, pl.BlockSpec((w, w), lambda b, i: (0, 0)), vec],
        out_specs=tok,
        out_shape=jax.ShapeDtypeStruct((bsz, s, w), F32),
        compiler_params=_cparams("parallel", "parallel"),
        name="s5_glu",
    )(u, y_fwd, y_bwd, d_skip.reshape(1, w), w_glu.astype(BF16), b_glu.reshape(1, w))


ATT_SCALE = HEAD_DIM ** -0.5
ATT_NEG = -1e30
ROPE_QUARTER = HEAD_DIM // 4
QC_W = WIN_HEADS * HEAD_DIM
KC_W = WIN_KV_HEADS * HEAD_DIM
QD_W = NA_HEADS * HEAD_DIM


def _rope_tables(n_ctx, t):
    pos = np.arange(t)
    rowcol = np.stack([pos // GRID_W, pos % GRID_W], 1).astype(np.float32)
    lane = np.arange(QC_W)
    j = lane % HEAD_DIM
    freqs = (ROPE_BASE ** (-np.arange(ROPE_QUARTER, dtype=np.float32) / ROPE_QUARTER)).astype(np.float32)
    ang = rowcol[:, j // (HEAD_DIM // 2)] * freqs[j % ROPE_QUARTER][None, :]
    ang = jnp.asarray(ang, F32)
    sign = np.where((j % (HEAD_DIM // 2)) < ROPE_QUARTER, -1.0, 1.0).astype(np.float32)
    cos = jnp.concatenate([jnp.ones((n_ctx, QC_W), F32), jnp.cos(ang)], 0)
    sin = jnp.concatenate([jnp.zeros((n_ctx, QC_W), F32), jnp.sin(ang) * sign[None, :]], 0)
    return cos, sin


def _rope(x, cos, sin):
    n = x.shape[-1]
    lane = lax.broadcasted_iota(jnp.int32, (1, n), 1)
    first = (lane % (HEAD_DIM // 2)) < ROPE_QUARTER
    partner = jnp.where(first, pltpu.roll(x, n - ROPE_QUARTER, 1), pltpu.roll(x, ROPE_QUARTER, 1))
    return x * cos + partner * sin


def _inproj_attn_kernel(x_ref, shift_ref, scale_ref, w_ref, cos_ref, sin_ref,
                        qc_ref, kc_ref, vc_ref, qd_ref, kd_ref, vd_ref):
    hm = x_ref[...] * (1.0 + scale_ref[...]) + shift_ref[...]
    z = jnp.dot(hm.astype(BF16), w_ref[...], preferred_element_type=F32)
    c0, c1, c2, c3, c4 = ODD_SPLITS
    cos = cos_ref[...]
    sin = sin_ref[...]
    qc_ref[...] = (_rope(z[:, 0:c0], cos, sin) * ATT_SCALE).astype(BF16)
    kc_ref[...] = _rope(z[:, c0:c1], cos[:, :KC_W], sin[:, :KC_W]).astype(BF16)
    vc_ref[...] = z[:, c1:c2].astype(BF16)
    qd_ref[...] = (z[:, c2:c3] * ATT_SCALE).astype(BF16)
    kd_ref[...] = z[:, c3:c4].astype(BF16)
    vd_ref[...] = z[:, c4:].astype(BF16)


def _inproj_attn(h, tab, which_shift, which_scale, w_bf16, n_ctx):
    bsz, s, d = h.shape
    n = w_bf16.shape[1]
    cos, sin = _rope_tables(n_ctx, s - n_ctx)
    widths = (QC_W, KC_W, KC_W, QD_W, QD_W, QD_W)
    return pl.pallas_call(
        _inproj_attn_kernel,
        grid=(bsz, s // ROW_BLOCK),
        in_specs=[pl.BlockSpec((None, ROW_BLOCK, d), lambda b, i: (b, i, 0)),
                  _tab_spec(which_shift), _tab_spec(which_scale),
                  pl.BlockSpec((d, n), lambda b, i: (0, 0)),
                  pl.BlockSpec((ROW_BLOCK, QC_W), lambda b, i: (i, 0)),
                  pl.BlockSpec((ROW_BLOCK, QC_W), lambda b, i: (i, 0))],
        out_specs=[pl.BlockSpec((None, ROW_BLOCK, w), lambda b, i: (b, i, 0)) for w in widths],
        out_shape=[jax.ShapeDtypeStruct((bsz, s, w), BF16) for w in widths],
        compiler_params=_cparams("parallel", "parallel"),
        name="inproj_attn",
    )(h, tab, tab, w_bf16, cos, sin)


def _softmax_pv(s, v, extra_logit=None):
    m = jnp.max(s, axis=1, keepdims=True)
    if extra_logit is not None:
        m = jnp.maximum(m, extra_logit)
    p = jnp.exp(s - m)
    den = jnp.sum(p, axis=1, keepdims=True)
    if extra_logit is not None:
        den = den + jnp.exp(extra_logit - m)
    return jnp.dot(p.astype(BF16), v, preferred_element_type=F32) / den


def _window_kernel(q_ref, kp_ref, kc_ref, kn_ref, kx_ref, vp_ref, vc_ref, vn_ref, vx_ref, sink_ref, o_ref, *, t_len):
    i = pl.program_id(1)
    wb = WIN_BLOCK
    qi = lax.broadcasted_iota(jnp.int32, (wb, 1), 0)
    kj = lax.broadcasted_iota(jnp.int32, (1, 3 * wb), 1) - wb
    kabs = i * wb + kj
    valid = (jnp.abs(kj - qi) <= WINDOW) & (kabs >= 0) & (kabs < t_len)
    n_ctx = kx_ref.shape[0]
    bias = jnp.concatenate([jnp.where(valid, 0.0, ATT_NEG), jnp.zeros((wb, n_ctx), F32)], axis=1)
    k_all = jnp.concatenate([kp_ref[...], kc_ref[...], kn_ref[...], kx_ref[...]], axis=0)
    v_all = jnp.concatenate([vp_ref[...], vc_ref[...], vn_ref[...], vx_ref[...]], axis=0)
    group = WIN_HEADS // WIN_KV_HEADS
    for hq in range(WIN_HEADS):
        g = hq // group
        kg = k_all[:, g * HEAD_DIM:(g + 1) * HEAD_DIM]
        vg = v_all[:, g * HEAD_DIM:(g + 1) * HEAD_DIM]
        q = q_ref[:, hq * HEAD_DIM:(hq + 1) * HEAD_DIM]
        s = lax.dot_general(q, kg, (((1,), (1,)), ((), ())), preferred_element_type=F32) + bias
        o_ref[:, hq * HEAD_DIM:(hq + 1) * HEAD_DIM] = _softmax_pv(s, vg, sink_ref[hq:hq + 1, 0:1])


def _window_gqa(qc, kc, vc, sink, n_ctx):
    bsz, s, _ = qc.shape
    t_len = s - n_ctx
    nb = t_len // WIN_BLOCK
    off = n_ctx // WIN_BLOCK

    def kv_spec(delta):
        return pl.BlockSpec((None, WIN_BLOCK, KC_W), lambda b, i: (b, jnp.clip(i + delta, 0, nb - 1) + off, 0))

    ctx_spec = pl.BlockSpec((None, n_ctx, KC_W), lambda b, i: (b, 0, 0))
    sink_tile = jnp.broadcast_to(sink.reshape(WIN_HEADS, 1), (WIN_HEADS, 128))
    return pl.pallas_call(
        functools.partial(_window_kernel, t_len=t_len),
        grid=(bsz, nb),
        in_specs=[pl.BlockSpec((None, WIN_BLOCK, QC_W), lambda b, i: (b, i + off, 0)),
                  kv_spec(-1), kv_spec(0), kv_spec(1), ctx_spec,
                  kv_spec(-1), kv_spec(0), kv_spec(1), ctx_spec,
                  pl.BlockSpec((WIN_HEADS, 128), lambda b, i: (0, 0))],
        out_specs=pl.BlockSpec((None, WIN_BLOCK, QC_W), lambda b, i: (b, i, 0)),
        out_shape=jax.ShapeDtypeStruct((bsz, t_len, QC_W), F32),
        compiler_params=_cparams("parallel", "parallel"),
        name="window_gqa",
    )(qc, kc, kc, kc, kc, vc, vc, vc, vc, sink_tile)


NA_QROWS = 2
NA_KROWS = NA_QROWS + NA_ROWS - 1


def _na_block_plan(rows):
    kh = min(NA_ROWS, rows)
    kw = min(NA_COLS, GRID_W)
    c = np.arange(GRID_W)
    cs = np.clip(c - kw // 2, 0, GRID_W - kw)
    kc = np.arange(GRID_W)
    col_valid = (kc[None, :] >= cs[:, None]) & (kc[None, :] < cs[:, None] + kw)
    dc = np.clip(kc[None, :] - c[:, None] + (NA_COLS - 1), 0, 2 * NA_COLS - 2)
    patterns, types, kbs = {}, [], []
    for r0 in range(0, rows, NA_QROWS):
        kb = int(np.clip(r0 - kh // 2, 0, rows - NA_KROWS))
        r = r0 + np.arange(NA_QROWS)
        rs = np.clip(r - kh // 2, 0, rows - kh)
        kr = kb + np.arange(NA_KROWS)
        row_valid = (kr[None, :] >= rs[:, None]) & (kr[None, :] < rs[:, None] + kh)
        dr = np.clip(kr[None, :] - r[:, None] + (NA_ROWS - 1), 0, 2 * NA_ROWS - 2)
        key = (row_valid.tobytes(), dr.tobytes())
        if key not in patterns:
            valid = row_valid[:, None, :, None] & col_valid[None, :, None, :]
            shape = valid.shape
            patterns[key] = (len(patterns), valid.reshape(NA_QROWS * GRID_W, NA_KROWS * GRID_W),
                             np.broadcast_to(dr[:, None, :, None], shape).reshape(valid.shape[0] * shape[1], -1),
                             np.broadcast_to(dc[None, :, None, :], shape).reshape(valid.shape[0] * shape[1], -1))
        types.append(patterns[key][0])
        kbs.append(kb)
    ordered = sorted(patterns.values(), key=lambda p: p[0])
    valid = np.stack([p[1] for p in ordered])
    dr = np.stack([p[2] for p in ordered])
    dc = np.stack([p[3] for p in ordered])
    return np.asarray(types, np.int32), np.asarray(kbs, np.int32), valid, dr, dc


def _na_kernel(types_ref, kbs_ref, q_ref, *refs):
    del types_ref, kbs_ref
    k_refs = refs[:NA_KROWS]
    v_refs = refs[NA_KROWS:2 * NA_KROWS]
    kx_ref, vx_ref, bias_ref, o_ref = refs[2 * NA_KROWS:]
    k_all = jnp.concatenate([r[...] for r in k_refs] + [kx_ref[...]], axis=0)
    v_all = jnp.concatenate([r[...] for r in v_refs] + [vx_ref[...]], axis=0)
    n_q = q_ref.shape[0]
    n_ctx = kx_ref.shape[0]
    zeros = jnp.zeros((n_q, n_ctx), F32)
    for hd in range(NA_HEADS):
        sl = slice(hd * HEAD_DIM, (hd + 1) * HEAD_DIM)
        s = lax.dot_general(q_ref[:, sl], k_all[:, sl], (((1,), (1,)), ((), ())), preferred_element_type=F32)
        s = s + jnp.concatenate([bias_ref[hd], zeros], axis=1)
        o_ref[:, sl] = _softmax_pv(s, v_all[:, sl])


def _neighbourhood_attention(qd, kd, vd, rpb, n_ctx):
    bsz, s, _ = qd.shape
    t_len = s - n_ctx
    rows = t_len // GRID_W
    types, kbs, valid, dr, dc = _na_block_plan(rows)
    n_types = valid.shape[0]
    dr_pat = dr.reshape(n_types, NA_QROWS, GRID_W, NA_KROWS, GRID_W)[:, :, 0, :, 0]
    dc_pat = dc.reshape(n_types, NA_QROWS, GRID_W, NA_KROWS, GRID_W)[0, 0, :, 0, :]
    oh_r = jnp.asarray(dr_pat[..., None] == np.arange(2 * NA_ROWS - 1), F32)
    oh_c = jnp.asarray(dc_pat[..., None] == np.arange(2 * NA_COLS - 1), F32)
    bias = jnp.einsum('tamr,hrs,cks->thacmk', oh_r, rpb, oh_c, precision=HI)
    bias = bias.reshape(n_types, NA_HEADS, NA_QROWS * GRID_W, NA_KROWS * GRID_W)
    bias = jnp.where(valid[:, None], bias, ATT_NEG)
    n_q = NA_QROWS * GRID_W
    off_q = n_ctx // n_q
    off_k = n_ctx // GRID_W

    def kv_spec(m):
        return pl.BlockSpec((None, GRID_W, QD_W), lambda b, i, types, kbs: (b, kbs[i] + m + off_k, 0))

    ctx_spec = pl.BlockSpec((None, n_ctx, QD_W), lambda b, i, types, kbs: (b, 0, 0))
    grid_spec = pltpu.PrefetchScalarGridSpec(
        num_scalar_prefetch=2,
        grid=(bsz, rows // NA_QROWS),
        in_specs=([pl.BlockSpec((None, n_q, QD_W), lambda b, i, types, kbs: (b, i + off_q, 0))]
                  + [kv_spec(m) for m in range(NA_KROWS)] + [kv_spec(m) for m in range(NA_KROWS)]
                  + [ctx_spec, ctx_spec,
                     pl.BlockSpec((None, NA_HEADS, n_q, NA_KROWS * GRID_W),
                                  lambda b, i, types, kbs: (types[i], 0, 0, 0))]),
        out_specs=pl.BlockSpec((None, n_q, QD_W), lambda b, i, types, kbs: (b, i, 0)),
    )
    return pl.pallas_call(
        _na_kernel,
        grid_spec=grid_spec,
        out_shape=jax.ShapeDtypeStruct((bsz, t_len, QD_W), F32),
        compiler_params=_cparams("parallel", "arbitrary"),
        name="neighbourhood_attention",
    )(jnp.asarray(types), jnp.asarray(kbs), qd, *([kd] * NA_KROWS), *([vd] * NA_KROWS), kd, vd, bias)


RWKV_CHUNK = 64
RWKV_SUPER = ROW_BLOCK // RWKV_CHUNK
HI = lax.Precision.HIGHEST


RWKV_PREC_PAIR = "bf16"
RWKV_PREC_INV = "bf16"
RWKV_PREC_OUT = "bf16"
RWKV_PREC_STATE = "bf16x3"


def _pdot(x, y, dims, mode):
    if mode == "f32":
        return lax.dot_general(x, y, dims, precision=HI, preferred_element_type=F32)
    xh = x.astype(BF16)
    yh = y.astype(BF16)
    out = lax.dot_general(xh, yh, dims, preferred_element_type=F32)
    if mode == "bf16x3":
        xl = (x - xh.astype(F32)).astype(BF16)
        yl = (y - yh.astype(F32)).astype(BF16)
        out = (out + lax.dot_general(xh, yl, dims, preferred_element_type=F32)
               + lax.dot_general(xl, yh, dims, preferred_element_type=F32))
    return out


def _split_dot(x, e_bf16):
    hi = x.astype(BF16)
    lo = (x - hi.astype(F32)).astype(BF16)
    return jnp.dot(hi, e_bf16, preferred_element_type=F32) + jnp.dot(lo, e_bf16, preferred_element_type=F32)


def _rwkv_prep_kernel(z_ref, zp_ref, zn_ref, mu_ref, w0_ref, w2_ref, a0_ref, a2_ref, g2_ref, kk_ref, ka_ref,
                      rk_ref, e_ref, r_o, v_o, kk_o, g_o, bonus_o, logw_o, kd_o, bb_o):
    i = pl.program_id(1)
    last = pl.num_programs(1) - 1
    tm = z_ref.shape[0]
    hw = RWKV_WIDTH
    z = z_ref[:, S5_WIDTH:]
    row = lax.broadcasted_iota(jnp.int32, (tm, 1), 0)
    prev_halo = jnp.where(i <= 1, 0.0, zp_ref[7:8, S5_WIDTH:])
    next_halo = jnp.where((i == 0) | (i == last), 0.0, zn_ref[0:1, S5_WIDTH:])
    prev = jnp.where(row == 0, prev_halo, pltpu.roll(z, 1, 0))
    nxt = jnp.where(row == tm - 1, next_halo, pltpu.roll(z, tm - 1, 0))
    zs = z + mu_ref[0:1, :] * (prev - z) + mu_ref[1:2, :] * (nxt - z)
    c0, c1, c2, c3, c4 = RWKV_SPLITS
    r = zs[:, :c0]
    k = zs[:, c0:c1]
    v = zs[:, c1:c2]
    e = e_ref[...]
    g_o[...] = jnp.dot(jax.nn.sigmoid(zs[:, c4:]).astype(BF16), g2_ref[...], preferred_element_type=F32)
    kk = k * kk_ref[...]
    kk = kk * lax.rsqrt(_split_dot(kk * kk, e) + 1e-12)
    w_lora = jnp.dot(jnp.tanh(zs[:, c2:c3]).astype(BF16), w2_ref[...], preferred_element_type=F32)
    a_lora = jnp.dot(zs[:, c3:c4].astype(BF16), a2_ref[...], preferred_element_type=F32)
    bonus = jnp.zeros((tm, hw), F32)
    for d in range(2):
        x = w0_ref[d:d + 1, :] + w_lora[:, d * hw:(d + 1) * hw]
        softplus_neg = jnp.maximum(-x, 0.0) + jnp.log(1.0 + jnp.exp(-jnp.abs(x)))
        logw_o[d] = -jnp.exp(-softplus_neg - 0.5)
        a = jax.nn.sigmoid(a0_ref[d:d + 1, :] + a_lora[:, d * hw:(d + 1) * hw])
        kd = k * (1.0 + (a - 1.0) * ka_ref[...])
        kd_o[d] = kd
        bb_o[d] = kk * a
        bonus = bonus + _split_dot(r * kd * rk_ref[...], e) * v
    r_o[...] = r
    v_o[...] = v
    kk_o[...] = kk
    bonus_o[...] = bonus


def _rwkv_chunk_kernel(r_ref, v_ref, kk_ref, logw_ref, kd_ref, bb_ref, mn_ref, ry_ref):
    fwd = pl.program_id(1) == 0
    cn = RWKV_CHUNK
    hd = HEAD_DIM
    rowi = lax.broadcasted_iota(jnp.int32, (cn, cn), 0)
    coli = lax.broadcasted_iota(jnp.int32, (cn, cn), 1)
    lag = (rowi - coli) * jnp.where(fwd, 1, -1)
    before_incl = lag >= 0
    before = lag > 0
    eye = (rowi == coli).astype(F32)
    logw = logw_ref[...]
    lg = jnp.dot(before_incl.astype(F32), logw, precision=HI, preferred_element_type=F32)
    lg_tot = jnp.sum(logw, axis=0, keepdims=True)
    a_t = -kk_ref[...] * jnp.exp(lg - logw)
    r_t = r_ref[...] * jnp.exp(lg)
    inv = jnp.exp(-lg)
    b_t = bb_ref[...] * inv
    k_t = kd_ref[...] * inv
    to_end = jnp.exp(lg_tot - lg)
    b_h = bb_ref[...] * to_end
    k_h = kd_ref[...] * to_end
    g_end = jnp.exp(lg_tot)
    nn = (((1,), (0,)), ((), ()))
    nt = (((1,), (1,)), ((), ()))
    tn = (((0,), (0,)), ((), ()))

    heads = range(RWKV_HEADS)
    sls = [slice(h * hd, (h + 1) * hd) for h in heads]
    vs = [v_ref[:, sl] for sl in sls]
    ps = [_pdot(jnp.concatenate([a_t[:, sl], r_t[:, sl]], axis=0),
                jnp.concatenate([b_t[:, sl], k_t[:, sl]], axis=0), nt, RWKV_PREC_PAIR) for sl in sls]
    n_ab = [jnp.where(before, p[:cn, :cn], 0.0) for p in ps]
    n_ak = [jnp.where(before, p[:cn, cn:], 0.0) for p in ps]
    n_rb = [jnp.where(before_incl, p[cn:, :cn], 0.0) for p in ps]
    n_rk = [jnp.where(before_incl, p[cn:, cn:], 0.0) for p in ps]
    tinv = [eye + n for n in n_ab]
    pw = n_ab
    for _ in range(int(math.log2(cn)) - 1):
        pw = [_pdot(x, x, nn, RWKV_PREC_INV) for x in pw]
        tinv = [t + _pdot(t, x, nn, RWKV_PREC_INV) for t, x in zip(tinv, pw)]
    akv = [_pdot(n, v, nn, RWKV_PREC_OUT) for n, v in zip(n_ak, vs)]
    rkv = [_pdot(n, v, nn, RWKV_PREC_OUT) for n, v in zip(n_rk, vs)]
    kv = [_pdot(k_h[:, sl], v, tn, RWKV_PREC_OUT) for sl, v in zip(sls, vs)]
    zz = [_pdot(t, jnp.concatenate([a_t[:, sl], w], axis=1), nn, RWKV_PREC_OUT)
          for t, sl, w in zip(tinv, sls, akv)]
    for h in heads:
        sl = sls[h]
        ry_ref[h] = _pdot(n_rb[h], zz[h], nn, RWKV_PREC_OUT) + jnp.concatenate([r_t[:, sl], rkv[h]], axis=1)
        mn_ref[h] = (_pdot(b_h[:, sl], zz[h], tn, RWKV_PREC_OUT)
                     + jnp.concatenate([eye * g_end[:, sl], kv[h]], axis=1))


def _rwkv_seq_kernel(mn_ref, ry_ref, y_ref, h_ref):
    @pl.when(pl.program_id(2) == 0)
    def _():
        h_ref[...] = jnp.zeros_like(h_ref)

    fwd = pl.program_id(1) == 0
    hd = HEAD_DIM
    for cc in range(RWKV_SUPER):
        c = jnp.where(fwd, cc, RWKV_SUPER - 1 - cc)
        row0 = pl.multiple_of(c * RWKV_CHUNK, RWKV_CHUNK)
        for h in range(RWKV_HEADS):
            mn = mn_ref[c, h]
            ry = ry_ref[c, h]
            state = h_ref[h]
            nn = (((1,), (0,)), ((), ()))
            y_ref[pl.ds(row0, RWKV_CHUNK), h * hd:(h + 1) * hd] = (
                _pdot(ry[:, :hd], state, nn, RWKV_PREC_STATE) + ry[:, hd:])
            h_ref[h] = _pdot(mn[:, :hd], state, nn, RWKV_PREC_STATE) + mn[:, hd:]


def _rwkv_post_kernel(y0_ref, y1_ref, bonus_ref, g_ref, gnw_ref, gnb_ref, e_ref, o_ref):
    y = y0_ref[...] + y1_ref[...]
    e = e_ref[...]
    mu = _split_dot(y, e) * (1.0 / HEAD_DIM)
    yc = y - mu
    var = _split_dot(yc * yc, e) * (1.0 / HEAD_DIM)
    yn = yc * lax.rsqrt(var + GN_EPS) * gnw_ref[...] + gnb_ref[...]
    o_ref[...] = (yn + bonus_ref[...]) * g_ref[...]


def _block_diag2(w):
    z = jnp.zeros_like(w[0])
    return jnp.concatenate([jnp.concatenate([w[0], z], 1), jnp.concatenate([z, w[1]], 1)], 0)


def _rwkv_mixer(z, shift_mu, w0, w2, a0, a2, g2, k_k, k_a, r_k, gn_w, gn_b):
    bsz, s, zw = z.shape
    hw = RWKV_WIDTH
    tm = ROW_BLOCK
    nblk = s // tm
    head_of = np.arange(hw) // HEAD_DIM
    e = jnp.asarray(head_of[:, None] == head_of[None, :], BF16)
    tok = pl.BlockSpec((None, tm, hw), lambda b, i: (b, i, 0))
    tok2 = pl.BlockSpec((None, 2, tm, hw), lambda b, i: (b, 0, i, 0))
    halo = tm // 8

    def full(shape):
        return pl.BlockSpec(shape, lambda b, i: (0,) * len(shape))

    r, v, kk, g, bonus, logw, kd, bb = pl.pallas_call(
        _rwkv_prep_kernel,
        grid=(bsz, nblk),
        in_specs=[pl.BlockSpec((None, tm, zw), lambda b, i: (b, i, 0)),
                  pl.BlockSpec((None, 8, zw), lambda b, i: (b, jnp.maximum(i * halo - 1, 0), 0)),
                  pl.BlockSpec((None, 8, zw), lambda b, i: (b, jnp.minimum((i + 1) * halo, s // 8 - 1), 0)),
                  full((2, RWKV_IN)), full((2, hw)), full((2 * RWKV_LORA_W, 2 * hw)), full((2, hw)),
                  full((2 * RWKV_LORA_A, 2 * hw)), full((RWKV_LORA_G, hw)), full((1, hw)), full((1, hw)),
                  full((1, hw)), full((hw, hw))],
        out_specs=[tok, tok, tok, tok, tok, tok2, tok2, tok2],
        out_shape=[jax.ShapeDtypeStruct((bsz, s, hw), F32)] * 5 + [jax.ShapeDtypeStruct((bsz, 2, s, hw), F32)] * 3,
        compiler_params=_cparams("parallel", "parallel"),
        name="rwkv_prep",
    )(z, z, z, shift_mu, w0, _block_diag2(w2).astype(BF16), a0, _block_diag2(a2).astype(BF16), g2.astype(BF16),
      k_k.reshape(1, hw), k_a.reshape(1, hw), r_k.reshape(1, hw), e)

    nch = s // RWKV_CHUNK
    ctok = pl.BlockSpec((None, RWKV_CHUNK, hw), lambda b, d, c: (b, c, 0))
    ctok2 = pl.BlockSpec((None, None, RWKV_CHUNK, hw), lambda b, d, c: (b, d, c, 0))
    mat_shape = jax.ShapeDtypeStruct((bsz, 2, nch, RWKV_HEADS, RWKV_CHUNK, 2 * HEAD_DIM), F32)
    mat_spec = pl.BlockSpec((None, None, None, RWKV_HEADS, RWKV_CHUNK, 2 * HEAD_DIM),
                            lambda b, d, c: (b, d, c, 0, 0, 0))
    mn, ry = pl.pallas_call(
        _rwkv_chunk_kernel,
        grid=(bsz, 2, nch),
        in_specs=[ctok, ctok, ctok, ctok2, ctok2, ctok2],
        out_specs=[mat_spec, mat_spec],
        out_shape=[mat_shape, mat_shape],
        compiler_params=_cparams("parallel", "parallel", "parallel"),
        name="rwkv_chunk",
    )(r, v, kk, logw, kd, bb)

    nsb = nch // RWKV_SUPER

    def sb_of(d, j):
        return jnp.where(d == 0, j, jnp.where(j == 0, 0, nsb - j))

    sup_shape = (bsz, 2, nsb, RWKV_SUPER, RWKV_HEADS, RWKV_CHUNK, 2 * HEAD_DIM)
    sup_spec = pl.BlockSpec((None, None, None, RWKV_SUPER, RWKV_HEADS, RWKV_CHUNK, 2 * HEAD_DIM),
                            lambda b, d, j: (b, d, sb_of(d, j), 0, 0, 0, 0))
    y = pl.pallas_call(
        _rwkv_seq_kernel,
        grid=(bsz, 2, nsb),
        in_specs=[sup_spec, sup_spec],
        out_specs=pl.BlockSpec((None, None, tm, hw), lambda b, d, j: (b, d, sb_of(d, j), 0)),
        out_shape=jax.ShapeDtypeStruct((bsz, 2, s, hw), F32),
        scratch_shapes=[pltpu.VMEM((RWKV_HEADS, HEAD_DIM, HEAD_DIM), F32)],
        compiler_params=_cparams("parallel", "parallel", "arbitrary"),
        name="rwkv_seq",
    )(mn.reshape(sup_shape), ry.reshape(sup_shape))

    vec = pl.BlockSpec((1, hw), lambda b, i: (0, 0))
    return pl.pallas_call(
        _rwkv_post_kernel,
        grid=(bsz, nblk),
        in_specs=[pl.BlockSpec((None, None, tm, hw), lambda b, i: (b, 0, i, 0)),
                  pl.BlockSpec((None, None, tm, hw), lambda b, i: (b, 1, i, 0)),
                  tok, tok, vec, vec, pl.BlockSpec((hw, hw), lambda b, i: (0, 0))],
        out_specs=tok,
        out_shape=jax.ShapeDtypeStruct((bsz, s, hw), F32),
        compiler_params=_cparams("parallel", "parallel"),
        name="rwkv_post",
    )(y, y, bonus, g, gn_w.reshape(1, hw), gn_b.reshape(1, hw), e)


def kernel(x, c, ctx, c_ctx,
           l0_w_mod, l0_b_mod, l0_w_in, l0_s5_lam_re, l0_s5_lam_im, l0_s5_log_dt, l0_s5_b_re, l0_s5_b_im,
           l0_s5_c_re, l0_s5_c_im, l0_s5_d, l0_w_glu, l0_b_glu, l0_rwkv_shift, l0_rwkv_w0, l0_rwkv_w2,
           l0_rwkv_a0, l0_rwkv_a2, l0_rwkv_g2, l0_rwkv_k_k, l0_rwkv_k_a, l0_rwkv_r_k, l0_rwkv_gn_w,
           l0_rwkv_gn_b, l0_w_out, l0_ln1_g, l0_ln1_b, l0_peer_wq, l0_peer_keys, l0_peer_u, l0_peer_v,
           l0_ln2_g, l0_ln2_b,
           l1_w_mod, l1_b_mod, l1_w_in, l1_sink, l1_rpb, l1_w_out, l1_ln1_g, l1_ln1_b, l1_peer_wq,
           l1_peer_keys, l1_peer_u, l1_peer_v, l1_ln2_g, l1_ln2_b):
    L = ctx.shape[1]
    h = jnp.concatenate([ctx, x], 1)

    tab = _mod_tables(c, c_ctx, l0_w_mod, l0_b_mod)
    z = _inproj(h, tab, 0, 1, l0_w_in.astype(BF16))
    ya = _s5_mixer(z[..., :S5_WIDTH], L, l0_s5_lam_re, l0_s5_lam_im, l0_s5_log_dt, l0_s5_b_re, l0_s5_b_im,
                   l0_s5_c_re, l0_s5_c_im, l0_s5_d, l0_w_glu, l0_b_glu)
    yb = _rwkv_mixer(z, l0_rwkv_shift, l0_rwkv_w0, l0_rwkv_w2, l0_rwkv_a0, l0_rwkv_a2,
                     l0_rwkv_g2, l0_rwkv_k_k, l0_rwkv_k_a, l0_rwkv_r_k, l0_rwkv_gn_w, l0_rwkv_gn_b)
    h = _outproj_ln(ya, yb, h, tab, 2, l0_w_out, l0_ln1_g, l0_ln1_b, 0)
    h = _peer_ln(h, tab, 0, l0_peer_wq, l0_peer_keys, l0_peer_u, l0_peer_v, l0_ln2_g, l0_ln2_b)

    tab = _mod_tables(c, c_ctx, l1_w_mod, l1_b_mod)
    qc, kc, vc, qd, kd, vd = _inproj_attn(h, tab, 0, 1, l1_w_in.astype(BF16), L)
    oc = _window_gqa(qc, kc, vc, l1_sink, L)
    od = _neighbourhood_attention(qd, kd, vd, l1_rpb, L)
    ctx_blocks = L // ROW_BLOCK
    hx = _outproj_ln(oc, od, h, tab, 2, l1_w_out, l1_ln1_g, l1_ln1_b, ctx_blocks)
    return _peer_ln(hx, tab, ctx_blocks, l1_peer_wq, l1_peer_keys, l1_peer_u, l1_peer_v, l1_ln2_g, l1_ln2_b)
```

```python
import functools
import math

import jax
import jax.numpy as jnp
import numpy as np
from jax import lax
from jax.experimental import pallas as pl
from jax.experimental.pallas import tpu as pltpu

F32 = jnp.float32
BF16 = jnp.bfloat16

D_MODEL = 1024
DEPTH = 2
GRID_W = 64
CTX_LEN = 256
HEAD_DIM = 64
S5_GROUP_CH = 16
S5_GROUPS = 16
S5_WIDTH = S5_GROUPS * S5_GROUP_CH
S5_STATE = 64
RWKV_HEADS = 8
RWKV_WIDTH = RWKV_HEADS * HEAD_DIM
RWKV_LORA_W = 64
RWKV_LORA_A = 64
RWKV_LORA_G = 128
RWKV_IN = 3 * RWKV_WIDTH + 2 * RWKV_LORA_W + 2 * RWKV_LORA_A + RWKV_LORA_G
RWKV_SPLITS = (RWKV_WIDTH, 2 * RWKV_WIDTH, 3 * RWKV_WIDTH,
               3 * RWKV_WIDTH + 2 * RWKV_LORA_W,
               3 * RWKV_WIDTH + 2 * RWKV_LORA_W + 2 * RWKV_LORA_A)
WIN_HEADS = 8
WIN_KV_HEADS = 2
WINDOW = 128
WIN_BLOCK = 128
NA_HEADS = 8
NA_ROWS = 8
NA_COLS = 16
ODD_SPLITS = (WIN_HEADS * HEAD_DIM, (WIN_HEADS + WIN_KV_HEADS) * HEAD_DIM,
              (WIN_HEADS + 2 * WIN_KV_HEADS) * HEAD_DIM,
              (WIN_HEADS + 2 * WIN_KV_HEADS + NA_HEADS) * HEAD_DIM,
              (WIN_HEADS + 2 * WIN_KV_HEADS + 2 * NA_HEADS) * HEAD_DIM)
ROPE_BASE = 10000.0
PEER_HEADS = 8
PEER_KEYS = 128
PEER_DK = 128
PEER_TOPK = 16
PEER_BLOCK = 128
ALPHA = (2.0 * DEPTH) ** 0.25
LN_EPS = 1e-5
GN_EPS = 64e-5

VMEM_LIMIT_BYTES = 56 * 1024 * 1024
ROW_BLOCK = 256


def _cparams(*sem):
    return pltpu.CompilerParams(dimension_semantics=sem, vmem_limit_bytes=VMEM_LIMIT_BYTES)


def _modulation_kernel(c_ref, w_ref, b_ref, o_ref):
    c = c_ref[...]
    s = c * jax.nn.sigmoid(c)
    o_ref[...] = jnp.dot(s.astype(BF16), w_ref[...].astype(BF16), preferred_element_type=F32) + b_ref[...]


def _modulation(cond, w_mod, b_mod):
    n, d = cond.shape
    nout = w_mod.shape[1]
    tn = 1536
    return pl.pallas_call(
        _modulation_kernel,
        grid=(nout // tn,),
        in_specs=[pl.BlockSpec((n, d), lambda j: (0, 0)),
                  pl.BlockSpec((d, tn), lambda j: (0, j)),
                  pl.BlockSpec((1, tn), lambda j: (0, j))],
        out_specs=pl.BlockSpec((n, tn), lambda j: (0, j)),
        out_shape=jax.ShapeDtypeStruct((n, nout), F32),
        compiler_params=_cparams("arbitrary"),
        name="modulation",
    )(cond, w_mod, b_mod.reshape(1, nout))


def _mod_tables(c, c_ctx, w_mod, b_mod):
    bsz = c.shape[0]
    cond = jnp.concatenate([c, c_ctx[None, :], jnp.zeros((8 - bsz - 1, c.shape[1]), F32)], 0)
    m = _modulation(cond, w_mod, b_mod)
    mx = m[:bsz].reshape(bsz, 1, 6, 1, D_MODEL)
    mc = jnp.broadcast_to(m[bsz].reshape(1, 1, 6, 1, D_MODEL), (bsz, 1, 6, 1, D_MODEL))
    return jnp.concatenate([mc, mx], 1)


def _tab_spec(which):
    return pl.BlockSpec((None, None, None, 1, D_MODEL),
                        lambda b, i: (b, jnp.minimum(i, 1), which, 0, 0))


def _inproj_kernel(x_ref, shift_ref, scale_ref, w_ref, o_ref):
    hm = x_ref[...] * (1.0 + scale_ref[...]) + shift_ref[...]
    o_ref[...] = jnp.dot(hm.astype(BF16), w_ref[...], preferred_element_type=F32)


def _inproj(h, tab, which_shift, which_scale, w_bf16):
    bsz, s, d = h.shape
    n = w_bf16.shape[1]
    return pl.pallas_call(
        _inproj_kernel,
        grid=(bsz, s // ROW_BLOCK),
        in_specs=[pl.BlockSpec((None, ROW_BLOCK, d), lambda b, i: (b, i, 0)),
                  _tab_spec(which_shift), _tab_spec(which_scale),
                  pl.BlockSpec((d, n), lambda b, i: (0, 0))],
        out_specs=pl.BlockSpec((None, ROW_BLOCK, n), lambda b, i: (b, i, 0)),
        out_shape=jax.ShapeDtypeStruct((bsz, s, n), F32),
        compiler_params=_cparams("parallel", "parallel"),
        name="inproj",
    )(h, tab, tab, w_bf16)


def _layer_norm_rows(z, g, b):
    mu = jnp.mean(z, -1, keepdims=True)
    zc = z - mu
    var = jnp.mean(zc * zc, -1, keepdims=True)
    return zc * lax.rsqrt(var + LN_EPS) * g + b


def _outproj_ln_kernel(ya_ref, yb_ref, h_ref, gate_ref, wa_ref, wb_ref, g_ref, b_ref, o_ref):
    proj = (jnp.dot(ya_ref[...].astype(BF16), wa_ref[...], preferred_element_type=F32)
            + jnp.dot(yb_ref[...].astype(BF16), wb_ref[...], preferred_element_type=F32))
    z = ALPHA * h_ref[...] + gate_ref[...] * proj
    o_ref[...] = _layer_norm_rows(z, g_ref[...], b_ref[...])


def _outproj_ln(ya, yb, h, tab, which_gate, w_out, ln_g, ln_b, row_off):
    bsz, rows, ka = ya.shape
    kb = yb.shape[-1]
    d = h.shape[-1]
    return pl.pallas_call(
        _outproj_ln_kernel,
        grid=(bsz, rows // ROW_BLOCK),
        in_specs=[pl.BlockSpec((None, ROW_BLOCK, ka), lambda b, i: (b, i, 0)),
                  pl.BlockSpec((None, ROW_BLOCK, kb), lambda b, i: (b, i, 0)),
                  pl.BlockSpec((None, ROW_BLOCK, d), lambda b, i: (b, i + row_off, 0)),
                  pl.BlockSpec((None, None, None, 1, d), lambda b, i: (b, jnp.minimum(i + row_off, 1), which_gate, 0, 0)),
                  pl.BlockSpec((ka, d), lambda b, i: (0, 0)),
                  pl.BlockSpec((kb, d), lambda b, i: (0, 0)),
                  pl.BlockSpec((1, d), lambda b, i: (0, 0)),
                  pl.BlockSpec((1, d), lambda b, i: (0, 0))],
        out_specs=pl.BlockSpec((None, ROW_BLOCK, d), lambda b, i: (b, i, 0)),
        out_shape=jax.ShapeDtypeStruct((bsz, rows, d), F32),
        compiler_params=_cparams("parallel", "parallel"),
        name="outproj_ln",
    )(ya, yb, h, tab, w_out[:ka].astype(BF16), w_out[ka:].astype(BF16), ln_g.reshape(1, d), ln_b.reshape(1, d))


PEER_NEG = -3.0e38
PEER_ETILE = 1024
PEER_SUBTILE = 256
SQRT_HALF = 0.7071067811865476


PEER_NO_RANK = 255.0


def _top_values(s, n, with_rank=False):
    vals = []
    rank = jnp.full(s.shape, PEER_NO_RANK, F32) if with_rank else None
    for k in range(n):
        m = jnp.max(s, axis=0, keepdims=True)
        vals.append(m)
        hit = s == m
        if with_rank:
            rank = jnp.where(hit, float(k), rank)
        s = jnp.where(hit, PEER_NEG, s)
    return (vals, rank) if with_rank else vals


def _peer_route_kernel(h_ref, shift_ref, scale_ref, wq_ref, keys_ref,
                       pin_ref, cnt_ref, e1_ref, r2_ref, e2_ref):
    pin = (h_ref[...] * (1.0 + scale_ref[...]) + shift_ref[...]).astype(BF16)
    pin_ref[...] = pin
    q = jnp.dot(pin, wq_ref[...], preferred_element_type=F32).astype(BF16)
    st = lax.dot_general(keys_ref[...], q, (((1,), (1,)), ((), ())), preferred_element_type=F32)
    for hd in range(PEER_HEADS):
        base = hd * 2 * PEER_KEYS
        s1 = st[base:base + PEER_KEYS]
        s2 = st[base + PEER_KEYS:base + 2 * PEER_KEYS]
        v1 = _top_values(s1, PEER_TOPK)
        v2, rank2 = _top_values(s2, PEER_TOPK, with_rank=True)
        v2all = jnp.concatenate(v2, axis=0)
        sums = [v1[a] + v2all for a in range(PEER_TOPK)]
        frontier = [sums[a][:(PEER_TOPK + 1) // (a + 1)] for a in range(PEER_TOPK)]
        top = _top_values(jnp.concatenate(frontier, axis=0), PEER_TOPK + 1)
        tau = 0.5 * (top[PEER_TOPK - 1] + top[PEER_TOPK])
        zsum = top[0] * 0.0
        for kk in range(PEER_TOPK):
            zsum = zsum + jnp.exp(top[kk] - top[0])
        need = tau - s1
        cnt = jnp.zeros_like(s1)
        for b in range(PEER_TOPK):
            cnt = cnt + jnp.where(v2[b] >= need, 1.0, 0.0)
        cnt_ref[hd] = jnp.where(s1 >= v1[PEER_TOPK - 1], cnt, 0.0)
        e1_ref[hd] = jnp.exp(s1 - v1[0])
        r2_ref[hd] = rank2.astype(BF16)
        e2_ref[hd] = (jnp.exp(s2 - v2[0]) / zsum).astype(BF16)


def _peer_dense_kernel(pin_ref, cnt_ref, e1_ref, r2_ref, e2_ref, u_ref, vt_ref,
                       h_ref, gate_ref, g_ref, b_ref, o_ref, acc_ref, w_ref):
    j = pl.program_id(2)

    last = pl.num_programs(2) - 1

    def apply_values(first):
        part = jnp.dot(vt_ref[...], w_ref[(j + 1) % 2], preferred_element_type=F32)
        if first:
            acc_ref[...] = part
        else:
            acc_ref[...] += part

    def score_and_gate():
        pin = pin_ref[...]
        cur = j % 2
        for sb in range(PEER_ETILE // PEER_SUBTILE):
            es = slice(sb * PEER_SUBTILE, (sb + 1) * PEER_SUBTILE)
            at = lax.dot_general(u_ref[es, :], pin, (((1,), (1,)), ((), ())), preferred_element_type=F32)
            for i2 in range(PEER_SUBTILE // PEER_KEYS):
                ii = sb * (PEER_SUBTILE // PEER_KEYS) + i2
                a = at[i2 * PEER_KEYS:(i2 + 1) * PEER_KEYS]
                act = (0.5 * a * (1.0 + lax.erf(a * SQRT_HALF))).astype(BF16)
                g = jnp.zeros(a.shape, BF16)
                for hd in range(PEER_HEADS):
                    cnt_row = cnt_ref[hd, pl.ds(ii, 1), :].astype(BF16)
                    c_row = e1_ref[hd, pl.ds(ii, 1), :].astype(BF16)
                    g = g + jnp.where(r2_ref[hd] < cnt_row, e2_ref[hd] * c_row, jnp.zeros((), BF16))
                w_ref[cur, ii * PEER_KEYS:(ii + 1) * PEER_KEYS, :] = g * act

    @pl.when(j == 0)
    def _():
        score_and_gate()

    @pl.when(j == 1)
    def _():
        apply_values(first=True)
        score_and_gate()

    @pl.when((j > 1) & (j < last))
    def _():
        apply_values(first=False)
        score_and_gate()

    @pl.when(j == last)
    def _():
        apply_values(first=False)
        z = ALPHA * h_ref[...] + gate_ref[...] * acc_ref[...].T
        o_ref[...] = _layer_norm_rows(z, g_ref[...], b_ref[...])


def _peer_block_diag_keys(sub_keys):
    hh, two, nk, dh = sub_keys.shape
    eye = jnp.eye(hh * two, dtype=F32)
    kb = sub_keys.reshape(hh * two, nk, dh)
    return jnp.einsum('akd,ab->akbd', kb, eye).reshape(hh * two * nk, hh * two * dh)


def _peer_ln(h, tab, seg_off, wq, sub_keys, u_tab, v_tab, ln_g, ln_b):
    bsz, s, d = h.shape
    tm = ROW_BLOCK
    nblk = s // tm
    hk = PEER_HEADS * 2 * PEER_KEYS
    keys_bd = _peer_block_diag_keys(sub_keys).astype(BF16)

    def tab_spec3(which):
        return pl.BlockSpec((None, None, None, 1, d),
                            lambda b, i, *_: (b, jnp.minimum(i + seg_off, 1), which, 0, 0))

    route_dims = (bsz, nblk, PEER_HEADS, PEER_KEYS, tm)
    route_spec = pl.BlockSpec((None, None, PEER_HEADS, PEER_KEYS, tm), lambda b, i: (b, i, 0, 0, 0))
    pin, cnt, e1, r2, e2 = pl.pallas_call(
        _peer_route_kernel,
        grid=(bsz, nblk),
        in_specs=[pl.BlockSpec((None, tm, d), lambda b, i: (b, i, 0)),
                  tab_spec3(3), tab_spec3(4),
                  pl.BlockSpec((d, PEER_HEADS * PEER_DK), lambda b, i: (0, 0)),
                  pl.BlockSpec((hk, PEER_HEADS * PEER_DK), lambda b, i: (0, 0))],
        out_specs=[pl.BlockSpec((None, tm, d), lambda b, i: (b, i, 0)),
                   route_spec, route_spec, route_spec, route_spec],
        out_shape=[jax.ShapeDtypeStruct((bsz, nblk * tm, d), BF16),
                   jax.ShapeDtypeStruct(route_dims, F32), jax.ShapeDtypeStruct(route_dims, F32),
                   jax.ShapeDtypeStruct(route_dims, BF16), jax.ShapeDtypeStruct(route_dims, BF16)],
        compiler_params=_cparams("parallel", "parallel"),
        name="peer_route",
    )(h, tab, tab, wq.astype(BF16), keys_bd)

    n_exp = u_tab.shape[0]
    rows_per_step = PEER_ETILE // PEER_KEYS
    full_spec = pl.BlockSpec((None, None, PEER_HEADS, PEER_KEYS, tm), lambda b, i, j: (b, i, 0, 0, 0))
    n_tiles = n_exp // PEER_ETILE
    row_spec = pl.BlockSpec((None, None, PEER_HEADS, rows_per_step, tm),
                            lambda b, i, j: (b, i, 0, jnp.minimum(j, n_tiles - 1), 0))
    return pl.pallas_call(
        _peer_dense_kernel,
        grid=(bsz, nblk, n_tiles + 1),
        in_specs=[pl.BlockSpec((None, tm, d), lambda b, i, j: (b, i, 0)),
                  row_spec, row_spec, full_spec, full_spec,
                  pl.BlockSpec((PEER_ETILE, d), lambda b, i, j: (jnp.minimum(j, n_tiles - 1), 0)),
                  pl.BlockSpec((d, PEER_ETILE), lambda b, i, j: (0, jnp.maximum(j - 1, 0))),
                  pl.BlockSpec((None, tm, d), lambda b, i, j: (b, i, 0)),
                  tab_spec3(5),
                  pl.BlockSpec((1, d), lambda b, i, j: (0, 0)),
                  pl.BlockSpec((1, d), lambda b, i, j: (0, 0))],
        out_specs=pl.BlockSpec((None, tm, d), lambda b, i, j: (b, i, 0)),
        out_shape=jax.ShapeDtypeStruct((bsz, nblk * tm, d), F32),
        scratch_shapes=[pltpu.VMEM((d, tm), F32), pltpu.VMEM((2, PEER_ETILE, tm), BF16)],
        compiler_params=_cparams("parallel", "parallel", "arbitrary"),
        name="peer_dense",
    )(pin, cnt, e1, r2, e2, u_tab.astype(BF16), v_tab.T.astype(BF16), h, tab,
      ln_g.reshape(1, d), ln_b.reshape(1, d))


S5_CHAINS = 8
S5_TCHUNK = 128
S5_NSTATE = S5_GROUPS * S5_STATE


def _s5_scan_kernel(uf_ref, ub_ref, wb_ref, wc_ref, a_ref, yf_ref, yb_ref,
                    fre_ref, fim_ref, bre_ref, bim_ref, h_ref):
    @pl.when(pl.program_id(0) == 0)
    def _():
        h_ref[...] = jnp.zeros_like(h_ref)

    rows = uf_ref.shape[0]
    steps = rows // S5_CHAINS
    uf = uf_ref[...].astype(BF16)
    ub = ub_ref[...].astype(BF16)
    fre_ref[...] = jnp.dot(uf, wb_ref[0], preferred_element_type=F32)
    fim_ref[...] = jnp.dot(uf, wb_ref[1], preferred_element_type=F32)
    bre_ref[...] = jnp.dot(ub, wb_ref[2], preferred_element_type=F32)
    bim_ref[...] = jnp.dot(ub, wb_ref[3], preferred_element_type=F32)
    a_re = a_ref[0]
    a_im = a_ref[1]
    is_fwd = lax.broadcasted_iota(jnp.int32, (S5_CHAINS, S5_NSTATE), 0) < (S5_CHAINS // 2)

    def step(t, carry):
        h_re, h_im = carry
        of = pl.multiple_of(t * S5_CHAINS, S5_CHAINS)
        ob = pl.multiple_of((steps - 1 - t) * S5_CHAINS, S5_CHAINS)
        n_re = a_re * h_re - a_im * h_im + jnp.where(is_fwd, fre_ref[pl.ds(of, S5_CHAINS), :],
                                                     bre_ref[pl.ds(ob, S5_CHAINS), :])
        n_im = a_re * h_im + a_im * h_re + jnp.where(is_fwd, fim_ref[pl.ds(of, S5_CHAINS), :],
                                                     bim_ref[pl.ds(ob, S5_CHAINS), :])
        fre_ref[pl.ds(of, S5_CHAINS), :] = n_re
        fim_ref[pl.ds(of, S5_CHAINS), :] = n_im
        bre_ref[pl.ds(ob, S5_CHAINS), :] = n_re
        bim_ref[pl.ds(ob, S5_CHAINS), :] = n_im
        return n_re, n_im

    h_re, h_im = lax.fori_loop(0, steps, step, (h_ref[0], h_ref[1]), unroll=2)
    h_ref[0] = h_re
    h_ref[1] = h_im
    yf_ref[...] = (jnp.dot(fre_ref[...].astype(BF16), wc_ref[0], preferred_element_type=F32)
                   - jnp.dot(fim_ref[...].astype(BF16), wc_ref[1], preferred_element_type=F32))
    yb_ref[...] = (jnp.dot(bre_ref[...].astype(BF16), wc_ref[2], preferred_element_type=F32)
                   - jnp.dot(bim_ref[...].astype(BF16), wc_ref[3], preferred_element_type=F32))


def _s5_glu_kernel(u_ref, y0_ref, y1_ref, d_ref, w_ref, b_ref, o_ref):
    y = u_ref[...] * d_ref[...] + y0_ref[...] + y1_ref[...]
    y = 0.5 * y * (1.0 + lax.erf(y * SQRT_HALF))
    gate = jnp.dot(y.astype(BF16), w_ref[...], preferred_element_type=F32) + b_ref[...]
    o_ref[...] = y * jax.nn.sigmoid(gate)


def _s5_params(lam_re, lam_im, log_dt, b_re, b_im, c_re, c_im):
    eye = jnp.eye(S5_GROUPS, dtype=F32)
    wb, wc, a_rows = [], [], [[], []]
    for d in range(2):
        lr, li = lam_re[d], lam_im[d]
        dt = jnp.exp(log_dt[d])[:, None]
        mag = jnp.exp(lr * dt)
        ab_re, ab_im = mag * jnp.cos(li * dt), mag * jnp.sin(li * dt)
        den = lr * lr + li * li
        nr = ab_re - 1.0
        coef_re = (nr * lr + ab_im * li) / den
        coef_im = (ab_im * lr - nr * li) / den
        bb_re = coef_re[..., None] * b_re[d] - coef_im[..., None] * b_im[d]
        bb_im = coef_re[..., None] * b_im[d] + coef_im[..., None] * b_re[d]
        for bb in (bb_re, bb_im):
            wb.append(jnp.einsum('gpi,gh->gihp', bb, eye).reshape(S5_WIDTH, S5_NSTATE))
        for cc in (c_re[d], c_im[d]):
            wc.append(jnp.einsum('gip,gh->gphi', cc, eye).reshape(S5_NSTATE, S5_WIDTH))
        for k, ab in enumerate((ab_re, ab_im)):
            a_rows[k].append(jnp.broadcast_to(ab.reshape(1, S5_NSTATE), (S5_CHAINS // 2, S5_NSTATE)))
    a_tiles = jnp.stack([jnp.concatenate(a_rows[0], 0), jnp.concatenate(a_rows[1], 0)])
    return jnp.stack(wb).astype(BF16), jnp.stack(wc).astype(BF16), a_tiles


def _s5_mixer(u, n_ctx, lam_re, lam_im, log_dt, b_re, b_im, c_re, c_im, d_skip, w_glu, b_glu):
    bsz, s, w = u.shape
    assert 2 * bsz == S5_CHAINS
    wb, wc, a_tiles = _s5_params(lam_re, lam_im, log_dt, b_re, b_im, c_re, c_im)
    ut = jnp.transpose(u, (1, 0, 2))
    chain = jnp.concatenate([ut, ut], 1).reshape(s * S5_CHAINS, w)
    rows = S5_TCHUNK * S5_CHAINS
    n_chunks = s // S5_TCHUNK
    ctx_chunks = n_ctx // S5_TCHUNK

    def mirror(i):
        return jnp.where(i < ctx_chunks, ctx_chunks - 1 - i, n_chunks - 1 - (i - ctx_chunks))

    state_buf = pltpu.VMEM((rows, S5_NSTATE), F32)
    y_shape = jax.ShapeDtypeStruct((s * S5_CHAINS, w), F32)
    yf, yb = pl.pallas_call(
        _s5_scan_kernel,
        grid=(n_chunks,),
        in_specs=[pl.BlockSpec((rows, w), lambda i: (i, 0)),
                  pl.BlockSpec((rows, w), lambda i: (mirror(i), 0)),
                  pl.BlockSpec((4, w, S5_NSTATE), lambda i: (0, 0, 0)),
                  pl.BlockSpec((4, S5_NSTATE, w), lambda i: (0, 0, 0)),
                  pl.BlockSpec((2, S5_CHAINS, S5_NSTATE), lambda i: (0, 0, 0))],
        out_specs=[pl.BlockSpec((rows, w), lambda i: (i, 0)),
                   pl.BlockSpec((rows, w), lambda i: (mirror(i), 0))],
        out_shape=[y_shape, y_shape],
        scratch_shapes=[state_buf, state_buf, state_buf, state_buf,
                        pltpu.VMEM((2, S5_CHAINS, S5_NSTATE), F32)],
        compiler_params=_cparams("arbitrary"),
        name="s5_scan",
    )(chain, chain, wb, wc, a_tiles)
    y_fwd = jnp.transpose(yf.reshape(s, S5_CHAINS, w)[:, :bsz], (1, 0, 2))
    y_bwd = jnp.transpose(yb.reshape(s, S5_CHAINS, w)[:, bsz:], (1, 0, 2))
    tok = pl.BlockSpec((None, ROW_BLOCK, w), lambda b, i: (b, i, 0))
    vec = pl.BlockSpec((1, w), lambda b, i: (0, 0))
    return pl.pallas_call(
        _s5_glu_kernel,
        grid=(bsz, s // ROW_BLOCK),
        in_specs=[tok, tok, tok, vec, pl.BlockSpec((w, w), lambda b, i: (0, 0)), vec],
        out_specs=tok,
        out_shape=jax.ShapeDtypeStruct((bsz, s, w), F32),
        compiler_params=_cparams("parallel", "parallel"),
        name="s5_glu",
    )(u, y_fwd, y_bwd, d_skip.reshape(1, w), w_glu.astype(BF16), b_glu.reshape(1, w))


ATT_SCALE = HEAD_DIM ** -0.5
ATT_NEG = -1e30
ROPE_QUARTER = HEAD_DIM // 4
QC_W = WIN_HEADS * HEAD_DIM
KC_W = WIN_KV_HEADS * HEAD_DIM
QD_W = NA_HEADS * HEAD_DIM


def _rope_tables(n_ctx, t):
    pos = np.arange(t)
    rowcol = np.stack([pos // GRID_W, pos % GRID_W], 1).astype(np.float32)
    lane = np.arange(QC_W)
    j = lane % HEAD_DIM
    freqs = (ROPE_BASE ** (-np.arange(ROPE_QUARTER, dtype=np.float32) / ROPE_QUARTER)).astype(np.float32)
    ang = rowcol[:, j // (HEAD_DIM // 2)] * freqs[j % ROPE_QUARTER][None, :]
    ang = jnp.asarray(ang, F32)
    sign = np.where((j % (HEAD_DIM // 2)) < ROPE_QUARTER, -1.0, 1.0).astype(np.float32)
    cos = jnp.concatenate([jnp.ones((n_ctx, QC_W), F32), jnp.cos(ang)], 0)
    sin = jnp.concatenate([jnp.zeros((n_ctx, QC_W), F32), jnp.sin(ang) * sign[None, :]], 0)
    return cos, sin


def _rope(x, cos, sin):
    n = x.shape[-1]
    lane = lax.broadcasted_iota(jnp.int32, (1, n), 1)
    first = (lane % (HEAD_DIM // 2)) < ROPE_QUARTER
    partner = jnp.where(first, pltpu.roll(x, n - ROPE_QUARTER, 1), pltpu.roll(x, ROPE_QUARTER, 1))
    return x * cos + partner * sin


def _inproj_attn_kernel(x_ref, shift_ref, scale_ref, w_ref, cos_ref, sin_ref,
                        qc_ref, kc_ref, vc_ref, qd_ref, kd_ref, vd_ref):
    hm = x_ref[...] * (1.0 + scale_ref[...]) + shift_ref[...]
    z = jnp.dot(hm.astype(BF16), w_ref[...], preferred_element_type=F32)
    c0, c1, c2, c3, c4 = ODD_SPLITS
    cos = cos_ref[...]
    sin = sin_ref[...]
    qc_ref[...] = (_rope(z[:, 0:c0], cos, sin) * ATT_SCALE).astype(BF16)
    kc_ref[...] = _rope(z[:, c0:c1], cos[:, :KC_W], sin[:, :KC_W]).astype(BF16)
    vc_ref[...] = z[:, c1:c2].astype(BF16)
    qd_ref[...] = (z[:, c2:c3] * ATT_SCALE).astype(BF16)
    kd_ref[...] = z[:, c3:c4].astype(BF16)
    vd_ref[...] = z[:, c4:].astype(BF16)


def _inproj_attn(h, tab, which_shift, which_scale, w_bf16, n_ctx):
    bsz, s, d = h.shape
    n = w_bf16.shape[1]
    cos, sin = _rope_tables(n_ctx, s - n_ctx)
    widths = (QC_W, KC_W, KC_W, QD_W, QD_W, QD_W)
    return pl.pallas_call(
        _inproj_attn_kernel,
        grid=(bsz, s // ROW_BLOCK),
        in_specs=[pl.BlockSpec((None, ROW_BLOCK, d), lambda b, i: (b, i, 0)),
                  _tab_spec(which_shift), _tab_spec(which_scale),
                  pl.BlockSpec((d, n), lambda b, i: (0, 0)),
                  pl.BlockSpec((ROW_BLOCK, QC_W), lambda b, i: (i, 0)),
                  pl.BlockSpec((ROW_BLOCK, QC_W), lambda b, i: (i, 0))],
        out_specs=[pl.BlockSpec((None, ROW_BLOCK, w), lambda b, i: (b, i, 0)) for w in widths],
        out_shape=[jax.ShapeDtypeStruct((bsz, s, w), BF16) for w in widths],
        compiler_params=_cparams("parallel", "parallel"),
        name="inproj_attn",
    )(h, tab, tab, w_bf16, cos, sin)


def _softmax_pv(s, v, extra_logit=None):
    m = jnp.max(s, axis=1, keepdims=True)
    if extra_logit is not None:
        m = jnp.maximum(m, extra_logit)
    p = jnp.exp(s - m)
    den = jnp.sum(p, axis=1, keepdims=True)
    if extra_logit is not None:
        den = den + jnp.exp(extra_logit - m)
    return jnp.dot(p.astype(BF16), v, preferred_element_type=F32) / den


def _window_kernel(q_ref, kp_ref, kc_ref, kn_ref, kx_ref, vp_ref, vc_ref, vn_ref, vx_ref, sink_ref, o_ref, *, t_len):
    i = pl.program_id(1)
    wb = WIN_BLOCK
    qi = lax.broadcasted_iota(jnp.int32, (wb, 1), 0)
    kj = lax.broadcasted_iota(jnp.int32, (1, 3 * wb), 1) - wb
    kabs = i * wb + kj
    valid = (jnp.abs(kj - qi) <= WINDOW) & (kabs >= 0) & (kabs < t_len)
    n_ctx = kx_ref.shape[0]
    bias = jnp.concatenate([jnp.where(valid, 0.0, ATT_NEG), jnp.zeros((wb, n_ctx), F32)], axis=1)
    k_all = jnp.concatenate([kp_ref[...], kc_ref[...], kn_ref[...], kx_ref[...]], axis=0)
    v_all = jnp.concatenate([vp_ref[...], vc_ref[...], vn_ref[...], vx_ref[...]], axis=0)
    group = WIN_HEADS // WIN_KV_HEADS
    for hq in range(WIN_HEADS):
        g = hq // group
        kg = k_all[:, g * HEAD_DIM:(g + 1) * HEAD_DIM]
        vg = v_all[:, g * HEAD_DIM:(g + 1) * HEAD_DIM]
        q = q_ref[:, hq * HEAD_DIM:(hq + 1) * HEAD_DIM]
        s = lax.dot_general(q, kg, (((1,), (1,)), ((), ())), preferred_element_type=F32) + bias
        o_ref[:, hq * HEAD_DIM:(hq + 1) * HEAD_DIM] = _softmax_pv(s, vg, sink_ref[hq:hq + 1, 0:1])


def _window_gqa(qc, kc, vc, sink, n_ctx):
    bsz, s, _ = qc.shape
    t_len = s - n_ctx
    nb = t_len // WIN_BLOCK
    off = n_ctx // WIN_BLOCK

    def kv_spec(delta):
        return pl.BlockSpec((None, WIN_BLOCK, KC_W), lambda b, i: (b, jnp.clip(i + delta, 0, nb - 1) + off, 0))

    ctx_spec = pl.BlockSpec((None, n_ctx, KC_W), lambda b, i: (b, 0, 0))
    sink_tile = jnp.broadcast_to(sink.reshape(WIN_HEADS, 1), (WIN_HEADS, 128))
    return pl.pallas_call(
        functools.partial(_window_kernel, t_len=t_len),
        grid=(bsz, nb),
        in_specs=[pl.BlockSpec((None, WIN_BLOCK, QC_W), lambda b, i: (b, i + off, 0)),
                  kv_spec(-1), kv_spec(0), kv_spec(1), ctx_spec,
                  kv_spec(-1), kv_spec(0), kv_spec(1), ctx_spec,
                  pl.BlockSpec((WIN_HEADS, 128), lambda b, i: (0, 0))],
        out_specs=pl.BlockSpec((None, WIN_BLOCK, QC_W), lambda b, i: (b, i, 0)),
        out_shape=jax.ShapeDtypeStruct((bsz, t_len, QC_W), F32),
        compiler_params=_cparams("parallel", "parallel"),
        name="window_gqa",
    )(qc, kc, kc, kc, kc, vc, vc, vc, vc, sink_tile)


NA_QROWS = 2
NA_KROWS = NA_QROWS + NA_ROWS - 1


def _na_block_plan(rows):
    kh = min(NA_ROWS, rows)
    kw = min(NA_COLS, GRID_W)
    c = np.arange(GRID_W)
    cs = np.clip(c - kw // 2, 0, GRID_W - kw)
    kc = np.arange(GRID_W)
    col_valid = (kc[None, :] >= cs[:, None]) & (kc[None, :] < cs[:, None] + kw)
    dc = np.clip(kc[None, :] - c[:, None] + (NA_COLS - 1), 0, 2 * NA_COLS - 2)
    patterns, types, kbs = {}, [], []
    for r0 in range(0, rows, NA_QROWS):
        kb = int(np.clip(r0 - kh // 2, 0, rows - NA_KROWS))
        r = r0 + np.arange(NA_QROWS)
        rs = np.clip(r - kh // 2, 0, rows - kh)
        kr = kb + np.arange(NA_KROWS)
        row_valid = (kr[None, :] >= rs[:, None]) & (kr[None, :] < rs[:, None] + kh)
        dr = np.clip(kr[None, :] - r[:, None] + (NA_ROWS - 1), 0, 2 * NA_ROWS - 2)
        key = (row_valid.tobytes(), dr.tobytes())
        if key not in patterns:
            valid = row_valid[:, None, :, None] & col_valid[None, :, None, :]
            shape = valid.shape
            patterns[key] = (len(patterns), valid.reshape(NA_QROWS * GRID_W, NA_KROWS * GRID_W),
                             np.broadcast_to(dr[:, None, :, None], shape).reshape(valid.shape[0] * shape[1], -1),
                             np.broadcast_to(dc[None, :, None, :], shape).reshape(valid.shape[0] * shape[1], -1))
        types.append(patterns[key][0])
        kbs.append(kb)
    ordered = sorted(patterns.values(), key=lambda p: p[0])
    valid = np.stack([p[1] for p in ordered])
    dr = np.stack([p[2] for p in ordered])
    dc = np.stack([p[3] for p in ordered])
    return np.asarray(types, np.int32), np.asarray(kbs, np.int32), valid, dr, dc


def _na_kernel(types_ref, kbs_ref, q_ref, *refs):
    del types_ref, kbs_ref
    k_refs = refs[:NA_KROWS]
    v_refs = refs[NA_KROWS:2 * NA_KROWS]
    kx_ref, vx_ref, bias_ref, o_ref = refs[2 * NA_KROWS:]
    k_all = jnp.concatenate([r[...] for r in k_refs] + [kx_ref[...]], axis=0)
    v_all = jnp.concatenate([r[...] for r in v_refs] + [vx_ref[...]], axis=0)
    n_q = q_ref.shape[0]
    n_ctx = kx_ref.shape[0]
    zeros = jnp.zeros((n_q, n_ctx), F32)
    for hd in range(NA_HEADS):
        sl = slice(hd * HEAD_DIM, (hd + 1) * HEAD_DIM)
        s = lax.dot_general(q_ref[:, sl], k_all[:, sl], (((1,), (1,)), ((), ())), preferred_element_type=F32)
        s = s + jnp.concatenate([bias_ref[hd], zeros], axis=1)
        o_ref[:, sl] = _softmax_pv(s, v_all[:, sl])


def _neighbourhood_attention(qd, kd, vd, rpb, n_ctx):
    bsz, s, _ = qd.shape
    t_len = s - n_ctx
    rows = t_len // GRID_W
    types, kbs, valid, dr, dc = _na_block_plan(rows)
    n_types = valid.shape[0]
    dr_pat = dr.reshape(n_types, NA_QROWS, GRID_W, NA_KROWS, GRID_W)[:, :, 0, :, 0]
    dc_pat = dc.reshape(n_types, NA_QROWS, GRID_W, NA_KROWS, GRID_W)[0, 0, :, 0, :]
    oh_r = jnp.asarray(dr_pat[..., None] == np.arange(2 * NA_ROWS - 1), F32)
    oh_c = jnp.asarray(dc_pat[..., None] == np.arange(2 * NA_COLS - 1), F32)
    bias = jnp.einsum('tamr,hrs,cks->thacmk', oh_r, rpb, oh_c, precision=HI)
    bias = bias.reshape(n_types, NA_HEADS, NA_QROWS * GRID_W, NA_KROWS * GRID_W)
    bias = jnp.where(valid[:, None], bias, ATT_NEG)
    n_q = NA_QROWS * GRID_W
    off_q = n_ctx // n_q
    off_k = n_ctx // GRID_W

    def kv_spec(m):
        return pl.BlockSpec((None, GRID_W, QD_W), lambda b, i, types, kbs: (b, kbs[i] + m + off_k, 0))

    ctx_spec = pl.BlockSpec((None, n_ctx, QD_W), lambda b, i, types, kbs: (b, 0, 0))
    grid_spec = pltpu.PrefetchScalarGridSpec(
        num_scalar_prefetch=2,
        grid=(bsz, rows // NA_QROWS),
        in_specs=([pl.BlockSpec((None, n_q, QD_W), lambda b, i, types, kbs: (b, i + off_q, 0))]
                  + [kv_spec(m) for m in range(NA_KROWS)] + [kv_spec(m) for m in range(NA_KROWS)]
                  + [ctx_spec, ctx_spec,
                     pl.BlockSpec((None, NA_HEADS, n_q, NA_KROWS * GRID_W),
                                  lambda b, i, types, kbs: (types[i], 0, 0, 0))]),
        out_specs=pl.BlockSpec((None, n_q, QD_W), lambda b, i, types, kbs: (b, i, 0)),
    )
    return pl.pallas_call(
        _na_kernel,
        grid_spec=grid_spec,
        out_shape=jax.ShapeDtypeStruct((bsz, t_len, QD_W), F32),
        compiler_params=_cparams("parallel", "arbitrary"),
        name="neighbourhood_attention",
    )(jnp.asarray(types), jnp.asarray(kbs), qd, *([kd] * NA_KROWS), *([vd] * NA_KROWS), kd, vd, bias)


RWKV_CHUNK = 64
RWKV_SUPER = ROW_BLOCK // RWKV_CHUNK
HI = lax.Precision.HIGHEST


RWKV_PREC_PAIR = "bf16"
RWKV_PREC_INV = "bf16"
RWKV_PREC_OUT = "bf16"
RWKV_PREC_STATE = "bf16x3"


def _pdot(x, y, dims, mode):
    if mode == "f32":
        return lax.dot_general(x, y, dims, precision=HI, preferred_element_type=F32)
    xh = x.astype(BF16)
    yh = y.astype(BF16)
    out = lax.dot_general(xh, yh, dims, preferred_element_type=F32)
    if mode == "bf16x3":
        xl = (x - xh.astype(F32)).astype(BF16)
        yl = (y - yh.astype(F32)).astype(BF16)
        out = (out + lax.dot_general(xh, yl, dims, preferred_element_type=F32)
               + lax.dot_general(xl, yh, dims, preferred_element_type=F32))
    return out


def _split_dot(x, e_bf16):
    hi = x.astype(BF16)
    lo = (x - hi.astype(F32)).astype(BF16)
    return jnp.dot(hi, e_bf16, preferred_element_type=F32) + jnp.dot(lo, e_bf16, preferred_element_type=F32)


def _rwkv_prep_kernel(z_ref, zp_ref, zn_ref, mu_ref, w0_ref, w2_ref, a0_ref, a2_ref, g2_ref, kk_ref, ka_ref,
                      rk_ref, e_ref, r_o, v_o, kk_o, g_o, bonus_o, logw_o, kd_o, bb_o):
    i = pl.program_id(1)
    last = pl.num_programs(1) - 1
    tm = z_ref.shape[0]
    hw = RWKV_WIDTH
    z = z_ref[:, S5_WIDTH:]
    row = lax.broadcasted_iota(jnp.int32, (tm, 1), 0)
    prev_halo = jnp.where(i <= 1, 0.0, zp_ref[7:8, S5_WIDTH:])
    next_halo = jnp.where((i == 0) | (i == last), 0.0, zn_ref[0:1, S5_WIDTH:])
    prev = jnp.where(row == 0, prev_halo, pltpu.roll(z, 1, 0))
    nxt = jnp.where(row == tm - 1, next_halo, pltpu.roll(z, tm - 1, 0))
    zs = z + mu_ref[0:1, :] * (prev - z) + mu_ref[1:2, :] * (nxt - z)
    c0, c1, c2, c3, c4 = RWKV_SPLITS
    r = zs[:, :c0]
    k = zs[:, c0:c1]
    v = zs[:, c1:c2]
    e = e_ref[...]
    g_o[...] = jnp.dot(jax.nn.sigmoid(zs[:, c4:]).astype(BF16), g2_ref[...], preferred_element_type=F32)
    kk = k * kk_ref[...]
    kk = kk * lax.rsqrt(_split_dot(kk * kk, e) + 1e-12)
    w_lora = jnp.dot(jnp.tanh(zs[:, c2:c3]).astype(BF16), w2_ref[...], preferred_element_type=F32)
    a_lora = jnp.dot(zs[:, c3:c4].astype(BF16), a2_ref[...], preferred_element_type=F32)
    bonus = jnp.zeros((tm, hw), F32)
    for d in range(2):
        x = w0_ref[d:d + 1, :] + w_lora[:, d * hw:(d + 1) * hw]
        softplus_neg = jnp.maximum(-x, 0.0) + jnp.log(1.0 + jnp.exp(-jnp.abs(x)))
        logw_o[d] = -jnp.exp(-softplus_neg - 0.5)
        a = jax.nn.sigmoid(a0_ref[d:d + 1, :] + a_lora[:, d * hw:(d + 1) * hw])
        kd = k * (1.0 + (a - 1.0) * ka_ref[...])
        kd_o[d] = kd
        bb_o[d] = kk * a
        bonus = bonus + _split_dot(r * kd * rk_ref[...], e) * v
    r_o[...] = r
    v_o[...] = v
    kk_o[...] = kk
    bonus_o[...] = bonus


def _rwkv_chunk_kernel(r_ref, v_ref, kk_ref, logw_ref, kd_ref, bb_ref, mn_ref, ry_ref):
    fwd = pl.program_id(1) == 0
    cn = RWKV_CHUNK
    hd = HEAD_DIM
    rowi = lax.broadcasted_iota(jnp.int32, (cn, cn), 0)
    coli = lax.broadcasted_iota(jnp.int32, (cn, cn), 1)
    lag = (rowi - coli) * jnp.where(fwd, 1, -1)
    before_incl = lag >= 0
    before = lag > 0
    eye = (rowi == coli).astype(F32)
    logw = logw_ref[...]
    lg = jnp.dot(before_incl.astype(F32), logw, precision=HI, preferred_element_type=F32)
    lg_tot = jnp.sum(logw, axis=0, keepdims=True)
    a_t = -kk_ref[...] * jnp.exp(lg - logw)
    r_t = r_ref[...] * jnp.exp(lg)
    inv = jnp.exp(-lg)
    b_t = bb_ref[...] * inv
    k_t = kd_ref[...] * inv
    to_end = jnp.exp(lg_tot - lg)
    b_h = bb_ref[...] * to_end
    k_h = kd_ref[...] * to_end
    g_end = jnp.exp(lg_tot)
    nn = (((1,), (0,)), ((), ()))
    nt = (((1,), (1,)), ((), ()))
    tn = (((0,), (0,)), ((), ()))

    heads = range(RWKV_HEADS)
    sls = [slice(h * hd, (h + 1) * hd) for h in heads]
    vs = [v_ref[:, sl] for sl in sls]
    ps = [_pdot(jnp.concatenate([a_t[:, sl], r_t[:, sl]], axis=0),
                jnp.concatenate([b_t[:, sl], k_t[:, sl]], axis=0), nt, RWKV_PREC_PAIR) for sl in sls]
    n_ab = [jnp.where(before, p[:cn, :cn], 0.0) for p in ps]
    n_ak = [jnp.where(before, p[:cn, cn:], 0.0) for p in ps]
    n_rb = [jnp.where(before_incl, p[cn:, :cn], 0.0) for p in ps]
    n_rk = [jnp.where(before_incl, p[cn:, cn:], 0.0) for p in ps]
    tinv = [eye + n for n in n_ab]
    pw = n_ab
    for _ in range(int(math.log2(cn)) - 1):
        pw = [_pdot(x, x, nn, RWKV_PREC_INV) for x in pw]
        tinv = [t + _pdot(t, x, nn, RWKV_PREC_INV) for t, x in zip(tinv, pw)]
    akv = [_pdot(n, v, nn, RWKV_PREC_OUT) for n, v in zip(n_ak, vs)]
    rkv = [_pdot(n, v, nn, RWKV_PREC_OUT) for n, v in zip(n_rk, vs)]
    kv = [_pdot(k_h[:, sl], v, tn, RWKV_PREC_OUT) for sl, v in zip(sls, vs)]
    zz = [_pdot(t, jnp.concatenate([a_t[:, sl], w], axis=1), nn, RWKV_PREC_OUT)
          for t, sl, w in zip(tinv, sls, akv)]
    for h in heads:
        sl = sls[h]
        ry_ref[h] = _pdot(n_rb[h], zz[h], nn, RWKV_PREC_OUT) + jnp.concatenate([r_t[:, sl], rkv[h]], axis=1)
        mn_ref[h] = (_pdot(b_h[:, sl], zz[h], tn, RWKV_PREC_OUT)
                     + jnp.concatenate([eye * g_end[:, sl], kv[h]], axis=1))


def _rwkv_seq_kernel(mn_ref, ry_ref, y_ref, h_ref):
    @pl.when(pl.program_id(2) == 0)
    def _():
        h_ref[...] = jnp.zeros_like(h_ref)

    fwd = pl.program_id(1) == 0
    hd = HEAD_DIM
    for cc in range(RWKV_SUPER):
        c = jnp.where(fwd, cc, RWKV_SUPER - 1 - cc)
        row0 = pl.multiple_of(c * RWKV_CHUNK, RWKV_CHUNK)
        for h in range(RWKV_HEADS):
            mn = mn_ref[c, h]
            ry = ry_ref[c, h]
            state = h_ref[h]
            nn = (((1,), (0,)), ((), ()))
            y_ref[pl.ds(row0, RWKV_CHUNK), h * hd:(h + 1) * hd] = (
                _pdot(ry[:, :hd], state, nn, RWKV_PREC_STATE) + ry[:, hd:])
            h_ref[h] = _pdot(mn[:, :hd], state, nn, RWKV_PREC_STATE) + mn[:, hd:]


def _rwkv_post_kernel(y0_ref, y1_ref, bonus_ref, g_ref, gnw_ref, gnb_ref, e_ref, o_ref):
    y = y0_ref[...] + y1_ref[...]
    e = e_ref[...]
    mu = _split_dot(y, e) * (1.0 / HEAD_DIM)
    yc = y - mu
    var = _split_dot(yc * yc, e) * (1.0 / HEAD_DIM)
    yn = yc * lax.rsqrt(var + GN_EPS) * gnw_ref[...] + gnb_ref[...]
    o_ref[...] = (yn + bonus_ref[...]) * g_ref[...]


def _block_diag2(w):
    z = jnp.zeros_like(w[0])
    return jnp.concatenate([jnp.concatenate([w[0], z], 1), jnp.concatenate([z, w[1]], 1)], 0)


def _rwkv_mixer(z, shift_mu, w0, w2, a0, a2, g2, k_k, k_a, r_k, gn_w, gn_b):
    bsz, s, zw = z.shape
    hw = RWKV_WIDTH
    tm = ROW_BLOCK
    nblk = s // tm
    head_of = np.arange(hw) // HEAD_DIM
    e = jnp.asarray(head_of[:, None] == head_of[None, :], BF16)
    tok = pl.BlockSpec((None, tm, hw), lambda b, i: (b, i, 0))
    tok2 = pl.BlockSpec((None, 2, tm, hw), lambda b, i: (b, 0, i, 0))
    halo = tm // 8

    def full(shape):
        return pl.BlockSpec(shape, lambda b, i: (0,) * len(shape))

    r, v, kk, g, bonus, logw, kd, bb = pl.pallas_call(
        _rwkv_prep_kernel,
        grid=(bsz, nblk),
        in_specs=[pl.BlockSpec((None, tm, zw), lambda b, i: (b, i, 0)),
                  pl.BlockSpec((None, 8, zw), lambda b, i: (b, jnp.maximum(i * halo - 1, 0), 0)),
                  pl.BlockSpec((None, 8, zw), lambda b, i: (b, jnp.minimum((i + 1) * halo, s // 8 - 1), 0)),
                  full((2, RWKV_IN)), full((2, hw)), full((2 * RWKV_LORA_W, 2 * hw)), full((2, hw)),
                  full((2 * RWKV_LORA_A, 2 * hw)), full((RWKV_LORA_G, hw)), full((1, hw)), full((1, hw)),
                  full((1, hw)), full((hw, hw))],
        out_specs=[tok, tok, tok, tok, tok, tok2, tok2, tok2],
        out_shape=[jax.ShapeDtypeStruct((bsz, s, hw), F32)] * 5 + [jax.ShapeDtypeStruct((bsz, 2, s, hw), F32)] * 3,
        compiler_params=_cparams("parallel", "parallel"),
        name="rwkv_prep",
    )(z, z, z, shift_mu, w0, _block_diag2(w2).astype(BF16), a0, _block_diag2(a2).astype(BF16), g2.astype(BF16),
      k_k.reshape(1, hw), k_a.reshape(1, hw), r_k.reshape(1, hw), e)

    nch = s // RWKV_CHUNK
    ctok = pl.BlockSpec((None, RWKV_CHUNK, hw), lambda b, d, c: (b, c, 0))
    ctok2 = pl.BlockSpec((None, None, RWKV_CHUNK, hw), lambda b, d, c: (b, d, c, 0))
    mat_shape = jax.ShapeDtypeStruct((bsz, 2, nch, RWKV_HEADS, RWKV_CHUNK, 2 * HEAD_DIM), F32)
    mat_spec = pl.BlockSpec((None, None, None, RWKV_HEADS, RWKV_CHUNK, 2 * HEAD_DIM),
                            lambda b, d, c: (b, d, c, 0, 0, 0))
    mn, ry = pl.pallas_call(
        _rwkv_chunk_kernel,
        grid=(bsz, 2, nch),
        in_specs=[ctok, ctok, ctok, ctok2, ctok2, ctok2],
        out_specs=[mat_spec, mat_spec],
        out_shape=[mat_shape, mat_shape],
        compiler_params=_cparams("parallel", "parallel", "parallel"),
        name="rwkv_chunk",
    )(r, v, kk, logw, kd, bb)

    nsb = nch // RWKV_SUPER

    def sb_of(d, j):
        return jnp.where(d == 0, j, jnp.where(j == 0, 0, nsb - j))

    sup_shape = (bsz, 2, nsb, RWKV_SUPER, RWKV_HEADS, RWKV_CHUNK, 2 * HEAD_DIM)
    sup_spec = pl.BlockSpec((None, None, None, RWKV_SUPER, RWKV_HEADS, RWKV_CHUNK, 2 * HEAD_DIM),
                            lambda b, d, j: (b, d, sb_of(d, j), 0, 0, 0, 0))
    y = pl.pallas_call(
        _rwkv_seq_kernel,
        grid=(bsz, 2, nsb),
        in_specs=[sup_spec, sup_spec],
        out_specs=pl.BlockSpec((None, None, tm, hw), lambda b, d, j: (b, d, sb_of(d, j), 0)),
        out_shape=jax.ShapeDtypeStruct((bsz, 2, s, hw), F32),
        scratch_shapes=[pltpu.VMEM((RWKV_HEADS, HEAD_DIM, HEAD_DIM), F32)],
        compiler_params=_cparams("parallel", "parallel", "arbitrary"),
        name="rwkv_seq",
    )(mn.reshape(sup_shape), ry.reshape(sup_shape))

    vec = pl.BlockSpec((1, hw), lambda b, i: (0, 0))
    return pl.pallas_call(
        _rwkv_post_kernel,
        grid=(bsz, nblk),
        in_specs=[pl.BlockSpec((None, None, tm, hw), lambda b, i: (b, 0, i, 0)),
                  pl.BlockSpec((None, None, tm, hw), lambda b, i: (b, 1, i, 0)),
                  tok, tok, vec, vec, pl.BlockSpec((hw, hw), lambda b, i: (0, 0))],
        out_specs=tok,
        out_shape=jax.ShapeDtypeStruct((bsz, s, hw), F32),
        compiler_params=_cparams("parallel", "parallel"),
        name="rwkv_post",
    )(y, y, bonus, g, gn_w.reshape(1, hw), gn_b.reshape(1, hw), e)


def kernel(x, c, ctx, c_ctx,
           l0_w_mod, l0_b_mod, l0_w_in, l0_s5_lam_re, l0_s5_lam_im, l0_s5_log_dt, l0_s5_b_re, l0_s5_b_im,
           l0_s5_c_re, l0_s5_c_im, l0_s5_d, l0_w_glu, l0_b_glu, l0_rwkv_shift, l0_rwkv_w0, l0_rwkv_w2,
           l0_rwkv_a0, l0_rwkv_a2, l0_rwkv_g2, l0_rwkv_k_k, l0_rwkv_k_a, l0_rwkv_r_k, l0_rwkv_gn_w,
           l0_rwkv_gn_b, l0_w_out, l0_ln1_g, l0_ln1_b, l0_peer_wq, l0_peer_keys, l0_peer_u, l0_peer_v,
           l0_ln2_g, l0_ln2_b,
           l1_w_mod, l1_b_mod, l1_w_in, l1_sink, l1_rpb, l1_w_out, l1_ln1_g, l1_ln1_b, l1_peer_wq,
           l1_peer_keys, l1_peer_u, l1_peer_v, l1_ln2_g, l1_ln2_b):
    L = ctx.shape[1]
    h = jnp.concatenate([ctx, x], 1)

    tab = _mod_tables(c, c_ctx, l0_w_mod, l0_b_mod)
    z = _inproj(h, tab, 0, 1, l0_w_in.astype(BF16))
    ya = _s5_mixer(z[..., :S5_WIDTH], L, l0_s5_lam_re, l0_s5_lam_im, l0_s5_log_dt, l0_s5_b_re, l0_s5_b_im,
                   l0_s5_c_re, l0_s5_c_im, l0_s5_d, l0_w_glu, l0_b_glu)
    yb = _rwkv_mixer(z, l0_rwkv_shift, l0_rwkv_w0, l0_rwkv_w2, l0_rwkv_a0, l0_rwkv_a2,
                     l0_rwkv_g2, l0_rwkv_k_k, l0_rwkv_k_a, l0_rwkv_r_k, l0_rwkv_gn_w, l0_rwkv_gn_b)
    h = _outproj_ln(ya, yb, h, tab, 2, l0_w_out, l0_ln1_g, l0_ln1_b, 0)
    h = _peer_ln(h, tab, 0, l0_peer_wq, l0_peer_keys, l0_peer_u, l0_peer_v, l0_ln2_g, l0_ln2_b)

    tab = _mod_tables(c, c_ctx, l1_w_mod, l1_b_mod)
    qc, kc, vc, qd, kd, vd = _inproj_attn(h, tab, 0, 1, l1_w_in.astype(BF16), L)
    oc = _window_gqa(qc, kc, vc, l1_sink, L)
    od = _neighbourhood_attention(qd, kd, vd, l1_rpb, L)
    ctx_blocks = L // ROW_BLOCK
    hx = _outproj_ln(oc, od, h, tab, 2, l1_w_out, l1_ln1_g, l1_ln1_b, ctx_blocks)
    return _peer_ln(hx, tab, ctx_blocks, l1_peer_wq, l1_peer_keys, l1_peer_u, l1_peer_v, l1_ln2_g, l1_ln2_b)
```

```python
import functools
import math

import jax
import jax.numpy as jnp
import numpy as np
from jax import lax
from jax.experimental import pallas as pl
from jax.experimental.pallas import tpu as pltpu

F32 = jnp.float32
BF16 = jnp.bfloat16

D_MODEL = 1024
DEPTH = 2
GRID_W = 64
CTX_LEN = 256
HEAD_DIM = 64
S5_GROUP_CH = 16
S5_GROUPS = 16
S5_WIDTH = S5_GROUPS * S5_GROUP_CH
S5_STATE = 64
RWKV_HEADS = 8
RWKV_WIDTH = RWKV_HEADS * HEAD_DIM
RWKV_LORA_W = 64
RWKV_LORA_A = 64
RWKV_LORA_G = 128
RWKV_IN = 3 * RWKV_WIDTH + 2 * RWKV_LORA_W + 2 * RWKV_LORA_A + RWKV_LORA_G
RWKV_SPLITS = (RWKV_WIDTH, 2 * RWKV_WIDTH, 3 * RWKV_WIDTH,
               3 * RWKV_WIDTH + 2 * RWKV_LORA_W,
               3 * RWKV_WIDTH + 2 * RWKV_LORA_W + 2 * RWKV_LORA_A)
WIN_HEADS = 8
WIN_KV_HEADS = 2
WINDOW = 128
WIN_BLOCK = 128
NA_HEADS = 8
NA_ROWS = 8
NA_COLS = 16
ODD_SPLITS = (WIN_HEADS * HEAD_DIM, (WIN_HEADS + WIN_KV_HEADS) * HEAD_DIM,
              (WIN_HEADS + 2 * WIN_KV_HEADS) * HEAD_DIM,
              (WIN_HEADS + 2 * WIN_KV_HEADS + NA_HEADS) * HEAD_DIM,
              (WIN_HEADS + 2 * WIN_KV_HEADS + 2 * NA_HEADS) * HEAD_DIM)
ROPE_BASE = 10000.0
PEER_HEADS = 8
PEER_KEYS = 128
PEER_DK = 128
PEER_TOPK = 16
PEER_BLOCK = 128
ALPHA = (2.0 * DEPTH) ** 0.25
LN_EPS = 1e-5
GN_EPS = 64e-5

VMEM_LIMIT_BYTES = 56 * 1024 * 1024
ROW_BLOCK = 256


def _cparams(*sem):
    return pltpu.CompilerParams(dimension_semantics=sem, vmem_limit_bytes=VMEM_LIMIT_BYTES)


def _modulation_kernel(c_ref, w_ref, b_ref, o_ref):
    c = c_ref[...]
    s = c * jax.nn.sigmoid(c)
    o_ref[...] = jnp.dot(s.astype(BF16), w_ref[...].astype(BF16), preferred_element_type=F32) + b_ref[...]


def _modulation(cond, w_mod, b_mod):
    n, d = cond.shape
    nout = w_mod.shape[1]
    tn = 1536
    return pl.pallas_call(
        _modulation_kernel,
        grid=(nout // tn,),
        in_specs=[pl.BlockSpec((n, d), lambda j: (0, 0)),
                  pl.BlockSpec((d, tn), lambda j: (0, j)),
                  pl.BlockSpec((1, tn), lambda j: (0, j))],
        out_specs=pl.BlockSpec((n, tn), lambda j: (0, j)),
        out_shape=jax.ShapeDtypeStruct((n, nout), F32),
        compiler_params=_cparams("arbitrary"),
        name="modulation",
    )(cond, w_mod, b_mod.reshape(1, nout))


def _mod_tables(c, c_ctx, w_mod, b_mod):
    bsz = c.shape[0]
    cond = jnp.concatenate([c, c_ctx[None, :], jnp.zeros((8 - bsz - 1, c.shape[1]), F32)], 0)
    m = _modulation(cond, w_mod, b_mod)
    mx = m[:bsz].reshape(bsz, 1, 6, 1, D_MODEL)
    mc = jnp.broadcast_to(m[bsz].reshape(1, 1, 6, 1, D_MODEL), (bsz, 1, 6, 1, D_MODEL))
    return jnp.concatenate([mc, mx], 1)


def _tab_spec(which):
    return pl.BlockSpec((None, None, None, 1, D_MODEL),
                        lambda b, i: (b, jnp.minimum(i, 1), which, 0, 0))


def _inproj_kernel(x_ref, shift_ref, scale_ref, w_ref, o_ref):
    hm = x_ref[...] * (1.0 + scale_ref[...]) + shift_ref[...]
    o_ref[...] = jnp.dot(hm.astype(BF16), w_ref[...], preferred_element_type=F32)


def _inproj(h, tab, which_shift, which_scale, w_bf16):
    bsz, s, d = h.shape
    n = w_bf16.shape[1]
    return pl.pallas_call(
        _inproj_kernel,
        grid=(bsz, s // ROW_BLOCK),
        in_specs=[pl.BlockSpec((None, ROW_BLOCK, d), lambda b, i: (b, i, 0)),
                  _tab_spec(which_shift), _tab_spec(which_scale),
                  pl.BlockSpec((d, n), lambda b, i: (0, 0))],
        out_specs=pl.BlockSpec((None, ROW_BLOCK, n), lambda b, i: (b, i, 0)),
        out_shape=jax.ShapeDtypeStruct((bsz, s, n), F32),
        compiler_params=_cparams("parallel", "parallel"),
        name="inproj",
    )(h, tab, tab, w_bf16)


def _layer_norm_rows(z, g, b):
    mu = jnp.mean(z, -1, keepdims=True)
    zc = z - mu
    var = jnp.mean(zc * zc, -1, keepdims=True)
    return zc * lax.rsqrt(var + LN_EPS) * g + b


def _outproj_ln_kernel(ya_ref, yb_ref, h_ref, gate_ref, wa_ref, wb_ref, g_ref, b_ref, o_ref):
    proj = (jnp.dot(ya_ref[...].astype(BF16), wa_ref[...], preferred_element_type=F32)
            + jnp.dot(yb_ref[...].astype(BF16), wb_ref[...], preferred_element_type=F32))
    z = ALPHA * h_ref[...] + gate_ref[...] * proj
    o_ref[...] = _layer_norm_rows(z, g_ref[...], b_ref[...])


def _outproj_ln(ya, yb, h, tab, which_gate, w_out, ln_g, ln_b, row_off):
    bsz, rows, ka = ya.shape
    kb = yb.shape[-1]
    d = h.shape[-1]
    return pl.pallas_call(
        _outproj_ln_kernel,
        grid=(bsz, rows // ROW_BLOCK),
        in_specs=[pl.BlockSpec((None, ROW_BLOCK, ka), lambda b, i: (b, i, 0)),
                  pl.BlockSpec((None, ROW_BLOCK, kb), lambda b, i: (b, i, 0)),
                  pl.BlockSpec((None, ROW_BLOCK, d), lambda b, i: (b, i + row_off, 0)),
                  pl.BlockSpec((None, None, None, 1, d), lambda b, i: (b, jnp.minimum(i + row_off, 1), which_gate, 0, 0)),
                  pl.BlockSpec((ka, d), lambda b, i: (0, 0)),
                  pl.BlockSpec((kb, d), lambda b, i: (0, 0)),
                  pl.BlockSpec((1, d), lambda b, i: (0, 0)),
                  pl.BlockSpec((1, d), lambda b, i: (0, 0))],
        out_specs=pl.BlockSpec((None, ROW_BLOCK, d), lambda b, i: (b, i, 0)),
        out_shape=jax.ShapeDtypeStruct((bsz, rows, d), F32),
        compiler_params=_cparams("parallel", "parallel"),
        name="outproj_ln",
    )(ya, yb, h, tab, w_out[:ka].astype(BF16), w_out[ka:].astype(BF16), ln_g.reshape(1, d), ln_b.reshape(1, d))


PEER_NEG = -3.0e38
PEER_ETILE = 2048
PEER_SUBTILE = 256
SQRT_HALF = 0.7071067811865476


PEER_NO_RANK = 255.0


def _top_values(s, n, with_rank=False):
    vals = []
    rank = jnp.full(s.shape, PEER_NO_RANK, F32) if with_rank else None
    for k in range(n):
        m = jnp.max(s, axis=0, keepdims=True)
        vals.append(m)
        hit = s == m
        if with_rank:
            rank = jnp.where(hit, float(k), rank)
        s = jnp.where(hit, PEER_NEG, s)
    return (vals, rank) if with_rank else vals


def _peer_route_kernel(h_ref, shift_ref, scale_ref, wq_ref, keys_ref,
                       pin_ref, cnt_ref, e1_ref, r2_ref, e2_ref):
    pin = (h_ref[...] * (1.0 + scale_ref[...]) + shift_ref[...]).astype(BF16)
    pin_ref[...] = pin
    q = jnp.dot(pin, wq_ref[...], preferred_element_type=F32).astype(BF16)
    st = lax.dot_general(keys_ref[...], q, (((1,), (1,)), ((), ())), preferred_element_type=F32)
    for hd in range(PEER_HEADS):
        base = hd * 2 * PEER_KEYS
        s1 = st[base:base + PEER_KEYS]
        s2 = st[base + PEER_KEYS:base + 2 * PEER_KEYS]
        v1 = _top_values(s1, PEER_TOPK)
        v2, rank2 = _top_values(s2, PEER_TOPK, with_rank=True)
        v2all = jnp.concatenate(v2, axis=0)
        sums = [v1[a] + v2all for a in range(PEER_TOPK)]
        frontier = [sums[a][:(PEER_TOPK + 1) // (a + 1)] for a in range(PEER_TOPK)]
        top = _top_values(jnp.concatenate(frontier, axis=0), PEER_TOPK + 1)
        tau = 0.5 * (top[PEER_TOPK - 1] + top[PEER_TOPK])
        zsum = top[0] * 0.0
        for kk in range(PEER_TOPK):
            zsum = zsum + jnp.exp(top[kk] - top[0])
        need = tau - s1
        cnt = jnp.zeros_like(s1)
        for b in range(PEER_TOPK):
            cnt = cnt + jnp.where(v2[b] >= need, 1.0, 0.0)
        cnt_ref[hd] = jnp.where(s1 >= v1[PEER_TOPK - 1], cnt, 0.0)
        e1_ref[hd] = jnp.exp(s1 - v1[0])
        r2_ref[hd] = rank2.astype(BF16)
        e2_ref[hd] = (jnp.exp(s2 - v2[0]) / zsum).astype(BF16)


def _peer_dense_kernel(pin_ref, cnt_ref, e1_ref, r2_ref, e2_ref, u_ref, vt_ref,
                       h_ref, gate_ref, g_ref, b_ref, o_ref, acc_ref, w_ref):
    j = pl.program_id(2)

    last = pl.num_programs(2) - 1

    def apply_values(first):
        part = jnp.dot(vt_ref[...], w_ref[(j + 1) % 2], preferred_element_type=F32)
        if first:
            acc_ref[...] = part
        else:
            acc_ref[...] += part

    def score_and_gate():
        pin = pin_ref[...]
        cur = j % 2
        for sb in range(PEER_ETILE // PEER_SUBTILE):
            es = slice(sb * PEER_SUBTILE, (sb + 1) * PEER_SUBTILE)
            at = lax.dot_general(u_ref[es, :], pin, (((1,), (1,)), ((), ())), preferred_element_type=F32)
            for i2 in range(PEER_SUBTILE // PEER_KEYS):
                ii = sb * (PEER_SUBTILE // PEER_KEYS) + i2
                a = at[i2 * PEER_KEYS:(i2 + 1) * PEER_KEYS]
                act = (0.5 * a * (1.0 + lax.erf(a * SQRT_HALF))).astype(BF16)
                g = jnp.zeros(a.shape, BF16)
                for hd in range(PEER_HEADS):
                    cnt_row = cnt_ref[hd, pl.ds(ii, 1), :].astype(BF16)
                    c_row = e1_ref[hd, pl.ds(ii, 1), :].astype(BF16)
                    g = g + jnp.where(r2_ref[hd] < cnt_row, e2_ref[hd] * c_row, jnp.zeros((), BF16))
                w_ref[cur, ii * PEER_KEYS:(ii + 1) * PEER_KEYS, :] = g * act

    @pl.when(j == 0)
    def _():
        score_and_gate()

    @pl.when(j == 1)
    def _():
        apply_values(first=True)
        score_and_gate()

    @pl.when((j > 1) & (j < last))
    def _():
        apply_values(first=False)
        score_and_gate()

    @pl.when(j == last)
    def _():
        apply_values(first=False)
        z = ALPHA * h_ref[...] + gate_ref[...] * acc_ref[...].T
        o_ref[...] = _layer_norm_rows(z, g_ref[...], b_ref[...])


def _peer_block_diag_keys(sub_keys):
    hh, two, nk, dh = sub_keys.shape
    eye = jnp.eye(hh * two, dtype=F32)
    kb = sub_keys.reshape(hh * two, nk, dh)
    return jnp.einsum('akd,ab->akbd', kb, eye).reshape(hh * two * nk, hh * two * dh)


def _peer_ln(h, tab, seg_off, wq, sub_keys, u_tab, v_tab, ln_g, ln_b):
    bsz, s, d = h.shape
    tm = ROW_BLOCK
    nblk = s // tm
    hk = PEER_HEADS * 2 * PEER_KEYS
    keys_bd = _peer_block_diag_keys(sub_keys).astype(BF16)

    def tab_spec3(which):
        return pl.BlockSpec((None, None, None, 1, d),
                            lambda b, i, *_: (b, jnp.minimum(i + seg_off, 1), which, 0, 0))

    route_dims = (bsz, nblk, PEER_HEADS, PEER_KEYS, tm)
    route_spec = pl.BlockSpec((None, None, PEER_HEADS, PEER_KEYS, tm), lambda b, i: (b, i, 0, 0, 0))
    pin, cnt, e1, r2, e2 = pl.pallas_call(
        _peer_route_kernel,
        grid=(bsz, nblk),
        in_specs=[pl.BlockSpec((None, tm, d), lambda b, i: (b, i, 0)),
                  tab_spec3(3), tab_spec3(4),
                  pl.BlockSpec((d, PEER_HEADS * PEER_DK), lambda b, i: (0, 0)),
                  pl.BlockSpec((hk, PEER_HEADS * PEER_DK), lambda b, i: (0, 0))],
        out_specs=[pl.BlockSpec((None, tm, d), lambda b, i: (b, i, 0)),
                   route_spec, route_spec, route_spec, route_spec],
        out_shape=[jax.ShapeDtypeStruct((bsz, nblk * tm, d), BF16),
                   jax.ShapeDtypeStruct(route_dims, F32), jax.ShapeDtypeStruct(route_dims, F32),
                   jax.ShapeDtypeStruct(route_dims, BF16), jax.ShapeDtypeStruct(route_dims, BF16)],
        compiler_params=_cparams("parallel", "parallel"),
        name="peer_route",
    )(h, tab, tab, wq.astype(BF16), keys_bd)

    n_exp = u_tab.shape[0]
    rows_per_step = PEER_ETILE // PEER_KEYS
    full_spec = pl.BlockSpec((None, None, PEER_HEADS, PEER_KEYS, tm), lambda b, i, j: (b, i, 0, 0, 0))
    n_tiles = n_exp // PEER_ETILE
    row_spec = pl.BlockSpec((None, None, PEER_HEADS, rows_per_step, tm),
                            lambda b, i, j: (b, i, 0, jnp.minimum(j, n_tiles - 1), 0))
    return pl.pallas_call(
        _peer_dense_kernel,
        grid=(bsz, nblk, n_tiles + 1),
        in_specs=[pl.BlockSpec((None, tm, d), lambda b, i, j: (b, i, 0)),
                  row_spec, row_spec, full_spec, full_spec,
                  pl.BlockSpec((PEER_ETILE, d), lambda b, i, j: (jnp.minimum(j, n_tiles - 1), 0)),
                  pl.BlockSpec((d, PEER_ETILE), lambda b, i, j: (0, jnp.maximum(j - 1, 0))),
                  pl.BlockSpec((None, tm, d), lambda b, i, j: (b, i, 0)),
                  tab_spec3(5),
                  pl.BlockSpec((1, d), lambda b, i, j: (0, 0)),
                  pl.BlockSpec((1, d), lambda b, i, j: (0, 0))],
        out_specs=pl.BlockSpec((None, tm, d), lambda b, i, j: (b, i, 0)),
        out_shape=jax.ShapeDtypeStruct((bsz, nblk * tm, d), F32),
        scratch_shapes=[pltpu.VMEM((d, tm), F32), pltpu.VMEM((2, PEER_ETILE, tm), BF16)],
        compiler_params=_cparams("parallel", "parallel", "arbitrary"),
        name="peer_dense",
    )(pin, cnt, e1, r2, e2, u_tab.astype(BF16), v_tab.T.astype(BF16), h, tab,
      ln_g.reshape(1, d), ln_b.reshape(1, d))


S5_CHAINS = 8
S5_TCHUNK = 128
S5_NSTATE = S5_GROUPS * S5_STATE


def _s5_scan_kernel(uf_ref, ub_ref, wb_ref, wc_ref, a_ref, yf_ref, yb_ref,
                    fre_ref, fim_ref, bre_ref, bim_ref, h_ref):
    @pl.when(pl.program_id(0) == 0)
    def _():
        h_ref[...] = jnp.zeros_like(h_ref)

    rows = uf_ref.shape[0]
    steps = rows // S5_CHAINS
    uf = uf_ref[...].astype(BF16)
    ub = ub_ref[...].astype(BF16)
    fre_ref[...] = jnp.dot(uf, wb_ref[0], preferred_element_type=F32)
    fim_ref[...] = jnp.dot(uf, wb_ref[1], preferred_element_type=F32)
    bre_ref[...] = jnp.dot(ub, wb_ref[2], preferred_element_type=F32)
    bim_ref[...] = jnp.dot(ub, wb_ref[3], preferred_element_type=F32)
    a_re = a_ref[0]
    a_im = a_ref[1]
    is_fwd = lax.broadcasted_iota(jnp.int32, (S5_CHAINS, S5_NSTATE), 0) < (S5_CHAINS // 2)

    def step(t, carry):
        h_re, h_im = carry
        of = pl.multiple_of(t * S5_CHAINS, S5_CHAINS)
        ob = pl.multiple_of((steps - 1 - t) * S5_CHAINS, S5_CHAINS)
        n_re = a_re * h_re - a_im * h_im + jnp.where(is_fwd, fre_ref[pl.ds(of, S5_CHAINS), :],
                                                     bre_ref[pl.ds(ob, S5_CHAINS), :])
        n_im = a_re * h_im + a_im * h_re + jnp.where(is_fwd, fim_ref[pl.ds(of, S5_CHAINS), :],
                                                     bim_ref[pl.ds(ob, S5_CHAINS), :])
        fre_ref[pl.ds(of, S5_CHAINS), :] = n_re
        fim_ref[pl.ds(of, S5_CHAINS), :] = n_im
        bre_ref[pl.ds(ob, S5_CHAINS), :] = n_re
        bim_ref[pl.ds(ob, S5_CHAINS), :] = n_im
        return n_re, n_im

    h_re, h_im = lax.fori_loop(0, steps, step, (h_ref[0], h_ref[1]), unroll=2)
    h_ref[0] = h_re
    h_ref[1] = h_im
    yf_ref[...] = (jnp.dot(fre_ref[...].astype(BF16), wc_ref[0], preferred_element_type=F32)
                   - jnp.dot(fim_ref[...].astype(BF16), wc_ref[1], preferred_element_type=F32))
    yb_ref[...] = (jnp.dot(bre_ref[...].astype(BF16), wc_ref[2], preferred_element_type=F32)
                   - jnp.dot(bim_ref[...].astype(BF16), wc_ref[3], preferred_element_type=F32))


def _s5_glu_kernel(u_ref, y0_ref, y1_ref, d_ref, w_ref, b_ref, o_ref):
    y = u_ref[...] * d_ref[...] + y0_ref[...] + y1_ref[...]
    y = 0.5 * y * (1.0 + lax.erf(y * SQRT_HALF))
    gate = jnp.dot(y.astype(BF16), w_ref[...], preferred_element_type=F32) + b_ref[...]
    o_ref[...] = y * jax.nn.sigmoid(gate)


def _s5_params(lam_re, lam_im, log_dt, b_re, b_im, c_re, c_im):
    eye = jnp.eye(S5_GROUPS, dtype=F32)
    wb, wc, a_rows = [], [], [[], []]
    for d in range(2):
        lr, li = lam_re[d], lam_im[d]
        dt = jnp.exp(log_dt[d])[:, None]
        mag = jnp.exp(lr * dt)
        ab_re, ab_im = mag * jnp.cos(li * dt), mag * jnp.sin(li * dt)
        den = lr * lr + li * li
        nr = ab_re - 1.0
        coef_re = (nr * lr + ab_im * li) / den
        coef_im = (ab_im * lr - nr * li) / den
        bb_re = coef_re[..., None] * b_re[d] - coef_im[..., None] * b_im[d]
        bb_im = coef_re[..., None] * b_im[d] + coef_im[..., None] * b_re[d]
        for bb in (bb_re, bb_im):
            wb.append(jnp.einsum('gpi,gh->gihp', bb, eye).reshape(S5_WIDTH, S5_NSTATE))
        for cc in (c_re[d], c_im[d]):
            wc.append(jnp.einsum('gip,gh->gphi', cc, eye).reshape(S5_NSTATE, S5_WIDTH))
        for k, ab in enumerate((ab_re, ab_im)):
            a_rows[k].append(jnp.broadcast_to(ab.reshape(1, S5_NSTATE), (S5_CHAINS // 2, S5_NSTATE)))
    a_tiles = jnp.stack([jnp.concatenate(a_rows[0], 0), jnp.concatenate(a_rows[1], 0)])
    return jnp.stack(wb).astype(BF16), jnp.stack(wc).astype(BF16), a_tiles


def _s5_mixer(u, n_ctx, lam_re, lam_im, log_dt, b_re, b_im, c_re, c_im, d_skip, w_glu, b_glu):
    bsz, s, w = u.shape
    assert 2 * bsz == S5_CHAINS
    wb, wc, a_tiles = _s5_params(lam_re, lam_im, log_dt, b_re, b_im, c_re, c_im)
    ut = jnp.transpose(u, (1, 0, 2))
    chain = jnp.concatenate([ut, ut], 1).reshape(s * S5_CHAINS, w)
    rows = S5_TCHUNK * S5_CHAINS
    n_chunks = s // S5_TCHUNK
    ctx_chunks = n_ctx // S5_TCHUNK

    def mirror(i):
        return jnp.where(i < ctx_chunks, ctx_chunks - 1 - i, n_chunks - 1 - (i - ctx_chunks))

    state_buf = pltpu.VMEM((rows, S5_NSTATE), F32)
    y_shape = jax.ShapeDtypeStruct((s * S5_CHAINS, w), F32)
    yf, yb = pl.pallas_call(
        _s5_scan_kernel,
        grid=(n_chunks,),
        in_specs=[pl.BlockSpec((rows, w), lambda i: (i, 0)),
                  pl.BlockSpec((rows, w), lambda i: (mirror(i), 0)),
                  pl.BlockSpec((4, w, S5_NSTATE), lambda i: (0, 0, 0)),
                  pl.BlockSpec((4, S5_NSTATE, w), lambda i: (0, 0, 0)),
                  pl.BlockSpec((2, S5_CHAINS, S5_NSTATE), lambda i: (0, 0, 0))],
        out_specs=[pl.BlockSpec((rows, w), lambda i: (i, 0)),
                   pl.BlockSpec((rows, w), lambda i: (mirror(i), 0))],
        out_shape=[y_shape, y_shape],
        scratch_shapes=[state_buf, state_buf, state_buf, state_buf,
                        pltpu.VMEM((2, S5_CHAINS, S5_NSTATE), F32)],
        compiler_params=_cparams("arbitrary"),
        name="s5_scan",
    )(chain, chain, wb, wc, a_tiles)
    y_fwd = jnp.transpose(yf.reshape(s, S5_CHAINS, w)[:, :bsz], (1, 0, 2))
    y_bwd = jnp.transpose(yb.reshape(s, S5_CHAINS, w)[:, bsz:], (1, 0, 2))
    tok = pl.BlockSpec((None, ROW_BLOCK, w), lambda b, i: (b, i, 0))
    vec = pl.BlockSpec((1, w), lambda b, i: (0, 0))
    return pl.pallas_call(
        _s5_glu_kernel,
        grid=(bsz, s // ROW_BLOCK),
        in_specs=[tok, tok, tok, vec, pl.BlockSpec((w, w), lambda b, i: (0, 0)), vec],
        out_specs=tok,
        out_shape=jax.ShapeDtypeStruct((bsz, s, w), F32),
        compiler_params=_cparams("parallel", "parallel"),
        name="s5_glu",
    )(u, y_fwd, y_bwd, d_skip.reshape(1, w), w_glu.astype(BF16), b_glu.reshape(1, w))


ATT_SCALE = HEAD_DIM ** -0.5
ATT_NEG = -1e30
ROPE_QUARTER = HEAD_DIM // 4
QC_W = WIN_HEADS * HEAD_DIM
KC_W = WIN_KV_HEADS * HEAD_DIM
QD_W = NA_HEADS * HEAD_DIM


def _rope_tables(n_ctx, t):
    pos = np.arange(t)
    rowcol = np.stack([pos // GRID_W, pos % GRID_W], 1).astype(np.float32)
    lane = np.arange(QC_W)
    j = lane % HEAD_DIM
    freqs = (ROPE_BASE ** (-np.arange(ROPE_QUARTER, dtype=np.float32) / ROPE_QUARTER)).astype(np.float32)
    ang = rowcol[:, j // (HEAD_DIM // 2)] * freqs[j % ROPE_QUARTER][None, :]
    ang = jnp.asarray(ang, F32)
    sign = np.where((j % (HEAD_DIM // 2)) < ROPE_QUARTER, -1.0, 1.0).astype(np.float32)
    cos = jnp.concatenate([jnp.ones((n_ctx, QC_W), F32), jnp.cos(ang)], 0)
    sin = jnp.concatenate([jnp.zeros((n_ctx, QC_W), F32), jnp.sin(ang) * sign[None, :]], 0)
    return cos, sin


def _rope(x, cos, sin):
    n = x.shape[-1]
    lane = lax.broadcasted_iota(jnp.int32, (1, n), 1)
    first = (lane % (HEAD_DIM // 2)) < ROPE_QUARTER
    partner = jnp.where(first, pltpu.roll(x, n - ROPE_QUARTER, 1), pltpu.roll(x, ROPE_QUARTER, 1))
    return x * cos + partner * sin


def _inproj_attn_kernel(x_ref, shift_ref, scale_ref, w_ref, cos_ref, sin_ref,
                        qc_ref, kc_ref, vc_ref, qd_ref, kd_ref, vd_ref):
    hm = x_ref[...] * (1.0 + scale_ref[...]) + shift_ref[...]
    z = jnp.dot(hm.astype(BF16), w_ref[...], preferred_element_type=F32)
    c0, c1, c2, c3, c4 = ODD_SPLITS
    cos = cos_ref[...]
    sin = sin_ref[...]
    qc_ref[...] = (_rope(z[:, 0:c0], cos, sin) * ATT_SCALE).astype(BF16)
    kc_ref[...] = _rope(z[:, c0:c1], cos[:, :KC_W], sin[:, :KC_W]).astype(BF16)
    vc_ref[...] = z[:, c1:c2].astype(BF16)
    qd_ref[...] = (z[:, c2:c3] * ATT_SCALE).astype(BF16)
    kd_ref[...] = z[:, c3:c4].astype(BF16)
    vd_ref[...] = z[:, c4:].astype(BF16)


def _inproj_attn(h, tab, which_shift, which_scale, w_bf16, n_ctx):
    bsz, s, d = h.shape
    n = w_bf16.shape[1]
    cos, sin = _rope_tables(n_ctx, s - n_ctx)
    widths = (QC_W, KC_W, KC_W, QD_W, QD_W, QD_W)
    return pl.pallas_call(
        _inproj_attn_kernel,
        grid=(bsz, s // ROW_BLOCK),
        in_specs=[pl.BlockSpec((None, ROW_BLOCK, d), lambda b, i: (b, i, 0)),
                  _tab_spec(which_shift), _tab_spec(which_scale),
                  pl.BlockSpec((d, n), lambda b, i: (0, 0)),
                  pl.BlockSpec((ROW_BLOCK, QC_W), lambda b, i: (i, 0)),
                  pl.BlockSpec((ROW_BLOCK, QC_W), lambda b, i: (i, 0))],
        out_specs=[pl.BlockSpec((None, ROW_BLOCK, w), lambda b, i: (b, i, 0)) for w in widths],
        out_shape=[jax.ShapeDtypeStruct((bsz, s, w), BF16) for w in widths],
        compiler_params=_cparams("parallel", "parallel"),
        name="inproj_attn",
    )(h, tab, tab, w_bf16, cos, sin)


def _softmax_pv_heads(scores, values, extra_logits=None):
    ms = [jnp.max(s, axis=1, keepdims=True) for s in scores]
    if extra_logits is not None:
        ms = [jnp.maximum(m, x) for m, x in zip(ms, extra_logits)]
    ps = [jnp.exp(s - m) for s, m in zip(scores, ms)]
    dens = [jnp.sum(p, axis=1, keepdims=True) for p in ps]
    if extra_logits is not None:
        dens = [d + jnp.exp(x - m) for d, x, m in zip(dens, extra_logits, ms)]
    outs = [jnp.dot(p.astype(BF16), v, preferred_element_type=F32) for p, v in zip(ps, values)]
    return [o / d for o, d in zip(outs, dens)]


def _window_kernel(q_ref, kp_ref, kc_ref, kn_ref, kx_ref, vp_ref, vc_ref, vn_ref, vx_ref, sink_ref, o_ref, *, t_len):
    i = pl.program_id(1)
    wb = WIN_BLOCK
    qi = lax.broadcasted_iota(jnp.int32, (wb, 1), 0)
    kj = lax.broadcasted_iota(jnp.int32, (1, 3 * wb), 1) - wb
    kabs = i * wb + kj
    valid = (jnp.abs(kj - qi) <= WINDOW) & (kabs >= 0) & (kabs < t_len)
    n_ctx = kx_ref.shape[0]
    bias = jnp.concatenate([jnp.where(valid, 0.0, ATT_NEG), jnp.zeros((wb, n_ctx), F32)], axis=1)
    k_all = jnp.concatenate([kp_ref[...], kc_ref[...], kn_ref[...], kx_ref[...]], axis=0)
    v_all = jnp.concatenate([vp_ref[...], vc_ref[...], vn_ref[...], vx_ref[...]], axis=0)
    group = WIN_HEADS // WIN_KV_HEADS
    nt = (((1,), (1,)), ((), ()))
    kgs = [k_all[:, g * HEAD_DIM:(g + 1) * HEAD_DIM] for g in range(WIN_KV_HEADS)]
    vgs = [v_all[:, g * HEAD_DIM:(g + 1) * HEAD_DIM] for g in range(WIN_KV_HEADS)]
    scores = [lax.dot_general(q_ref[:, hq * HEAD_DIM:(hq + 1) * HEAD_DIM], kgs[hq // group], nt,
                              preferred_element_type=F32) + bias for hq in range(WIN_HEADS)]
    outs = _softmax_pv_heads(scores, [vgs[hq // group] for hq in range(WIN_HEADS)],
                             [sink_ref[hq:hq + 1, 0:1] for hq in range(WIN_HEADS)])
    for hq in range(WIN_HEADS):
        o_ref[:, hq * HEAD_DIM:(hq + 1) * HEAD_DIM] = outs[hq]


def _window_gqa(qc, kc, vc, sink, n_ctx):
    bsz, s, _ = qc.shape
    t_len = s - n_ctx
    nb = t_len // WIN_BLOCK
    off = n_ctx // WIN_BLOCK

    def kv_spec(delta):
        return pl.BlockSpec((None, WIN_BLOCK, KC_W), lambda b, i: (b, jnp.clip(i + delta, 0, nb - 1) + off, 0))

    ctx_spec = pl.BlockSpec((None, n_ctx, KC_W), lambda b, i: (b, 0, 0))
    sink_tile = jnp.broadcast_to(sink.reshape(WIN_HEADS, 1), (WIN_HEADS, 128))
    return pl.pallas_call(
        functools.partial(_window_kernel, t_len=t_len),
        grid=(bsz, nb),
        in_specs=[pl.BlockSpec((None, WIN_BLOCK, QC_W), lambda b, i: (b, i + off, 0)),
                  kv_spec(-1), kv_spec(0), kv_spec(1), ctx_spec,
                  kv_spec(-1), kv_spec(0), kv_spec(1), ctx_spec,
                  pl.BlockSpec((WIN_HEADS, 128), lambda b, i: (0, 0))],
        out_specs=pl.BlockSpec((None, WIN_BLOCK, QC_W), lambda b, i: (b, i, 0)),
        out_shape=jax.ShapeDtypeStruct((bsz, t_len, QC_W), F32),
        compiler_params=_cparams("parallel", "parallel"),
        name="window_gqa",
    )(qc, kc, kc, kc, kc, vc, vc, vc, vc, sink_tile)


NA_QROWS = 2
NA_KROWS = NA_QROWS + NA_ROWS - 1


def _na_block_plan(rows):
    kh = min(NA_ROWS, rows)
    kw = min(NA_COLS, GRID_W)
    c = np.arange(GRID_W)
    cs = np.clip(c - kw // 2, 0, GRID_W - kw)
    kc = np.arange(GRID_W)
    col_valid = (kc[None, :] >= cs[:, None]) & (kc[None, :] < cs[:, None] + kw)
    dc = np.clip(kc[None, :] - c[:, None] + (NA_COLS - 1), 0, 2 * NA_COLS - 2)
    patterns, types, kbs = {}, [], []
    for r0 in range(0, rows, NA_QROWS):
        kb = int(np.clip(r0 - kh // 2, 0, rows - NA_KROWS))
        r = r0 + np.arange(NA_QROWS)
        rs = np.clip(r - kh // 2, 0, rows - kh)
        kr = kb + np.arange(NA_KROWS)
        row_valid = (kr[None, :] >= rs[:, None]) & (kr[None, :] < rs[:, None] + kh)
        dr = np.clip(kr[None, :] - r[:, None] + (NA_ROWS - 1), 0, 2 * NA_ROWS - 2)
        key = (row_valid.tobytes(), dr.tobytes())
        if key not in patterns:
            valid = row_valid[:, None, :, None] & col_valid[None, :, None, :]
            shape = valid.shape
            patterns[key] = (len(patterns), valid.reshape(NA_QROWS * GRID_W, NA_KROWS * GRID_W),
                             np.broadcast_to(dr[:, None, :, None], shape).reshape(valid.shape[0] * shape[1], -1),
                             np.broadcast_to(dc[None, :, None, :], shape).reshape(valid.shape[0] * shape[1], -1))
        types.append(patterns[key][0])
        kbs.append(kb)
    ordered = sorted(patterns.values(), key=lambda p: p[0])
    valid = np.stack([p[1] for p in ordered])
    dr = np.stack([p[2] for p in ordered])
    dc = np.stack([p[3] for p in ordered])
    return np.asarray(types, np.int32), np.asarray(kbs, np.int32), valid, dr, dc


def _na_kernel(types_ref, kbs_ref, q_ref, *refs):
    del types_ref, kbs_ref
    k_refs = refs[:NA_KROWS]
    v_refs = refs[NA_KROWS:2 * NA_KROWS]
    kx_ref, vx_ref, bias_ref, o_ref = refs[2 * NA_KROWS:]
    k_all = jnp.concatenate([r[...] for r in k_refs] + [kx_ref[...]], axis=0)
    v_all = jnp.concatenate([r[...] for r in v_refs] + [vx_ref[...]], axis=0)
    n_q = q_ref.shape[0]
    n_ctx = kx_ref.shape[0]
    zeros = jnp.zeros((n_q, n_ctx), F32)
    nt = (((1,), (1,)), ((), ()))
    sls = [slice(hd * HEAD_DIM, (hd + 1) * HEAD_DIM) for hd in range(NA_HEADS)]
    scores = [lax.dot_general(q_ref[:, sl], k_all[:, sl], nt, preferred_element_type=F32)
              + jnp.concatenate([bias_ref[hd], zeros], axis=1) for hd, sl in enumerate(sls)]
    outs = _softmax_pv_heads(scores, [v_all[:, sl] for sl in sls])
    for sl, out in zip(sls, outs):
        o_ref[:, sl] = out


def _neighbourhood_attention(qd, kd, vd, rpb, n_ctx):
    bsz, s, _ = qd.shape
    t_len = s - n_ctx
    rows = t_len // GRID_W
    types, kbs, valid, dr, dc = _na_block_plan(rows)
    n_types = valid.shape[0]
    dr_pat = dr.reshape(n_types, NA_QROWS, GRID_W, NA_KROWS, GRID_W)[:, :, 0, :, 0]
    dc_pat = dc.reshape(n_types, NA_QROWS, GRID_W, NA_KROWS, GRID_W)[0, 0, :, 0, :]
    oh_r = jnp.asarray(dr_pat[..., None] == np.arange(2 * NA_ROWS - 1), F32)
    oh_c = jnp.asarray(dc_pat[..., None] == np.arange(2 * NA_COLS - 1), F32)
    bias = jnp.einsum('tamr,hrs,cks->thacmk', oh_r, rpb, oh_c, precision=HI)
    bias = bias.reshape(n_types, NA_HEADS, NA_QROWS * GRID_W, NA_KROWS * GRID_W)
    bias = jnp.where(valid[:, None], bias, ATT_NEG)
    n_q = NA_QROWS * GRID_W
    off_q = n_ctx // n_q
    off_k = n_ctx // GRID_W

    def kv_spec(m):
        return pl.BlockSpec((None, GRID_W, QD_W), lambda b, i, types, kbs: (b, kbs[i] + m + off_k, 0))

    ctx_spec = pl.BlockSpec((None, n_ctx, QD_W), lambda b, i, types, kbs: (b, 0, 0))
    grid_spec = pltpu.PrefetchScalarGridSpec(
        num_scalar_prefetch=2,
        grid=(bsz, rows // NA_QROWS),
        in_specs=([pl.BlockSpec((None, n_q, QD_W), lambda b, i, types, kbs: (b, i + off_q, 0))]
                  + [kv_spec(m) for m in range(NA_KROWS)] + [kv_spec(m) for m in range(NA_KROWS)]
                  + [ctx_spec, ctx_spec,
                     pl.BlockSpec((None, NA_HEADS, n_q, NA_KROWS * GRID_W),
                                  lambda b, i, types, kbs: (types[i], 0, 0, 0))]),
        out_specs=pl.BlockSpec((None, n_q, QD_W), lambda b, i, types, kbs: (b, i, 0)),
    )
    return pl.pallas_call(
        _na_kernel,
        grid_spec=grid_spec,
        out_shape=jax.ShapeDtypeStruct((bsz, t_len, QD_W), F32),
        compiler_params=_cparams("parallel", "arbitrary"),
        name="neighbourhood_attention",
    )(jnp.asarray(types), jnp.asarray(kbs), qd, *([kd] * NA_KROWS), *([vd] * NA_KROWS), kd, vd, bias)


RWKV_CHUNK = 64
RWKV_SUPER = ROW_BLOCK // RWKV_CHUNK
RWKV_CHUNKS_PER_STEP = 4
HI = lax.Precision.HIGHEST


RWKV_PREC_PAIR = "bf16"
RWKV_PREC_INV = "bf16"
RWKV_PREC_OUT = "bf16"
RWKV_PREC_STATE = "bf16x3"


def _pdot(x, y, dims, mode):
    if mode == "f32":
        return lax.dot_general(x, y, dims, precision=HI, preferred_element_type=F32)
    xh = x.astype(BF16)
    yh = y.astype(BF16)
    out = lax.dot_general(xh, yh, dims, preferred_element_type=F32)
    if mode == "bf16x3":
        xl = (x - xh.astype(F32)).astype(BF16)
        yl = (y - yh.astype(F32)).astype(BF16)
        out = (out + lax.dot_general(xh, yl, dims, preferred_element_type=F32)
               + lax.dot_general(xl, yh, dims, preferred_element_type=F32))
    return out


def _split_dot(x, e_bf16):
    hi = x.astype(BF16)
    lo = (x - hi.astype(F32)).astype(BF16)
    return jnp.dot(hi, e_bf16, preferred_element_type=F32) + jnp.dot(lo, e_bf16, preferred_element_type=F32)


def _rwkv_prep_kernel(z_ref, zp_ref, zn_ref, mu_ref, w0_ref, w2_ref, a0_ref, a2_ref, g2_ref, kk_ref, ka_ref,
                      rk_ref, e_ref, r_o, v_o, kk_o, g_o, bonus_o, logw_o, kd_o, bb_o):
    i = pl.program_id(1)
    last = pl.num_programs(1) - 1
    tm = z_ref.shape[0]
    hw = RWKV_WIDTH
    z = z_ref[:, S5_WIDTH:]
    row = lax.broadcasted_iota(jnp.int32, (tm, 1), 0)
    prev_halo = jnp.where(i <= 1, 0.0, zp_ref[7:8, S5_WIDTH:])
    next_halo = jnp.where((i == 0) | (i == last), 0.0, zn_ref[0:1, S5_WIDTH:])
    prev = jnp.where(row == 0, prev_halo, pltpu.roll(z, 1, 0))
    nxt = jnp.where(row == tm - 1, next_halo, pltpu.roll(z, tm - 1, 0))
    zs = z + mu_ref[0:1, :] * (prev - z) + mu_ref[1:2, :] * (nxt - z)
    c0, c1, c2, c3, c4 = RWKV_SPLITS
    r = zs[:, :c0]
    k = zs[:, c0:c1]
    v = zs[:, c1:c2]
    e = e_ref[...]
    g_o[...] = jnp.dot(jax.nn.sigmoid(zs[:, c4:]).astype(BF16), g2_ref[...], preferred_element_type=F32)
    kk = k * kk_ref[...]
    kk = kk * lax.rsqrt(_split_dot(kk * kk, e) + 1e-12)
    w_lora = jnp.dot(jnp.tanh(zs[:, c2:c3]).astype(BF16), w2_ref[...], preferred_element_type=F32)
    a_lora = jnp.dot(zs[:, c3:c4].astype(BF16), a2_ref[...], preferred_element_type=F32)
    bonus = jnp.zeros((tm, hw), F32)
    for d in range(2):
        x = w0_ref[d:d + 1, :] + w_lora[:, d * hw:(d + 1) * hw]
        softplus_neg = jnp.maximum(-x, 0.0) + jnp.log(1.0 + jnp.exp(-jnp.abs(x)))
        logw_o[d] = -jnp.exp(-softplus_neg - 0.5)
        a = jax.nn.sigmoid(a0_ref[d:d + 1, :] + a_lora[:, d * hw:(d + 1) * hw])
        kd = k * (1.0 + (a - 1.0) * ka_ref[...])
        kd_o[d] = kd
        bb_o[d] = kk * a
        bonus = bonus + _split_dot(r * kd * rk_ref[...], e) * v
    r_o[...] = r
    v_o[...] = v
    kk_o[...] = kk
    bonus_o[...] = bonus


def _rwkv_chunk_kernel(r_ref, v_ref, kk_ref, logw_ref, kd_ref, bb_ref, mn_ref, ry_ref):
    fwd = pl.program_id(1) == 0
    cn = RWKV_CHUNK
    hd = HEAD_DIM
    rowi = lax.broadcasted_iota(jnp.int32, (cn, cn), 0)
    coli = lax.broadcasted_iota(jnp.int32, (cn, cn), 1)
    lag = (rowi - coli) * jnp.where(fwd, 1, -1)
    before_incl = lag >= 0
    before = lag > 0
    eye = (rowi == coli).astype(F32)
    nn = (((1,), (0,)), ((), ()))
    nt = (((1,), (1,)), ((), ()))
    tn = (((0,), (0,)), ((), ()))

    a_t, r_t, b_t, k_t, b_h, k_h, g_end, vs, where = [], [], [], [], [], [], [], [], []
    for cc in range(RWKV_CHUNKS_PER_STEP):
        rs = slice(cc * cn, (cc + 1) * cn)
        logw = logw_ref[rs, :]
        lg = jnp.dot(before_incl.astype(F32), logw, precision=HI, preferred_element_type=F32)
        lg_tot = jnp.sum(logw, axis=0, keepdims=True)
        inv = jnp.exp(-lg)
        to_end = jnp.exp(lg_tot - lg)
        full = dict(a_t=-kk_ref[rs, :] * jnp.exp(lg - logw), r_t=r_ref[rs, :] * jnp.exp(lg),
                    b_t=bb_ref[rs, :] * inv, k_t=kd_ref[rs, :] * inv,
                    b_h=bb_ref[rs, :] * to_end, k_h=kd_ref[rs, :] * to_end, g_end=jnp.exp(lg_tot))
        for h in range(RWKV_HEADS):
            sl = slice(h * hd, (h + 1) * hd)
            a_t.append(full["a_t"][:, sl])
            r_t.append(full["r_t"][:, sl])
            b_t.append(full["b_t"][:, sl])
            k_t.append(full["k_t"][:, sl])
            b_h.append(full["b_h"][:, sl])
            k_h.append(full["k_h"][:, sl])
            g_end.append(full["g_end"][:, sl])
            vs.append(v_ref[rs, sl])
            where.append((cc, h))
    ps = [_pdot(jnp.concatenate([a, r], axis=0), jnp.concatenate([b, k], axis=0), nt, RWKV_PREC_PAIR)
          for a, r, b, k in zip(a_t, r_t, b_t, k_t)]
    n_ab = [jnp.where(before, p[:cn, :cn], 0.0) for p in ps]
    n_ak = [jnp.where(before, p[:cn, cn:], 0.0) for p in ps]
    n_rb = [jnp.where(before_incl, p[cn:, :cn], 0.0) for p in ps]
    n_rk = [jnp.where(before_incl, p[cn:, cn:], 0.0) for p in ps]
    tinv = [eye + n for n in n_ab]
    pw = n_ab
    for _ in range(int(math.log2(cn)) - 1):
        pw = [_pdot(x, x, nn, RWKV_PREC_INV) for x in pw]
        tinv = [t + _pdot(t, x, nn, RWKV_PREC_INV) for t, x in zip(tinv, pw)]
    akv = [_pdot(n, v, nn, RWKV_PREC_OUT) for n, v in zip(n_ak, vs)]
    rkv = [_pdot(n, v, nn, RWKV_PREC_OUT) for n, v in zip(n_rk, vs)]
    kv = [_pdot(k, v, tn, RWKV_PREC_OUT) for k, v in zip(k_h, vs)]
    zz = [_pdot(t, jnp.concatenate([a, w], axis=1), nn, RWKV_PREC_OUT)
          for t, a, w in zip(tinv, a_t, akv)]
    for u, (cc, h) in enumerate(where):
        ry_ref[cc, h] = (_pdot(n_rb[u], zz[u], nn, RWKV_PREC_OUT)
                         + jnp.concatenate([r_t[u], rkv[u]], axis=1))
        mn_ref[cc, h] = (_pdot(b_h[u], zz[u], tn, RWKV_PREC_OUT)
                         + jnp.concatenate([eye * g_end[u], kv[u]], axis=1))


def _rwkv_seq_kernel(mn_ref, ry_ref, y_ref, h_ref):
    @pl.when(pl.program_id(2) == 0)
    def _():
        h_ref[...] = jnp.zeros_like(h_ref)

    fwd = pl.program_id(1) == 0
    hd = HEAD_DIM
    for cc in range(RWKV_SUPER):
        c = jnp.where(fwd, cc, RWKV_SUPER - 1 - cc)
        row0 = pl.multiple_of(c * RWKV_CHUNK, RWKV_CHUNK)
        for h in range(RWKV_HEADS):
            mn = mn_ref[c, h]
            ry = ry_ref[c, h]
            state = h_ref[h]
            nn = (((1,), (0,)), ((), ()))
            y_ref[pl.ds(row0, RWKV_CHUNK), h * hd:(h + 1) * hd] = (
                _pdot(ry[:, :hd], state, nn, RWKV_PREC_STATE) + ry[:, hd:])
            h_ref[h] = _pdot(mn[:, :hd], state, nn, RWKV_PREC_STATE) + mn[:, hd:]


def _rwkv_post_kernel(y0_ref, y1_ref, bonus_ref, g_ref, gnw_ref, gnb_ref, e_ref, o_ref):
    y = y0_ref[...] + y1_ref[...]
    e = e_ref[...]
    mu = _split_dot(y, e) * (1.0 / HEAD_DIM)
    yc = y - mu
    var = _split_dot(yc * yc, e) * (1.0 / HEAD_DIM)
    yn = yc * lax.rsqrt(var + GN_EPS) * gnw_ref[...] + gnb_ref[...]
    o_ref[...] = (yn + bonus_ref[...]) * g_ref[...]


def _block_diag2(w):
    z = jnp.zeros_like(w[0])
    return jnp.concatenate([jnp.concatenate([w[0], z], 1), jnp.concatenate([z, w[1]], 1)], 0)


def _rwkv_mixer(z, shift_mu, w0, w2, a0, a2, g2, k_k, k_a, r_k, gn_w, gn_b):
    bsz, s, zw = z.shape
    hw = RWKV_WIDTH
    tm = ROW_BLOCK
    nblk = s // tm
    head_of = np.arange(hw) // HEAD_DIM
    e = jnp.asarray(head_of[:, None] == head_of[None, :], BF16)
    tok = pl.BlockSpec((None, tm, hw), lambda b, i: (b, i, 0))
    tok2 = pl.BlockSpec((None, 2, tm, hw), lambda b, i: (b, 0, i, 0))
    halo = tm // 8

    def full(shape):
        return pl.BlockSpec(shape, lambda b, i: (0,) * len(shape))

    r, v, kk, g, bonus, logw, kd, bb = pl.pallas_call(
        _rwkv_prep_kernel,
        grid=(bsz, nblk),
        in_specs=[pl.BlockSpec((None, tm, zw), lambda b, i: (b, i, 0)),
                  pl.BlockSpec((None, 8, zw), lambda b, i: (b, jnp.maximum(i * halo - 1, 0), 0)),
                  pl.BlockSpec((None, 8, zw), lambda b, i: (b, jnp.minimum((i + 1) * halo, s // 8 - 1), 0)),
                  full((2, RWKV_IN)), full((2, hw)), full((2 * RWKV_LORA_W, 2 * hw)), full((2, hw)),
                  full((2 * RWKV_LORA_A, 2 * hw)), full((RWKV_LORA_G, hw)), full((1, hw)), full((1, hw)),
                  full((1, hw)), full((hw, hw))],
        out_specs=[tok, tok, tok, tok, tok, tok2, tok2, tok2],
        out_shape=[jax.ShapeDtypeStruct((bsz, s, hw), F32)] * 5 + [jax.ShapeDtypeStruct((bsz, 2, s, hw), F32)] * 3,
        compiler_params=_cparams("parallel", "parallel"),
        name="rwkv_prep",
    )(z, z, z, shift_mu, w0, _block_diag2(w2).astype(BF16), a0, _block_diag2(a2).astype(BF16), g2.astype(BF16),
      k_k.reshape(1, hw), k_a.reshape(1, hw), r_k.reshape(1, hw), e)

    nch = s // RWKV_CHUNK
    step_rows = RWKV_CHUNKS_PER_STEP * RWKV_CHUNK
    ctok = pl.BlockSpec((None, step_rows, hw), lambda b, d, c: (b, c, 0))
    ctok2 = pl.BlockSpec((None, None, step_rows, hw), lambda b, d, c: (b, d, c, 0))
    mat_shape = jax.ShapeDtypeStruct((bsz, 2, nch, RWKV_HEADS, RWKV_CHUNK, 2 * HEAD_DIM), F32)
    mat_spec = pl.BlockSpec((None, None, RWKV_CHUNKS_PER_STEP, RWKV_HEADS, RWKV_CHUNK, 2 * HEAD_DIM),
                            lambda b, d, c: (b, d, c, 0, 0, 0))
    mn, ry = pl.pallas_call(
        _rwkv_chunk_kernel,
        grid=(bsz, 2, nch // RWKV_CHUNKS_PER_STEP),
        in_specs=[ctok, ctok, ctok, ctok2, ctok2, ctok2],
        out_specs=[mat_spec, mat_spec],
        out_shape=[mat_shape, mat_shape],
        compiler_params=_cparams("parallel", "parallel", "parallel"),
        name="rwkv_chunk",
    )(r, v, kk, logw, kd, bb)

    nsb = nch // RWKV_SUPER

    def sb_of(d, j):
        return jnp.where(d == 0, j, jnp.where(j == 0, 0, nsb - j))

    sup_shape = (bsz, 2, nsb, RWKV_SUPER, RWKV_HEADS, RWKV_CHUNK, 2 * HEAD_DIM)
    sup_spec = pl.BlockSpec((None, None, None, RWKV_SUPER, RWKV_HEADS, RWKV_CHUNK, 2 * HEAD_DIM),
                            lambda b, d, j: (b, d, sb_of(d, j), 0, 0, 0, 0))
    y = pl.pallas_call(
        _rwkv_seq_kernel,
        grid=(bsz, 2, nsb),
        in_specs=[sup_spec, sup_spec],
        out_specs=pl.BlockSpec((None, None, tm, hw), lambda b, d, j: (b, d, sb_of(d, j), 0)),
        out_shape=jax.ShapeDtypeStruct((bsz, 2, s, hw), F32),
        scratch_shapes=[pltpu.VMEM((RWKV_HEADS, HEAD_DIM, HEAD_DIM), F32)],
        compiler_params=_cparams("parallel", "parallel", "arbitrary"),
        name="rwkv_seq",
    )(mn.reshape(sup_shape), ry.reshape(sup_shape))

    vec = pl.BlockSpec((1, hw), lambda b, i: (0, 0))
    return pl.pallas_call(
        _rwkv_post_kernel,
        grid=(bsz, nblk),
        in_specs=[pl.BlockSpec((None, None, tm, hw), lambda b, i: (b, 0, i, 0)),
                  pl.BlockSpec((None, None, tm, hw), lambda b, i: (b, 1, i, 0)),
                  tok, tok, vec, vec, pl.BlockSpec((hw, hw), lambda b, i: (0, 0))],
        out_specs=tok,
        out_shape=jax.ShapeDtypeStruct((bsz, s, hw), F32),
        compiler_params=_cparams("parallel", "parallel"),
        name="rwkv_post",
    )(y, y, bonus, g, gn_w.reshape(1, hw), gn_b.reshape(1, hw), e)


def kernel(x, c, ctx, c_ctx,
           l0_w_mod, l0_b_mod, l0_w_in, l0_s5_lam_re, l0_s5_lam_im, l0_s5_log_dt, l0_s5_b_re, l0_s5_b_im,
           l0_s5_c_re, l0_s5_c_im, l0_s5_d, l0_w_glu, l0_b_glu, l0_rwkv_shift, l0_rwkv_w0, l0_rwkv_w2,
           l0_rwkv_a0, l0_rwkv_a2, l0_rwkv_g2, l0_rwkv_k_k, l0_rwkv_k_a, l0_rwkv_r_k, l0_rwkv_gn_w,
           l0_rwkv_gn_b, l0_w_out, l0_ln1_g, l0_ln1_b, l0_peer_wq, l0_peer_keys, l0_peer_u, l0_peer_v,
           l0_ln2_g, l0_ln2_b,
           l1_w_mod, l1_b_mod, l1_w_in, l1_sink, l1_rpb, l1_w_out, l1_ln1_g, l1_ln1_b, l1_peer_wq,
           l1_peer_keys, l1_peer_u, l1_peer_v, l1_ln2_g, l1_ln2_b):
    L = ctx.shape[1]
    h = jnp.concatenate([ctx, x], 1)

    tab = _mod_tables(c, c_ctx, l0_w_mod, l0_b_mod)
    z = _inproj(h, tab, 0, 1, l0_w_in.astype(BF16))
    ya = _s5_mixer(z[..., :S5_WIDTH], L, l0_s5_lam_re, l0_s5_lam_im, l0_s5_log_dt, l0_s5_b_re, l0_s5_b_im,
                   l0_s5_c_re, l0_s5_c_im, l0_s5_d, l0_w_glu, l0_b_glu)
    yb = _rwkv_mixer(z, l0_rwkv_shift, l0_rwkv_w0, l0_rwkv_w2, l0_rwkv_a0, l0_rwkv_a2,
                     l0_rwkv_g2, l0_rwkv_k_k, l0_rwkv_k_a, l0_rwkv_r_k, l0_rwkv_gn_w, l0_rwkv_gn_b)
    h = _outproj_ln(ya, yb, h, tab, 2, l0_w_out, l0_ln1_g, l0_ln1_b, 0)
    h = _peer_ln(h, tab, 0, l0_peer_wq, l0_peer_keys, l0_peer_u, l0_peer_v, l0_ln2_g, l0_ln2_b)

    tab = _mod_tables(c, c_ctx, l1_w_mod, l1_b_mod)
    qc, kc, vc, qd, kd, vd = _inproj_attn(h, tab, 0, 1, l1_w_in.astype(BF16), L)
    oc = _window_gqa(qc, kc, vc, l1_sink, L)
    od = _neighbourhood_attention(qd, kd, vd, l1_rpb, L)
    ctx_blocks = L // ROW_BLOCK
    hx = _outproj_ln(oc, od, h, tab, 2, l1_w_out, l1_ln1_g, l1_ln1_b, ctx_blocks)
    return _peer_ln(hx, tab, ctx_blocks, l1_peer_wq, l1_peer_keys, l1_peer_u, l1_peer_v, l1_ln2_g, l1_ln2_b)
```

```python
import functools
import math

import jax
import jax.numpy as jnp
import numpy as np
from jax import lax
from jax.experimental import pallas as pl
from jax.experimental.pallas import tpu as pltpu

F32 = jnp.float32
BF16 = jnp.bfloat16

D_MODEL = 1024
DEPTH = 2
GRID_W = 64
CTX_LEN = 256
HEAD_DIM = 64
S5_GROUP_CH = 16
S5_GROUPS = 16
S5_WIDTH = S5_GROUPS * S5_GROUP_CH
S5_STATE = 64
RWKV_HEADS = 8
RWKV_WIDTH = RWKV_HEADS * HEAD_DIM
RWKV_LORA_W = 64
RWKV_LORA_A = 64
RWKV_LORA_G = 128
RWKV_IN = 3 * RWKV_WIDTH + 2 * RWKV_LORA_W + 2 * RWKV_LORA_A + RWKV_LORA_G
RWKV_SPLITS = (RWKV_WIDTH, 2 * RWKV_WIDTH, 3 * RWKV_WIDTH,
               3 * RWKV_WIDTH + 2 * RWKV_LORA_W,
               3 * RWKV_WIDTH + 2 * RWKV_LORA_W + 2 * RWKV_LORA_A)
WIN_HEADS = 8
WIN_KV_HEADS = 2
WINDOW = 128
WIN_BLOCK = 128
NA_HEADS = 8
NA_ROWS = 8
NA_COLS = 16
ODD_SPLITS = (WIN_HEADS * HEAD_DIM, (WIN_HEADS + WIN_KV_HEADS) * HEAD_DIM,
              (WIN_HEADS + 2 * WIN_KV_HEADS) * HEAD_DIM,
              (WIN_HEADS + 2 * WIN_KV_HEADS + NA_HEADS) * HEAD_DIM,
              (WIN_HEADS + 2 * WIN_KV_HEADS + 2 * NA_HEADS) * HEAD_DIM)
ROPE_BASE = 10000.0
PEER_HEADS = 8
PEER_KEYS = 128
PEER_DK = 128
PEER_TOPK = 16
PEER_BLOCK = 128
ALPHA = (2.0 * DEPTH) ** 0.25
LN_EPS = 1e-5
GN_EPS = 64e-5

VMEM_LIMIT_BYTES = 56 * 1024 * 1024
ROW_BLOCK = 256


def _cparams(*sem):
    return pltpu.CompilerParams(dimension_semantics=sem, vmem_limit_bytes=VMEM_LIMIT_BYTES)


def _modulation_kernel(c_ref, w_ref, b_ref, o_ref):
    c = c_ref[...]
    s = c * jax.nn.sigmoid(c)
    o_ref[...] = jnp.dot(s.astype(BF16), w_ref[...].astype(BF16), preferred_element_type=F32) + b_ref[...]


def _modulation(cond, w_mod, b_mod):
    n, d = cond.shape
    nout = w_mod.shape[1]
    tn = 1536
    return pl.pallas_call(
        _modulation_kernel,
        grid=(nout // tn,),
        in_specs=[pl.BlockSpec((n, d), lambda j: (0, 0)),
                  pl.BlockSpec((d, tn), lambda j: (0, j)),
                  pl.BlockSpec((1, tn), lambda j: (0, j))],
        out_specs=pl.BlockSpec((n, tn), lambda j: (0, j)),
        out_shape=jax.ShapeDtypeStruct((n, nout), F32),
        compiler_params=_cparams("arbitrary"),
        name="modulation",
    )(cond, w_mod, b_mod.reshape(1, nout))


def _mod_tables(c, c_ctx, w_mod, b_mod):
    bsz = c.shape[0]
    cond = jnp.concatenate([c, c_ctx[None, :], jnp.zeros((8 - bsz - 1, c.shape[1]), F32)], 0)
    m = _modulation(cond, w_mod, b_mod)
    mx = m[:bsz].reshape(bsz, 1, 6, 1, D_MODEL)
    mc = jnp.broadcast_to(m[bsz].reshape(1, 1, 6, 1, D_MODEL), (bsz, 1, 6, 1, D_MODEL))
    return jnp.concatenate([mc, mx], 1)


def _tab_spec(which):
    return pl.BlockSpec((None, None, None, 1, D_MODEL),
                        lambda b, i: (b, jnp.minimum(i, 1), which, 0, 0))


def _inproj_kernel(x_ref, shift_ref, scale_ref, w_ref, o_ref):
    hm = x_ref[...] * (1.0 + scale_ref[...]) + shift_ref[...]
    o_ref[...] = jnp.dot(hm.astype(BF16), w_ref[...], preferred_element_type=F32)


def _inproj(h, tab, which_shift, which_scale, w_bf16):
    bsz, s, d = h.shape
    n = w_bf16.shape[1]
    return pl.pallas_call(
        _inproj_kernel,
        grid=(bsz, s // ROW_BLOCK),
        in_specs=[pl.BlockSpec((None, ROW_BLOCK, d), lambda b, i: (b, i, 0)),
                  _tab_spec(which_shift), _tab_spec(which_scale),
                  pl.BlockSpec((d, n), lambda b, i: (0, 0))],
        out_specs=pl.BlockSpec((None, ROW_BLOCK, n), lambda b, i: (b, i, 0)),
        out_shape=jax.ShapeDtypeStruct((bsz, s, n), F32),
        compiler_params=_cparams("parallel", "parallel"),
        name="inproj",
    )(h, tab, tab, w_bf16)


def _layer_norm_rows(z, g, b):
    mu = jnp.mean(z, -1, keepdims=True)
    zc = z - mu
    var = jnp.mean(zc * zc, -1, keepdims=True)
    return zc * lax.rsqrt(var + LN_EPS) * g + b


def _outproj_ln_kernel(ya_ref, yb_ref, h_ref, gate_ref, wa_ref, wb_ref, g_ref, b_ref, o_ref):
    proj = (jnp.dot(ya_ref[...].astype(BF16), wa_ref[...], preferred_element_type=F32)
            + jnp.dot(yb_ref[...].astype(BF16), wb_ref[...], preferred_element_type=F32))
    z = ALPHA * h_ref[...] + gate_ref[...] * proj
    o_ref[...] = _layer_norm_rows(z, g_ref[...], b_ref[...])


def _outproj_ln(ya, yb, h, tab, which_gate, w_out, ln_g, ln_b, row_off):
    bsz, rows, ka = ya.shape
    kb = yb.shape[-1]
    d = h.shape[-1]
    return pl.pallas_call(
        _outproj_ln_kernel,
        grid=(bsz, rows // ROW_BLOCK),
        in_specs=[pl.BlockSpec((None, ROW_BLOCK, ka), lambda b, i: (b, i, 0)),
                  pl.BlockSpec((None, ROW_BLOCK, kb), lambda b, i: (b, i, 0)),
                  pl.BlockSpec((None, ROW_BLOCK, d), lambda b, i: (b, i + row_off, 0)),
                  pl.BlockSpec((None, None, None, 1, d), lambda b, i: (b, jnp.minimum(i + row_off, 1), which_gate, 0, 0)),
                  pl.BlockSpec((ka, d), lambda b, i: (0, 0)),
                  pl.BlockSpec((kb, d), lambda b, i: (0, 0)),
                  pl.BlockSpec((1, d), lambda b, i: (0, 0)),
                  pl.BlockSpec((1, d), lambda b, i: (0, 0))],
        out_specs=pl.BlockSpec((None, ROW_BLOCK, d), lambda b, i: (b, i, 0)),
        out_shape=jax.ShapeDtypeStruct((bsz, rows, d), F32),
        compiler_params=_cparams("parallel", "parallel"),
        name="outproj_ln",
    )(ya, yb, h, tab, w_out[:ka].astype(BF16), w_out[ka:].astype(BF16), ln_g.reshape(1, d), ln_b.reshape(1, d))


PEER_NEG = -3.0e38
PEER_ETILE = 4096
PEER_SUBTILE = 256
SQRT_HALF = 0.7071067811865476


PEER_NO_RANK = 255.0
PEER_ROUTE_GROUP = 4


def _top_values(arrays, n, with_rank):
    arrays = list(arrays)
    vals = [[] for _ in arrays]
    ranks = [jnp.full(s.shape, PEER_NO_RANK, F32) if w else None for s, w in zip(arrays, with_rank)]
    for k in range(n):
        ms = [jnp.max(s, axis=0, keepdims=True) for s in arrays]
        hits = [s == m for s, m in zip(arrays, ms)]
        for i, m in enumerate(ms):
            vals[i].append(m)
            if with_rank[i]:
                ranks[i] = jnp.where(hits[i], float(k), ranks[i])
        arrays = [jnp.where(hit, PEER_NEG, s) for hit, s in zip(hits, arrays)]
    return vals, ranks


def _oddeven_merge_sort_network(n):
    pairs = []
    p = 1
    while p < n:
        k = p
        while k >= 1:
            for j in range(k % p, n - k, 2 * k):
                for i in range(min(k, n - j - k)):
                    if (i + j) // (2 * p) == (i + j + k) // (2 * p):
                        pairs.append((i + j, i + j + k))
            k //= 2
        p *= 2
    return pairs


def _top_values_sorted(s, n):
    sub = 8
    groups = s.shape[0] // sub
    x = s.reshape(groups, sub, s.shape[1])
    cols = [x[g] for g in range(groups)]
    for lo, hi in _oddeven_merge_sort_network(groups):
        a, b = cols[lo], cols[hi]
        cols[lo], cols[hi] = jnp.maximum(a, b), jnp.minimum(a, b)
    assert groups >= n
    vals = []
    for k in range(n):
        m = jnp.max(cols[0], axis=0, keepdims=True)
        vals.append(m)
        hit = cols[0] == m
        for g in range(n - k - 1):
            cols[g] = jnp.where(hit, cols[g + 1], cols[g])
    return vals


def _peer_route_kernel(h_ref, shift_ref, scale_ref, wq_ref, keys_ref,
                       pin_ref, cnt_ref, e1_ref, r2_ref, e2_ref):
    pin = (h_ref[...] * (1.0 + scale_ref[...]) + shift_ref[...]).astype(BF16)
    pin_ref[...] = pin
    q = jnp.dot(pin, wq_ref[...], preferred_element_type=F32).astype(BF16)
    st = lax.dot_general(keys_ref[...], q, (((1,), (1,)), ((), ())), preferred_element_type=F32)
    for h0 in range(0, PEER_HEADS, PEER_ROUTE_GROUP):
        hds = list(range(h0, h0 + PEER_ROUTE_GROUP))
        s1s = [st[hd * 2 * PEER_KEYS:hd * 2 * PEER_KEYS + PEER_KEYS] for hd in hds]
        s2s = [st[hd * 2 * PEER_KEYS + PEER_KEYS:(hd + 1) * 2 * PEER_KEYS] for hd in hds]
        v1s = [_top_values_sorted(s1, PEER_TOPK) for s1 in s1s]
        v2s, rank2s = _top_values(s2s, PEER_TOPK, [True] * len(hds))
        frontiers = []
        for v1, v2 in zip(v1s, v2s):
            v2all = jnp.concatenate(v2, axis=0)
            frontiers.append(jnp.concatenate([v1[a] + v2all[:(PEER_TOPK + 1) // (a + 1)] for a in range(PEER_TOPK)],
                                             axis=0))
        tops, _ = _top_values(frontiers, PEER_TOPK + 1, [False] * len(hds))
        for hd, s1, s2, v1, v2, rank2, top in zip(hds, s1s, s2s, v1s, v2s, rank2s, tops):
            tau = 0.5 * (top[PEER_TOPK - 1] + top[PEER_TOPK])
            zsum = top[0] * 0.0
            for kk in range(PEER_TOPK):
                zsum = zsum + jnp.exp(top[kk] - top[0])
            need = tau - s1
            cnt = jnp.zeros_like(s1)
            for b in range(PEER_TOPK):
                cnt = cnt + jnp.where(v2[b] >= need, 1.0, 0.0)
            cnt_ref[hd] = jnp.where(s1 >= v1[PEER_TOPK - 1], cnt, 0.0)
            e1_ref[hd] = jnp.exp(s1 - v1[0])
            r2_ref[hd] = rank2.astype(BF16)
            e2_ref[hd] = (jnp.exp(s2 - v2[0]) / zsum).astype(BF16)


def _peer_dense_kernel(pin_ref, cnt_ref, e1_ref, r2_ref, e2_ref, u_ref, vt_ref,
                       h_ref, gate_ref, g_ref, b_ref, o_ref, acc_ref, w_ref):
    j = pl.program_id(2)

    last = pl.num_programs(2) - 1

    def apply_values(first):
        part = jnp.dot(vt_ref[...], w_ref[(j + 1) % 2], preferred_element_type=F32)
        if first:
            acc_ref[...] = part
        else:
            acc_ref[...] += part

    def score_and_gate():
        pin = pin_ref[...]
        cur = j % 2
        for sb in range(PEER_ETILE // PEER_SUBTILE):
            es = slice(sb * PEER_SUBTILE, (sb + 1) * PEER_SUBTILE)
            at = lax.dot_general(u_ref[es, :], pin, (((1,), (1,)), ((), ())), preferred_element_type=F32)
            for i2 in range(PEER_SUBTILE // PEER_KEYS):
                ii = sb * (PEER_SUBTILE // PEER_KEYS) + i2
                a = at[i2 * PEER_KEYS:(i2 + 1) * PEER_KEYS]
                act = (0.5 * a * (1.0 + lax.erf(a * SQRT_HALF))).astype(BF16)
                g = jnp.zeros(a.shape, BF16)
                for hd in range(PEER_HEADS):
                    cnt_row = cnt_ref[hd, pl.ds(ii, 1), :].astype(BF16)
                    c_row = e1_ref[hd, pl.ds(ii, 1), :].astype(BF16)
                    g = g + jnp.where(r2_ref[hd] < cnt_row, e2_ref[hd] * c_row, jnp.zeros((), BF16))
                w_ref[cur, ii * PEER_KEYS:(ii + 1) * PEER_KEYS, :] = g * act

    @pl.when(j == 0)
    def _():
        score_and_gate()

    @pl.when(j == 1)
    def _():
        apply_values(first=True)
        score_and_gate()

    @pl.when((j > 1) & (j < last))
    def _():
        apply_values(first=False)
        score_and_gate()

    @pl.when(j == last)
    def _():
        apply_values(first=False)
        z = ALPHA * h_ref[...] + gate_ref[...] * acc_ref[...].T
        o_ref[...] = _layer_norm_rows(z, g_ref[...], b_ref[...])


def _peer_block_diag_keys(sub_keys):
    hh, two, nk, dh = sub_keys.shape
    eye = jnp.eye(hh * two, dtype=F32)
    kb = sub_keys.reshape(hh * two, nk, dh)
    return jnp.einsum('akd,ab->akbd', kb, eye).reshape(hh * two * nk, hh * two * dh)


def _peer_ln(h, tab, seg_off, wq, sub_keys, u_tab, v_tab, ln_g, ln_b):
    bsz, s, d = h.shape
    tm = ROW_BLOCK
    nblk = s // tm
    hk = PEER_HEADS * 2 * PEER_KEYS
    keys_bd = _peer_block_diag_keys(sub_keys).astype(BF16)

    def tab_spec3(which):
        return pl.BlockSpec((None, None, None, 1, d),
                            lambda b, i, *_: (b, jnp.minimum(i + seg_off, 1), which, 0, 0))

    route_dims = (bsz, nblk, PEER_HEADS, PEER_KEYS, tm)
    route_spec = pl.BlockSpec((None, None, PEER_HEADS, PEER_KEYS, tm), lambda b, i: (b, i, 0, 0, 0))
    pin, cnt, e1, r2, e2 = pl.pallas_call(
        _peer_route_kernel,
        grid=(bsz, nblk),
        in_specs=[pl.BlockSpec((None, tm, d), lambda b, i: (b, i, 0)),
                  tab_spec3(3), tab_spec3(4),
                  pl.BlockSpec((d, PEER_HEADS * PEER_DK), lambda b, i: (0, 0)),
                  pl.BlockSpec((hk, PEER_HEADS * PEER_DK), lambda b, i: (0, 0))],
        out_specs=[pl.BlockSpec((None, tm, d), lambda b, i: (b, i, 0)),
                   route_spec, route_spec, route_spec, route_spec],
        out_shape=[jax.ShapeDtypeStruct((bsz, nblk * tm, d), BF16),
                   jax.ShapeDtypeStruct(route_dims, F32), jax.ShapeDtypeStruct(route_dims, F32),
                   jax.ShapeDtypeStruct(route_dims, BF16), jax.ShapeDtypeStruct(route_dims, BF16)],
        compiler_params=_cparams("parallel", "parallel"),
        name="peer_route",
    )(h, tab, tab, wq.astype(BF16), keys_bd)

    n_exp = u_tab.shape[0]
    rows_per_step = PEER_ETILE // PEER_KEYS
    full_spec = pl.BlockSpec((None, None, PEER_HEADS, PEER_KEYS, tm), lambda b, i, j: (b, i, 0, 0, 0))
    n_tiles = n_exp // PEER_ETILE
    row_spec = pl.BlockSpec((None, None, PEER_HEADS, rows_per_step, tm),
                            lambda b, i, j: (b, i, 0, jnp.minimum(j, n_tiles - 1), 0))
    return pl.pallas_call(
        _peer_dense_kernel,
        grid=(bsz, nblk, n_tiles + 1),
        in_specs=[pl.BlockSpec((None, tm, d), lambda b, i, j: (b, i, 0)),
                  row_spec, row_spec, full_spec, full_spec,
                  pl.BlockSpec((PEER_ETILE, d), lambda b, i, j: (jnp.minimum(j, n_tiles - 1), 0)),
                  pl.BlockSpec((d, PEER_ETILE), lambda b, i, j: (0, jnp.maximum(j - 1, 0))),
                  pl.BlockSpec((None, tm, d), lambda b, i, j: (b, i, 0)),
                  tab_spec3(5),
                  pl.BlockSpec((1, d), lambda b, i, j: (0, 0)),
                  pl.BlockSpec((1, d), lambda b, i, j: (0, 0))],
        out_specs=pl.BlockSpec((None, tm, d), lambda b, i, j: (b, i, 0)),
        out_shape=jax.ShapeDtypeStruct((bsz, nblk * tm, d), F32),
        scratch_shapes=[pltpu.VMEM((d, tm), F32), pltpu.VMEM((2, PEER_ETILE, tm), BF16)],
        compiler_params=_cparams("parallel", "parallel", "arbitrary"),
        name="peer_dense",
    )(pin, cnt, e1, r2, e2, u_tab.astype(BF16), v_tab.T.astype(BF16), h, tab,
      ln_g.reshape(1, d), ln_b.reshape(1, d))


S5_CHAINS = 8
S5_TCHUNK = 128
S5_NSTATE = S5_GROUPS * S5_STATE


def _s5_scan_kernel(uf_ref, ub_ref, wb_ref, wc_ref, a_ref, yf_ref, yb_ref,
                    fre_ref, fim_ref, bre_ref, bim_ref, h_ref):
    @pl.when(pl.program_id(0) == 0)
    def _():
        h_ref[...] = jnp.zeros_like(h_ref)

    rows = uf_ref.shape[0]
    steps = rows // S5_CHAINS
    uf = uf_ref[...].astype(BF16)
    ub = ub_ref[...].astype(BF16)
    fre_ref[...] = jnp.dot(uf, wb_ref[0], preferred_element_type=F32)
    fim_ref[...] = jnp.dot(uf, wb_ref[1], preferred_element_type=F32)
    bre_ref[...] = jnp.dot(ub, wb_ref[2], preferred_element_type=F32)
    bim_ref[...] = jnp.dot(ub, wb_ref[3], preferred_element_type=F32)
    a_re = a_ref[0]
    a_im = a_ref[1]
    is_fwd = lax.broadcasted_iota(jnp.int32, (S5_CHAINS, S5_NSTATE), 0) < (S5_CHAINS // 2)

    def step(t, carry):
        h_re, h_im = carry
        of = pl.multiple_of(t * S5_CHAINS, S5_CHAINS)
        ob = pl.multiple_of((steps - 1 - t) * S5_CHAINS, S5_CHAINS)
        n_re = a_re * h_re - a_im * h_im + jnp.where(is_fwd, fre_ref[pl.ds(of, S5_CHAINS), :],
                                                     bre_ref[pl.ds(ob, S5_CHAINS), :])
        n_im = a_re * h_im + a_im * h_re + jnp.where(is_fwd, fim_ref[pl.ds(of, S5_CHAINS), :],
                                                     bim_ref[pl.ds(ob, S5_CHAINS), :])
        fre_ref[pl.ds(of, S5_CHAINS), :] = n_re
        fim_ref[pl.ds(of, S5_CHAINS), :] = n_im
        bre_ref[pl.ds(ob, S5_CHAINS), :] = n_re
        bim_ref[pl.ds(ob, S5_CHAINS), :] = n_im
        return n_re, n_im

    h_re, h_im = lax.fori_loop(0, steps, step, (h_ref[0], h_ref[1]), unroll=2)
    h_ref[0] = h_re
    h_ref[1] = h_im
    yf_ref[...] = (jnp.dot(fre_ref[...].astype(BF16), wc_ref[0], preferred_element_type=F32)
                   - jnp.dot(fim_ref[...].astype(BF16), wc_ref[1], preferred_element_type=F32))
    yb_ref[...] = (jnp.dot(bre_ref[...].astype(BF16), wc_ref[2], preferred_element_type=F32)
                   - jnp.dot(bim_ref[...].astype(BF16), wc_ref[3], preferred_element_type=F32))


def _s5_glu_kernel(u_ref, y0_ref, y1_ref, d_ref, w_ref, b_ref, o_ref):
    y = u_ref[...] * d_ref[...] + y0_ref[...] + y1_ref[...]
    y = 0.5 * y * (1.0 + lax.erf(y * SQRT_HALF))
    gate = jnp.dot(y.astype(BF16), w_ref[...], preferred_element_type=F32) + b_ref[...]
    o_ref[...] = y * jax.nn.sigmoid(gate)


def _s5_params(lam_re, lam_im, log_dt, b_re, b_im, c_re, c_im):
    eye = jnp.eye(S5_GROUPS, dtype=F32)
    wb, wc, a_rows = [], [], [[], []]
    for d in range(2):
        lr, li = lam_re[d], lam_im[d]
        dt = jnp.exp(log_dt[d])[:, None]
        mag = jnp.exp(lr * dt)
        ab_re, ab_im = mag * jnp.cos(li * dt), mag * jnp.sin(li * dt)
        den = lr * lr + li * li
        nr = ab_re - 1.0
        coef_re = (nr * lr + ab_im * li) / den
        coef_im = (ab_im * lr - nr * li) / den
        bb_re = coef_re[..., None] * b_re[d] - coef_im[..., None] * b_im[d]
        bb_im = coef_re[..., None] * b_im[d] + coef_im[..., None] * b_re[d]
        for bb in (bb_re, bb_im):
            wb.append(jnp.einsum('gpi,gh->gihp', bb, eye).reshape(S5_WIDTH, S5_NSTATE))
        for cc in (c_re[d], c_im[d]):
            wc.append(jnp.einsum('gip,gh->gphi', cc, eye).reshape(S5_NSTATE, S5_WIDTH))
        for k, ab in enumerate((ab_re, ab_im)):
            a_rows[k].append(jnp.broadcast_to(ab.reshape(1, S5_NSTATE), (S5_CHAINS // 2, S5_NSTATE)))
    a_tiles = jnp.stack([jnp.concatenate(a_rows[0], 0), jnp.concatenate(a_rows[1], 0)])
    return jnp.stack(wb).astype(BF16), jnp.stack(wc).astype(BF16), a_tiles


def _s5_mixer(u, n_ctx, lam_re, lam_im, log_dt, b_re, b_im, c_re, c_im, d_skip, w_glu, b_glu):
    bsz, s, w = u.shape
    assert 2 * bsz == S5_CHAINS
    wb, wc, a_tiles = _s5_params(lam_re, lam_im, log_dt, b_re, b_im, c_re, c_im)
    ut = jnp.transpose(u, (1, 0, 2))
    chain = jnp.concatenate([ut, ut], 1).reshape(s * S5_CHAINS, w)
    rows = S5_TCHUNK * S5_CHAINS
    n_chunks = s // S5_TCHUNK
    ctx_chunks = n_ctx // S5_TCHUNK

    def mirror(i):
        return jnp.where(i < ctx_chunks, ctx_chunks - 1 - i, n_chunks - 1 - (i - ctx_chunks))

    state_buf = pltpu.VMEM((rows, S5_NSTATE), F32)
    y_shape = jax.ShapeDtypeStruct((s * S5_CHAINS, w), F32)
    yf, yb = pl.pallas_call(
        _s5_scan_kernel,
        grid=(n_chunks,),
        in_specs=[pl.BlockSpec((rows, w), lambda i: (i, 0)),
                  pl.BlockSpec((rows, w), lambda i: (mirror(i), 0)),
                  pl.BlockSpec((4, w, S5_NSTATE), lambda i: (0, 0, 0)),
                  pl.BlockSpec((4, S5_NSTATE, w), lambda i: (0, 0, 0)),
                  pl.BlockSpec((2, S5_CHAINS, S5_NSTATE), lambda i: (0, 0, 0))],
        out_specs=[pl.BlockSpec((rows, w), lambda i: (i, 0)),
                   pl.BlockSpec((rows, w), lambda i: (mirror(i), 0))],
        out_shape=[y_shape, y_shape],
        scratch_shapes=[state_buf, state_buf, state_buf, state_buf,
                        pltpu.VMEM((2, S5_CHAINS, S5_NSTATE), F32)],
        compiler_params=_cparams("arbitrary"),
        name="s5_scan",
    )(chain, chain, wb, wc, a_tiles)
    y_fwd = jnp.transpose(yf.reshape(s, S5_CHAINS, w)[:, :bsz], (1, 0, 2))
    y_bwd = jnp.transpose(yb.reshape(s, S5_CHAINS, w)[:, bsz:], (1, 0, 2))
    tok = pl.BlockSpec((None, ROW_BLOCK, w), lambda b, i: (b, i, 0))
    vec = pl.BlockSpec((1, w), lambda b, i: (0, 0))
    return pl.pallas_call(
        _s5_glu_kernel,
        grid=(bsz, s // ROW_BLOCK),
        in_specs=[tok, tok, tok, vec, pl.BlockSpec((w, w), lambda b, i: (0, 0)), vec],
        out_specs=tok,
        out_shape=jax.ShapeDtypeStruct((bsz, s, w), F32),
        compiler_params=_cparams("parallel", "parallel"),
        name="s5_glu",
    )(u, y_fwd, y_bwd, d_skip.reshape(1, w), w_glu.astype(BF16), b_glu.reshape(1, w))


ATT_SCALE = HEAD_DIM ** -0.5
ATT_NEG = -1e30
ROPE_QUARTER = HEAD_DIM // 4
QC_W = WIN_HEADS * HEAD_DIM
KC_W = WIN_KV_HEADS * HEAD_DIM
QD_W = NA_HEADS * HEAD_DIM


def _rope_tables(n_ctx, t):
    pos = np.arange(t)
    rowcol = np.stack([pos // GRID_W, pos % GRID_W], 1).astype(np.float32)
    lane = np.arange(QC_W)
    j = lane % HEAD_DIM
    freqs = (ROPE_BASE ** (-np.arange(ROPE_QUARTER, dtype=np.float32) / ROPE_QUARTER)).astype(np.float32)
    ang = rowcol[:, j // (HEAD_DIM // 2)] * freqs[j % ROPE_QUARTER][None, :]
    ang = jnp.asarray(ang, F32)
    sign = np.where((j % (HEAD_DIM // 2)) < ROPE_QUARTER, -1.0, 1.0).astype(np.float32)
    cos = jnp.concatenate([jnp.ones((n_ctx, QC_W), F32), jnp.cos(ang)], 0)
    sin = jnp.concatenate([jnp.zeros((n_ctx, QC_W), F32), jnp.sin(ang) * sign[None, :]], 0)
    return cos, sin


def _rope(x, cos, sin):
    n = x.shape[-1]
    lane = lax.broadcasted_iota(jnp.int32, (1, n), 1)
    first = (lane % (HEAD_DIM // 2)) < ROPE_QUARTER
    partner = jnp.where(first, pltpu.roll(x, n - ROPE_QUARTER, 1), pltpu.roll(x, ROPE_QUARTER, 1))
    return x * cos + partner * sin


def _inproj_attn_kernel(x_ref, shift_ref, scale_ref, w_ref, cos_ref, sin_ref,
                        qc_ref, kc_ref, vc_ref, qd_ref, kd_ref, vd_ref):
    hm = x_ref[...] * (1.0 + scale_ref[...]) + shift_ref[...]
    z = jnp.dot(hm.astype(BF16), w_ref[...], preferred_element_type=F32)
    c0, c1, c2, c3, c4 = ODD_SPLITS
    cos = cos_ref[...]
    sin = sin_ref[...]
    qc_ref[...] = (_rope(z[:, 0:c0], cos, sin) * ATT_SCALE).astype(BF16)
    kc_ref[...] = _rope(z[:, c0:c1], cos[:, :KC_W], sin[:, :KC_W]).astype(BF16)
    vc_ref[...] = z[:, c1:c2].astype(BF16)
    qd_ref[...] = (z[:, c2:c3] * ATT_SCALE).astype(BF16)
    kd_ref[...] = z[:, c3:c4].astype(BF16)
    vd_ref[...] = z[:, c4:].astype(BF16)


def _inproj_attn(h, tab, which_shift, which_scale, w_bf16, n_ctx):
    bsz, s, d = h.shape
    n = w_bf16.shape[1]
    cos, sin = _rope_tables(n_ctx, s - n_ctx)
    widths = (QC_W, KC_W, KC_W, QD_W, QD_W, QD_W)
    return pl.pallas_call(
        _inproj_attn_kernel,
        grid=(bsz, s // ROW_BLOCK),
        in_specs=[pl.BlockSpec((None, ROW_BLOCK, d), lambda b, i: (b, i, 0)),
                  _tab_spec(which_shift), _tab_spec(which_scale),
                  pl.BlockSpec((d, n), lambda b, i: (0, 0)),
                  pl.BlockSpec((ROW_BLOCK, QC_W), lambda b, i: (i, 0)),
                  pl.BlockSpec((ROW_BLOCK, QC_W), lambda b, i: (i, 0))],
        out_specs=[pl.BlockSpec((None, ROW_BLOCK, w), lambda b, i: (b, i, 0)) for w in widths],
        out_shape=[jax.ShapeDtypeStruct((bsz, s, w), BF16) for w in widths],
        compiler_params=_cparams("parallel", "parallel"),
        name="inproj_attn",
    )(h, tab, tab, w_bf16, cos, sin)


def _softmax_pv_heads(scores, values, extra_logits=None):
    ms = [jnp.max(s, axis=1, keepdims=True) for s in scores]
    if extra_logits is not None:
        ms = [jnp.maximum(m, x) for m, x in zip(ms, extra_logits)]
    ps = [jnp.exp(s - m) for s, m in zip(scores, ms)]
    dens = [jnp.sum(p, axis=1, keepdims=True) for p in ps]
    if extra_logits is not None:
        dens = [d + jnp.exp(x - m) for d, x, m in zip(dens, extra_logits, ms)]
    outs = [jnp.dot(p.astype(BF16), v, preferred_element_type=F32) for p, v in zip(ps, values)]
    return [o / d for o, d in zip(outs, dens)]


def _window_kernel(q_ref, kp_ref, kc_ref, kn_ref, kx_ref, vp_ref, vc_ref, vn_ref, vx_ref, sink_ref, o_ref, *, t_len):
    i = pl.program_id(1)
    wb = WIN_BLOCK
    qi = lax.broadcasted_iota(jnp.int32, (wb, 1), 0)
    kj = lax.broadcasted_iota(jnp.int32, (1, 3 * wb), 1) - wb
    kabs = i * wb + kj
    valid = (jnp.abs(kj - qi) <= WINDOW) & (kabs >= 0) & (kabs < t_len)
    n_ctx = kx_ref.shape[0]
    bias = jnp.concatenate([jnp.where(valid, 0.0, ATT_NEG), jnp.zeros((wb, n_ctx), F32)], axis=1)
    k_all = jnp.concatenate([kp_ref[...], kc_ref[...], kn_ref[...], kx_ref[...]], axis=0)
    v_all = jnp.concatenate([vp_ref[...], vc_ref[...], vn_ref[...], vx_ref[...]], axis=0)
    group = WIN_HEADS // WIN_KV_HEADS
    nt = (((1,), (1,)), ((), ()))
    kgs = [k_all[:, g * HEAD_DIM:(g + 1) * HEAD_DIM] for g in range(WIN_KV_HEADS)]
    vgs = [v_all[:, g * HEAD_DIM:(g + 1) * HEAD_DIM] for g in range(WIN_KV_HEADS)]
    scores = [lax.dot_general(q_ref[:, hq * HEAD_DIM:(hq + 1) * HEAD_DIM], kgs[hq // group], nt,
                              preferred_element_type=F32) + bias for hq in range(WIN_HEADS)]
    outs = _softmax_pv_heads(scores, [vgs[hq // group] for hq in range(WIN_HEADS)],
                             [sink_ref[hq:hq + 1, 0:1] for hq in range(WIN_HEADS)])
    for hq in range(WIN_HEADS):
        o_ref[:, hq * HEAD_DIM:(hq + 1) * HEAD_DIM] = outs[hq]


def _window_gqa(qc, kc, vc, sink, n_ctx):
    bsz, s, _ = qc.shape
    t_len = s - n_ctx
    nb = t_len // WIN_BLOCK
    off = n_ctx // WIN_BLOCK

    def kv_spec(delta):
        return pl.BlockSpec((None, WIN_BLOCK, KC_W), lambda b, i: (b, jnp.clip(i + delta, 0, nb - 1) + off, 0))

    ctx_spec = pl.BlockSpec((None, n_ctx, KC_W), lambda b, i: (b, 0, 0))
    sink_tile = jnp.broadcast_to(sink.reshape(WIN_HEADS, 1), (WIN_HEADS, 128))
    return pl.pallas_call(
        functools.partial(_window_kernel, t_len=t_len),
        grid=(bsz, nb),
        in_specs=[pl.BlockSpec((None, WIN_BLOCK, QC_W), lambda b, i: (b, i + off, 0)),
                  kv_spec(-1), kv_spec(0), kv_spec(1), ctx_spec,
                  kv_spec(-1), kv_spec(0), kv_spec(1), ctx_spec,
                  pl.BlockSpec((WIN_HEADS, 128), lambda b, i: (0, 0))],
        out_specs=pl.BlockSpec((None, WIN_BLOCK, QC_W), lambda b, i: (b, i, 0)),
        out_shape=jax.ShapeDtypeStruct((bsz, t_len, QC_W), F32),
        compiler_params=_cparams("parallel", "parallel"),
        name="window_gqa",
    )(qc, kc, kc, kc, kc, vc, vc, vc, vc, sink_tile)


NA_QROWS = 2
NA_KROWS = NA_QROWS + NA_ROWS - 1


def _na_block_plan(rows):
    kh = min(NA_ROWS, rows)
    kw = min(NA_COLS, GRID_W)
    c = np.arange(GRID_W)
    cs = np.clip(c - kw // 2, 0, GRID_W - kw)
    kc = np.arange(GRID_W)
    col_valid = (kc[None, :] >= cs[:, None]) & (kc[None, :] < cs[:, None] + kw)
    dc = np.clip(kc[None, :] - c[:, None] + (NA_COLS - 1), 0, 2 * NA_COLS - 2)
    patterns, types, kbs = {}, [], []
    for r0 in range(0, rows, NA_QROWS):
        kb = int(np.clip(r0 - kh // 2, 0, rows - NA_KROWS))
        r = r0 + np.arange(NA_QROWS)
        rs = np.clip(r - kh // 2, 0, rows - kh)
        kr = kb + np.arange(NA_KROWS)
        row_valid = (kr[None, :] >= rs[:, None]) & (kr[None, :] < rs[:, None] + kh)
        dr = np.clip(kr[None, :] - r[:, None] + (NA_ROWS - 1), 0, 2 * NA_ROWS - 2)
        key = (row_valid.tobytes(), dr.tobytes())
        if key not in patterns:
            valid = row_valid[:, None, :, None] & col_valid[None, :, None, :]
            shape = valid.shape
            patterns[key] = (len(patterns), valid.reshape(NA_QROWS * GRID_W, NA_KROWS * GRID_W),
                             np.broadcast_to(dr[:, None, :, None], shape).reshape(valid.shape[0] * shape[1], -1),
                             np.broadcast_to(dc[None, :, None, :], shape).reshape(valid.shape[0] * shape[1], -1))
        types.append(patterns[key][0])
        kbs.append(kb)
    ordered = sorted(patterns.values(), key=lambda p: p[0])
    valid = np.stack([p[1] for p in ordered])
    dr = np.stack([p[2] for p in ordered])
    dc = np.stack([p[3] for p in ordered])
    return np.asarray(types, np.int32), np.asarray(kbs, np.int32), valid, dr, dc


def _na_kernel(types_ref, kbs_ref, q_ref, *refs):
    del types_ref, kbs_ref
    k_refs = refs[:NA_KROWS]
    v_refs = refs[NA_KROWS:2 * NA_KROWS]
    kx_ref, vx_ref, bias_ref, o_ref = refs[2 * NA_KROWS:]
    k_all = jnp.concatenate([r[...] for r in k_refs] + [kx_ref[...]], axis=0)
    v_all = jnp.concatenate([r[...] for r in v_refs] + [vx_ref[...]], axis=0)
    n_q = q_ref.shape[0]
    n_ctx = kx_ref.shape[0]
    zeros = jnp.zeros((n_q, n_ctx), F32)
    nt = (((1,), (1,)), ((), ()))
    sls = [slice(hd * HEAD_DIM, (hd + 1) * HEAD_DIM) for hd in range(NA_HEADS)]
    scores = [lax.dot_general(q_ref[:, sl], k_all[:, sl], nt, preferred_element_type=F32)
              + jnp.concatenate([bias_ref[hd], zeros], axis=1) for hd, sl in enumerate(sls)]
    outs = _softmax_pv_heads(scores, [v_all[:, sl] for sl in sls])
    for sl, out in zip(sls, outs):
        o_ref[:, sl] = out


def _neighbourhood_attention(qd, kd, vd, rpb, n_ctx):
    bsz, s, _ = qd.shape
    t_len = s - n_ctx
    rows = t_len // GRID_W
    types, kbs, valid, dr, dc = _na_block_plan(rows)
    n_types = valid.shape[0]
    dr_pat = dr.reshape(n_types, NA_QROWS, GRID_W, NA_KROWS, GRID_W)[:, :, 0, :, 0]
    dc_pat = dc.reshape(n_types, NA_QROWS, GRID_W, NA_KROWS, GRID_W)[0, 0, :, 0, :]
    oh_r = jnp.asarray(dr_pat[..., None] == np.arange(2 * NA_ROWS - 1), F32)
    oh_c = jnp.asarray(dc_pat[..., None] == np.arange(2 * NA_COLS - 1), F32)
    bias = jnp.einsum('tamr,hrs,cks->thacmk', oh_r, rpb, oh_c, precision=HI)
    bias = bias.reshape(n_types, NA_HEADS, NA_QROWS * GRID_W, NA_KROWS * GRID_W)
    bias = jnp.where(valid[:, None], bias, ATT_NEG)
    n_q = NA_QROWS * GRID_W
    off_q = n_ctx // n_q
    off_k = n_ctx // GRID_W

    def kv_spec(m):
        return pl.BlockSpec((None, GRID_W, QD_W), lambda b, i, types, kbs: (b, kbs[i] + m + off_k, 0))

    ctx_spec = pl.BlockSpec((None, n_ctx, QD_W), lambda b, i, types, kbs: (b, 0, 0))
    grid_spec = pltpu.PrefetchScalarGridSpec(
        num_scalar_prefetch=2,
        grid=(bsz, rows // NA_QROWS),
        in_specs=([pl.BlockSpec((None, n_q, QD_W), lambda b, i, types, kbs: (b, i + off_q, 0))]
                  + [kv_spec(m) for m in range(NA_KROWS)] + [kv_spec(m) for m in range(NA_KROWS)]
                  + [ctx_spec, ctx_spec,
                     pl.BlockSpec((None, NA_HEADS, n_q, NA_KROWS * GRID_W),
                                  lambda b, i, types, kbs: (types[i], 0, 0, 0))]),
        out_specs=pl.BlockSpec((None, n_q, QD_W), lambda b, i, types, kbs: (b, i, 0)),
    )
    return pl.pallas_call(
        _na_kernel,
        grid_spec=grid_spec,
        out_shape=jax.ShapeDtypeStruct((bsz, t_len, QD_W), F32),
        compiler_params=_cparams("parallel", "arbitrary"),
        name="neighbourhood_attention",
    )(jnp.asarray(types), jnp.asarray(kbs), qd, *([kd] * NA_KROWS), *([vd] * NA_KROWS), kd, vd, bias)


RWKV_CHUNK = 64
RWKV_SUPER = ROW_BLOCK // RWKV_CHUNK
RWKV_CHUNKS_PER_STEP = 4
HI = lax.Precision.HIGHEST


RWKV_PREC_PAIR = "bf16"
RWKV_PREC_INV = "bf16"
RWKV_PREC_OUT = "bf16"
RWKV_PREC_STATE = "bf16x3"


def _pdot(x, y, dims, mode):
    if mode == "f32":
        return lax.dot_general(x, y, dims, precision=HI, preferred_element_type=F32)
    xh = x.astype(BF16)
    yh = y.astype(BF16)
    out = lax.dot_general(xh, yh, dims, preferred_element_type=F32)
    if mode == "bf16x3":
        xl = (x - xh.astype(F32)).astype(BF16)
        yl = (y - yh.astype(F32)).astype(BF16)
        out = (out + lax.dot_general(xh, yl, dims, preferred_element_type=F32)
               + lax.dot_general(xl, yh, dims, preferred_element_type=F32))
    return out


def _split_dot(x, e_bf16):
    hi = x.astype(BF16)
    lo = (x - hi.astype(F32)).astype(BF16)
    return jnp.dot(hi, e_bf16, preferred_element_type=F32) + jnp.dot(lo, e_bf16, preferred_element_type=F32)


def _rwkv_prep_kernel(z_ref, zp_ref, zn_ref, mu_ref, w0_ref, w2_ref, a0_ref, a2_ref, g2_ref, kk_ref, ka_ref,
                      rk_ref, e_ref, r_o, v_o, kk_o, g_o, bonus_o, logw_o, kd_o, bb_o):
    i = pl.program_id(1)
    last = pl.num_programs(1) - 1
    tm = z_ref.shape[0]
    hw = RWKV_WIDTH
    z = z_ref[:, S5_WIDTH:]
    row = lax.broadcasted_iota(jnp.int32, (tm, 1), 0)
    prev_halo = jnp.where(i <= 1, 0.0, zp_ref[7:8, S5_WIDTH:])
    next_halo = jnp.where((i == 0) | (i == last), 0.0, zn_ref[0:1, S5_WIDTH:])
    prev = jnp.where(row == 0, prev_halo, pltpu.roll(z, 1, 0))
    nxt = jnp.where(row == tm - 1, next_halo, pltpu.roll(z, tm - 1, 0))
    zs = z + mu_ref[0:1, :] * (prev - z) + mu_ref[1:2, :] * (nxt - z)
    c0, c1, c2, c3, c4 = RWKV_SPLITS
    r = zs[:, :c0]
    k = zs[:, c0:c1]
    v = zs[:, c1:c2]
    e = e_ref[...]
    g_o[...] = jnp.dot(jax.nn.sigmoid(zs[:, c4:]).astype(BF16), g2_ref[...], preferred_element_type=F32)
    kk = k * kk_ref[...]
    kk = kk * lax.rsqrt(_split_dot(kk * kk, e) + 1e-12)
    w_lora = jnp.dot(jnp.tanh(zs[:, c2:c3]).astype(BF16), w2_ref[...], preferred_element_type=F32)
    a_lora = jnp.dot(zs[:, c3:c4].astype(BF16), a2_ref[...], preferred_element_type=F32)
    bonus = jnp.zeros((tm, hw), F32)
    for d in range(2):
        x = w0_ref[d:d + 1, :] + w_lora[:, d * hw:(d + 1) * hw]
        softplus_neg = jnp.maximum(-x, 0.0) + jnp.log(1.0 + jnp.exp(-jnp.abs(x)))
        logw_o[d] = -jnp.exp(-softplus_neg - 0.5)
        a = jax.nn.sigmoid(a0_ref[d:d + 1, :] + a_lora[:, d * hw:(d + 1) * hw])
        kd = k * (1.0 + (a - 1.0) * ka_ref[...])
        kd_o[d] = kd
        bb_o[d] = kk * a
        bonus = bonus + _split_dot(r * kd * rk_ref[...], e) * v
    r_o[...] = r
    v_o[...] = v
    kk_o[...] = kk
    bonus_o[...] = bonus


def _rwkv_chunk_kernel(r_ref, v_ref, kk_ref, logw_ref, kd_ref, bb_ref, mn_ref, ry_ref):
    fwd = pl.program_id(1) == 0
    cn = RWKV_CHUNK
    hd = HEAD_DIM
    rowi = lax.broadcasted_iota(jnp.int32, (cn, cn), 0)
    coli = lax.broadcasted_iota(jnp.int32, (cn, cn), 1)
    lag = (rowi - coli) * jnp.where(fwd, 1, -1)
    before_incl = lag >= 0
    before = lag > 0
    eye = (rowi == coli).astype(F32)
    nn = (((1,), (0,)), ((), ()))
    nt = (((1,), (1,)), ((), ()))
    tn = (((0,), (0,)), ((), ()))

    a_t, r_t, b_t, k_t, b_h, k_h, g_end, vs, where = [], [], [], [], [], [], [], [], []
    for cc in range(RWKV_CHUNKS_PER_STEP):
        rs = slice(cc * cn, (cc + 1) * cn)
        logw = logw_ref[rs, :]
        lg = jnp.dot(before_incl.astype(F32), logw, precision=HI, preferred_element_type=F32)
        lg_tot = jnp.sum(logw, axis=0, keepdims=True)
        inv = jnp.exp(-lg)
        to_end = jnp.exp(lg_tot - lg)
        full = dict(a_t=-kk_ref[rs, :] * jnp.exp(lg - logw), r_t=r_ref[rs, :] * jnp.exp(lg),
                    b_t=bb_ref[rs, :] * inv, k_t=kd_ref[rs, :] * inv,
                    b_h=bb_ref[rs, :] * to_end, k_h=kd_ref[rs, :] * to_end, g_end=jnp.exp(lg_tot))
        for h in range(RWKV_HEADS):
            sl = slice(h * hd, (h + 1) * hd)
            a_t.append(full["a_t"][:, sl])
            r_t.append(full["r_t"][:, sl])
            b_t.append(full["b_t"][:, sl])
            k_t.append(full["k_t"][:, sl])
            b_h.append(full["b_h"][:, sl])
            k_h.append(full["k_h"][:, sl])
            g_end.append(full["g_end"][:, sl])
            vs.append(v_ref[rs, sl])
            where.append((cc, h))
    ps = [_pdot(jnp.concatenate([a, r], axis=0), jnp.concatenate([b, k], axis=0), nt, RWKV_PREC_PAIR)
          for a, r, b, k in zip(a_t, r_t, b_t, k_t)]
    n_ab = [jnp.where(before, p[:cn, :cn], 0.0) for p in ps]
    n_ak = [jnp.where(before, p[:cn, cn:], 0.0) for p in ps]
    n_rb = [jnp.where(before_incl, p[cn:, :cn], 0.0) for p in ps]
    n_rk = [jnp.where(before_incl, p[cn:, cn:], 0.0) for p in ps]
    tinv = [eye + n for n in n_ab]
    pw = n_ab
    for _ in range(int(math.log2(cn)) - 1):
        pw = [_pdot(x, x, nn, RWKV_PREC_INV) for x in pw]
        tinv = [t + _pdot(t, x, nn, RWKV_PREC_INV) for t, x in zip(tinv, pw)]
    akv = [_pdot(n, v, nn, RWKV_PREC_OUT) for n, v in zip(n_ak, vs)]
    rkv = [_pdot(n, v, nn, RWKV_PREC_OUT) for n, v in zip(n_rk, vs)]
    kv = [_pdot(k, v, tn, RWKV_PREC_OUT) for k, v in zip(k_h, vs)]
    zz = [_pdot(t, jnp.concatenate([a, w], axis=1), nn, RWKV_PREC_OUT)
          for t, a, w in zip(tinv, a_t, akv)]
    for u, (cc, h) in enumerate(where):
        ry_ref[cc, h] = (_pdot(n_rb[u], zz[u], nn, RWKV_PREC_OUT)
                         + jnp.concatenate([r_t[u], rkv[u]], axis=1))
        mn_ref[cc, h] = (_pdot(b_h[u], zz[u], tn, RWKV_PREC_OUT)
                         + jnp.concatenate([eye * g_end[u], kv[u]], axis=1))


def _rwkv_seq_kernel(mn_ref, ry_ref, y_ref, h_ref):
    @pl.when(pl.program_id(2) == 0)
    def _():
        h_ref[...] = jnp.zeros_like(h_ref)

    fwd = pl.program_id(1) == 0
    hd = HEAD_DIM
    for cc in range(RWKV_SUPER):
        c = jnp.where(fwd, cc, RWKV_SUPER - 1 - cc)
        row0 = pl.multiple_of(c * RWKV_CHUNK, RWKV_CHUNK)
        for h in range(RWKV_HEADS):
            mn = mn_ref[c, h]
            ry = ry_ref[c, h]
            state = h_ref[h]
            nn = (((1,), (0,)), ((), ()))
            y_ref[pl.ds(row0, RWKV_CHUNK), h * hd:(h + 1) * hd] = (
                _pdot(ry[:, :hd], state, nn, RWKV_PREC_STATE) + ry[:, hd:])
            h_ref[h] = _pdot(mn[:, :hd], state, nn, RWKV_PREC_STATE) + mn[:, hd:]


def _rwkv_post_kernel(y0_ref, y1_ref, bonus_ref, g_ref, gnw_ref, gnb_ref, e_ref, o_ref):
    y = y0_ref[...] + y1_ref[...]
    e = e_ref[...]
    mu = _split_dot(y, e) * (1.0 / HEAD_DIM)
    yc = y - mu
    var = _split_dot(yc * yc, e) * (1.0 / HEAD_DIM)
    yn = yc * lax.rsqrt(var + GN_EPS) * gnw_ref[...] + gnb_ref[...]
    o_ref[...] = (yn + bonus_ref[...]) * g_ref[...]


def _block_diag2(w):
    z = jnp.zeros_like(w[0])
    return jnp.concatenate([jnp.concatenate([w[0], z], 1), jnp.concatenate([z, w[1]], 1)], 0)


def _rwkv_mixer(z, shift_mu, w0, w2, a0, a2, g2, k_k, k_a, r_k, gn_w, gn_b):
    bsz, s, zw = z.shape
    hw = RWKV_WIDTH
    tm = ROW_BLOCK
    nblk = s // tm
    head_of = np.arange(hw) // HEAD_DIM
    e = jnp.asarray(head_of[:, None] == head_of[None, :], BF16)
    tok = pl.BlockSpec((None, tm, hw), lambda b, i: (b, i, 0))
    tok2 = pl.BlockSpec((None, 2, tm, hw), lambda b, i: (b, 0, i, 0))
    halo = tm // 8

    def full(shape):
        return pl.BlockSpec(shape, lambda b, i: (0,) * len(shape))

    r, v, kk, g, bonus, logw, kd, bb = pl.pallas_call(
        _rwkv_prep_kernel,
        grid=(bsz, nblk),
        in_specs=[pl.BlockSpec((None, tm, zw), lambda b, i: (b, i, 0)),
                  pl.BlockSpec((None, 8, zw), lambda b, i: (b, jnp.maximum(i * halo - 1, 0), 0)),
                  pl.BlockSpec((None, 8, zw), lambda b, i: (b, jnp.minimum((i + 1) * halo, s // 8 - 1), 0)),
                  full((2, RWKV_IN)), full((2, hw)), full((2 * RWKV_LORA_W, 2 * hw)), full((2, hw)),
                  full((2 * RWKV_LORA_A, 2 * hw)), full((RWKV_LORA_G, hw)), full((1, hw)), full((1, hw)),
                  full((1, hw)), full((hw, hw))],
        out_specs=[tok, tok, tok, tok, tok, tok2, tok2, tok2],
        out_shape=[jax.ShapeDtypeStruct((bsz, s, hw), F32)] * 5 + [jax.ShapeDtypeStruct((bsz, 2, s, hw), F32)] * 3,
        compiler_params=_cparams("parallel", "parallel"),
        name="rwkv_prep",
    )(z, z, z, shift_mu, w0, _block_diag2(w2).astype(BF16), a0, _block_diag2(a2).astype(BF16), g2.astype(BF16),
      k_k.reshape(1, hw), k_a.reshape(1, hw), r_k.reshape(1, hw), e)

    nch = s // RWKV_CHUNK
    step_rows = RWKV_CHUNKS_PER_STEP * RWKV_CHUNK
    ctok = pl.BlockSpec((None, step_rows, hw), lambda b, d, c: (b, c, 0))
    ctok2 = pl.BlockSpec((None, None, step_rows, hw), lambda b, d, c: (b, d, c, 0))
    mat_shape = jax.ShapeDtypeStruct((bsz, 2, nch, RWKV_HEADS, RWKV_CHUNK, 2 * HEAD_DIM), F32)
    mat_spec = pl.BlockSpec((None, None, RWKV_CHUNKS_PER_STEP, RWKV_HEADS, RWKV_CHUNK, 2 * HEAD_DIM),
                            lambda b, d, c: (b, d, c, 0, 0, 0))
    mn, ry = pl.pallas_call(
        _rwkv_chunk_kernel,
        grid=(bsz, 2, nch // RWKV_CHUNKS_PER_STEP),
        in_specs=[ctok, ctok, ctok, ctok2, ctok2, ctok2],
        out_specs=[mat_spec, mat_spec],
        out_shape=[mat_shape, mat_shape],
        compiler_params=_cparams("parallel", "parallel", "parallel"),
        name="rwkv_chunk",
    )(r, v, kk, logw, kd, bb)

    nsb = nch // RWKV_SUPER

    def sb_of(d, j):
        return jnp.where(d == 0, j, jnp.where(j == 0, 0, nsb - j))

    sup_shape = (bsz, 2, nsb, RWKV_SUPER, RWKV_HEADS, RWKV_CHUNK, 2 * HEAD_DIM)
    sup_spec = pl.BlockSpec((None, None, None, RWKV_SUPER, RWKV_HEADS, RWKV_CHUNK, 2 * HEAD_DIM),
                            lambda b, d, j: (b, d, sb_of(d, j), 0, 0, 0, 0))
    y = pl.pallas_call(
        _rwkv_seq_kernel,
        grid=(bsz, 2, nsb),
        in_specs=[sup_spec, sup_spec],
        out_specs=pl.BlockSpec((None, None, tm, hw), lambda b, d, j: (b, d, sb_of(d, j), 0)),
        out_shape=jax.ShapeDtypeStruct((bsz, 2, s, hw), F32),
        scratch_shapes=[pltpu.VMEM((RWKV_HEADS, HEAD_DIM, HEAD_DIM), F32)],
        compiler_params=_cparams("parallel", "parallel", "arbitrary"),
        name="rwkv_seq",
    )(mn.reshape(sup_shape), ry.reshape(sup_shape))

    vec = pl.BlockSpec((1, hw), lambda b, i: (0, 0))
    return pl.pallas_call(
        _rwkv_post_kernel,
        grid=(bsz, nblk),
        in_specs=[pl.BlockSpec((None, None, tm, hw), lambda b, i: (b, 0, i, 0)),
                  pl.BlockSpec((None, None, tm, hw), lambda b, i: (b, 1, i, 0)),
                  tok, tok, vec, vec, pl.BlockSpec((hw, hw), lambda b, i: (0, 0))],
        out_specs=tok,
        out_shape=jax.ShapeDtypeStruct((bsz, s, hw), F32),
        compiler_params=_cparams("parallel", "parallel"),
        name="rwkv_post",
    )(y, y, bonus, g, gn_w.reshape(1, hw), gn_b.reshape(1, hw), e)


def kernel(x, c, ctx, c_ctx,
           l0_w_mod, l0_b_mod, l0_w_in, l0_s5_lam_re, l0_s5_lam_im, l0_s5_log_dt, l0_s5_b_re, l0_s5_b_im,
           l0_s5_c_re, l0_s5_c_im, l0_s5_d, l0_w_glu, l0_b_glu, l0_rwkv_shift, l0_rwkv_w0, l0_rwkv_w2,
           l0_rwkv_a0, l0_rwkv_a2, l0_rwkv_g2, l0_rwkv_k_k, l0_rwkv_k_a, l0_rwkv_r_k, l0_rwkv_gn_w,
           l0_rwkv_gn_b, l0_w_out, l0_ln1_g, l0_ln1_b, l0_peer_wq, l0_peer_keys, l0_peer_u, l0_peer_v,
           l0_ln2_g, l0_ln2_b,
           l1_w_mod, l1_b_mod, l1_w_in, l1_sink, l1_rpb, l1_w_out, l1_ln1_g, l1_ln1_b, l1_peer_wq,
           l1_peer_keys, l1_peer_u, l1_peer_v, l1_ln2_g, l1_ln2_b):
    L = ctx.shape[1]
    h = jnp.concatenate([ctx, x], 1)

    tab = _mod_tables(c, c_ctx, l0_w_mod, l0_b_mod)
    z = _inproj(h, tab, 0, 1, l0_w_in.astype(BF16))
    ya = _s5_mixer(z[..., :S5_WIDTH], L, l0_s5_lam_re, l0_s5_lam_im, l0_s5_log_dt, l0_s5_b_re, l0_s5_b_im,
                   l0_s5_c_re, l0_s5_c_im, l0_s5_d, l0_w_glu, l0_b_glu)
    yb = _rwkv_mixer(z, l0_rwkv_shift, l0_rwkv_w0, l0_rwkv_w2, l0_rwkv_a0, l0_rwkv_a2,
                     l0_rwkv_g2, l0_rwkv_k_k, l0_rwkv_k_a, l0_rwkv_r_k, l0_rwkv_gn_w, l0_rwkv_gn_b)
    h = _outproj_ln(ya, yb, h, tab, 2, l0_w_out, l0_ln1_g, l0_ln1_b, 0)
    h = _peer_ln(h, tab, 0, l0_peer_wq, l0_peer_keys, l0_peer_u, l0_peer_v, l0_ln2_g, l0_ln2_b)

    tab = _mod_tables(c, c_ctx, l1_w_mod, l1_b_mod)
    qc, kc, vc, qd, kd, vd = _inproj_attn(h, tab, 0, 1, l1_w_in.astype(BF16), L)
    oc = _window_gqa(qc, kc, vc, l1_sink, L)
    od = _neighbourhood_attention(qd, kd, vd, l1_rpb, L)
    ctx_blocks = L // ROW_BLOCK
    hx = _outproj_ln(oc, od, h, tab, 2, l1_w_out, l1_ln1_g, l1_ln1_b, ctx_blocks)
    return _peer_ln(hx, tab, ctx_blocks, l1_peer_wq, l1_peer_keys, l1_peer_u, l1_peer_v, l1_ln2_g, l1_ln2_b)
```

```python
import functools
import math

import jax
import jax.numpy as jnp
import numpy as np
from jax import lax
from jax.experimental import pallas as pl
from jax.experimental.pallas import tpu as pltpu

F32 = jnp.float32
BF16 = jnp.bfloat16

D_MODEL = 1024
DEPTH = 2
GRID_W = 64
CTX_LEN = 256
HEAD_DIM = 64
S5_GROUP_CH = 16
S5_GROUPS = 16
S5_WIDTH = S5_GROUPS * S5_GROUP_CH
S5_STATE = 64
RWKV_HEADS = 8
RWKV_WIDTH = RWKV_HEADS * HEAD_DIM
RWKV_LORA_W = 64
RWKV_LORA_A = 64
RWKV_LORA_G = 128
RWKV_IN = 3 * RWKV_WIDTH + 2 * RWKV_LORA_W + 2 * RWKV_LORA_A + RWKV_LORA_G
RWKV_SPLITS = (RWKV_WIDTH, 2 * RWKV_WIDTH, 3 * RWKV_WIDTH,
               3 * RWKV_WIDTH + 2 * RWKV_LORA_W,
               3 * RWKV_WIDTH + 2 * RWKV_LORA_W + 2 * RWKV_LORA_A)
WIN_HEADS = 8
WIN_KV_HEADS = 2
WINDOW = 128
WIN_BLOCK = 128
NA_HEADS = 8
NA_ROWS = 8
NA_COLS = 16
ODD_SPLITS = (WIN_HEADS * HEAD_DIM, (WIN_HEADS + WIN_KV_HEADS) * HEAD_DIM,
              (WIN_HEADS + 2 * WIN_KV_HEADS) * HEAD_DIM,
              (WIN_HEADS + 2 * WIN_KV_HEADS + NA_HEADS) * HEAD_DIM,
              (WIN_HEADS + 2 * WIN_KV_HEADS + 2 * NA_HEADS) * HEAD_DIM)
ROPE_BASE = 10000.0
PEER_HEADS = 8
PEER_KEYS = 128
PEER_DK = 128
PEER_TOPK = 16
PEER_BLOCK = 128
ALPHA = (2.0 * DEPTH) ** 0.25
LN_EPS = 1e-5
GN_EPS = 64e-5

VMEM_LIMIT_BYTES = 56 * 1024 * 1024
ROW_BLOCK = 256


def _cparams(*sem):
    return pltpu.CompilerParams(dimension_semantics=sem, vmem_limit_bytes=VMEM_LIMIT_BYTES)


def _modulation_kernel(c_ref, w_ref, b_ref, o_ref):
    c = c_ref[...]
    s = c * jax.nn.sigmoid(c)
    o_ref[...] = jnp.dot(s.astype(BF16), w_ref[...].astype(BF16), preferred_element_type=F32) + b_ref[...]


def _modulation(cond, w_mod, b_mod):
    n, d = cond.shape
    nout = w_mod.shape[1]
    tn = 1536
    return pl.pallas_call(
        _modulation_kernel,
        grid=(nout // tn,),
        in_specs=[pl.BlockSpec((n, d), lambda j: (0, 0)),
                  pl.BlockSpec((d, tn), lambda j: (0, j)),
                  pl.BlockSpec((1, tn), lambda j: (0, j))],
        out_specs=pl.BlockSpec((n, tn), lambda j: (0, j)),
        out_shape=jax.ShapeDtypeStruct((n, nout), F32),
        compiler_params=_cparams("arbitrary"),
        name="modulation",
    )(cond, w_mod, b_mod.reshape(1, nout))


def _mod_tables(c, c_ctx, w_mod, b_mod):
    bsz = c.shape[0]
    cond = jnp.concatenate([c, c_ctx[None, :], jnp.zeros((8 - bsz - 1, c.shape[1]), F32)], 0)
    m = _modulation(cond, w_mod, b_mod)
    mx = m[:bsz].reshape(bsz, 1, 6, 1, D_MODEL)
    mc = jnp.broadcast_to(m[bsz].reshape(1, 1, 6, 1, D_MODEL), (bsz, 1, 6, 1, D_MODEL))
    return jnp.concatenate([mc, mx], 1)


def _tab_spec(which):
    return pl.BlockSpec((None, None, None, 1, D_MODEL),
                        lambda b, i: (b, jnp.minimum(i, 1), which, 0, 0))


def _inproj_kernel(x_ref, shift_ref, scale_ref, w_ref, o_ref):
    hm = x_ref[...] * (1.0 + scale_ref[...]) + shift_ref[...]
    o_ref[...] = jnp.dot(hm.astype(BF16), w_ref[...], preferred_element_type=F32)


def _inproj(h, tab, which_shift, which_scale, w_bf16):
    bsz, s, d = h.shape
    n = w_bf16.shape[1]
    return pl.pallas_call(
        _inproj_kernel,
        grid=(bsz, s // ROW_BLOCK),
        in_specs=[pl.BlockSpec((None, ROW_BLOCK, d), lambda b, i: (b, i, 0)),
                  _tab_spec(which_shift), _tab_spec(which_scale),
                  pl.BlockSpec((d, n), lambda b, i: (0, 0))],
        out_specs=pl.BlockSpec((None, ROW_BLOCK, n), lambda b, i: (b, i, 0)),
        out_shape=jax.ShapeDtypeStruct((bsz, s, n), F32),
        compiler_params=_cparams("parallel", "parallel"),
        name="inproj",
    )(h, tab, tab, w_bf16)


def _layer_norm_rows(z, g, b):
    mu = jnp.mean(z, -1, keepdims=True)
    zc = z - mu
    var = jnp.mean(zc * zc, -1, keepdims=True)
    return zc * lax.rsqrt(var + LN_EPS) * g + b


def _outproj_ln_kernel(ya_ref, yb_ref, h_ref, gate_ref, wa_ref, wb_ref, g_ref, b_ref, o_ref):
    proj = (jnp.dot(ya_ref[...].astype(BF16), wa_ref[...], preferred_element_type=F32)
            + jnp.dot(yb_ref[...].astype(BF16), wb_ref[...], preferred_element_type=F32))
    z = ALPHA * h_ref[...] + gate_ref[...] * proj
    o_ref[...] = _layer_norm_rows(z, g_ref[...], b_ref[...])


def _outproj_ln(ya, yb, h, tab, which_gate, w_out, ln_g, ln_b, row_off):
    bsz, rows, ka = ya.shape
    kb = yb.shape[-1]
    d = h.shape[-1]
    return pl.pallas_call(
        _outproj_ln_kernel,
        grid=(bsz, rows // ROW_BLOCK),
        in_specs=[pl.BlockSpec((None, ROW_BLOCK, ka), lambda b, i: (b, i, 0)),
                  pl.BlockSpec((None, ROW_BLOCK, kb), lambda b, i: (b, i, 0)),
                  pl.BlockSpec((None, ROW_BLOCK, d), lambda b, i: (b, i + row_off, 0)),
                  pl.BlockSpec((None, None, None, 1, d), lambda b, i: (b, jnp.minimum(i + row_off, 1), which_gate, 0, 0)),
                  pl.BlockSpec((ka, d), lambda b, i: (0, 0)),
                  pl.BlockSpec((kb, d), lambda b, i: (0, 0)),
                  pl.BlockSpec((1, d), lambda b, i: (0, 0)),
                  pl.BlockSpec((1, d), lambda b, i: (0, 0))],
        out_specs=pl.BlockSpec((None, ROW_BLOCK, d), lambda b, i: (b, i, 0)),
        out_shape=jax.ShapeDtypeStruct((bsz, rows, d), F32),
        compiler_params=_cparams("parallel", "parallel"),
        name="outproj_ln",
    )(ya, yb, h, tab, w_out[:ka].astype(BF16), w_out[ka:].astype(BF16), ln_g.reshape(1, d), ln_b.reshape(1, d))


PEER_NEG = -3.0e38
PEER_ETILE = 2048
PEER_SUBTILE = 256
SQRT_HALF = 0.7071067811865476


PEER_ROUTE_GROUP = 4


def _top_values(arrays, n):
    arrays = list(arrays)
    vals = [[] for _ in arrays]
    for _ in range(n):
        ms = [jnp.max(s, axis=0, keepdims=True) for s in arrays]
        for i, m in enumerate(ms):
            vals[i].append(m)
        arrays = [jnp.where(s == m, PEER_NEG, s) for s, m in zip(arrays, ms)]
    return vals


def _oddeven_merge_sort_network(n):
    pairs = []
    p = 1
    while p < n:
        k = p
        while k >= 1:
            for j in range(k % p, n - k, 2 * k):
                for i in range(min(k, n - j - k)):
                    if (i + j) // (2 * p) == (i + j + k) // (2 * p):
                        pairs.append((i + j, i + j + k))
            k //= 2
        p *= 2
    return pairs


def _top_values_sorted(s, n):
    sub = 8
    groups = s.shape[0] // sub
    x = s.reshape(groups, sub, s.shape[1])
    cols = [x[g] for g in range(groups)]
    for lo, hi in _oddeven_merge_sort_network(groups):
        a, b = cols[lo], cols[hi]
        cols[lo], cols[hi] = jnp.maximum(a, b), jnp.minimum(a, b)
    assert groups >= n
    vals = []
    for k in range(n):
        m = jnp.max(cols[0], axis=0, keepdims=True)
        vals.append(m)
        hit = cols[0] == m
        for g in range(n - k - 1):
            cols[g] = jnp.where(hit, cols[g + 1], cols[g])
    return vals


def _peer_route_kernel(h_ref, shift_ref, scale_ref, wq_ref, keys_ref,
                       pin_ref, cnt_ref, e1_ref, r2_ref, e2_ref):
    pin = (h_ref[...] * (1.0 + scale_ref[...]) + shift_ref[...]).astype(BF16)
    pin_ref[...] = pin
    q = jnp.dot(pin, wq_ref[...], preferred_element_type=F32).astype(BF16)
    st = lax.dot_general(keys_ref[...], q, (((1,), (1,)), ((), ())), preferred_element_type=F32)
    for h0 in range(0, PEER_HEADS, PEER_ROUTE_GROUP):
        hds = list(range(h0, h0 + PEER_ROUTE_GROUP))
        s1s = [st[hd * 2 * PEER_KEYS:hd * 2 * PEER_KEYS + PEER_KEYS] for hd in hds]
        s2s = [st[hd * 2 * PEER_KEYS + PEER_KEYS:(hd + 1) * 2 * PEER_KEYS] for hd in hds]
        v1s = [_top_values_sorted(s1, PEER_TOPK) for s1 in s1s]
        v2s = [_top_values_sorted(s2, PEER_TOPK) for s2 in s2s]
        rank2s = []
        for s2, v2 in zip(s2s, v2s):
            rank = jnp.zeros_like(s2)
            for b in range(PEER_TOPK):
                rank = rank + jnp.where(v2[b] > s2, 1.0, 0.0)
            rank2s.append(rank)
        frontiers = []
        for v1, v2 in zip(v1s, v2s):
            v2all = jnp.concatenate(v2, axis=0)
            frontiers.append(jnp.concatenate([v1[a] + v2all[:(PEER_TOPK + 1) // (a + 1)] for a in range(PEER_TOPK)],
                                             axis=0))
        tops = _top_values(frontiers, PEER_TOPK + 1)
        for hd, s1, s2, v1, v2, rank2, top in zip(hds, s1s, s2s, v1s, v2s, rank2s, tops):
            tau = 0.5 * (top[PEER_TOPK - 1] + top[PEER_TOPK])
            zsum = top[0] * 0.0
            for kk in range(PEER_TOPK):
                zsum = zsum + jnp.exp(top[kk] - top[0])
            need = tau - s1
            cnt = jnp.zeros_like(s1)
            for b in range(PEER_TOPK):
                cnt = cnt + jnp.where(v2[b] >= need, 1.0, 0.0)
            cnt_ref[hd] = jnp.where(s1 >= v1[PEER_TOPK - 1], cnt, 0.0)
            e1_ref[hd] = jnp.exp(s1 - v1[0])
            r2_ref[hd] = rank2.astype(BF16)
            e2_ref[hd] = (jnp.exp(s2 - v2[0]) / zsum).astype(BF16)


def _peer_dense_kernel(pin_ref, cnt_ref, e1_ref, r2_ref, e2_ref, u_ref, vt_ref,
                       h_ref, gate_ref, g_ref, b_ref, o_ref, acc_ref, w_ref):
    j = pl.program_id(2)

    last = pl.num_programs(2) - 1

    def apply_values(first):
        part = jnp.dot(vt_ref[...], w_ref[(j + 1) % 2], preferred_element_type=F32)
        if first:
            acc_ref[...] = part
        else:
            acc_ref[...] += part

    def score_and_gate():
        pin = pin_ref[...]
        cur = j % 2
        for sb in range(PEER_ETILE // PEER_SUBTILE):
            es = slice(sb * PEER_SUBTILE, (sb + 1) * PEER_SUBTILE)
            at = lax.dot_general(u_ref[es, :], pin, (((1,), (1,)), ((), ())), preferred_element_type=F32)
            for i2 in range(PEER_SUBTILE // PEER_KEYS):
                ii = sb * (PEER_SUBTILE // PEER_KEYS) + i2
                a = at[i2 * PEER_KEYS:(i2 + 1) * PEER_KEYS]
                act = (0.5 * a * (1.0 + lax.erf(a * SQRT_HALF))).astype(BF16)
                g = jnp.zeros(a.shape, BF16)
                for hd in range(PEER_HEADS):
                    cnt_row = cnt_ref[hd, pl.ds(ii, 1), :].astype(BF16)
                    c_row = e1_ref[hd, pl.ds(ii, 1), :].astype(BF16)
                    g = g + jnp.where(r2_ref[hd] < cnt_row, e2_ref[hd] * c_row, jnp.zeros((), BF16))
                w_ref[cur, ii * PEER_KEYS:(ii + 1) * PEER_KEYS, :] = g * act

    @pl.when(j == 0)
    def _():
        score_and_gate()

    @pl.when(j == 1)
    def _():
        apply_values(first=True)
        score_and_gate()

    @pl.when((j > 1) & (j < last))
    def _():
        apply_values(first=False)
        score_and_gate()

    @pl.when(j == last)
    def _():
        apply_values(first=False)
        z = ALPHA * h_ref[...] + gate_ref[...] * acc_ref[...].T
        o_ref[...] = _layer_norm_rows(z, g_ref[...], b_ref[...])


def _peer_block_diag_keys(sub_keys):
    hh, two, nk, dh = sub_keys.shape
    eye = jnp.eye(hh * two, dtype=F32)
    kb = sub_keys.reshape(hh * two, nk, dh)
    return jnp.einsum('akd,ab->akbd', kb, eye).reshape(hh * two * nk, hh * two * dh)


def _peer_ln(h, tab, seg_off, wq, sub_keys, u_tab, v_tab, ln_g, ln_b):
    bsz, s, d = h.shape
    tm = ROW_BLOCK
    nblk = s // tm
    hk = PEER_HEADS * 2 * PEER_KEYS
    keys_bd = _peer_block_diag_keys(sub_keys).astype(BF16)

    def tab_spec3(which):
        return pl.BlockSpec((None, None, None, 1, d),
                            lambda b, i, *_: (b, jnp.minimum(i + seg_off, 1), which, 0, 0))

    route_dims = (bsz, nblk, PEER_HEADS, PEER_KEYS, tm)
    route_spec = pl.BlockSpec((None, None, PEER_HEADS, PEER_KEYS, tm), lambda b, i: (b, i, 0, 0, 0))
    pin, cnt, e1, r2, e2 = pl.pallas_call(
        _peer_route_kernel,
        grid=(bsz, nblk),
        in_specs=[pl.BlockSpec((None, tm, d), lambda b, i: (b, i, 0)),
                  tab_spec3(3), tab_spec3(4),
                  pl.BlockSpec((d, PEER_HEADS * PEER_DK), lambda b, i: (0, 0)),
                  pl.BlockSpec((hk, PEER_HEADS * PEER_DK), lambda b, i: (0, 0))],
        out_specs=[pl.BlockSpec((None, tm, d), lambda b, i: (b, i, 0)),
                   route_spec, route_spec, route_spec, route_spec],
        out_shape=[jax.ShapeDtypeStruct((bsz, nblk * tm, d), BF16),
                   jax.ShapeDtypeStruct(route_dims, F32), jax.ShapeDtypeStruct(route_dims, F32),
                   jax.ShapeDtypeStruct(route_dims, BF16), jax.ShapeDtypeStruct(route_dims, BF16)],
        compiler_params=_cparams("parallel", "parallel"),
        name="peer_route",
    )(h, tab, tab, wq.astype(BF16), keys_bd)

    n_exp = u_tab.shape[0]
    rows_per_step = PEER_ETILE // PEER_KEYS
    full_spec = pl.BlockSpec((None, None, PEER_HEADS, PEER_KEYS, tm), lambda b, i, j: (b, i, 0, 0, 0))
    n_tiles = n_exp // PEER_ETILE
    row_spec = pl.BlockSpec((None, None, PEER_HEADS, rows_per_step, tm),
                            lambda b, i, j: (b, i, 0, jnp.minimum(j, n_tiles - 1), 0))
    return pl.pallas_call(
        _peer_dense_kernel,
        grid=(bsz, nblk, n_tiles + 1),
        in_specs=[pl.BlockSpec((None, tm, d), lambda b, i, j: (b, i, 0)),
                  row_spec, row_spec, full_spec, full_spec,
                  pl.BlockSpec((PEER_ETILE, d), lambda b, i, j: (jnp.minimum(j, n_tiles - 1), 0)),
                  pl.BlockSpec((d, PEER_ETILE), lambda b, i, j: (0, jnp.maximum(j - 1, 0))),
                  pl.BlockSpec((None, tm, d), lambda b, i, j: (b, i, 0)),
                  tab_spec3(5),
                  pl.BlockSpec((1, d), lambda b, i, j: (0, 0)),
                  pl.BlockSpec((1, d), lambda b, i, j: (0, 0))],
        out_specs=pl.BlockSpec((None, tm, d), lambda b, i, j: (b, i, 0)),
        out_shape=jax.ShapeDtypeStruct((bsz, nblk * tm, d), F32),
        scratch_shapes=[pltpu.VMEM((d, tm), F32), pltpu.VMEM((2, PEER_ETILE, tm), BF16)],
        compiler_params=_cparams("parallel", "parallel", "arbitrary"),
        name="peer_dense",
    )(pin, cnt, e1, r2, e2, u_tab.astype(BF16), v_tab.T.astype(BF16), h, tab,
      ln_g.reshape(1, d), ln_b.reshape(1, d))


S5_CHAINS = 8
S5_TCHUNK = 128
S5_NSTATE = S5_GROUPS * S5_STATE


def _s5_scan_kernel(uf_ref, ub_ref, wb_ref, wc_ref, a_ref, yf_ref, yb_ref,
                    fre_ref, fim_ref, bre_ref, bim_ref, h_ref):
    @pl.when(pl.program_id(0) == 0)
    def _():
        h_ref[...] = jnp.zeros_like(h_ref)

    rows = uf_ref.shape[0]
    steps = rows // S5_CHAINS
    uf = uf_ref[...].astype(BF16)
    ub = ub_ref[...].astype(BF16)
    fre_ref[...] = jnp.dot(uf, wb_ref[0], preferred_element_type=F32)
    fim_ref[...] = jnp.dot(uf, wb_ref[1], preferred_element_type=F32)
    bre_ref[...] = jnp.dot(ub, wb_ref[2], preferred_element_type=F32)
    bim_ref[...] = jnp.dot(ub, wb_ref[3], preferred_element_type=F32)
    a_re = a_ref[0]
    a_im = a_ref[1]
    is_fwd = lax.broadcasted_iota(jnp.int32, (S5_CHAINS, S5_NSTATE), 0) < (S5_CHAINS // 2)

    def step(t, carry):
        h_re, h_im = carry
        of = pl.multiple_of(t * S5_CHAINS, S5_CHAINS)
        ob = pl.multiple_of((steps - 1 - t) * S5_CHAINS, S5_CHAINS)
        n_re = a_re * h_re - a_im * h_im + jnp.where(is_fwd, fre_ref[pl.ds(of, S5_CHAINS), :],
                                                     bre_ref[pl.ds(ob, S5_CHAINS), :])
        n_im = a_re * h_im + a_im * h_re + jnp.where(is_fwd, fim_ref[pl.ds(of, S5_CHAINS), :],
                                                     bim_ref[pl.ds(ob, S5_CHAINS), :])
        fre_ref[pl.ds(of, S5_CHAINS), :] = n_re
        fim_ref[pl.ds(of, S5_CHAINS), :] = n_im
        bre_ref[pl.ds(ob, S5_CHAINS), :] = n_re
        bim_ref[pl.ds(ob, S5_CHAINS), :] = n_im
        return n_re, n_im

    h_re, h_im = lax.fori_loop(0, steps, step, (h_ref[0], h_ref[1]), unroll=2)
    h_ref[0] = h_re
    h_ref[1] = h_im
    yf_ref[...] = (jnp.dot(fre_ref[...].astype(BF16), wc_ref[0], preferred_element_type=F32)
                   - jnp.dot(fim_ref[...].astype(BF16), wc_ref[1], preferred_element_type=F32))
    yb_ref[...] = (jnp.dot(bre_ref[...].astype(BF16), wc_ref[2], preferred_element_type=F32)
                   - jnp.dot(bim_ref[...].astype(BF16), wc_ref[3], preferred_element_type=F32))


def _s5_glu_kernel(u_ref, y0_ref, y1_ref, d_ref, w_ref, b_ref, o_ref):
    y = u_ref[...] * d_ref[...] + y0_ref[...] + y1_ref[...]
    y = 0.5 * y * (1.0 + lax.erf(y * SQRT_HALF))
    gate = jnp.dot(y.astype(BF16), w_ref[...], preferred_element_type=F32) + b_ref[...]
    o_ref[...] = y * jax.nn.sigmoid(gate)


def _s5_params(lam_re, lam_im, log_dt, b_re, b_im, c_re, c_im):
    eye = jnp.eye(S5_GROUPS, dtype=F32)
    wb, wc, a_rows = [], [], [[], []]
    for d in range(2):
        lr, li = lam_re[d], lam_im[d]
        dt = jnp.exp(log_dt[d])[:, None]
        mag = jnp.exp(lr * dt)
        ab_re, ab_im = mag * jnp.cos(li * dt), mag * jnp.sin(li * dt)
        den = lr * lr + li * li
        nr = ab_re - 1.0
        coef_re = (nr * lr + ab_im * li) / den
        coef_im = (ab_im * lr - nr * li) / den
        bb_re = coef_re[..., None] * b_re[d] - coef_im[..., None] * b_im[d]
        bb_im = coef_re[..., None] * b_im[d] + coef_im[..., None] * b_re[d]
        for bb in (bb_re, bb_im):
            wb.append(jnp.einsum('gpi,gh->gihp', bb, eye).reshape(S5_WIDTH, S5_NSTATE))
        for cc in (c_re[d], c_im[d]):
            wc.append(jnp.einsum('gip,gh->gphi', cc, eye).reshape(S5_NSTATE, S5_WIDTH))
        for k, ab in enumerate((ab_re, ab_im)):
            a_rows[k].append(jnp.broadcast_to(ab.reshape(1, S5_NSTATE), (S5_CHAINS // 2, S5_NSTATE)))
    a_tiles = jnp.stack([jnp.concatenate(a_rows[0], 0), jnp.concatenate(a_rows[1], 0)])
    return jnp.stack(wb).astype(BF16), jnp.stack(wc).astype(BF16), a_tiles


def _s5_mixer(u, n_ctx, lam_re, lam_im, log_dt, b_re, b_im, c_re, c_im, d_skip, w_glu, b_glu):
    bsz, s, w = u.shape
    assert 2 * bsz == S5_CHAINS
    wb, wc, a_tiles = _s5_params(lam_re, lam_im, log_dt, b_re, b_im, c_re, c_im)
    ut = jnp.transpose(u, (1, 0, 2))
    chain = jnp.concatenate([ut, ut], 1).reshape(s * S5_CHAINS, w)
    rows = S5_TCHUNK * S5_CHAINS
    n_chunks = s // S5_TCHUNK
    ctx_chunks = n_ctx // S5_TCHUNK

    def mirror(i):
        return jnp.where(i < ctx_chunks, ctx_chunks - 1 - i, n_chunks - 1 - (i - ctx_chunks))

    state_buf = pltpu.VMEM((rows, S5_NSTATE), F32)
    y_shape = jax.ShapeDtypeStruct((s * S5_CHAINS, w), F32)
    yf, yb = pl.pallas_call(
        _s5_scan_kernel,
        grid=(n_chunks,),
        in_specs=[pl.BlockSpec((rows, w), lambda i: (i, 0)),
                  pl.BlockSpec((rows, w), lambda i: (mirror(i), 0)),
                  pl.BlockSpec((4, w, S5_NSTATE), lambda i: (0, 0, 0)),
                  pl.BlockSpec((4, S5_NSTATE, w), lambda i: (0, 0, 0)),
                  pl.BlockSpec((2, S5_CHAINS, S5_NSTATE), lambda i: (0, 0, 0))],
        out_specs=[pl.BlockSpec((rows, w), lambda i: (i, 0)),
                   pl.BlockSpec((rows, w), lambda i: (mirror(i), 0))],
        out_shape=[y_shape, y_shape],
        scratch_shapes=[state_buf, state_buf, state_buf, state_buf,
                        pltpu.VMEM((2, S5_CHAINS, S5_NSTATE), F32)],
        compiler_params=_cparams("arbitrary"),
        name="s5_scan",
    )(chain, chain, wb, wc, a_tiles)
    y_fwd = jnp.transpose(yf.reshape(s, S5_CHAINS, w)[:, :bsz], (1, 0, 2))
    y_bwd = jnp.transpose(yb.reshape(s, S5_CHAINS, w)[:, bsz:], (1, 0, 2))
    tok = pl.BlockSpec((None, ROW_BLOCK, w), lambda b, i: (b, i, 0))
    vec = pl.BlockSpec((1, w), lambda b, i: (0, 0))
    return pl.pallas_call(
        _s5_glu_kernel,
        grid=(bsz, s // ROW_BLOCK),
        in_specs=[tok, tok, tok, vec, pl.BlockSpec((w, w), lambda b, i: (0, 0)), vec],
        out_specs=tok,
        out_shape=jax.ShapeDtypeStruct((bsz, s, w), F32),
        compiler_params=_cparams("parallel", "parallel"),
        name="s5_glu",
    )(u, y_fwd, y_bwd, d_skip.reshape(1, w), w_glu.astype(BF16), b_glu.reshape(1, w))


ATT_SCALE = HEAD_DIM ** -0.5
ATT_NEG = -1e30
ROPE_QUARTER = HEAD_DIM // 4
QC_W = WIN_HEADS * HEAD_DIM
KC_W = WIN_KV_HEADS * HEAD_DIM
QD_W = NA_HEADS * HEAD_DIM


def _rope_tables(n_ctx, t):
    pos = np.arange(t)
    rowcol = np.stack([pos // GRID_W, pos % GRID_W], 1).astype(np.float32)
    lane = np.arange(QC_W)
    j = lane % HEAD_DIM
    freqs = (ROPE_BASE ** (-np.arange(ROPE_QUARTER, dtype=np.float32) / ROPE_QUARTER)).astype(np.float32)
    ang = rowcol[:, j // (HEAD_DIM // 2)] * freqs[j % ROPE_QUARTER][None, :]
    ang = jnp.asarray(ang, F32)
    sign = np.where((j % (HEAD_DIM // 2)) < ROPE_QUARTER, -1.0, 1.0).astype(np.float32)
    cos = jnp.concatenate([jnp.ones((n_ctx, QC_W), F32), jnp.cos(ang)], 0)
    sin = jnp.concatenate([jnp.zeros((n_ctx, QC_W), F32), jnp.sin(ang) * sign[None, :]], 0)
    return cos, sin


def _rope(x, cos, sin):
    n = x.shape[-1]
    lane = lax.broadcasted_iota(jnp.int32, (1, n), 1)
    first = (lane % (HEAD_DIM // 2)) < ROPE_QUARTER
    partner = jnp.where(first, pltpu.roll(x, n - ROPE_QUARTER, 1), pltpu.roll(x, ROPE_QUARTER, 1))
    return x * cos + partner * sin


def _inproj_attn_kernel(x_ref, shift_ref, scale_ref, w_ref, cos_ref, sin_ref,
                        qc_ref, kc_ref, vc_ref, qd_ref, kd_ref, vd_ref):
    hm = x_ref[...] * (1.0 + scale_ref[...]) + shift_ref[...]
    z = jnp.dot(hm.astype(BF16), w_ref[...], preferred_element_type=F32)
    c0, c1, c2, c3, c4 = ODD_SPLITS
    cos = cos_ref[...]
    sin = sin_ref[...]
    qc_ref[...] = (_rope(z[:, 0:c0], cos, sin) * ATT_SCALE).astype(BF16)
    kc_ref[...] = _rope(z[:, c0:c1], cos[:, :KC_W], sin[:, :KC_W]).astype(BF16)
    vc_ref[...] = z[:, c1:c2].astype(BF16)
    qd_ref[...] = (z[:, c2:c3] * ATT_SCALE).astype(BF16)
    kd_ref[...] = z[:, c3:c4].astype(BF16)
    vd_ref[...] = z[:, c4:].astype(BF16)


def _inproj_attn(h, tab, which_shift, which_scale, w_bf16, n_ctx):
    bsz, s, d = h.shape
    n = w_bf16.shape[1]
    cos, sin = _rope_tables(n_ctx, s - n_ctx)
    widths = (QC_W, KC_W, KC_W, QD_W, QD_W, QD_W)
    return pl.pallas_call(
        _inproj_attn_kernel,
        grid=(bsz, s // ROW_BLOCK),
        in_specs=[pl.BlockSpec((None, ROW_BLOCK, d), lambda b, i: (b, i, 0)),
                  _tab_spec(which_shift), _tab_spec(which_scale),
                  pl.BlockSpec((d, n), lambda b, i: (0, 0)),
                  pl.BlockSpec((ROW_BLOCK, QC_W), lambda b, i: (i, 0)),
                  pl.BlockSpec((ROW_BLOCK, QC_W), lambda b, i: (i, 0))],
        out_specs=[pl.BlockSpec((None, ROW_BLOCK, w), lambda b, i: (b, i, 0)) for w in widths],
        out_shape=[jax.ShapeDtypeStruct((bsz, s, w), BF16) for w in widths],
        compiler_params=_cparams("parallel", "parallel"),
        name="inproj_attn",
    )(h, tab, tab, w_bf16, cos, sin)


def _softmax_pv_heads(scores, values, extra_logits=None):
    ms = [jnp.max(s, axis=1, keepdims=True) for s in scores]
    if extra_logits is not None:
        ms = [jnp.maximum(m, x) for m, x in zip(ms, extra_logits)]
    ps = [jnp.exp(s - m) for s, m in zip(scores, ms)]
    dens = [jnp.sum(p, axis=1, keepdims=True) for p in ps]
    if extra_logits is not None:
        dens = [d + jnp.exp(x - m) for d, x, m in zip(dens, extra_logits, ms)]
    outs = [jnp.dot(p.astype(BF16), v, preferred_element_type=F32) for p, v in zip(ps, values)]
    return [o / d for o, d in zip(outs, dens)]


def _window_kernel(q_ref, kp_ref, kc_ref, kn_ref, kx_ref, vp_ref, vc_ref, vn_ref, vx_ref, sink_ref, o_ref, *, t_len):
    i = pl.program_id(1)
    wb = WIN_BLOCK
    qi = lax.broadcasted_iota(jnp.int32, (wb, 1), 0)
    kj = lax.broadcasted_iota(jnp.int32, (1, 3 * wb), 1) - wb
    kabs = i * wb + kj
    valid = (jnp.abs(kj - qi) <= WINDOW) & (kabs >= 0) & (kabs < t_len)
    n_ctx = kx_ref.shape[0]
    bias = jnp.concatenate([jnp.where(valid, 0.0, ATT_NEG), jnp.zeros((wb, n_ctx), F32)], axis=1)
    k_all = jnp.concatenate([kp_ref[...], kc_ref[...], kn_ref[...], kx_ref[...]], axis=0)
    v_all = jnp.concatenate([vp_ref[...], vc_ref[...], vn_ref[...], vx_ref[...]], axis=0)
    group = WIN_HEADS // WIN_KV_HEADS
    nt = (((1,), (1,)), ((), ()))
    kgs = [k_all[:, g * HEAD_DIM:(g + 1) * HEAD_DIM] for g in range(WIN_KV_HEADS)]
    vgs = [v_all[:, g * HEAD_DIM:(g + 1) * HEAD_DIM] for g in range(WIN_KV_HEADS)]
    scores = [lax.dot_general(q_ref[:, hq * HEAD_DIM:(hq + 1) * HEAD_DIM], kgs[hq // group], nt,
                              preferred_element_type=F32) + bias for hq in range(WIN_HEADS)]
    outs = _softmax_pv_heads(scores, [vgs[hq // group] for hq in range(WIN_HEADS)],
                             [sink_ref[hq:hq + 1, 0:1] for hq in range(WIN_HEADS)])
    for hq in range(WIN_HEADS):
        o_ref[:, hq * HEAD_DIM:(hq + 1) * HEAD_DIM] = outs[hq]


def _window_gqa(qc, kc, vc, sink, n_ctx):
    bsz, s, _ = qc.shape
    t_len = s - n_ctx
    nb = t_len // WIN_BLOCK
    off = n_ctx // WIN_BLOCK

    def kv_spec(delta):
        return pl.BlockSpec((None, WIN_BLOCK, KC_W), lambda b, i: (b, jnp.clip(i + delta, 0, nb - 1) + off, 0))

    ctx_spec = pl.BlockSpec((None, n_ctx, KC_W), lambda b, i: (b, 0, 0))
    sink_tile = jnp.broadcast_to(sink.reshape(WIN_HEADS, 1), (WIN_HEADS, 128))
    return pl.pallas_call(
        functools.partial(_window_kernel, t_len=t_len),
        grid=(bsz, nb),
        in_specs=[pl.BlockSpec((None, WIN_BLOCK, QC_W), lambda b, i: (b, i + off, 0)),
                  kv_spec(-1), kv_spec(0), kv_spec(1), ctx_spec,
                  kv_spec(-1), kv_spec(0), kv_spec(1), ctx_spec,
                  pl.BlockSpec((WIN_HEADS, 128), lambda b, i: (0, 0))],
        out_specs=pl.BlockSpec((None, WIN_BLOCK, QC_W), lambda b, i: (b, i, 0)),
        out_shape=jax.ShapeDtypeStruct((bsz, t_len, QC_W), F32),
        compiler_params=_cparams("parallel", "parallel"),
        name="window_gqa",
    )(qc, kc, kc, kc, kc, vc, vc, vc, vc, sink_tile)


NA_QROWS = 2
NA_KROWS = NA_QROWS + NA_ROWS - 1


def _na_block_plan(rows):
    kh = min(NA_ROWS, rows)
    kw = min(NA_COLS, GRID_W)
    c = np.arange(GRID_W)
    cs = np.clip(c - kw // 2, 0, GRID_W - kw)
    kc = np.arange(GRID_W)
    col_valid = (kc[None, :] >= cs[:, None]) & (kc[None, :] < cs[:, None] + kw)
    dc = np.clip(kc[None, :] - c[:, None] + (NA_COLS - 1), 0, 2 * NA_COLS - 2)
    patterns, types, kbs = {}, [], []
    for r0 in range(0, rows, NA_QROWS):
        kb = int(np.clip(r0 - kh // 2, 0, rows - NA_KROWS))
        r = r0 + np.arange(NA_QROWS)
        rs = np.clip(r - kh // 2, 0, rows - kh)
        kr = kb + np.arange(NA_KROWS)
        row_valid = (kr[None, :] >= rs[:, None]) & (kr[None, :] < rs[:, None] + kh)
        dr = np.clip(kr[None, :] - r[:, None] + (NA_ROWS - 1), 0, 2 * NA_ROWS - 2)
        key = (row_valid.tobytes(), dr.tobytes())
        if key not in patterns:
            valid = row_valid[:, None, :, None] & col_valid[None, :, None, :]
            shape = valid.shape
            patterns[key] = (len(patterns), valid.reshape(NA_QROWS * GRID_W, NA_KROWS * GRID_W),
                             np.broadcast_to(dr[:, None, :, None], shape).reshape(valid.shape[0] * shape[1], -1),
                             np.broadcast_to(dc[None, :, None, :], shape).reshape(valid.shape[0] * shape[1], -1))
        types.append(patterns[key][0])
        kbs.append(kb)
    ordered = sorted(patterns.values(), key=lambda p: p[0])
    valid = np.stack([p[1] for p in ordered])
    dr = np.stack([p[2] for p in ordered])
    dc = np.stack([p[3] for p in ordered])
    return np.asarray(types, np.int32), np.asarray(kbs, np.int32), valid, dr, dc


def _na_kernel(types_ref, kbs_ref, q_ref, *refs):
    del types_ref, kbs_ref
    k_refs = refs[:NA_KROWS]
    v_refs = refs[NA_KROWS:2 * NA_KROWS]
    kx_ref, vx_ref, bias_ref, o_ref = refs[2 * NA_KROWS:]
    k_all = jnp.concatenate([r[...] for r in k_refs] + [kx_ref[...]], axis=0)
    v_all = jnp.concatenate([r[...] for r in v_refs] + [vx_ref[...]], axis=0)
    n_q = q_ref.shape[0]
    n_ctx = kx_ref.shape[0]
    zeros = jnp.zeros((n_q, n_ctx), F32)
    nt = (((1,), (1,)), ((), ()))
    sls = [slice(hd * HEAD_DIM, (hd + 1) * HEAD_DIM) for hd in range(NA_HEADS)]
    scores = [lax.dot_general(q_ref[:, sl], k_all[:, sl], nt, preferred_element_type=F32)
              + jnp.concatenate([bias_ref[hd], zeros], axis=1) for hd, sl in enumerate(sls)]
    outs = _softmax_pv_heads(scores, [v_all[:, sl] for sl in sls])
    for sl, out in zip(sls, outs):
        o_ref[:, sl] = out


def _neighbourhood_attention(qd, kd, vd, rpb, n_ctx):
    bsz, s, _ = qd.shape
    t_len = s - n_ctx
    rows = t_len // GRID_W
    types, kbs, valid, dr, dc = _na_block_plan(rows)
    n_types = valid.shape[0]
    dr_pat = dr.reshape(n_types, NA_QROWS, GRID_W, NA_KROWS, GRID_W)[:, :, 0, :, 0]
    dc_pat = dc.reshape(n_types, NA_QROWS, GRID_W, NA_KROWS, GRID_W)[0, 0, :, 0, :]
    oh_r = jnp.asarray(dr_pat[..., None] == np.arange(2 * NA_ROWS - 1), F32)
    oh_c = jnp.asarray(dc_pat[..., None] == np.arange(2 * NA_COLS - 1), F32)
    bias = jnp.einsum('tamr,hrs,cks->thacmk', oh_r, rpb, oh_c, precision=HI)
    bias = bias.reshape(n_types, NA_HEADS, NA_QROWS * GRID_W, NA_KROWS * GRID_W)
    bias = jnp.where(valid[:, None], bias, ATT_NEG)
    n_q = NA_QROWS * GRID_W
    off_q = n_ctx // n_q
    off_k = n_ctx // GRID_W

    def kv_spec(m):
        return pl.BlockSpec((None, GRID_W, QD_W), lambda b, i, types, kbs: (b, kbs[i] + m + off_k, 0))

    ctx_spec = pl.BlockSpec((None, n_ctx, QD_W), lambda b, i, types, kbs: (b, 0, 0))
    grid_spec = pltpu.PrefetchScalarGridSpec(
        num_scalar_prefetch=2,
        grid=(bsz, rows // NA_QROWS),
        in_specs=([pl.BlockSpec((None, n_q, QD_W), lambda b, i, types, kbs: (b, i + off_q, 0))]
                  + [kv_spec(m) for m in range(NA_KROWS)] + [kv_spec(m) for m in range(NA_KROWS)]
                  + [ctx_spec, ctx_spec,
                     pl.BlockSpec((None, NA_HEADS, n_q, NA_KROWS * GRID_W),
                                  lambda b, i, types, kbs: (types[i], 0, 0, 0))]),
        out_specs=pl.BlockSpec((None, n_q, QD_W), lambda b, i, types, kbs: (b, i, 0)),
    )
    return pl.pallas_call(
        _na_kernel,
        grid_spec=grid_spec,
        out_shape=jax.ShapeDtypeStruct((bsz, t_len, QD_W), F32),
        compiler_params=_cparams("parallel", "arbitrary"),
        name="neighbourhood_attention",
    )(jnp.asarray(types), jnp.asarray(kbs), qd, *([kd] * NA_KROWS), *([vd] * NA_KROWS), kd, vd, bias)


RWKV_CHUNK = 64
RWKV_SUPER = ROW_BLOCK // RWKV_CHUNK
RWKV_CHUNKS_PER_STEP = 4
HI = lax.Precision.HIGHEST


RWKV_PREC_PAIR = "bf16"
RWKV_PREC_INV = "bf16"
RWKV_PREC_OUT = "bf16"
RWKV_PREC_STATE = "bf16x3"


def _pdot(x, y, dims, mode):
    if mode == "f32":
        return lax.dot_general(x, y, dims, precision=HI, preferred_element_type=F32)
    xh = x.astype(BF16)
    yh = y.astype(BF16)
    out = lax.dot_general(xh, yh, dims, preferred_element_type=F32)
    if mode == "bf16x3":
        xl = (x - xh.astype(F32)).astype(BF16)
        yl = (y - yh.astype(F32)).astype(BF16)
        out = (out + lax.dot_general(xh, yl, dims, preferred_element_type=F32)
               + lax.dot_general(xl, yh, dims, preferred_element_type=F32))
    return out


def _split_dot(x, e_bf16):
    hi = x.astype(BF16)
    lo = (x - hi.astype(F32)).astype(BF16)
    return jnp.dot(hi, e_bf16, preferred_element_type=F32) + jnp.dot(lo, e_bf16, preferred_element_type=F32)


def _rwkv_prep_kernel(z_ref, zp_ref, zn_ref, mu_ref, w0_ref, w2_ref, a0_ref, a2_ref, g2_ref, kk_ref, ka_ref,
                      rk_ref, e_ref, r_o, v_o, kk_o, g_o, bonus_o, logw_o, kd_o, bb_o):
    i = pl.program_id(1)
    last = pl.num_programs(1) - 1
    tm = z_ref.shape[0]
    hw = RWKV_WIDTH
    z = z_ref[:, S5_WIDTH:]
    row = lax.broadcasted_iota(jnp.int32, (tm, 1), 0)
    prev_halo = jnp.where(i <= 1, 0.0, zp_ref[7:8, S5_WIDTH:])
    next_halo = jnp.where((i == 0) | (i == last), 0.0, zn_ref[0:1, S5_WIDTH:])
    prev = jnp.where(row == 0, prev_halo, pltpu.roll(z, 1, 0))
    nxt = jnp.where(row == tm - 1, next_halo, pltpu.roll(z, tm - 1, 0))
    zs = z + mu_ref[0:1, :] * (prev - z) + mu_ref[1:2, :] * (nxt - z)
    c0, c1, c2, c3, c4 = RWKV_SPLITS
    r = zs[:, :c0]
    k = zs[:, c0:c1]
    v = zs[:, c1:c2]
    e = e_ref[...]
    g_o[...] = jnp.dot(jax.nn.sigmoid(zs[:, c4:]).astype(BF16), g2_ref[...], preferred_element_type=F32)
    kk = k * kk_ref[...]
    kk = kk * lax.rsqrt(_split_dot(kk * kk, e) + 1e-12)
    w_lora = jnp.dot(jnp.tanh(zs[:, c2:c3]).astype(BF16), w2_ref[...], preferred_element_type=F32)
    a_lora = jnp.dot(zs[:, c3:c4].astype(BF16), a2_ref[...], preferred_element_type=F32)
    bonus = jnp.zeros((tm, hw), F32)
    for d in range(2):
        x = w0_ref[d:d + 1, :] + w_lora[:, d * hw:(d + 1) * hw]
        softplus_neg = jnp.maximum(-x, 0.0) + jnp.log(1.0 + jnp.exp(-jnp.abs(x)))
        logw_o[d] = -jnp.exp(-softplus_neg - 0.5)
        a = jax.nn.sigmoid(a0_ref[d:d + 1, :] + a_lora[:, d * hw:(d + 1) * hw])
        kd = k * (1.0 + (a - 1.0) * ka_ref[...])
        kd_o[d] = kd
        bb_o[d] = kk * a
        bonus = bonus + _split_dot(r * kd * rk_ref[...], e) * v
    r_o[...] = r
    v_o[...] = v
    kk_o[...] = kk
    bonus_o[...] = bonus


def _rwkv_chunk_kernel(r_ref, v_ref, kk_ref, logw_ref, kd_ref, bb_ref, mn_ref, ry_ref):
    fwd = pl.program_id(1) == 0
    cn = RWKV_CHUNK
    hd = HEAD_DIM
    rowi = lax.broadcasted_iota(jnp.int32, (cn, cn), 0)
    coli = lax.broadcasted_iota(jnp.int32, (cn, cn), 1)
    lag = (rowi - coli) * jnp.where(fwd, 1, -1)
    before_incl = lag >= 0
    before = lag > 0
    eye = (rowi == coli).astype(F32)
    nn = (((1,), (0,)), ((), ()))
    nt = (((1,), (1,)), ((), ()))
    tn = (((0,), (0,)), ((), ()))

    a_t, r_t, b_t, k_t, b_h, k_h, g_end, vs, where = [], [], [], [], [], [], [], [], []
    for cc in range(RWKV_CHUNKS_PER_STEP):
        rs = slice(cc * cn, (cc + 1) * cn)
        logw = logw_ref[rs, :]
        lg = jnp.dot(before_incl.astype(F32), logw, precision=HI, preferred_element_type=F32)
        lg_tot = jnp.sum(logw, axis=0, keepdims=True)
        inv = jnp.exp(-lg)
        to_end = jnp.exp(lg_tot - lg)
        full = dict(a_t=-kk_ref[rs, :] * jnp.exp(lg - logw), r_t=r_ref[rs, :] * jnp.exp(lg),
                    b_t=bb_ref[rs, :] * inv, k_t=kd_ref[rs, :] * inv,
                    b_h=bb_ref[rs, :] * to_end, k_h=kd_ref[rs, :] * to_end, g_end=jnp.exp(lg_tot))
        for h in range(RWKV_HEADS):
            sl = slice(h * hd, (h + 1) * hd)
            a_t.append(full["a_t"][:, sl])
            r_t.append(full["r_t"][:, sl])
            b_t.append(full["b_t"][:, sl])
            k_t.append(full["k_t"][:, sl])
            b_h.append(full["b_h"][:, sl])
            k_h.append(full["k_h"][:, sl])
            g_end.append(full["g_end"][:, sl])
            vs.append(v_ref[rs, sl])
            where.append((cc, h))
    ps = [_pdot(jnp.concatenate([a, r], axis=0), jnp.concatenate([b, k], axis=0), nt, RWKV_PREC_PAIR)
          for a, r, b, k in zip(a_t, r_t, b_t, k_t)]
    n_ab = [jnp.where(before, p[:cn, :cn], 0.0) for p in ps]
    n_ak = [jnp.where(before, p[:cn, cn:], 0.0) for p in ps]
    n_rb = [jnp.where(before_incl, p[cn:, :cn], 0.0) for p in ps]
    n_rk = [jnp.where(before_incl, p[cn:, cn:], 0.0) for p in ps]
    tinv = [eye + n for n in n_ab]
    pw = n_ab
    for _ in range(int(math.log2(cn)) - 1):
        pw = [_pdot(x, x, nn, RWKV_PREC_INV) for x in pw]
        tinv = [t + _pdot(t, x, nn, RWKV_PREC_INV) for t, x in zip(tinv, pw)]
    akv = [_pdot(n, v, nn, RWKV_PREC_OUT) for n, v in zip(n_ak, vs)]
    rkv = [_pdot(n, v, nn, RWKV_PREC_OUT) for n, v in zip(n_rk, vs)]
    kv = [_pdot(k, v, tn, RWKV_PREC_OUT) for k, v in zip(k_h, vs)]
    zz = [_pdot(t, jnp.concatenate([a, w], axis=1), nn, RWKV_PREC_OUT)
          for t, a, w in zip(tinv, a_t, akv)]
    for u, (cc, h) in enumerate(where):
        ry_ref[cc, h] = (_pdot(n_rb[u], zz[u], nn, RWKV_PREC_OUT)
                         + jnp.concatenate([r_t[u], rkv[u]], axis=1))
        mn_ref[cc, h] = (_pdot(b_h[u], zz[u], tn, RWKV_PREC_OUT)
                         + jnp.concatenate([eye * g_end[u], kv[u]], axis=1))


def _rwkv_seq_kernel(mn_ref, ry_ref, y_ref, h_ref):
    @pl.when(pl.program_id(2) == 0)
    def _():
        h_ref[...] = jnp.zeros_like(h_ref)

    fwd = pl.program_id(1) == 0
    hd = HEAD_DIM
    for cc in range(RWKV_SUPER):
        c = jnp.where(fwd, cc, RWKV_SUPER - 1 - cc)
        row0 = pl.multiple_of(c * RWKV_CHUNK, RWKV_CHUNK)
        for h in range(RWKV_HEADS):
            mn = mn_ref[c, h]
            ry = ry_ref[c, h]
            state = h_ref[h]
            nn = (((1,), (0,)), ((), ()))
            y_ref[pl.ds(row0, RWKV_CHUNK), h * hd:(h + 1) * hd] = (
                _pdot(ry[:, :hd], state, nn, RWKV_PREC_STATE) + ry[:, hd:])
            h_ref[h] = _pdot(mn[:, :hd], state, nn, RWKV_PREC_STATE) + mn[:, hd:]


def _rwkv_post_kernel(y0_ref, y1_ref, bonus_ref, g_ref, gnw_ref, gnb_ref, e_ref, o_ref):
    y = y0_ref[...] + y1_ref[...]
    e = e_ref[...]
    mu = _split_dot(y, e) * (1.0 / HEAD_DIM)
    yc = y - mu
    var = _split_dot(yc * yc, e) * (1.0 / HEAD_DIM)
    yn = yc * lax.rsqrt(var + GN_EPS) * gnw_ref[...] + gnb_ref[...]
    o_ref[...] = (yn + bonus_ref[...]) * g_ref[...]


def _block_diag2(w):
    z = jnp.zeros_like(w[0])
    return jnp.concatenate([jnp.concatenate([w[0], z], 1), jnp.concatenate([z, w[1]], 1)], 0)


def _rwkv_mixer(z, shift_mu, w0, w2, a0, a2, g2, k_k, k_a, r_k, gn_w, gn_b):
    bsz, s, zw = z.shape
    hw = RWKV_WIDTH
    tm = ROW_BLOCK
    nblk = s // tm
    head_of = np.arange(hw) // HEAD_DIM
    e = jnp.asarray(head_of[:, None] == head_of[None, :], BF16)
    tok = pl.BlockSpec((None, tm, hw), lambda b, i: (b, i, 0))
    tok2 = pl.BlockSpec((None, 2, tm, hw), lambda b, i: (b, 0, i, 0))
    halo = tm // 8

    def full(shape):
        return pl.BlockSpec(shape, lambda b, i: (0,) * len(shape))

    r, v, kk, g, bonus, logw, kd, bb = pl.pallas_call(
        _rwkv_prep_kernel,
        grid=(bsz, nblk),
        in_specs=[pl.BlockSpec((None, tm, zw), lambda b, i: (b, i, 0)),
                  pl.BlockSpec((None, 8, zw), lambda b, i: (b, jnp.maximum(i * halo - 1, 0), 0)),
                  pl.BlockSpec((None, 8, zw), lambda b, i: (b, jnp.minimum((i + 1) * halo, s // 8 - 1), 0)),
                  full((2, RWKV_IN)), full((2, hw)), full((2 * RWKV_LORA_W, 2 * hw)), full((2, hw)),
                  full((2 * RWKV_LORA_A, 2 * hw)), full((RWKV_LORA_G, hw)), full((1, hw)), full((1, hw)),
                  full((1, hw)), full((hw, hw))],
        out_specs=[tok, tok, tok, tok, tok, tok2, tok2, tok2],
        out_shape=[jax.ShapeDtypeStruct((bsz, s, hw), F32)] * 5 + [jax.ShapeDtypeStruct((bsz, 2, s, hw), F32)] * 3,
        compiler_params=_cparams("parallel", "parallel"),
        name="rwkv_prep",
    )(z, z, z, shift_mu, w0, _block_diag2(w2).astype(BF16), a0, _block_diag2(a2).astype(BF16), g2.astype(BF16),
      k_k.reshape(1, hw), k_a.reshape(1, hw), r_k.reshape(1, hw), e)

    nch = s // RWKV_CHUNK
    step_rows = RWKV_CHUNKS_PER_STEP * RWKV_CHUNK
    ctok = pl.BlockSpec((None, step_rows, hw), lambda b, d, c: (b, c, 0))
    ctok2 = pl.BlockSpec((None, None, step_rows, hw), lambda b, d, c: (b, d, c, 0))
    mat_shape = jax.ShapeDtypeStruct((bsz, 2, nch, RWKV_HEADS, RWKV_CHUNK, 2 * HEAD_DIM), F32)
    mat_spec = pl.BlockSpec((None, None, RWKV_CHUNKS_PER_STEP, RWKV_HEADS, RWKV_CHUNK, 2 * HEAD_DIM),
                            lambda b, d, c: (b, d, c, 0, 0, 0))
    mn, ry = pl.pallas_call(
        _rwkv_chunk_kernel,
        grid=(bsz, 2, nch // RWKV_CHUNKS_PER_STEP),
        in_specs=[ctok, ctok, ctok, ctok2, ctok2, ctok2],
        out_specs=[mat_spec, mat_spec],
        out_shape=[mat_shape, mat_shape],
        compiler_params=_cparams("parallel", "parallel", "parallel"),
        name="rwkv_chunk",
    )(r, v, kk, logw, kd, bb)

    nsb = nch // RWKV_SUPER

    def sb_of(d, j):
        return jnp.where(d == 0, j, jnp.where(j == 0, 0, nsb - j))

    sup_shape = (bsz, 2, nsb, RWKV_SUPER, RWKV_HEADS, RWKV_CHUNK, 2 * HEAD_DIM)
    sup_spec = pl.BlockSpec((None, None, None, RWKV_SUPER, RWKV_HEADS, RWKV_CHUNK, 2 * HEAD_DIM),
                            lambda b, d, j: (b, d, sb_of(d, j), 0, 0, 0, 0))
    y = pl.pallas_call(
        _rwkv_seq_kernel,
        grid=(bsz, 2, nsb),
        in_specs=[sup_spec, sup_spec],
        out_specs=pl.BlockSpec((None, None, tm, hw), lambda b, d, j: (b, d, sb_of(d, j), 0)),
        out_shape=jax.ShapeDtypeStruct((bsz, 2, s, hw), F32),
        scratch_shapes=[pltpu.VMEM((RWKV_HEADS, HEAD_DIM, HEAD_DIM), F32)],
        compiler_params=_cparams("parallel", "parallel", "arbitrary"),
        name="rwkv_seq",
    )(mn.reshape(sup_shape), ry.reshape(sup_shape))

    vec = pl.BlockSpec((1, hw), lambda b, i: (0, 0))
    return pl.pallas_call(
        _rwkv_post_kernel,
        grid=(bsz, nblk),
        in_specs=[pl.BlockSpec((None, None, tm, hw), lambda b, i: (b, 0, i, 0)),
                  pl.BlockSpec((None, None, tm, hw), lambda b, i: (b, 1, i, 0)),
                  tok, tok, vec, vec, pl.BlockSpec((hw, hw), lambda b, i: (0, 0))],
        out_specs=tok,
        out_shape=jax.ShapeDtypeStruct((bsz, s, hw), F32),
        compiler_params=_cparams("parallel", "parallel"),
        name="rwkv_post",
    )(y, y, bonus, g, gn_w.reshape(1, hw), gn_b.reshape(1, hw), e)


def kernel(x, c, ctx, c_ctx,
           l0_w_mod, l0_b_mod, l0_w_in, l0_s5_lam_re, l0_s5_lam_im, l0_s5_log_dt, l0_s5_b_re, l0_s5_b_im,
           l0_s5_c_re, l0_s5_c_im, l0_s5_d, l0_w_glu, l0_b_glu, l0_rwkv_shift, l0_rwkv_w0, l0_rwkv_w2,
           l0_rwkv_a0, l0_rwkv_a2, l0_rwkv_g2, l0_rwkv_k_k, l0_rwkv_k_a, l0_rwkv_r_k, l0_rwkv_gn_w,
           l0_rwkv_gn_b, l0_w_out, l0_ln1_g, l0_ln1_b, l0_peer_wq, l0_peer_keys, l0_peer_u, l0_peer_v,
           l0_ln2_g, l0_ln2_b,
           l1_w_mod, l1_b_mod, l1_w_in, l1_sink, l1_rpb, l1_w_out, l1_ln1_g, l1_ln1_b, l1_peer_wq,
           l1_peer_keys, l1_peer_u, l1_peer_v, l1_ln2_g, l1_ln2_b):
    L = ctx.shape[1]
    h = jnp.concatenate([ctx, x], 1)

    tab = _mod_tables(c, c_ctx, l0_w_mod, l0_b_mod)
    z = _inproj(h, tab, 0, 1, l0_w_in.astype(BF16))
    ya = _s5_mixer(z[..., :S5_WIDTH], L, l0_s5_lam_re, l0_s5_lam_im, l0_s5_log_dt, l0_s5_b_re, l0_s5_b_im,
                   l0_s5_c_re, l0_s5_c_im, l0_s5_d, l0_w_glu, l0_b_glu)
    yb = _rwkv_mixer(z, l0_rwkv_shift, l0_rwkv_w0, l0_rwkv_w2, l0_rwkv_a0, l0_rwkv_a2,
                     l0_rwkv_g2, l0_rwkv_k_k, l0_rwkv_k_a, l0_rwkv_r_k, l0_rwkv_gn_w, l0_rwkv_gn_b)
    h = _outproj_ln(ya, yb, h, tab, 2, l0_w_out, l0_ln1_g, l0_ln1_b, 0)
    h = _peer_ln(h, tab, 0, l0_peer_wq, l0_peer_keys, l0_peer_u, l0_peer_v, l0_ln2_g, l0_ln2_b)

    tab = _mod_tables(c, c_ctx, l1_w_mod, l1_b_mod)
    qc, kc, vc, qd, kd, vd = _inproj_attn(h, tab, 0, 1, l1_w_in.astype(BF16), L)
    oc = _window_gqa(qc, kc, vc, l1_sink, L)
    od = _neighbourhood_attention(qd, kd, vd, l1_rpb, L)
    ctx_blocks = L // ROW_BLOCK
    hx = _outproj_ln(oc, od, h, tab, 2, l1_w_out, l1_ln1_g, l1_ln1_b, ctx_blocks)
    return _peer_ln(hx, tab, ctx_blocks, l1_peer_wq, l1_peer_keys, l1_peer_u, l1_peer_v, l1_ln2_g, l1_ln2_b)
```

```python
import functools
import math

import jax
import jax.numpy as jnp
import numpy as np
from jax import lax
from jax.experimental import pallas as pl
from jax.experimental.pallas import tpu as pltpu

F32 = jnp.float32
BF16 = jnp.bfloat16

D_MODEL = 1024
DEPTH = 2
GRID_W = 64
CTX_LEN = 256
HEAD_DIM = 64
S5_GROUP_CH = 16
S5_GROUPS = 16
S5_WIDTH = S5_GROUPS * S5_GROUP_CH
S5_STATE = 64
RWKV_HEADS = 8
RWKV_WIDTH = RWKV_HEADS * HEAD_DIM
RWKV_LORA_W = 64
RWKV_LORA_A = 64
RWKV_LORA_G = 128
RWKV_IN = 3 * RWKV_WIDTH + 2 * RWKV_LORA_W + 2 * RWKV_LORA_A + RWKV_LORA_G
RWKV_SPLITS = (RWKV_WIDTH, 2 * RWKV_WIDTH, 3 * RWKV_WIDTH,
               3 * RWKV_WIDTH + 2 * RWKV_LORA_W,
               3 * RWKV_WIDTH + 2 * RWKV_LORA_W + 2 * RWKV_LORA_A)
WIN_HEADS = 8
WIN_KV_HEADS = 2
WINDOW = 128
WIN_BLOCK = 128
NA_HEADS = 8
NA_ROWS = 8
NA_COLS = 16
ODD_SPLITS = (WIN_HEADS * HEAD_DIM, (WIN_HEADS + WIN_KV_HEADS) * HEAD_DIM,
              (WIN_HEADS + 2 * WIN_KV_HEADS) * HEAD_DIM,
              (WIN_HEADS + 2 * WIN_KV_HEADS + NA_HEADS) * HEAD_DIM,
              (WIN_HEADS + 2 * WIN_KV_HEADS + 2 * NA_HEADS) * HEAD_DIM)
ROPE_BASE = 10000.0
PEER_HEADS = 8
PEER_KEYS = 128
PEER_DK = 128
PEER_TOPK = 16
ALPHA = (2.0 * DEPTH) ** 0.25
LN_EPS = 1e-5
GN_EPS = 64e-5

VMEM_LIMIT_BYTES = 56 * 1024 * 1024
LANES = 128
SUBLANES = 8
ROW_BLOCK = 256


def _cparams(*sem):
    return pltpu.CompilerParams(dimension_semantics=sem, vmem_limit_bytes=VMEM_LIMIT_BYTES)


def _modulation_kernel(c_ref, w_ref, b_ref, o_ref):
    c = c_ref[...]
    s = c * jax.nn.sigmoid(c)
    o_ref[...] = jnp.dot(s.astype(BF16), w_ref[...].astype(BF16), preferred_element_type=F32) + b_ref[...]


def _modulation(cond, w_mod, b_mod):
    n, d = cond.shape
    nout = w_mod.shape[1]
    tn = 1536
    return pl.pallas_call(
        _modulation_kernel,
        grid=(nout // tn,),
        in_specs=[pl.BlockSpec((n, d), lambda j: (0, 0)),
                  pl.BlockSpec((d, tn), lambda j: (0, j)),
                  pl.BlockSpec((1, tn), lambda j: (0, j))],
        out_specs=pl.BlockSpec((n, tn), lambda j: (0, j)),
        out_shape=jax.ShapeDtypeStruct((n, nout), F32),
        compiler_params=_cparams("arbitrary"),
        name="modulation",
    )(cond, w_mod, b_mod.reshape(1, nout))


def _mod_tables(c, c_ctx, w_mod, b_mod):
    bsz = c.shape[0]
    cond = jnp.concatenate([c, c_ctx[None, :], jnp.zeros((SUBLANES - bsz - 1, c.shape[1]), F32)], 0)
    m = _modulation(cond, w_mod, b_mod)
    mx = m[:bsz].reshape(bsz, 1, 6, 1, D_MODEL)
    mc = jnp.broadcast_to(m[bsz].reshape(1, 1, 6, 1, D_MODEL), (bsz, 1, 6, 1, D_MODEL))
    return jnp.concatenate([mc, mx], 1)


def _tab_spec(which):
    return pl.BlockSpec((None, None, None, 1, D_MODEL),
                        lambda b, i: (b, jnp.minimum(i, 1), which, 0, 0))


def _inproj_kernel(x_ref, shift_ref, scale_ref, w_ref, o_ref):
    hm = x_ref[...] * (1.0 + scale_ref[...]) + shift_ref[...]
    o_ref[...] = jnp.dot(hm.astype(BF16), w_ref[...], preferred_element_type=F32)


def _inproj(h, tab, which_shift, which_scale, w_bf16):
    bsz, s, d = h.shape
    n = w_bf16.shape[1]
    return pl.pallas_call(
        _inproj_kernel,
        grid=(bsz, s // ROW_BLOCK),
        in_specs=[pl.BlockSpec((None, ROW_BLOCK, d), lambda b, i: (b, i, 0)),
                  _tab_spec(which_shift), _tab_spec(which_scale),
                  pl.BlockSpec((d, n), lambda b, i: (0, 0))],
        out_specs=pl.BlockSpec((None, ROW_BLOCK, n), lambda b, i: (b, i, 0)),
        out_shape=jax.ShapeDtypeStruct((bsz, s, n), F32),
        compiler_params=_cparams("parallel", "parallel"),
        name="inproj",
    )(h, tab, tab, w_bf16)


def _layer_norm_rows(z, g, b):
    mu = jnp.mean(z, -1, keepdims=True)
    zc = z - mu
    var = jnp.mean(zc * zc, -1, keepdims=True)
    return zc * lax.rsqrt(var + LN_EPS) * g + b


def _outproj_ln_kernel(ya_ref, yb_ref, h_ref, gate_ref, wa_ref, wb_ref, g_ref, b_ref, o_ref):
    proj = (jnp.dot(ya_ref[...].astype(BF16), wa_ref[...], preferred_element_type=F32)
            + jnp.dot(yb_ref[...].astype(BF16), wb_ref[...], preferred_element_type=F32))
    z = ALPHA * h_ref[...] + gate_ref[...] * proj
    o_ref[...] = _layer_norm_rows(z, g_ref[...], b_ref[...])


def _outproj_ln(ya, yb, h, tab, which_gate, w_out, ln_g, ln_b, row_off):
    bsz, rows, ka = ya.shape
    kb = yb.shape[-1]
    d = h.shape[-1]
    return pl.pallas_call(
        _outproj_ln_kernel,
        grid=(bsz, rows // ROW_BLOCK),
        in_specs=[pl.BlockSpec((None, ROW_BLOCK, ka), lambda b, i: (b, i, 0)),
                  pl.BlockSpec((None, ROW_BLOCK, kb), lambda b, i: (b, i, 0)),
                  pl.BlockSpec((None, ROW_BLOCK, d), lambda b, i: (b, i + row_off, 0)),
                  pl.BlockSpec((None, None, None, 1, d), lambda b, i: (b, jnp.minimum(i + row_off, 1), which_gate, 0, 0)),
                  pl.BlockSpec((ka, d), lambda b, i: (0, 0)),
                  pl.BlockSpec((kb, d), lambda b, i: (0, 0)),
                  pl.BlockSpec((1, d), lambda b, i: (0, 0)),
                  pl.BlockSpec((1, d), lambda b, i: (0, 0))],
        out_specs=pl.BlockSpec((None, ROW_BLOCK, d), lambda b, i: (b, i, 0)),
        out_shape=jax.ShapeDtypeStruct((bsz, rows, d), F32),
        compiler_params=_cparams("parallel", "parallel"),
        name="outproj_ln",
    )(ya, yb, h, tab, w_out[:ka].astype(BF16), w_out[ka:].astype(BF16), ln_g.reshape(1, d), ln_b.reshape(1, d))


PEER_NEG = -3.0e38
PEER_ETILE = 2048
PEER_SUBTILE = 256
SQRT_HALF = 0.7071067811865476


PEER_ROUTE_GROUP = 4


def _top_values(arrays, n):
    arrays = list(arrays)
    vals = [[] for _ in arrays]
    for _ in range(n):
        ms = [jnp.max(s, axis=0, keepdims=True) for s in arrays]
        for i, m in enumerate(ms):
            vals[i].append(m)
        arrays = [jnp.where(s == m, PEER_NEG, s) for s, m in zip(arrays, ms)]
    return vals


def _oddeven_merge_sort_network(n):
    pairs = []
    p = 1
    while p < n:
        k = p
        while k >= 1:
            for j in range(k % p, n - k, 2 * k):
                for i in range(min(k, n - j - k)):
                    if (i + j) // (2 * p) == (i + j + k) // (2 * p):
                        pairs.append((i + j, i + j + k))
            k //= 2
        p *= 2
    return pairs


def _top_values_sorted(s, n):
    sub = SUBLANES
    groups = s.shape[0] // sub
    x = s.reshape(groups, sub, s.shape[1])
    cols = [x[g] for g in range(groups)]
    for lo, hi in _oddeven_merge_sort_network(groups):
        a, b = cols[lo], cols[hi]
        cols[lo], cols[hi] = jnp.maximum(a, b), jnp.minimum(a, b)
    assert groups >= n
    vals = []
    for k in range(n):
        m = jnp.max(cols[0], axis=0, keepdims=True)
        vals.append(m)
        hit = cols[0] == m
        for g in range(n - k - 1):
            cols[g] = jnp.where(hit, cols[g + 1], cols[g])
    return vals


def _peer_route_kernel(h_ref, shift_ref, scale_ref, wq_ref, keys_ref,
                       pin_ref, cnt_ref, e1_ref, r2_ref, e2_ref):
    pin = (h_ref[...] * (1.0 + scale_ref[...]) + shift_ref[...]).astype(BF16)
    pin_ref[...] = pin
    q = jnp.dot(pin, wq_ref[...], preferred_element_type=F32).astype(BF16)
    st = lax.dot_general(keys_ref[...], q, (((1,), (1,)), ((), ())), preferred_element_type=F32)
    for h0 in range(0, PEER_HEADS, PEER_ROUTE_GROUP):
        hds = list(range(h0, h0 + PEER_ROUTE_GROUP))
        s1s = [st[hd * 2 * PEER_KEYS:hd * 2 * PEER_KEYS + PEER_KEYS] for hd in hds]
        s2s = [st[hd * 2 * PEER_KEYS + PEER_KEYS:(hd + 1) * 2 * PEER_KEYS] for hd in hds]
        v1s = [_top_values_sorted(s1, PEER_TOPK) for s1 in s1s]
        v2s = [_top_values_sorted(s2, PEER_TOPK) for s2 in s2s]
        rank2s = []
        for s2, v2 in zip(s2s, v2s):
            rank = jnp.zeros_like(s2)
            for b in range(PEER_TOPK):
                rank = rank + jnp.where(v2[b] > s2, 1.0, 0.0)
            rank2s.append(rank)
        frontiers = []
        for v1, v2 in zip(v1s, v2s):
            v2all = jnp.concatenate(v2, axis=0)
            frontiers.append(jnp.concatenate([v1[a] + v2all[:(PEER_TOPK + 1) // (a + 1)] for a in range(PEER_TOPK)],
                                             axis=0))
        tops = _top_values(frontiers, PEER_TOPK + 1)
        for hd, s1, s2, v1, v2, rank2, top in zip(hds, s1s, s2s, v1s, v2s, rank2s, tops):
            tau = 0.5 * (top[PEER_TOPK - 1] + top[PEER_TOPK])
            zsum = top[0] * 0.0
            for kk in range(PEER_TOPK):
                zsum = zsum + jnp.exp(top[kk] - top[0])
            need = tau - s1
            cnt = jnp.zeros_like(s1)
            for b in range(PEER_TOPK):
                cnt = cnt + jnp.where(v2[b] >= need, 1.0, 0.0)
            cnt_ref[hd] = jnp.where(s1 >= v1[PEER_TOPK - 1], cnt, 0.0)
            e1_ref[hd] = jnp.exp(s1 - v1[0])
            r2_ref[hd] = rank2.astype(BF16)
            e2_ref[hd] = (jnp.exp(s2 - v2[0]) / zsum).astype(BF16)


def _peer_dense_kernel(pin_ref, cnt_ref, e1_ref, r2_ref, e2_ref, u_ref, vt_ref,
                       h_ref, gate_ref, g_ref, b_ref, o_ref, acc_ref, w_ref):
    j = pl.program_id(2)

    last = pl.num_programs(2) - 1

    def apply_values(first):
        part = jnp.dot(vt_ref[...], w_ref[(j + 1) % 2], preferred_element_type=F32)
        if first:
            acc_ref[...] = part
        else:
            acc_ref[...] += part

    def score_and_gate():
        pin = pin_ref[...]
        cur = j % 2
        for sb in range(PEER_ETILE // PEER_SUBTILE):
            es = slice(sb * PEER_SUBTILE, (sb + 1) * PEER_SUBTILE)
            at = lax.dot_general(u_ref[es, :], pin, (((1,), (1,)), ((), ())), preferred_element_type=F32)
            for i2 in range(PEER_SUBTILE // PEER_KEYS):
                ii = sb * (PEER_SUBTILE // PEER_KEYS) + i2
                a = at[i2 * PEER_KEYS:(i2 + 1) * PEER_KEYS]
                act = (0.5 * a * (1.0 + lax.erf(a * SQRT_HALF))).astype(BF16)
                g = jnp.zeros(a.shape, BF16)
                for hd in range(PEER_HEADS):
                    cnt_row = cnt_ref[hd, pl.ds(ii, 1), :].astype(BF16)
                    c_row = e1_ref[hd, pl.ds(ii, 1), :].astype(BF16)
                    g = g + jnp.where(r2_ref[hd] < cnt_row, e2_ref[hd] * c_row, jnp.zeros((), BF16))
                w_ref[cur, ii * PEER_KEYS:(ii + 1) * PEER_KEYS, :] = g * act

    @pl.when(j == 0)
    def _():
        score_and_gate()

    @pl.when(j == 1)
    def _():
        apply_values(first=True)
        score_and_gate()

    @pl.when((j > 1) & (j < last))
    def _():
        apply_values(first=False)
        score_and_gate()

    @pl.when(j == last)
    def _():
        apply_values(first=False)
        z = ALPHA * h_ref[...] + gate_ref[...] * acc_ref[...].T
        o_ref[...] = _layer_norm_rows(z, g_ref[...], b_ref[...])


def _peer_block_diag_keys(sub_keys):
    hh, two, nk, dh = sub_keys.shape
    eye = jnp.eye(hh * two, dtype=F32)
    kb = sub_keys.reshape(hh * two, nk, dh)
    return jnp.einsum('akd,ab->akbd', kb, eye).reshape(hh * two * nk, hh * two * dh)


def _peer_ln(h, tab, seg_off, wq, sub_keys, u_tab, v_tab, ln_g, ln_b):
    bsz, s, d = h.shape
    tm = ROW_BLOCK
    nblk = s // tm
    hk = PEER_HEADS * 2 * PEER_KEYS
    keys_bd = _peer_block_diag_keys(sub_keys).astype(BF16)

    def tab_spec3(which):
        return pl.BlockSpec((None, None, None, 1, d),
                            lambda b, i, *_: (b, jnp.minimum(i + seg_off, 1), which, 0, 0))

    route_dims = (bsz, nblk, PEER_HEADS, PEER_KEYS, tm)
    route_spec = pl.BlockSpec((None, None, PEER_HEADS, PEER_KEYS, tm), lambda b, i: (b, i, 0, 0, 0))
    pin, cnt, e1, r2, e2 = pl.pallas_call(
        _peer_route_kernel,
        grid=(bsz, nblk),
        in_specs=[pl.BlockSpec((None, tm, d), lambda b, i: (b, i, 0)),
                  tab_spec3(3), tab_spec3(4),
                  pl.BlockSpec((d, PEER_HEADS * PEER_DK), lambda b, i: (0, 0)),
                  pl.BlockSpec((hk, PEER_HEADS * PEER_DK), lambda b, i: (0, 0))],
        out_specs=[pl.BlockSpec((None, tm, d), lambda b, i: (b, i, 0)),
                   route_spec, route_spec, route_spec, route_spec],
        out_shape=[jax.ShapeDtypeStruct((bsz, nblk * tm, d), BF16),
                   jax.ShapeDtypeStruct(route_dims, F32), jax.ShapeDtypeStruct(route_dims, F32),
                   jax.ShapeDtypeStruct(route_dims, BF16), jax.ShapeDtypeStruct(route_dims, BF16)],
        compiler_params=_cparams("parallel", "parallel"),
        name="peer_route",
    )(h, tab, tab, wq.astype(BF16), keys_bd)

    n_exp = u_tab.shape[0]
    rows_per_step = PEER_ETILE // PEER_KEYS
    full_spec = pl.BlockSpec((None, None, PEER_HEADS, PEER_KEYS, tm), lambda b, i, j: (b, i, 0, 0, 0))
    n_tiles = n_exp // PEER_ETILE
    row_spec = pl.BlockSpec((None, None, PEER_HEADS, rows_per_step, tm),
                            lambda b, i, j: (b, i, 0, jnp.minimum(j, n_tiles - 1), 0))
    return pl.pallas_call(
        _peer_dense_kernel,
        grid=(bsz, nblk, n_tiles + 1),
        in_specs=[pl.BlockSpec((None, tm, d), lambda b, i, j: (b, i, 0)),
                  row_spec, row_spec, full_spec, full_spec,
                  pl.BlockSpec((PEER_ETILE, d), lambda b, i, j: (jnp.minimum(j, n_tiles - 1), 0)),
                  pl.BlockSpec((d, PEER_ETILE), lambda b, i, j: (0, jnp.maximum(j - 1, 0))),
                  pl.BlockSpec((None, tm, d), lambda b, i, j: (b, i, 0)),
                  tab_spec3(5),
                  pl.BlockSpec((1, d), lambda b, i, j: (0, 0)),
                  pl.BlockSpec((1, d), lambda b, i, j: (0, 0))],
        out_specs=pl.BlockSpec((None, tm, d), lambda b, i, j: (b, i, 0)),
        out_shape=jax.ShapeDtypeStruct((bsz, nblk * tm, d), F32),
        scratch_shapes=[pltpu.VMEM((d, tm), F32), pltpu.VMEM((2, PEER_ETILE, tm), BF16)],
        compiler_params=_cparams("parallel", "parallel", "arbitrary"),
        name="peer_dense",
    )(pin, cnt, e1, r2, e2, u_tab.astype(BF16), v_tab.T.astype(BF16), h, tab,
      ln_g.reshape(1, d), ln_b.reshape(1, d))


S5_CHAINS = 8
S5_TCHUNK = 128
S5_NSTATE = S5_GROUPS * S5_STATE


def _s5_scan_kernel(uf_ref, ub_ref, wb_ref, wc_ref, a_ref, yf_ref, yb_ref,
                    fre_ref, fim_ref, bre_ref, bim_ref, h_ref):
    @pl.when(pl.program_id(0) == 0)
    def _():
        h_ref[...] = jnp.zeros_like(h_ref)

    rows = uf_ref.shape[0]
    steps = rows // S5_CHAINS
    uf = uf_ref[...].astype(BF16)
    ub = ub_ref[...].astype(BF16)
    fre_ref[...] = jnp.dot(uf, wb_ref[0], preferred_element_type=F32)
    fim_ref[...] = jnp.dot(uf, wb_ref[1], preferred_element_type=F32)
    bre_ref[...] = jnp.dot(ub, wb_ref[2], preferred_element_type=F32)
    bim_ref[...] = jnp.dot(ub, wb_ref[3], preferred_element_type=F32)
    a_re = a_ref[0]
    a_im = a_ref[1]
    is_fwd = lax.broadcasted_iota(jnp.int32, (S5_CHAINS, S5_NSTATE), 0) < (S5_CHAINS // 2)

    def step(t, carry):
        h_re, h_im = carry
        of = pl.multiple_of(t * S5_CHAINS, S5_CHAINS)
        ob = pl.multiple_of((steps - 1 - t) * S5_CHAINS, S5_CHAINS)
        n_re = a_re * h_re - a_im * h_im + jnp.where(is_fwd, fre_ref[pl.ds(of, S5_CHAINS), :],
                                                     bre_ref[pl.ds(ob, S5_CHAINS), :])
        n_im = a_re * h_im + a_im * h_re + jnp.where(is_fwd, fim_ref[pl.ds(of, S5_CHAINS), :],
                                                     bim_ref[pl.ds(ob, S5_CHAINS), :])
        fre_ref[pl.ds(of, S5_CHAINS), :] = n_re
        fim_ref[pl.ds(of, S5_CHAINS), :] = n_im
        bre_ref[pl.ds(ob, S5_CHAINS), :] = n_re
        bim_ref[pl.ds(ob, S5_CHAINS), :] = n_im
        return n_re, n_im

    h_re, h_im = lax.fori_loop(0, steps, step, (h_ref[0], h_ref[1]), unroll=2)
    h_ref[0] = h_re
    h_ref[1] = h_im
    yf_ref[...] = (jnp.dot(fre_ref[...].astype(BF16), wc_ref[0], preferred_element_type=F32)
                   - jnp.dot(fim_ref[...].astype(BF16), wc_ref[1], preferred_element_type=F32))
    yb_ref[...] = (jnp.dot(bre_ref[...].astype(BF16), wc_ref[2], preferred_element_type=F32)
                   - jnp.dot(bim_ref[...].astype(BF16), wc_ref[3], preferred_element_type=F32))


def _s5_glu_kernel(u_ref, y0_ref, y1_ref, d_ref, w_ref, b_ref, o_ref):
    y = u_ref[...] * d_ref[...] + y0_ref[...] + y1_ref[...]
    y = 0.5 * y * (1.0 + lax.erf(y * SQRT_HALF))
    gate = jnp.dot(y.astype(BF16), w_ref[...], preferred_element_type=F32) + b_ref[...]
    o_ref[...] = y * jax.nn.sigmoid(gate)


def _s5_params(lam_re, lam_im, log_dt, b_re, b_im, c_re, c_im):
    eye = jnp.eye(S5_GROUPS, dtype=F32)
    wb, wc, a_rows = [], [], [[], []]
    for d in range(2):
        lr, li = lam_re[d], lam_im[d]
        dt = jnp.exp(log_dt[d])[:, None]
        mag = jnp.exp(lr * dt)
        ab_re, ab_im = mag * jnp.cos(li * dt), mag * jnp.sin(li * dt)
        den = lr * lr + li * li
        nr = ab_re - 1.0
        coef_re = (nr * lr + ab_im * li) / den
        coef_im = (ab_im * lr - nr * li) / den
        bb_re = coef_re[..., None] * b_re[d] - coef_im[..., None] * b_im[d]
        bb_im = coef_re[..., None] * b_im[d] + coef_im[..., None] * b_re[d]
        for bb in (bb_re, bb_im):
            wb.append(jnp.einsum('gpi,gh->gihp', bb, eye).reshape(S5_WIDTH, S5_NSTATE))
        for cc in (c_re[d], c_im[d]):
            wc.append(jnp.einsum('gip,gh->gphi', cc, eye).reshape(S5_NSTATE, S5_WIDTH))
        for k, ab in enumerate((ab_re, ab_im)):
            a_rows[k].append(jnp.broadcast_to(ab.reshape(1, S5_NSTATE), (S5_CHAINS // 2, S5_NSTATE)))
    a_tiles = jnp.stack([jnp.concatenate(a_rows[0], 0), jnp.concatenate(a_rows[1], 0)])
    return jnp.stack(wb).astype(BF16), jnp.stack(wc).astype(BF16), a_tiles


def _s5_mixer(u, n_ctx, lam_re, lam_im, log_dt, b_re, b_im, c_re, c_im, d_skip, w_glu, b_glu):
    bsz, s, w = u.shape
    assert 2 * bsz == S5_CHAINS
    wb, wc, a_tiles = _s5_params(lam_re, lam_im, log_dt, b_re, b_im, c_re, c_im)
    ut = jnp.transpose(u, (1, 0, 2))
    chain = jnp.concatenate([ut, ut], 1).reshape(s * S5_CHAINS, w)
    rows = S5_TCHUNK * S5_CHAINS
    n_chunks = s // S5_TCHUNK
    ctx_chunks = n_ctx // S5_TCHUNK

    def mirror(i):
        return jnp.where(i < ctx_chunks, ctx_chunks - 1 - i, n_chunks - 1 - (i - ctx_chunks))

    state_buf = pltpu.VMEM((rows, S5_NSTATE), F32)
    y_shape = jax.ShapeDtypeStruct((s * S5_CHAINS, w), F32)
    yf, yb = pl.pallas_call(
        _s5_scan_kernel,
        grid=(n_chunks,),
        in_specs=[pl.BlockSpec((rows, w), lambda i: (i, 0)),
                  pl.BlockSpec((rows, w), lambda i: (mirror(i), 0)),
                  pl.BlockSpec((4, w, S5_NSTATE), lambda i: (0, 0, 0)),
                  pl.BlockSpec((4, S5_NSTATE, w), lambda i: (0, 0, 0)),
                  pl.BlockSpec((2, S5_CHAINS, S5_NSTATE), lambda i: (0, 0, 0))],
        out_specs=[pl.BlockSpec((rows, w), lambda i: (i, 0)),
                   pl.BlockSpec((rows, w), lambda i: (mirror(i), 0))],
        out_shape=[y_shape, y_shape],
        scratch_shapes=[state_buf, state_buf, state_buf, state_buf,
                        pltpu.VMEM((2, S5_CHAINS, S5_NSTATE), F32)],
        compiler_params=_cparams("arbitrary"),
        name="s5_scan",
    )(chain, chain, wb, wc, a_tiles)
    y_fwd = jnp.transpose(yf.reshape(s, S5_CHAINS, w)[:, :bsz], (1, 0, 2))
    y_bwd = jnp.transpose(yb.reshape(s, S5_CHAINS, w)[:, bsz:], (1, 0, 2))
    tok = pl.BlockSpec((None, ROW_BLOCK, w), lambda b, i: (b, i, 0))
    vec = pl.BlockSpec((1, w), lambda b, i: (0, 0))
    return pl.pallas_call(
        _s5_glu_kernel,
        grid=(bsz, s // ROW_BLOCK),
        in_specs=[tok, tok, tok, vec, pl.BlockSpec((w, w), lambda b, i: (0, 0)), vec],
        out_specs=tok,
        out_shape=jax.ShapeDtypeStruct((bsz, s, w), F32),
        compiler_params=_cparams("parallel", "parallel"),
        name="s5_glu",
    )(u, y_fwd, y_bwd, d_skip.reshape(1, w), w_glu.astype(BF16), b_glu.reshape(1, w))


ATT_SCALE = HEAD_DIM ** -0.5
ATT_NEG = -1e30
ROPE_QUARTER = HEAD_DIM // 4
QC_W = WIN_HEADS * HEAD_DIM
KC_W = WIN_KV_HEADS * HEAD_DIM
QD_W = NA_HEADS * HEAD_DIM


def _rope_tables(n_ctx, t):
    pos = np.arange(t)
    rowcol = np.stack([pos // GRID_W, pos % GRID_W], 1).astype(np.float32)
    lane = np.arange(QC_W)
    j = lane % HEAD_DIM
    freqs = (ROPE_BASE ** (-np.arange(ROPE_QUARTER, dtype=np.float32) / ROPE_QUARTER)).astype(np.float32)
    ang = rowcol[:, j // (HEAD_DIM // 2)] * freqs[j % ROPE_QUARTER][None, :]
    ang = jnp.asarray(ang, F32)
    sign = np.where((j % (HEAD_DIM // 2)) < ROPE_QUARTER, -1.0, 1.0).astype(np.float32)
    cos = jnp.concatenate([jnp.ones((n_ctx, QC_W), F32), jnp.cos(ang)], 0)
    sin = jnp.concatenate([jnp.zeros((n_ctx, QC_W), F32), jnp.sin(ang) * sign[None, :]], 0)
    return cos, sin


def _rope(x, cos, sin):
    n = x.shape[-1]
    lane = lax.broadcasted_iota(jnp.int32, (1, n), 1)
    first = (lane % (HEAD_DIM // 2)) < ROPE_QUARTER
    partner = jnp.where(first, pltpu.roll(x, n - ROPE_QUARTER, 1), pltpu.roll(x, ROPE_QUARTER, 1))
    return x * cos + partner * sin


def _inproj_attn_kernel(x_ref, shift_ref, scale_ref, w_ref, cos_ref, sin_ref,
                        qc_ref, kc_ref, vc_ref, qd_ref, kd_ref, vd_ref):
    hm = x_ref[...] * (1.0 + scale_ref[...]) + shift_ref[...]
    z = jnp.dot(hm.astype(BF16), w_ref[...], preferred_element_type=F32)
    c0, c1, c2, c3, c4 = ODD_SPLITS
    cos = cos_ref[...]
    sin = sin_ref[...]
    qc_ref[...] = (_rope(z[:, 0:c0], cos, sin) * ATT_SCALE).astype(BF16)
    kc_ref[...] = _rope(z[:, c0:c1], cos[:, :KC_W], sin[:, :KC_W]).astype(BF16)
    vc_ref[...] = z[:, c1:c2].astype(BF16)
    qd_ref[...] = (z[:, c2:c3] * ATT_SCALE).astype(BF16)
    kd_ref[...] = z[:, c3:c4].astype(BF16)
    vd_ref[...] = z[:, c4:].astype(BF16)


def _inproj_attn(h, tab, which_shift, which_scale, w_bf16, n_ctx):
    bsz, s, d = h.shape
    n = w_bf16.shape[1]
    cos, sin = _rope_tables(n_ctx, s - n_ctx)
    widths = (QC_W, KC_W, KC_W, QD_W, QD_W, QD_W)
    return pl.pallas_call(
        _inproj_attn_kernel,
        grid=(bsz, s // ROW_BLOCK),
        in_specs=[pl.BlockSpec((None, ROW_BLOCK, d), lambda b, i: (b, i, 0)),
                  _tab_spec(which_shift), _tab_spec(which_scale),
                  pl.BlockSpec((d, n), lambda b, i: (0, 0)),
                  pl.BlockSpec((ROW_BLOCK, QC_W), lambda b, i: (i, 0)),
                  pl.BlockSpec((ROW_BLOCK, QC_W), lambda b, i: (i, 0))],
        out_specs=[pl.BlockSpec((None, ROW_BLOCK, w), lambda b, i: (b, i, 0)) for w in widths],
        out_shape=[jax.ShapeDtypeStruct((bsz, s, w), BF16) for w in widths],
        compiler_params=_cparams("parallel", "parallel"),
        name="inproj_attn",
    )(h, tab, tab, w_bf16, cos, sin)


def _softmax_pv_heads(scores, values, extra_logits=None):
    ms = [jnp.max(s, axis=1, keepdims=True) for s in scores]
    if extra_logits is not None:
        ms = [jnp.maximum(m, x) for m, x in zip(ms, extra_logits)]
    ps = [jnp.exp(s - m) for s, m in zip(scores, ms)]
    dens = [jnp.sum(p, axis=1, keepdims=True) for p in ps]
    if extra_logits is not None:
        dens = [d + jnp.exp(x - m) for d, x, m in zip(dens, extra_logits, ms)]
    outs = [jnp.dot(p.astype(BF16), v, preferred_element_type=F32) for p, v in zip(ps, values)]
    return [o / d for o, d in zip(outs, dens)]


def _window_kernel(q_ref, kp_ref, kc_ref, kn_ref, kx_ref, vp_ref, vc_ref, vn_ref, vx_ref, sink_ref, o_ref, *, t_len):
    i = pl.program_id(1)
    wb = WIN_BLOCK
    qi = lax.broadcasted_iota(jnp.int32, (wb, 1), 0)
    kj = lax.broadcasted_iota(jnp.int32, (1, 3 * wb), 1) - wb
    kabs = i * wb + kj
    valid = (jnp.abs(kj - qi) <= WINDOW) & (kabs >= 0) & (kabs < t_len)
    n_ctx = kx_ref.shape[0]
    bias = jnp.concatenate([jnp.where(valid, 0.0, ATT_NEG), jnp.zeros((wb, n_ctx), F32)], axis=1)
    k_all = jnp.concatenate([kp_ref[...], kc_ref[...], kn_ref[...], kx_ref[...]], axis=0)
    v_all = jnp.concatenate([vp_ref[...], vc_ref[...], vn_ref[...], vx_ref[...]], axis=0)
    group = WIN_HEADS // WIN_KV_HEADS
    nt = (((1,), (1,)), ((), ()))
    kgs = [k_all[:, g * HEAD_DIM:(g + 1) * HEAD_DIM] for g in range(WIN_KV_HEADS)]
    vgs = [v_all[:, g * HEAD_DIM:(g + 1) * HEAD_DIM] for g in range(WIN_KV_HEADS)]
    scores = [lax.dot_general(q_ref[:, hq * HEAD_DIM:(hq + 1) * HEAD_DIM], kgs[hq // group], nt,
                              preferred_element_type=F32) + bias for hq in range(WIN_HEADS)]
    outs = _softmax_pv_heads(scores, [vgs[hq // group] for hq in range(WIN_HEADS)],
                             [sink_ref[hq:hq + 1, 0:1] for hq in range(WIN_HEADS)])
    for hq in range(WIN_HEADS):
        o_ref[:, hq * HEAD_DIM:(hq + 1) * HEAD_DIM] = outs[hq]


def _window_gqa(qc, kc, vc, sink, n_ctx):
    bsz, s, _ = qc.shape
    t_len = s - n_ctx
    nb = t_len // WIN_BLOCK
    off = n_ctx // WIN_BLOCK

    def kv_spec(delta):
        return pl.BlockSpec((None, WIN_BLOCK, KC_W), lambda b, i: (b, jnp.clip(i + delta, 0, nb - 1) + off, 0))

    ctx_spec = pl.BlockSpec((None, n_ctx, KC_W), lambda b, i: (b, 0, 0))
    sink_tile = jnp.broadcast_to(sink.reshape(WIN_HEADS, 1), (WIN_HEADS, LANES))
    return pl.pallas_call(
        functools.partial(_window_kernel, t_len=t_len),
        grid=(bsz, nb),
        in_specs=[pl.BlockSpec((None, WIN_BLOCK, QC_W), lambda b, i: (b, i + off, 0)),
                  kv_spec(-1), kv_spec(0), kv_spec(1), ctx_spec,
                  kv_spec(-1), kv_spec(0), kv_spec(1), ctx_spec,
                  pl.BlockSpec((WIN_HEADS, LANES), lambda b, i: (0, 0))],
        out_specs=pl.BlockSpec((None, WIN_BLOCK, QC_W), lambda b, i: (b, i, 0)),
        out_shape=jax.ShapeDtypeStruct((bsz, t_len, QC_W), F32),
        compiler_params=_cparams("parallel", "parallel"),
        name="window_gqa",
    )(qc, kc, kc, kc, kc, vc, vc, vc, vc, sink_tile)


NA_QROWS = 4
NA_KROWS = NA_QROWS + NA_ROWS - 1


def _na_block_plan(rows):
    kh = min(NA_ROWS, rows)
    kw = min(NA_COLS, GRID_W)
    c = np.arange(GRID_W)
    cs = np.clip(c - kw // 2, 0, GRID_W - kw)
    kc = np.arange(GRID_W)
    col_valid = (kc[None, :] >= cs[:, None]) & (kc[None, :] < cs[:, None] + kw)
    dc = np.clip(kc[None, :] - c[:, None] + (NA_COLS - 1), 0, 2 * NA_COLS - 2)
    patterns, types, kbs = {}, [], []
    for r0 in range(0, rows, NA_QROWS):
        kb = int(np.clip(r0 - kh // 2, 0, rows - NA_KROWS))
        r = r0 + np.arange(NA_QROWS)
        rs = np.clip(r - kh // 2, 0, rows - kh)
        kr = kb + np.arange(NA_KROWS)
        row_valid = (kr[None, :] >= rs[:, None]) & (kr[None, :] < rs[:, None] + kh)
        dr = np.clip(kr[None, :] - r[:, None] + (NA_ROWS - 1), 0, 2 * NA_ROWS - 2)
        key = (row_valid.tobytes(), dr.tobytes())
        if key not in patterns:
            valid = row_valid[:, None, :, None] & col_valid[None, :, None, :]
            shape = valid.shape
            patterns[key] = (len(patterns), valid.reshape(NA_QROWS * GRID_W, NA_KROWS * GRID_W),
                             np.broadcast_to(dr[:, None, :, None], shape).reshape(valid.shape[0] * shape[1], -1),
                             np.broadcast_to(dc[None, :, None, :], shape).reshape(valid.shape[0] * shape[1], -1))
        types.append(patterns[key][0])
        kbs.append(kb)
    ordered = sorted(patterns.values(), key=lambda p: p[0])
    valid = np.stack([p[1] for p in ordered])
    dr = np.stack([p[2] for p in ordered])
    dc = np.stack([p[3] for p in ordered])
    return np.asarray(types, np.int32), np.asarray(kbs, np.int32), valid, dr, dc


def _na_kernel(types_ref, kbs_ref, q_ref, *refs):
    del types_ref, kbs_ref
    k_refs = refs[:NA_KROWS]
    v_refs = refs[NA_KROWS:2 * NA_KROWS]
    kx_ref, vx_ref, bias_ref, o_ref = refs[2 * NA_KROWS:]
    k_all = jnp.concatenate([r[...] for r in k_refs] + [kx_ref[...]], axis=0)
    v_all = jnp.concatenate([r[...] for r in v_refs] + [vx_ref[...]], axis=0)
    n_q = q_ref.shape[0]
    n_ctx = kx_ref.shape[0]
    zeros = jnp.zeros((n_q, n_ctx), F32)
    nt = (((1,), (1,)), ((), ()))
    sls = [slice(hd * HEAD_DIM, (hd + 1) * HEAD_DIM) for hd in range(NA_HEADS)]
    scores = [lax.dot_general(q_ref[:, sl], k_all[:, sl], nt, preferred_element_type=F32)
              + jnp.concatenate([bias_ref[hd], zeros], axis=1) for hd, sl in enumerate(sls)]
    outs = _softmax_pv_heads(scores, [v_all[:, sl] for sl in sls])
    for sl, out in zip(sls, outs):
        o_ref[:, sl] = out


def _neighbourhood_attention(qd, kd, vd, rpb, n_ctx):
    bsz, s, _ = qd.shape
    t_len = s - n_ctx
    rows = t_len // GRID_W
    types, kbs, valid, dr, dc = _na_block_plan(rows)
    n_types = valid.shape[0]
    dr_pat = dr.reshape(n_types, NA_QROWS, GRID_W, NA_KROWS, GRID_W)[:, :, 0, :, 0]
    dc_pat = dc.reshape(n_types, NA_QROWS, GRID_W, NA_KROWS, GRID_W)[0, 0, :, 0, :]
    oh_r = jnp.asarray(dr_pat[..., None] == np.arange(2 * NA_ROWS - 1), F32)
    oh_c = jnp.asarray(dc_pat[..., None] == np.arange(2 * NA_COLS - 1), F32)
    bias = jnp.einsum('tamr,hrs,cks->thacmk', oh_r, rpb, oh_c, precision=HI)
    bias = bias.reshape(n_types, NA_HEADS, NA_QROWS * GRID_W, NA_KROWS * GRID_W)
    bias = jnp.where(valid[:, None], bias, ATT_NEG)
    n_q = NA_QROWS * GRID_W
    off_q = n_ctx // n_q
    off_k = n_ctx // GRID_W

    def kv_spec(m):
        return pl.BlockSpec((None, GRID_W, QD_W), lambda b, i, types, kbs: (b, kbs[i] + m + off_k, 0))

    ctx_spec = pl.BlockSpec((None, n_ctx, QD_W), lambda b, i, types, kbs: (b, 0, 0))
    grid_spec = pltpu.PrefetchScalarGridSpec(
        num_scalar_prefetch=2,
        grid=(bsz, rows // NA_QROWS),
        in_specs=([pl.BlockSpec((None, n_q, QD_W), lambda b, i, types, kbs: (b, i + off_q, 0))]
                  + [kv_spec(m) for m in range(NA_KROWS)] + [kv_spec(m) for m in range(NA_KROWS)]
                  + [ctx_spec, ctx_spec,
                     pl.BlockSpec((None, NA_HEADS, n_q, NA_KROWS * GRID_W),
                                  lambda b, i, types, kbs: (types[i], 0, 0, 0))]),
        out_specs=pl.BlockSpec((None, n_q, QD_W), lambda b, i, types, kbs: (b, i, 0)),
    )
    return pl.pallas_call(
        _na_kernel,
        grid_spec=grid_spec,
        out_shape=jax.ShapeDtypeStruct((bsz, t_len, QD_W), F32),
        compiler_params=_cparams("parallel", "arbitrary"),
        name="neighbourhood_attention",
    )(jnp.asarray(types), jnp.asarray(kbs), qd, *([kd] * NA_KROWS), *([vd] * NA_KROWS), kd, vd, bias)


RWKV_CHUNK = 64
RWKV_SUPER = ROW_BLOCK // RWKV_CHUNK
RWKV_CHUNKS_PER_STEP = 4
HI = lax.Precision.HIGHEST


RWKV_PREC_PAIR = "bf16"
RWKV_PREC_INV = "bf16"
RWKV_PREC_OUT = "bf16"
RWKV_PREC_STATE = "bf16x3"


def _pdot(x, y, dims, mode):
    if mode == "f32":
        return lax.dot_general(x, y, dims, precision=HI, preferred_element_type=F32)
    xh = x.astype(BF16)
    yh = y.astype(BF16)
    out = lax.dot_general(xh, yh, dims, preferred_element_type=F32)
    if mode == "bf16x3":
        xl = (x - xh.astype(F32)).astype(BF16)
        yl = (y - yh.astype(F32)).astype(BF16)
        out = (out + lax.dot_general(xh, yl, dims, preferred_element_type=F32)
               + lax.dot_general(xl, yh, dims, preferred_element_type=F32))
    return out


def _split_dot(x, e_bf16):
    hi = x.astype(BF16)
    lo = (x - hi.astype(F32)).astype(BF16)
    return jnp.dot(hi, e_bf16, preferred_element_type=F32) + jnp.dot(lo, e_bf16, preferred_element_type=F32)


def _rwkv_prep_kernel(z_ref, zp_ref, zn_ref, mu_ref, w0_ref, w2_ref, a0_ref, a2_ref, g2_ref, kk_ref, ka_ref,
                      rk_ref, e_ref, r_o, v_o, kk_o, g_o, bonus_o, logw_o, kd_o, bb_o):
    i = pl.program_id(1)
    last = pl.num_programs(1) - 1
    tm = z_ref.shape[0]
    hw = RWKV_WIDTH
    z = z_ref[:, S5_WIDTH:]
    row = lax.broadcasted_iota(jnp.int32, (tm, 1), 0)
    prev_halo = jnp.where(i <= 1, 0.0, zp_ref[SUBLANES - 1:SUBLANES, S5_WIDTH:])
    next_halo = jnp.where((i == 0) | (i == last), 0.0, zn_ref[0:1, S5_WIDTH:])
    prev = jnp.where(row == 0, prev_halo, pltpu.roll(z, 1, 0))
    nxt = jnp.where(row == tm - 1, next_halo, pltpu.roll(z, tm - 1, 0))
    zs = z + mu_ref[0:1, :] * (prev - z) + mu_ref[1:2, :] * (nxt - z)
    c0, c1, c2, c3, c4 = RWKV_SPLITS
    r = zs[:, :c0]
    k = zs[:, c0:c1]
    v = zs[:, c1:c2]
    e = e_ref[...]
    g_o[...] = jnp.dot(jax.nn.sigmoid(zs[:, c4:]).astype(BF16), g2_ref[...], preferred_element_type=F32)
    kk = k * kk_ref[...]
    kk = kk * lax.rsqrt(_split_dot(kk * kk, e) + 1e-12)
    w_lora = jnp.dot(jnp.tanh(zs[:, c2:c3]).astype(BF16), w2_ref[...], preferred_element_type=F32)
    a_lora = jnp.dot(zs[:, c3:c4].astype(BF16), a2_ref[...], preferred_element_type=F32)
    bonus = jnp.zeros((tm, hw), F32)
    for d in range(2):
        x = w0_ref[d:d + 1, :] + w_lora[:, d * hw:(d + 1) * hw]
        softplus_neg = jnp.maximum(-x, 0.0) + jnp.log(1.0 + jnp.exp(-jnp.abs(x)))
        logw_o[d] = -jnp.exp(-softplus_neg - 0.5)
        a = jax.nn.sigmoid(a0_ref[d:d + 1, :] + a_lora[:, d * hw:(d + 1) * hw])
        kd = k * (1.0 + (a - 1.0) * ka_ref[...])
        kd_o[d] = kd
        bb_o[d] = kk * a
        bonus = bonus + _split_dot(r * kd * rk_ref[...], e) * v
    r_o[...] = r
    v_o[...] = v
    kk_o[...] = kk
    bonus_o[...] = bonus


def _rwkv_chunk_kernel(r_ref, v_ref, kk_ref, logw_ref, kd_ref, bb_ref, mn_ref, ry_ref):
    fwd = pl.program_id(1) == 0
    cn = RWKV_CHUNK
    hd = HEAD_DIM
    rowi = lax.broadcasted_iota(jnp.int32, (cn, cn), 0)
    coli = lax.broadcasted_iota(jnp.int32, (cn, cn), 1)
    lag = (rowi - coli) * jnp.where(fwd, 1, -1)
    before_incl = lag >= 0
    before = lag > 0
    eye = (rowi == coli).astype(F32)
    nn = (((1,), (0,)), ((), ()))
    nt = (((1,), (1,)), ((), ()))
    tn = (((0,), (0,)), ((), ()))

    a_t, r_t, b_t, k_t, b_h, k_h, g_end, vs, where = [], [], [], [], [], [], [], [], []
    for cc in range(RWKV_CHUNKS_PER_STEP):
        rs = slice(cc * cn, (cc + 1) * cn)
        logw = logw_ref[rs, :]
        lg = jnp.dot(before_incl.astype(F32), logw, precision=HI, preferred_element_type=F32)
        lg_tot = jnp.sum(logw, axis=0, keepdims=True)
        inv = jnp.exp(-lg)
        to_end = jnp.exp(lg_tot - lg)
        full = dict(a_t=-kk_ref[rs, :] * jnp.exp(lg - logw), r_t=r_ref[rs, :] * jnp.exp(lg),
                    b_t=bb_ref[rs, :] * inv, k_t=kd_ref[rs, :] * inv,
                    b_h=bb_ref[rs, :] * to_end, k_h=kd_ref[rs, :] * to_end, g_end=jnp.exp(lg_tot))
        for h in range(RWKV_HEADS):
            sl = slice(h * hd, (h + 1) * hd)
            a_t.append(full["a_t"][:, sl])
            r_t.append(full["r_t"][:, sl])
            b_t.append(full["b_t"][:, sl])
            k_t.append(full["k_t"][:, sl])
            b_h.append(full["b_h"][:, sl])
            k_h.append(full["k_h"][:, sl])
            g_end.append(full["g_end"][:, sl])
            vs.append(v_ref[rs, sl])
            where.append((cc, h))
    ps = [_pdot(jnp.concatenate([a, r], axis=0), jnp.concatenate([b, k], axis=0), nt, RWKV_PREC_PAIR)
          for a, r, b, k in zip(a_t, r_t, b_t, k_t)]
    n_ab = [jnp.where(before, p[:cn, :cn], 0.0) for p in ps]
    n_ak = [jnp.where(before, p[:cn, cn:], 0.0) for p in ps]
    n_rb = [jnp.where(before_incl, p[cn:, :cn], 0.0) for p in ps]
    n_rk = [jnp.where(before_incl, p[cn:, cn:], 0.0) for p in ps]
    tinv = [eye + n for n in n_ab]
    pw = n_ab
    for _ in range(int(math.log2(cn)) - 1):
        pw = [_pdot(x, x, nn, RWKV_PREC_INV) for x in pw]
        tinv = [t + _pdot(t, x, nn, RWKV_PREC_INV) for t, x in zip(tinv, pw)]
    akv = [_pdot(n, v, nn, RWKV_PREC_OUT) for n, v in zip(n_ak, vs)]
    rkv = [_pdot(n, v, nn, RWKV_PREC_OUT) for n, v in zip(n_rk, vs)]
    kv = [_pdot(k, v, tn, RWKV_PREC_OUT) for k, v in zip(k_h, vs)]
    zz = [_pdot(t, jnp.concatenate([a, w], axis=1), nn, RWKV_PREC_OUT)
          for t, a, w in zip(tinv, a_t, akv)]
    for u, (cc, h) in enumerate(where):
        ry_ref[cc, h] = (_pdot(n_rb[u], zz[u], nn, RWKV_PREC_OUT)
                         + jnp.concatenate([r_t[u], rkv[u]], axis=1))
        mn_ref[cc, h] = (_pdot(b_h[u], zz[u], tn, RWKV_PREC_OUT)
                         + jnp.concatenate([eye * g_end[u], kv[u]], axis=1))


def _rwkv_seq_kernel(mn_ref, ry_ref, y_ref, h_ref):
    @pl.when(pl.program_id(2) == 0)
    def _():
        h_ref[...] = jnp.zeros_like(h_ref)

    fwd = pl.program_id(1) == 0
    hd = HEAD_DIM
    for cc in range(RWKV_SUPER):
        c = jnp.where(fwd, cc, RWKV_SUPER - 1 - cc)
        row0 = pl.multiple_of(c * RWKV_CHUNK, RWKV_CHUNK)
        for h in range(RWKV_HEADS):
            mn = mn_ref[c, h]
            ry = ry_ref[c, h]
            state = h_ref[h]
            nn = (((1,), (0,)), ((), ()))
            y_ref[pl.ds(row0, RWKV_CHUNK), h * hd:(h + 1) * hd] = (
                _pdot(ry[:, :hd], state, nn, RWKV_PREC_STATE) + ry[:, hd:])
            h_ref[h] = _pdot(mn[:, :hd], state, nn, RWKV_PREC_STATE) + mn[:, hd:]


def _rwkv_post_kernel(y0_ref, y1_ref, bonus_ref, g_ref, gnw_ref, gnb_ref, e_ref, o_ref):
    y = y0_ref[...] + y1_ref[...]
    e = e_ref[...]
    mu = _split_dot(y, e) * (1.0 / HEAD_DIM)
    yc = y - mu
    var = _split_dot(yc * yc, e) * (1.0 / HEAD_DIM)
    yn = yc * lax.rsqrt(var + GN_EPS) * gnw_ref[...] + gnb_ref[...]
    o_ref[...] = (yn + bonus_ref[...]) * g_ref[...]


def _block_diag2(w):
    z = jnp.zeros_like(w[0])
    return jnp.concatenate([jnp.concatenate([w[0], z], 1), jnp.concatenate([z, w[1]], 1)], 0)


def _rwkv_mixer(z, shift_mu, w0, w2, a0, a2, g2, k_k, k_a, r_k, gn_w, gn_b):
    bsz, s, zw = z.shape
    hw = RWKV_WIDTH
    tm = ROW_BLOCK
    nblk = s // tm
    head_of = np.arange(hw) // HEAD_DIM
    e = jnp.asarray(head_of[:, None] == head_of[None, :], BF16)
    tok = pl.BlockSpec((None, tm, hw), lambda b, i: (b, i, 0))
    tok2 = pl.BlockSpec((None, 2, tm, hw), lambda b, i: (b, 0, i, 0))
    halo = tm // SUBLANES

    def full(shape):
        return pl.BlockSpec(shape, lambda b, i: (0,) * len(shape))

    r, v, kk, g, bonus, logw, kd, bb = pl.pallas_call(
        _rwkv_prep_kernel,
        grid=(bsz, nblk),
        in_specs=[pl.BlockSpec((None, tm, zw), lambda b, i: (b, i, 0)),
                  pl.BlockSpec((None, SUBLANES, zw), lambda b, i: (b, jnp.maximum(i * halo - 1, 0), 0)),
                  pl.BlockSpec((None, SUBLANES, zw),
                               lambda b, i: (b, jnp.minimum((i + 1) * halo, s // SUBLANES - 1), 0)),
                  full((2, RWKV_IN)), full((2, hw)), full((2 * RWKV_LORA_W, 2 * hw)), full((2, hw)),
                  full((2 * RWKV_LORA_A, 2 * hw)), full((RWKV_LORA_G, hw)), full((1, hw)), full((1, hw)),
                  full((1, hw)), full((hw, hw))],
        out_specs=[tok, tok, tok, tok, tok, tok2, tok2, tok2],
        out_shape=[jax.ShapeDtypeStruct((bsz, s, hw), F32)] * 5 + [jax.ShapeDtypeStruct((bsz, 2, s, hw), F32)] * 3,
        compiler_params=_cparams("parallel", "parallel"),
        name="rwkv_prep",
    )(z, z, z, shift_mu, w0, _block_diag2(w2).astype(BF16), a0, _block_diag2(a2).astype(BF16), g2.astype(BF16),
      k_k.reshape(1, hw), k_a.reshape(1, hw), r_k.reshape(1, hw), e)

    nch = s // RWKV_CHUNK
    step_rows = RWKV_CHUNKS_PER_STEP * RWKV_CHUNK
    ctok = pl.BlockSpec((None, step_rows, hw), lambda b, d, c: (b, c, 0))
    ctok2 = pl.BlockSpec((None, None, step_rows, hw), lambda b, d, c: (b, d, c, 0))
    mat_shape = jax.ShapeDtypeStruct((bsz, 2, nch, RWKV_HEADS, RWKV_CHUNK, 2 * HEAD_DIM), F32)
    mat_spec = pl.BlockSpec((None, None, RWKV_CHUNKS_PER_STEP, RWKV_HEADS, RWKV_CHUNK, 2 * HEAD_DIM),
                            lambda b, d, c: (b, d, c, 0, 0, 0))
    mn, ry = pl.pallas_call(
        _rwkv_chunk_kernel,
        grid=(bsz, 2, nch // RWKV_CHUNKS_PER_STEP),
        in_specs=[ctok, ctok, ctok, ctok2, ctok2, ctok2],
        out_specs=[mat_spec, mat_spec],
        out_shape=[mat_shape, mat_shape],
        compiler_params=_cparams("parallel", "parallel", "parallel"),
        name="rwkv_chunk",
    )(r, v, kk, logw, kd, bb)

    nsb = nch // RWKV_SUPER

    def sb_of(d, j):
        return jnp.where(d == 0, j, jnp.where(j == 0, 0, nsb - j))

    sup_shape = (bsz, 2, nsb, RWKV_SUPER, RWKV_HEADS, RWKV_CHUNK, 2 * HEAD_DIM)
    sup_spec = pl.BlockSpec((None, None, None, RWKV_SUPER, RWKV_HEADS, RWKV_CHUNK, 2 * HEAD_DIM),
                            lambda b, d, j: (b, d, sb_of(d, j), 0, 0, 0, 0))
    y = pl.pallas_call(
        _rwkv_seq_kernel,
        grid=(bsz, 2, nsb),
        in_specs=[sup_spec, sup_spec],
        out_specs=pl.BlockSpec((None, None, tm, hw), lambda b, d, j: (b, d, sb_of(d, j), 0)),
        out_shape=jax.ShapeDtypeStruct((bsz, 2, s, hw), F32),
        scratch_shapes=[pltpu.VMEM((RWKV_HEADS, HEAD_DIM, HEAD_DIM), F32)],
        compiler_params=_cparams("parallel", "parallel", "arbitrary"),
        name="rwkv_seq",
    )(mn.reshape(sup_shape), ry.reshape(sup_shape))

    vec = pl.BlockSpec((1, hw), lambda b, i: (0, 0))
    return pl.pallas_call(
        _rwkv_post_kernel,
        grid=(bsz, nblk),
        in_specs=[pl.BlockSpec((None, None, tm, hw), lambda b, i: (b, 0, i, 0)),
                  pl.BlockSpec((None, None, tm, hw), lambda b, i: (b, 1, i, 0)),
                  tok, tok, vec, vec, pl.BlockSpec((hw, hw), lambda b, i: (0, 0))],
        out_specs=tok,
        out_shape=jax.ShapeDtypeStruct((bsz, s, hw), F32),
        compiler_params=_cparams("parallel", "parallel"),
        name="rwkv_post",
    )(y, y, bonus, g, gn_w.reshape(1, hw), gn_b.reshape(1, hw), e)


def kernel(x, c, ctx, c_ctx,
           l0_w_mod, l0_b_mod, l0_w_in, l0_s5_lam_re, l0_s5_lam_im, l0_s5_log_dt, l0_s5_b_re, l0_s5_b_im,
           l0_s5_c_re, l0_s5_c_im, l0_s5_d, l0_w_glu, l0_b_glu, l0_rwkv_shift, l0_rwkv_w0, l0_rwkv_w2,
           l0_rwkv_a0, l0_rwkv_a2, l0_rwkv_g2, l0_rwkv_k_k, l0_rwkv_k_a, l0_rwkv_r_k, l0_rwkv_gn_w,
           l0_rwkv_gn_b, l0_w_out, l0_ln1_g, l0_ln1_b, l0_peer_wq, l0_peer_keys, l0_peer_u, l0_peer_v,
           l0_ln2_g, l0_ln2_b,
           l1_w_mod, l1_b_mod, l1_w_in, l1_sink, l1_rpb, l1_w_out, l1_ln1_g, l1_ln1_b, l1_peer_wq,
           l1_peer_keys, l1_peer_u, l1_peer_v, l1_ln2_g, l1_ln2_b):
    L = ctx.shape[1]
    h = jnp.concatenate([ctx, x], 1)

    tab = _mod_tables(c, c_ctx, l0_w_mod, l0_b_mod)
    z = _inproj(h, tab, 0, 1, l0_w_in.astype(BF16))
    ya = _s5_mixer(z[..., :S5_WIDTH], L, l0_s5_lam_re, l0_s5_lam_im, l0_s5_log_dt, l0_s5_b_re, l0_s5_b_im,
                   l0_s5_c_re, l0_s5_c_im, l0_s5_d, l0_w_glu, l0_b_glu)
    yb = _rwkv_mixer(z, l0_rwkv_shift, l0_rwkv_w0, l0_rwkv_w2, l0_rwkv_a0, l0_rwkv_a2,
                     l0_rwkv_g2, l0_rwkv_k_k, l0_rwkv_k_a, l0_rwkv_r_k, l0_rwkv_gn_w, l0_rwkv_gn_b)
    h = _outproj_ln(ya, yb, h, tab, 2, l0_w_out, l0_ln1_g, l0_ln1_b, 0)
    h = _peer_ln(h, tab, 0, l0_peer_wq, l0_peer_keys, l0_peer_u, l0_peer_v, l0_ln2_g, l0_ln2_b)

    tab = _mod_tables(c, c_ctx, l1_w_mod, l1_b_mod)
    qc, kc, vc, qd, kd, vd = _inproj_attn(h, tab, 0, 1, l1_w_in.astype(BF16), L)
    oc = _window_gqa(qc, kc, vc, l1_sink, L)
    od = _neighbourhood_attention(qd, kd, vd, l1_rpb, L)
    ctx_blocks = L // ROW_BLOCK
    hx = _outproj_ln(oc, od, h, tab, 2, l1_w_out, l1_ln1_g, l1_ln1_b, ctx_blocks)
    return _peer_ln(hx, tab, ctx_blocks, l1_peer_wq, l1_peer_keys, l1_peer_u, l1_peer_v, l1_ln2_g, l1_ln2_b)
```

```python
import functools
import math

import jax
import jax.numpy as jnp
import numpy as np
from jax import lax
from jax.experimental import pallas as pl
from jax.experimental.pallas import tpu as pltpu

F32 = jnp.float32
BF16 = jnp.bfloat16

D_MODEL = 1024
DEPTH = 2
GRID_W = 64
CTX_LEN = 256
HEAD_DIM = 64
S5_GROUP_CH = 16
S5_GROUPS = 16
S5_WIDTH = S5_GROUPS * S5_GROUP_CH
S5_STATE = 64
RWKV_HEADS = 8
RWKV_WIDTH = RWKV_HEADS * HEAD_DIM
RWKV_LORA_W = 64
RWKV_LORA_A = 64
RWKV_LORA_G = 128
RWKV_IN = 3 * RWKV_WIDTH + 2 * RWKV_LORA_W + 2 * RWKV_LORA_A + RWKV_LORA_G
RWKV_SPLITS = (RWKV_WIDTH, 2 * RWKV_WIDTH, 3 * RWKV_WIDTH,
               3 * RWKV_WIDTH + 2 * RWKV_LORA_W,
               3 * RWKV_WIDTH + 2 * RWKV_LORA_W + 2 * RWKV_LORA_A)
WIN_HEADS = 8
WIN_KV_HEADS = 2
WINDOW = 128
WIN_BLOCK = 128
NA_HEADS = 8
NA_ROWS = 8
NA_COLS = 16
ODD_SPLITS = (WIN_HEADS * HEAD_DIM, (WIN_HEADS + WIN_KV_HEADS) * HEAD_DIM,
              (WIN_HEADS + 2 * WIN_KV_HEADS) * HEAD_DIM,
              (WIN_HEADS + 2 * WIN_KV_HEADS + NA_HEADS) * HEAD_DIM,
              (WIN_HEADS + 2 * WIN_KV_HEADS + 2 * NA_HEADS) * HEAD_DIM)
ROPE_BASE = 10000.0
PEER_HEADS = 8
PEER_KEYS = 128
PEER_DK = 128
PEER_TOPK = 16
ALPHA = (2.0 * DEPTH) ** 0.25
LN_EPS = 1e-5
GN_EPS = 64e-5

VMEM_LIMIT_BYTES = 56 * 1024 * 1024
LANES = 128
SUBLANES = 8
ROW_BLOCK = 256


def _cparams(*sem):
    return pltpu.CompilerParams(dimension_semantics=sem, vmem_limit_bytes=VMEM_LIMIT_BYTES)


def _modulation_kernel(c_ref, w_ref, b_ref, o_ref):
    c = c_ref[...]
    s = c * jax.nn.sigmoid(c)
    o_ref[...] = jnp.dot(s.astype(BF16), w_ref[...].astype(BF16), preferred_element_type=F32) + b_ref[...]


def _modulation(cond, w_mod, b_mod):
    n, d = cond.shape
    nout = w_mod.shape[1]
    tn = 1536
    return pl.pallas_call(
        _modulation_kernel,
        grid=(nout // tn,),
        in_specs=[pl.BlockSpec((n, d), lambda j: (0, 0)),
                  pl.BlockSpec((d, tn), lambda j: (0, j)),
                  pl.BlockSpec((1, tn), lambda j: (0, j))],
        out_specs=pl.BlockSpec((n, tn), lambda j: (0, j)),
        out_shape=jax.ShapeDtypeStruct((n, nout), F32),
        compiler_params=_cparams("arbitrary"),
        name="modulation",
    )(cond, w_mod, b_mod.reshape(1, nout))


def _mod_tables(c, c_ctx, w_mod, b_mod):
    bsz = c.shape[0]
    cond = jnp.concatenate([c, c_ctx[None, :], jnp.zeros((SUBLANES - bsz - 1, c.shape[1]), F32)], 0)
    m = _modulation(cond, w_mod, b_mod)
    mx = m[:bsz].reshape(bsz, 1, 6, 1, D_MODEL)
    mc = jnp.broadcast_to(m[bsz].reshape(1, 1, 6, 1, D_MODEL), (bsz, 1, 6, 1, D_MODEL))
    return jnp.concatenate([mc, mx], 1)


def _tab_spec(which):
    return pl.BlockSpec((None, None, None, 1, D_MODEL),
                        lambda b, i: (b, jnp.minimum(i, 1), which, 0, 0))


def _inproj_kernel(x_ref, shift_ref, scale_ref, w_ref, o_ref):
    hm = x_ref[...] * (1.0 + scale_ref[...]) + shift_ref[...]
    o_ref[...] = jnp.dot(hm.astype(BF16), w_ref[...], preferred_element_type=F32)


def _inproj(h, tab, which_shift, which_scale, w_bf16):
    bsz, s, d = h.shape
    n = w_bf16.shape[1]
    return pl.pallas_call(
        _inproj_kernel,
        grid=(bsz, s // ROW_BLOCK),
        in_specs=[pl.BlockSpec((None, ROW_BLOCK, d), lambda b, i: (b, i, 0)),
                  _tab_spec(which_shift), _tab_spec(which_scale),
                  pl.BlockSpec((d, n), lambda b, i: (0, 0))],
        out_specs=pl.BlockSpec((None, ROW_BLOCK, n), lambda b, i: (b, i, 0)),
        out_shape=jax.ShapeDtypeStruct((bsz, s, n), F32),
        compiler_params=_cparams("parallel", "parallel"),
        name="inproj",
    )(h, tab, tab, w_bf16)


def _layer_norm_rows(z, g, b):
    mu = jnp.mean(z, -1, keepdims=True)
    zc = z - mu
    var = jnp.mean(zc * zc, -1, keepdims=True)
    return zc * lax.rsqrt(var + LN_EPS) * g + b


def _outproj_ln_kernel(ya_ref, yb_ref, h_ref, gate_ref, wa_ref, wb_ref, g_ref, b_ref, o_ref):
    proj = (jnp.dot(ya_ref[...].astype(BF16), wa_ref[...], preferred_element_type=F32)
            + jnp.dot(yb_ref[...].astype(BF16), wb_ref[...], preferred_element_type=F32))
    z = ALPHA * h_ref[...] + gate_ref[...] * proj
    o_ref[...] = _layer_norm_rows(z, g_ref[...], b_ref[...])


def _outproj_ln(ya, yb, h, tab, which_gate, w_out, ln_g, ln_b, row_off):
    bsz, rows, ka = ya.shape
    kb = yb.shape[-1]
    d = h.shape[-1]
    return pl.pallas_call(
        _outproj_ln_kernel,
        grid=(bsz, rows // ROW_BLOCK),
        in_specs=[pl.BlockSpec((None, ROW_BLOCK, ka), lambda b, i: (b, i, 0)),
                  pl.BlockSpec((None, ROW_BLOCK, kb), lambda b, i: (b, i, 0)),
                  pl.BlockSpec((None, ROW_BLOCK, d), lambda b, i: (b, i + row_off, 0)),
                  pl.BlockSpec((None, None, None, 1, d), lambda b, i: (b, jnp.minimum(i + row_off, 1), which_gate, 0, 0)),
                  pl.BlockSpec((ka, d), lambda b, i: (0, 0)),
                  pl.BlockSpec((kb, d), lambda b, i: (0, 0)),
                  pl.BlockSpec((1, d), lambda b, i: (0, 0)),
                  pl.BlockSpec((1, d), lambda b, i: (0, 0))],
        out_specs=pl.BlockSpec((None, ROW_BLOCK, d), lambda b, i: (b, i, 0)),
        out_shape=jax.ShapeDtypeStruct((bsz, rows, d), F32),
        compiler_params=_cparams("parallel", "parallel"),
        name="outproj_ln",
    )(ya, yb, h, tab, w_out[:ka].astype(BF16), w_out[ka:].astype(BF16), ln_g.reshape(1, d), ln_b.reshape(1, d))


PEER_NEG = -3.0e38
PEER_ETILE = 2048
PEER_SUBTILE = 256
SQRT_HALF = 0.7071067811865476


PEER_ROUTE_GROUP = 4


def _top_values(arrays, n):
    arrays = list(arrays)
    vals = [[] for _ in arrays]
    for _ in range(n):
        ms = [jnp.max(s, axis=0, keepdims=True) for s in arrays]
        for i, m in enumerate(ms):
            vals[i].append(m)
        arrays = [jnp.where(s == m, PEER_NEG, s) for s, m in zip(arrays, ms)]
    return vals


def _oddeven_merge_sort_network(n):
    pairs = []
    p = 1
    while p < n:
        k = p
        while k >= 1:
            for j in range(k % p, n - k, 2 * k):
                for i in range(min(k, n - j - k)):
                    if (i + j) // (2 * p) == (i + j + k) // (2 * p):
                        pairs.append((i + j, i + j + k))
            k //= 2
        p *= 2
    return pairs


def _top_values_sorted(s, n):
    sub = SUBLANES
    groups = s.shape[0] // sub
    x = s.reshape(groups, sub, s.shape[1])
    cols = [x[g] for g in range(groups)]
    for lo, hi in _oddeven_merge_sort_network(groups):
        a, b = cols[lo], cols[hi]
        cols[lo], cols[hi] = jnp.maximum(a, b), jnp.minimum(a, b)
    assert groups >= n
    vals = []
    for k in range(n):
        m = jnp.max(cols[0], axis=0, keepdims=True)
        vals.append(m)
        hit = cols[0] == m
        for g in range(n - k - 1):
            cols[g] = jnp.where(hit, cols[g + 1], cols[g])
    return vals


def _peer_route_kernel(h_ref, shift_ref, scale_ref, wq_ref, keys_ref,
                       pin_ref, cnt_ref, e1_ref, r2_ref, e2_ref):
    pin = (h_ref[...] * (1.0 + scale_ref[...]) + shift_ref[...]).astype(BF16)
    pin_ref[...] = pin
    q = jnp.dot(pin, wq_ref[...], preferred_element_type=F32).astype(BF16)
    st = lax.dot_general(keys_ref[...], q, (((1,), (1,)), ((), ())), preferred_element_type=F32)
    for h0 in range(0, PEER_HEADS, PEER_ROUTE_GROUP):
        hds = list(range(h0, h0 + PEER_ROUTE_GROUP))
        s1s = [st[hd * 2 * PEER_KEYS:hd * 2 * PEER_KEYS + PEER_KEYS] for hd in hds]
        s2s = [st[hd * 2 * PEER_KEYS + PEER_KEYS:(hd + 1) * 2 * PEER_KEYS] for hd in hds]
        v1s = [_top_values_sorted(s1, PEER_TOPK) for s1 in s1s]
        v2s = [_top_values_sorted(s2, PEER_TOPK) for s2 in s2s]
        rank2s = []
        for s2, v2 in zip(s2s, v2s):
            rank = jnp.zeros_like(s2)
            for b in range(PEER_TOPK):
                rank = rank + jnp.where(v2[b] > s2, 1.0, 0.0)
            rank2s.append(rank)
        frontiers = []
        for v1, v2 in zip(v1s, v2s):
            v2all = jnp.concatenate(v2, axis=0)
            frontiers.append(jnp.concatenate([v1[a] + v2all[:(PEER_TOPK + 1) // (a + 1)] for a in range(PEER_TOPK)],
                                             axis=0))
        tops = _top_values(frontiers, PEER_TOPK + 1)
        for hd, s1, s2, v1, v2, rank2, top in zip(hds, s1s, s2s, v1s, v2s, rank2s, tops):
            tau = 0.5 * (top[PEER_TOPK - 1] + top[PEER_TOPK])
            zsum = top[0] * 0.0
            for kk in range(PEER_TOPK):
                zsum = zsum + jnp.exp(top[kk] - top[0])
            need = tau - s1
            cnt = jnp.zeros_like(s1)
            for b in range(PEER_TOPK):
                cnt = cnt + jnp.where(v2[b] >= need, 1.0, 0.0)
            cnt_ref[hd] = jnp.where(s1 >= v1[PEER_TOPK - 1], cnt, 0.0)
            e1_ref[hd] = jnp.exp(s1 - v1[0])
            r2_ref[hd] = rank2.astype(BF16)
            e2_ref[hd] = (jnp.exp(s2 - v2[0]) / zsum).astype(BF16)


def _peer_dense_kernel(pin_ref, cnt_ref, e1_ref, r2_ref, e2_ref, u_ref, vt_ref,
                       h_ref, gate_ref, g_ref, b_ref, o_ref, acc_ref, w_ref):
    j = pl.program_id(2)

    last = pl.num_programs(2) - 1

    def apply_values(first):
        part = jnp.dot(vt_ref[...], w_ref[(j + 1) % 2], preferred_element_type=F32)
        if first:
            acc_ref[...] = part
        else:
            acc_ref[...] += part

    def score_and_gate():
        pin = pin_ref[...]
        cur = j % 2
        for sb in range(PEER_ETILE // PEER_SUBTILE):
            es = slice(sb * PEER_SUBTILE, (sb + 1) * PEER_SUBTILE)
            at = lax.dot_general(u_ref[es, :], pin, (((1,), (1,)), ((), ())), preferred_element_type=F32)
            for i2 in range(PEER_SUBTILE // PEER_KEYS):
                ii = sb * (PEER_SUBTILE // PEER_KEYS) + i2
                a = at[i2 * PEER_KEYS:(i2 + 1) * PEER_KEYS]
                act = (0.5 * a * (1.0 + lax.erf(a * SQRT_HALF))).astype(BF16)
                g = jnp.zeros(a.shape, BF16)
                for hd in range(PEER_HEADS):
                    cnt_row = cnt_ref[hd, pl.ds(ii, 1), :].astype(BF16)
                    c_row = e1_ref[hd, pl.ds(ii, 1), :].astype(BF16)
                    g = g + jnp.where(r2_ref[hd] < cnt_row, e2_ref[hd] * c_row, jnp.zeros((), BF16))
                w_ref[cur, ii * PEER_KEYS:(ii + 1) * PEER_KEYS, :] = g * act

    @pl.when(j == 0)
    def _():
        score_and_gate()

    @pl.when(j == 1)
    def _():
        apply_values(first=True)
        score_and_gate()

    @pl.when((j > 1) & (j < last))
    def _():
        apply_values(first=False)
        score_and_gate()

    @pl.when(j == last)
    def _():
        apply_values(first=False)
        z = ALPHA * h_ref[...] + gate_ref[...] * acc_ref[...].T
        o_ref[...] = _layer_norm_rows(z, g_ref[...], b_ref[...])


def _peer_block_diag_keys(sub_keys):
    hh, two, nk, dh = sub_keys.shape
    eye = jnp.eye(hh * two, dtype=F32)
    kb = sub_keys.reshape(hh * two, nk, dh)
    return jnp.einsum('akd,ab->akbd', kb, eye).reshape(hh * two * nk, hh * two * dh)


def _peer_ln(h, tab, seg_off, wq, sub_keys, u_tab, v_tab, ln_g, ln_b):
    bsz, s, d = h.shape
    tm = ROW_BLOCK
    nblk = s // tm
    hk = PEER_HEADS * 2 * PEER_KEYS
    keys_bd = _peer_block_diag_keys(sub_keys).astype(BF16)

    def tab_spec3(which):
        return pl.BlockSpec((None, None, None, 1, d),
                            lambda b, i, *_: (b, jnp.minimum(i + seg_off, 1), which, 0, 0))

    route_dims = (bsz, nblk, PEER_HEADS, PEER_KEYS, tm)
    route_spec = pl.BlockSpec((None, None, PEER_HEADS, PEER_KEYS, tm), lambda b, i: (b, i, 0, 0, 0))
    pin, cnt, e1, r2, e2 = pl.pallas_call(
        _peer_route_kernel,
        grid=(bsz, nblk),
        in_specs=[pl.BlockSpec((None, tm, d), lambda b, i: (b, i, 0)),
                  tab_spec3(3), tab_spec3(4),
                  pl.BlockSpec((d, PEER_HEADS * PEER_DK), lambda b, i: (0, 0)),
                  pl.BlockSpec((hk, PEER_HEADS * PEER_DK), lambda b, i: (0, 0))],
        out_specs=[pl.BlockSpec((None, tm, d), lambda b, i: (b, i, 0)),
                   route_spec, route_spec, route_spec, route_spec],
        out_shape=[jax.ShapeDtypeStruct((bsz, nblk * tm, d), BF16),
                   jax.ShapeDtypeStruct(route_dims, F32), jax.ShapeDtypeStruct(route_dims, F32),
                   jax.ShapeDtypeStruct(route_dims, BF16), jax.ShapeDtypeStruct(route_dims, BF16)],
        compiler_params=_cparams("parallel", "parallel"),
        name="peer_route",
    )(h, tab, tab, wq.astype(BF16), keys_bd)

    n_exp = u_tab.shape[0]
    rows_per_step = PEER_ETILE // PEER_KEYS
    full_spec = pl.BlockSpec((None, None, PEER_HEADS, PEER_KEYS, tm), lambda b, i, j: (b, i, 0, 0, 0))
    n_tiles = n_exp // PEER_ETILE
    row_spec = pl.BlockSpec((None, None, PEER_HEADS, rows_per_step, tm),
                            lambda b, i, j: (b, i, 0, jnp.minimum(j, n_tiles - 1), 0))
    return pl.pallas_call(
        _peer_dense_kernel,
        grid=(bsz, nblk, n_tiles + 1),
        in_specs=[pl.BlockSpec((None, tm, d), lambda b, i, j: (b, i, 0)),
                  row_spec, row_spec, full_spec, full_spec,
                  pl.BlockSpec((PEER_ETILE, d), lambda b, i, j: (jnp.minimum(j, n_tiles - 1), 0)),
                  pl.BlockSpec((d, PEER_ETILE), lambda b, i, j: (0, jnp.maximum(j - 1, 0))),
                  pl.BlockSpec((None, tm, d), lambda b, i, j: (b, i, 0)),
                  tab_spec3(5),
                  pl.BlockSpec((1, d), lambda b, i, j: (0, 0)),
                  pl.BlockSpec((1, d), lambda b, i, j: (0, 0))],
        out_specs=pl.BlockSpec((None, tm, d), lambda b, i, j: (b, i, 0)),
        out_shape=jax.ShapeDtypeStruct((bsz, nblk * tm, d), F32),
        scratch_shapes=[pltpu.VMEM((d, tm), F32), pltpu.VMEM((2, PEER_ETILE, tm), BF16)],
        compiler_params=_cparams("parallel", "parallel", "arbitrary"),
        name="peer_dense",
    )(pin, cnt, e1, r2, e2, u_tab.astype(BF16), v_tab.T.astype(BF16), h, tab,
      ln_g.reshape(1, d), ln_b.reshape(1, d))


S5_CHAINS = 8
S5_TCHUNK = 128
S5_NSTATE = S5_GROUPS * S5_STATE


def _s5_scan_kernel(uf_ref, ub_ref, wb_ref, wc_ref, a_ref, yf_ref, yb_ref,
                    fre_ref, fim_ref, bre_ref, bim_ref, h_ref):
    @pl.when(pl.program_id(0) == 0)
    def _():
        h_ref[...] = jnp.zeros_like(h_ref)

    rows = uf_ref.shape[0]
    steps = rows // S5_CHAINS
    uf = uf_ref[...].astype(BF16)
    ub = ub_ref[...].astype(BF16)
    fre_ref[...] = jnp.dot(uf, wb_ref[0], preferred_element_type=F32)
    fim_ref[...] = jnp.dot(uf, wb_ref[1], preferred_element_type=F32)
    bre_ref[...] = jnp.dot(ub, wb_ref[2], preferred_element_type=F32)
    bim_ref[...] = jnp.dot(ub, wb_ref[3], preferred_element_type=F32)
    a_re = a_ref[0]
    a_im = a_ref[1]
    is_fwd = lax.broadcasted_iota(jnp.int32, (S5_CHAINS, S5_NSTATE), 0) < (S5_CHAINS // 2)

    def step(t, carry):
        h_re, h_im = carry
        of = pl.multiple_of(t * S5_CHAINS, S5_CHAINS)
        ob = pl.multiple_of((steps - 1 - t) * S5_CHAINS, S5_CHAINS)
        n_re = a_re * h_re - a_im * h_im + jnp.where(is_fwd, fre_ref[pl.ds(of, S5_CHAINS), :],
                                                     bre_ref[pl.ds(ob, S5_CHAINS), :])
        n_im = a_re * h_im + a_im * h_re + jnp.where(is_fwd, fim_ref[pl.ds(of, S5_CHAINS), :],
                                                     bim_ref[pl.ds(ob, S5_CHAINS), :])
        fre_ref[pl.ds(of, S5_CHAINS), :] = n_re
        fim_ref[pl.ds(of, S5_CHAINS), :] = n_im
        bre_ref[pl.ds(ob, S5_CHAINS), :] = n_re
        bim_ref[pl.ds(ob, S5_CHAINS), :] = n_im
        return n_re, n_im

    h_re, h_im = lax.fori_loop(0, steps, step, (h_ref[0], h_ref[1]), unroll=2)
    h_ref[0] = h_re
    h_ref[1] = h_im
    yf_ref[...] = (jnp.dot(fre_ref[...].astype(BF16), wc_ref[0], preferred_element_type=F32)
                   - jnp.dot(fim_ref[...].astype(BF16), wc_ref[1], preferred_element_type=F32))
    yb_ref[...] = (jnp.dot(bre_ref[...].astype(BF16), wc_ref[2], preferred_element_type=F32)
                   - jnp.dot(bim_ref[...].astype(BF16), wc_ref[3], preferred_element_type=F32))


def _s5_glu_kernel(u_ref, y0_ref, y1_ref, d_ref, w_ref, b_ref, o_ref):
    y = u_ref[...] * d_ref[...] + y0_ref[...] + y1_ref[...]
    y = 0.5 * y * (1.0 + lax.erf(y * SQRT_HALF))
    gate = jnp.dot(y.astype(BF16), w_ref[...], preferred_element_type=F32) + b_ref[...]
    o_ref[...] = y * jax.nn.sigmoid(gate)


def _s5_params(lam_re, lam_im, log_dt, b_re, b_im, c_re, c_im):
    eye = jnp.eye(S5_GROUPS, dtype=F32)
    wb, wc, a_rows = [], [], [[], []]
    for d in range(2):
        lr, li = lam_re[d], lam_im[d]
        dt = jnp.exp(log_dt[d])[:, None]
        mag = jnp.exp(lr * dt)
        ab_re, ab_im = mag * jnp.cos(li * dt), mag * jnp.sin(li * dt)
        den = lr * lr + li * li
        nr = ab_re - 1.0
        coef_re = (nr * lr + ab_im * li) / den
        coef_im = (ab_im * lr - nr * li) / den
        bb_re = coef_re[..., None] * b_re[d] - coef_im[..., None] * b_im[d]
        bb_im = coef_re[..., None] * b_im[d] + coef_im[..., None] * b_re[d]
        for bb in (bb_re, bb_im):
            wb.append(jnp.einsum('gpi,gh->gihp', bb, eye).reshape(S5_WIDTH, S5_NSTATE))
        for cc in (c_re[d], c_im[d]):
            wc.append(jnp.einsum('gip,gh->gphi', cc, eye).reshape(S5_NSTATE, S5_WIDTH))
        for k, ab in enumerate((ab_re, ab_im)):
            a_rows[k].append(jnp.broadcast_to(ab.reshape(1, S5_NSTATE), (S5_CHAINS // 2, S5_NSTATE)))
    a_tiles = jnp.stack([jnp.concatenate(a_rows[0], 0), jnp.concatenate(a_rows[1], 0)])
    return jnp.stack(wb).astype(BF16), jnp.stack(wc).astype(BF16), a_tiles


def _s5_mixer(u, n_ctx, lam_re, lam_im, log_dt, b_re, b_im, c_re, c_im, d_skip, w_glu, b_glu):
    bsz, s, w = u.shape
    assert 2 * bsz == S5_CHAINS
    wb, wc, a_tiles = _s5_params(lam_re, lam_im, log_dt, b_re, b_im, c_re, c_im)
    ut = jnp.transpose(u, (1, 0, 2))
    chain = jnp.concatenate([ut, ut], 1).reshape(s * S5_CHAINS, w)
    rows = S5_TCHUNK * S5_CHAINS
    n_chunks = s // S5_TCHUNK
    ctx_chunks = n_ctx // S5_TCHUNK

    def mirror(i):
        return jnp.where(i < ctx_chunks, ctx_chunks - 1 - i, n_chunks - 1 - (i - ctx_chunks))

    state_buf = pltpu.VMEM((rows, S5_NSTATE), F32)
    y_shape = jax.ShapeDtypeStruct((s * S5_CHAINS, w), F32)
    yf, yb = pl.pallas_call(
        _s5_scan_kernel,
        grid=(n_chunks,),
        in_specs=[pl.BlockSpec((rows, w), lambda i: (i, 0)),
                  pl.BlockSpec((rows, w), lambda i: (mirror(i), 0)),
                  pl.BlockSpec((4, w, S5_NSTATE), lambda i: (0, 0, 0)),
                  pl.BlockSpec((4, S5_NSTATE, w), lambda i: (0, 0, 0)),
                  pl.BlockSpec((2, S5_CHAINS, S5_NSTATE), lambda i: (0, 0, 0))],
        out_specs=[pl.BlockSpec((rows, w), lambda i: (i, 0)),
                   pl.BlockSpec((rows, w), lambda i: (mirror(i), 0))],
        out_shape=[y_shape, y_shape],
        scratch_shapes=[state_buf, state_buf, state_buf, state_buf,
                        pltpu.VMEM((2, S5_CHAINS, S5_NSTATE), F32)],
        compiler_params=_cparams("arbitrary"),
        name="s5_scan",
    )(chain, chain, wb, wc, a_tiles)
    y_fwd = jnp.transpose(yf.reshape(s, S5_CHAINS, w)[:, :bsz], (1, 0, 2))
    y_bwd = jnp.transpose(yb.reshape(s, S5_CHAINS, w)[:, bsz:], (1, 0, 2))
    tok = pl.BlockSpec((None, ROW_BLOCK, w), lambda b, i: (b, i, 0))
    vec = pl.BlockSpec((1, w), lambda b, i: (0, 0))
    return pl.pallas_call(
        _s5_glu_kernel,
        grid=(bsz, s // ROW_BLOCK),
        in_specs=[tok, tok, tok, vec, pl.BlockSpec((w, w), lambda b, i: (0, 0)), vec],
        out_specs=tok,
        out_shape=jax.ShapeDtypeStruct((bsz, s, w), F32),
        compiler_params=_cparams("parallel", "parallel"),
        name="s5_glu",
    )(u, y_fwd, y_bwd, d_skip.reshape(1, w), w_glu.astype(BF16), b_glu.reshape(1, w))


ATT_SCALE = HEAD_DIM ** -0.5
ATT_NEG = -1e30
ROPE_QUARTER = HEAD_DIM // 4
QC_W = WIN_HEADS * HEAD_DIM
KC_W = WIN_KV_HEADS * HEAD_DIM
QD_W = NA_HEADS * HEAD_DIM


def _rope_tables(n_ctx, t):
    pos = np.arange(t)
    rowcol = np.stack([pos // GRID_W, pos % GRID_W], 1).astype(np.float32)
    lane = np.arange(QC_W)
    j = lane % HEAD_DIM
    freqs = (ROPE_BASE ** (-np.arange(ROPE_QUARTER, dtype=np.float32) / ROPE_QUARTER)).astype(np.float32)
    ang = rowcol[:, j // (HEAD_DIM // 2)] * freqs[j % ROPE_QUARTER][None, :]
    ang = jnp.asarray(ang, F32)
    sign = np.where((j % (HEAD_DIM // 2)) < ROPE_QUARTER, -1.0, 1.0).astype(np.float32)
    cos = jnp.concatenate([jnp.ones((n_ctx, QC_W), F32), jnp.cos(ang)], 0)
    sin = jnp.concatenate([jnp.zeros((n_ctx, QC_W), F32), jnp.sin(ang) * sign[None, :]], 0)
    return cos, sin


def _rope(x, cos, sin):
    n = x.shape[-1]
    lane = lax.broadcasted_iota(jnp.int32, (1, n), 1)
    first = (lane % (HEAD_DIM // 2)) < ROPE_QUARTER
    partner = jnp.where(first, pltpu.roll(x, n - ROPE_QUARTER, 1), pltpu.roll(x, ROPE_QUARTER, 1))
    return x * cos + partner * sin


def _inproj_attn_kernel(x_ref, shift_ref, scale_ref, w_ref, cos_ref, sin_ref,
                        qc_ref, kc_ref, vc_ref, qd_ref, kd_ref, vd_ref):
    hm = x_ref[...] * (1.0 + scale_ref[...]) + shift_ref[...]
    z = jnp.dot(hm.astype(BF16), w_ref[...], preferred_element_type=F32)
    c0, c1, c2, c3, c4 = ODD_SPLITS
    cos = cos_ref[...]
    sin = sin_ref[...]
    qc_ref[...] = (_rope(z[:, 0:c0], cos, sin) * ATT_SCALE).astype(BF16)
    kc_ref[...] = _rope(z[:, c0:c1], cos[:, :KC_W], sin[:, :KC_W]).astype(BF16)
    vc_ref[...] = z[:, c1:c2].astype(BF16)
    qd_ref[...] = (z[:, c2:c3] * ATT_SCALE).astype(BF16)
    kd_ref[...] = z[:, c3:c4].astype(BF16)
    vd_ref[...] = z[:, c4:].astype(BF16)


def _inproj_attn(h, tab, which_shift, which_scale, w_bf16, n_ctx):
    bsz, s, d = h.shape
    n = w_bf16.shape[1]
    cos, sin = _rope_tables(n_ctx, s - n_ctx)
    widths = (QC_W, KC_W, KC_W, QD_W, QD_W, QD_W)
    return pl.pallas_call(
        _inproj_attn_kernel,
        grid=(bsz, s // ROW_BLOCK),
        in_specs=[pl.BlockSpec((None, ROW_BLOCK, d), lambda b, i: (b, i, 0)),
                  _tab_spec(which_shift), _tab_spec(which_scale),
                  pl.BlockSpec((d, n), lambda b, i: (0, 0)),
                  pl.BlockSpec((ROW_BLOCK, QC_W), lambda b, i: (i, 0)),
                  pl.BlockSpec((ROW_BLOCK, QC_W), lambda b, i: (i, 0))],
        out_specs=[pl.BlockSpec((None, ROW_BLOCK, w), lambda b, i: (b, i, 0)) for w in widths],
        out_shape=[jax.ShapeDtypeStruct((bsz, s, w), BF16) for w in widths],
        compiler_params=_cparams("parallel", "parallel"),
        name="inproj_attn",
    )(h, tab, tab, w_bf16, cos, sin)


def _softmax_pv_heads(scores, values, extra_logits=None):
    ms = [jnp.max(s, axis=1, keepdims=True) for s in scores]
    if extra_logits is not None:
        ms = [jnp.maximum(m, x) for m, x in zip(ms, extra_logits)]
    ps = [jnp.exp(s - m) for s, m in zip(scores, ms)]
    dens = [jnp.sum(p, axis=1, keepdims=True) for p in ps]
    if extra_logits is not None:
        dens = [d + jnp.exp(x - m) for d, x, m in zip(dens, extra_logits, ms)]
    outs = [jnp.dot(p.astype(BF16), v, preferred_element_type=F32) for p, v in zip(ps, values)]
    return [o / d for o, d in zip(outs, dens)]


def _window_kernel(q_ref, kp_ref, kc_ref, kn_ref, kx_ref, vp_ref, vc_ref, vn_ref, vx_ref, sink_ref, o_ref, *, t_len):
    i = pl.program_id(1)
    wb = WIN_BLOCK
    qi = lax.broadcasted_iota(jnp.int32, (wb, 1), 0)
    kj = lax.broadcasted_iota(jnp.int32, (1, 3 * wb), 1) - wb
    kabs = i * wb + kj
    valid = (jnp.abs(kj - qi) <= WINDOW) & (kabs >= 0) & (kabs < t_len)
    n_ctx = kx_ref.shape[0]
    bias = jnp.concatenate([jnp.where(valid, 0.0, ATT_NEG), jnp.zeros((wb, n_ctx), F32)], axis=1)
    k_all = jnp.concatenate([kp_ref[...], kc_ref[...], kn_ref[...], kx_ref[...]], axis=0)
    v_all = jnp.concatenate([vp_ref[...], vc_ref[...], vn_ref[...], vx_ref[...]], axis=0)
    group = WIN_HEADS // WIN_KV_HEADS
    nt = (((1,), (1,)), ((), ()))
    kgs = [k_all[:, g * HEAD_DIM:(g + 1) * HEAD_DIM] for g in range(WIN_KV_HEADS)]
    vgs = [v_all[:, g * HEAD_DIM:(g + 1) * HEAD_DIM] for g in range(WIN_KV_HEADS)]
    scores = [lax.dot_general(q_ref[:, hq * HEAD_DIM:(hq + 1) * HEAD_DIM], kgs[hq // group], nt,
                              preferred_element_type=F32) + bias for hq in range(WIN_HEADS)]
    outs = _softmax_pv_heads(scores, [vgs[hq // group] for hq in range(WIN_HEADS)],
                             [sink_ref[hq:hq + 1, 0:1] for hq in range(WIN_HEADS)])
    for hq in range(WIN_HEADS):
        o_ref[:, hq * HEAD_DIM:(hq + 1) * HEAD_DIM] = outs[hq]


def _window_gqa(qc, kc, vc, sink, n_ctx):
    bsz, s, _ = qc.shape
    t_len = s - n_ctx
    nb = t_len // WIN_BLOCK
    off = n_ctx // WIN_BLOCK

    def kv_spec(delta):
        return pl.BlockSpec((None, WIN_BLOCK, KC_W), lambda b, i: (b, jnp.clip(i + delta, 0, nb - 1) + off, 0))

    ctx_spec = pl.BlockSpec((None, n_ctx, KC_W), lambda b, i: (b, 0, 0))
    sink_tile = jnp.broadcast_to(sink.reshape(WIN_HEADS, 1), (WIN_HEADS, LANES))
    return pl.pallas_call(
        functools.partial(_window_kernel, t_len=t_len),
        grid=(bsz, nb),
        in_specs=[pl.BlockSpec((None, WIN_BLOCK, QC_W), lambda b, i: (b, i + off, 0)),
                  kv_spec(-1), kv_spec(0), kv_spec(1), ctx_spec,
                  kv_spec(-1), kv_spec(0), kv_spec(1), ctx_spec,
                  pl.BlockSpec((WIN_HEADS, LANES), lambda b, i: (0, 0))],
        out_specs=pl.BlockSpec((None, WIN_BLOCK, QC_W), lambda b, i: (b, i, 0)),
        out_shape=jax.ShapeDtypeStruct((bsz, t_len, QC_W), F32),
        compiler_params=_cparams("parallel", "parallel"),
        name="window_gqa",
    )(qc, kc, kc, kc, kc, vc, vc, vc, vc, sink_tile)


NA_QROWS = 4
NA_KROWS = NA_QROWS + NA_ROWS - 1


def _na_block_plan(rows):
    kh = min(NA_ROWS, rows)
    kw = min(NA_COLS, GRID_W)
    c = np.arange(GRID_W)
    cs = np.clip(c - kw // 2, 0, GRID_W - kw)
    kc = np.arange(GRID_W)
    col_valid = (kc[None, :] >= cs[:, None]) & (kc[None, :] < cs[:, None] + kw)
    dc = np.clip(kc[None, :] - c[:, None] + (NA_COLS - 1), 0, 2 * NA_COLS - 2)
    patterns, types, kbs = {}, [], []
    for r0 in range(0, rows, NA_QROWS):
        kb = int(np.clip(r0 - kh // 2, 0, rows - NA_KROWS))
        r = r0 + np.arange(NA_QROWS)
        rs = np.clip(r - kh // 2, 0, rows - kh)
        kr = kb + np.arange(NA_KROWS)
        row_valid = (kr[None, :] >= rs[:, None]) & (kr[None, :] < rs[:, None] + kh)
        dr = np.clip(kr[None, :] - r[:, None] + (NA_ROWS - 1), 0, 2 * NA_ROWS - 2)
        key = (row_valid.tobytes(), dr.tobytes())
        if key not in patterns:
            valid = row_valid[:, None, :, None] & col_valid[None, :, None, :]
            shape = valid.shape
            patterns[key] = (len(patterns), valid.reshape(NA_QROWS * GRID_W, NA_KROWS * GRID_W),
                             np.broadcast_to(dr[:, None, :, None], shape).reshape(valid.shape[0] * shape[1], -1),
                             np.broadcast_to(dc[None, :, None, :], shape).reshape(valid.shape[0] * shape[1], -1))
        types.append(patterns[key][0])
        kbs.append(kb)
    ordered = sorted(patterns.values(), key=lambda p: p[0])
    valid = np.stack([p[1] for p in ordered])
    dr = np.stack([p[2] for p in ordered])
    dc = np.stack([p[3] for p in ordered])
    return np.asarray(types, np.int32), np.asarray(kbs, np.int32), valid, dr, dc


def _na_kernel(types_ref, kbs_ref, q_ref, *refs):
    del types_ref, kbs_ref
    k_refs = refs[:NA_KROWS]
    v_refs = refs[NA_KROWS:2 * NA_KROWS]
    kx_ref, vx_ref, bias_ref, o_ref = refs[2 * NA_KROWS:]
    k_all = jnp.concatenate([r[...] for r in k_refs] + [kx_ref[...]], axis=0)
    v_all = jnp.concatenate([r[...] for r in v_refs] + [vx_ref[...]], axis=0)
    n_q = q_ref.shape[0]
    n_ctx = kx_ref.shape[0]
    zeros = jnp.zeros((n_q, n_ctx), F32)
    nt = (((1,), (1,)), ((), ()))
    sls = [slice(hd * HEAD_DIM, (hd + 1) * HEAD_DIM) for hd in range(NA_HEADS)]
    scores = [lax.dot_general(q_ref[:, sl], k_all[:, sl], nt, preferred_element_type=F32)
              + jnp.concatenate([bias_ref[hd], zeros], axis=1) for hd, sl in enumerate(sls)]
    outs = _softmax_pv_heads(scores, [v_all[:, sl] for sl in sls])
    for sl, out in zip(sls, outs):
        o_ref[:, sl] = out


def _neighbourhood_attention(qd, kd, vd, rpb, n_ctx):
    bsz, s, _ = qd.shape
    t_len = s - n_ctx
    rows = t_len // GRID_W
    types, kbs, valid, dr, dc = _na_block_plan(rows)
    n_types = valid.shape[0]
    dr_pat = dr.reshape(n_types, NA_QROWS, GRID_W, NA_KROWS, GRID_W)[:, :, 0, :, 0]
    dc_pat = dc.reshape(n_types, NA_QROWS, GRID_W, NA_KROWS, GRID_W)[0, 0, :, 0, :]
    oh_r = jnp.asarray(dr_pat[..., None] == np.arange(2 * NA_ROWS - 1), F32)
    oh_c = jnp.asarray(dc_pat[..., None] == np.arange(2 * NA_COLS - 1), F32)
    bias = jnp.einsum('tamr,hrs,cks->thacmk', oh_r, rpb, oh_c, precision=HI)
    bias = bias.reshape(n_types, NA_HEADS, NA_QROWS * GRID_W, NA_KROWS * GRID_W)
    bias = jnp.where(valid[:, None], bias, ATT_NEG)
    n_q = NA_QROWS * GRID_W
    off_q = n_ctx // n_q
    off_k = n_ctx // GRID_W

    def kv_spec(m):
        return pl.BlockSpec((None, GRID_W, QD_W), lambda b, i, types, kbs: (b, kbs[i] + m + off_k, 0))

    ctx_spec = pl.BlockSpec((None, n_ctx, QD_W), lambda b, i, types, kbs: (b, 0, 0))
    grid_spec = pltpu.PrefetchScalarGridSpec(
        num_scalar_prefetch=2,
        grid=(bsz, rows // NA_QROWS),
        in_specs=([pl.BlockSpec((None, n_q, QD_W), lambda b, i, types, kbs: (b, i + off_q, 0))]
                  + [kv_spec(m) for m in range(NA_KROWS)] + [kv_spec(m) for m in range(NA_KROWS)]
                  + [ctx_spec, ctx_spec,
                     pl.BlockSpec((None, NA_HEADS, n_q, NA_KROWS * GRID_W),
                                  lambda b, i, types, kbs: (types[i], 0, 0, 0))]),
        out_specs=pl.BlockSpec((None, n_q, QD_W), lambda b, i, types, kbs: (b, i, 0)),
    )
    return pl.pallas_call(
        _na_kernel,
        grid_spec=grid_spec,
        out_shape=jax.ShapeDtypeStruct((bsz, t_len, QD_W), F32),
        compiler_params=_cparams("parallel", "arbitrary"),
        name="neighbourhood_attention",
    )(jnp.asarray(types), jnp.asarray(kbs), qd, *([kd] * NA_KROWS), *([vd] * NA_KROWS), kd, vd, bias)


RWKV_CHUNK = 64
RWKV_SUPER = ROW_BLOCK // RWKV_CHUNK
RWKV_CHUNKS_PER_STEP = 4
HI = lax.Precision.HIGHEST


RWKV_PREC_PAIR = "bf16"
RWKV_PREC_INV = "bf16"
RWKV_PREC_OUT = "bf16"
RWKV_PREC_STATE = "bf16x3"


def _pdot(x, y, dims, mode):
    if mode == "f32":
        return lax.dot_general(x, y, dims, precision=HI, preferred_element_type=F32)
    xh = x.astype(BF16)
    yh = y.astype(BF16)
    out = lax.dot_general(xh, yh, dims, preferred_element_type=F32)
    if mode == "bf16x3":
        xl = (x - xh.astype(F32)).astype(BF16)
        yl = (y - yh.astype(F32)).astype(BF16)
        out = (out + lax.dot_general(xh, yl, dims, preferred_element_type=F32)
               + lax.dot_general(xl, yh, dims, preferred_element_type=F32))
    return out


def _split_dot(x, e_bf16):
    hi = x.astype(BF16)
    lo = (x - hi.astype(F32)).astype(BF16)
    return jnp.dot(hi, e_bf16, preferred_element_type=F32) + jnp.dot(lo, e_bf16, preferred_element_type=F32)


def _rwkv_prep_kernel(z_ref, zp_ref, zn_ref, mu_ref, w0_ref, w2_ref, a0_ref, a2_ref, g2_ref, kk_ref, ka_ref,
                      rk_ref, e_ref, r_o, v_o, kk_o, g_o, bonus_o, logw_o, kd_o, bb_o):
    i = pl.program_id(1)
    last = pl.num_programs(1) - 1
    tm = z_ref.shape[0]
    hw = RWKV_WIDTH
    z = z_ref[:, S5_WIDTH:]
    row = lax.broadcasted_iota(jnp.int32, (tm, 1), 0)
    prev_halo = jnp.where(i <= 1, 0.0, zp_ref[SUBLANES - 1:SUBLANES, S5_WIDTH:])
    next_halo = jnp.where((i == 0) | (i == last), 0.0, zn_ref[0:1, S5_WIDTH:])
    prev = jnp.where(row == 0, prev_halo, pltpu.roll(z, 1, 0))
    nxt = jnp.where(row == tm - 1, next_halo, pltpu.roll(z, tm - 1, 0))
    zs = z + mu_ref[0:1, :] * (prev - z) + mu_ref[1:2, :] * (nxt - z)
    c0, c1, c2, c3, c4 = RWKV_SPLITS
    r = zs[:, :c0]
    k = zs[:, c0:c1]
    v = zs[:, c1:c2]
    e = e_ref[...]
    g_o[...] = jnp.dot(jax.nn.sigmoid(zs[:, c4:]).astype(BF16), g2_ref[...], preferred_element_type=F32)
    kk = k * kk_ref[...]
    kk = kk * lax.rsqrt(_split_dot(kk * kk, e) + 1e-12)
    w_lora = jnp.dot(jnp.tanh(zs[:, c2:c3]).astype(BF16), w2_ref[...], preferred_element_type=F32)
    a_lora = jnp.dot(zs[:, c3:c4].astype(BF16), a2_ref[...], preferred_element_type=F32)
    bonus = jnp.zeros((tm, hw), F32)
    for d in range(2):
        x = w0_ref[d:d + 1, :] + w_lora[:, d * hw:(d + 1) * hw]
        softplus_neg = jnp.maximum(-x, 0.0) + jnp.log(1.0 + jnp.exp(-jnp.abs(x)))
        logw_o[d] = -jnp.exp(-softplus_neg - 0.5)
        a = jax.nn.sigmoid(a0_ref[d:d + 1, :] + a_lora[:, d * hw:(d + 1) * hw])
        kd = k * (1.0 + (a - 1.0) * ka_ref[...])
        kd_o[d] = kd
        bb_o[d] = kk * a
        bonus = bonus + _split_dot(r * kd * rk_ref[...], e) * v
    r_o[...] = r
    v_o[...] = v
    kk_o[...] = kk
    bonus_o[...] = bonus


def _rwkv_chunk_kernel(r_ref, v_ref, kk_ref, logw_ref, kd_ref, bb_ref, mn_ref, ry_ref):
    fwd = pl.program_id(1) == 0
    cn = RWKV_CHUNK
    hd = HEAD_DIM
    rowi = lax.broadcasted_iota(jnp.int32, (cn, cn), 0)
    coli = lax.broadcasted_iota(jnp.int32, (cn, cn), 1)
    lag = (rowi - coli) * jnp.where(fwd, 1, -1)
    before_incl = lag >= 0
    before = lag > 0
    eye = (rowi == coli).astype(F32)
    nn = (((1,), (0,)), ((), ()))
    nt = (((1,), (1,)), ((), ()))
    tn = (((0,), (0,)), ((), ()))

    a_t, r_t, b_t, k_t, b_h, k_h, g_end, vs, where = [], [], [], [], [], [], [], [], []
    for cc in range(RWKV_CHUNKS_PER_STEP):
        rs = slice(cc * cn, (cc + 1) * cn)
        logw = logw_ref[rs, :]
        lg = jnp.dot(before_incl.astype(F32), logw, precision=HI, preferred_element_type=F32)
        lg_tot = jnp.sum(logw, axis=0, keepdims=True)
        inv = jnp.exp(-lg)
        to_end = jnp.exp(lg_tot - lg)
        full = dict(a_t=-kk_ref[rs, :] * jnp.exp(lg - logw), r_t=r_ref[rs, :] * jnp.exp(lg),
                    b_t=bb_ref[rs, :] * inv, k_t=kd_ref[rs, :] * inv,
                    b_h=bb_ref[rs, :] * to_end, k_h=kd_ref[rs, :] * to_end, g_end=jnp.exp(lg_tot))
        for h in range(RWKV_HEADS):
            sl = slice(h * hd, (h + 1) * hd)
            a_t.append(full["a_t"][:, sl])
            r_t.append(full["r_t"][:, sl])
            b_t.append(full["b_t"][:, sl])
            k_t.append(full["k_t"][:, sl])
            b_h.append(full["b_h"][:, sl])
            k_h.append(full["k_h"][:, sl])
            g_end.append(full["g_end"][:, sl])
            vs.append(v_ref[rs, sl])
            where.append((cc, h))
    ps = [_pdot(jnp.concatenate([a, r], axis=0), jnp.concatenate([b, k], axis=0), nt, RWKV_PREC_PAIR)
          for a, r, b, k in zip(a_t, r_t, b_t, k_t)]
    n_ab = [jnp.where(before, p[:cn, :cn], 0.0) for p in ps]
    n_ak = [jnp.where(before, p[:cn, cn:], 0.0) for p in ps]
    n_rb = [jnp.where(before_incl, p[cn:, :cn], 0.0) for p in ps]
    n_rk = [jnp.where(before_incl, p[cn:, cn:], 0.0) for p in ps]
    tinv = [eye + n for n in n_ab]
    pw = n_ab
    for _ in range(int(math.log2(cn)) - 1):
        pw = [_pdot(x, x, nn, RWKV_PREC_INV) for x in pw]
        tinv = [t + _pdot(t, x, nn, RWKV_PREC_INV) for t, x in zip(tinv, pw)]
    akv = [_pdot(n, v, nn, RWKV_PREC_OUT) for n, v in zip(n_ak, vs)]
    rkv = [_pdot(n, v, nn, RWKV_PREC_OUT) for n, v in zip(n_rk, vs)]
    kv = [_pdot(k, v, tn, RWKV_PREC_OUT) for k, v in zip(k_h, vs)]
    zz = [_pdot(t, jnp.concatenate([a, w], axis=1), nn, RWKV_PREC_OUT)
          for t, a, w in zip(tinv, a_t, akv)]
    for u, (cc, h) in enumerate(where):
        ry_ref[cc, h] = (_pdot(n_rb[u], zz[u], nn, RWKV_PREC_OUT)
                         + jnp.concatenate([r_t[u], rkv[u]], axis=1))
        mn_ref[cc, h] = (_pdot(b_h[u], zz[u], tn, RWKV_PREC_OUT)
                         + jnp.concatenate([eye * g_end[u], kv[u]], axis=1))


def _rwkv_seq_kernel(mn_ref, ry_ref, y_ref, h_ref):
    @pl.when(pl.program_id(2) == 0)
    def _():
        h_ref[...] = jnp.zeros_like(h_ref)

    fwd = pl.program_id(1) == 0
    hd = HEAD_DIM
    for cc in range(RWKV_SUPER):
        c = jnp.where(fwd, cc, RWKV_SUPER - 1 - cc)
        row0 = pl.multiple_of(c * RWKV_CHUNK, RWKV_CHUNK)
        for h in range(RWKV_HEADS):
            mn = mn_ref[c, h]
            ry = ry_ref[c, h]
            state = h_ref[h]
            nn = (((1,), (0,)), ((), ()))
            y_ref[pl.ds(row0, RWKV_CHUNK), h * hd:(h + 1) * hd] = (
                _pdot(ry[:, :hd], state, nn, RWKV_PREC_STATE) + ry[:, hd:])
            h_ref[h] = _pdot(mn[:, :hd], state, nn, RWKV_PREC_STATE) + mn[:, hd:]


def _outproj_ln_rwkv_kernel(ya_ref, y0_ref, y1_ref, bonus_ref, g_ref, gnw_ref, gnb_ref, e_ref,
                            h_ref, gate_ref, wa_ref, wb_ref, lng_ref, lnb_ref, o_ref):
    y = y0_ref[...] + y1_ref[...]
    e = e_ref[...]
    mu = _split_dot(y, e) * (1.0 / HEAD_DIM)
    yc = y - mu
    var = _split_dot(yc * yc, e) * (1.0 / HEAD_DIM)
    yn = yc * lax.rsqrt(var + GN_EPS) * gnw_ref[...] + gnb_ref[...]
    yb = (yn + bonus_ref[...]) * g_ref[...]
    proj = (jnp.dot(ya_ref[...].astype(BF16), wa_ref[...], preferred_element_type=F32)
            + jnp.dot(yb.astype(BF16), wb_ref[...], preferred_element_type=F32))
    z = ALPHA * h_ref[...] + gate_ref[...] * proj
    o_ref[...] = _layer_norm_rows(z, lng_ref[...], lnb_ref[...])


def _block_diag2(w):
    z = jnp.zeros_like(w[0])
    return jnp.concatenate([jnp.concatenate([w[0], z], 1), jnp.concatenate([z, w[1]], 1)], 0)


def _rwkv_mixer_outproj_ln(z, shift_mu, w0, w2, a0, a2, g2, k_k, k_a, r_k, gn_w, gn_b,
                           ya, h, tab, which_gate, w_out, ln_g, ln_b):
    bsz, s, zw = z.shape
    hw = RWKV_WIDTH
    tm = ROW_BLOCK
    nblk = s // tm
    head_of = np.arange(hw) // HEAD_DIM
    e = jnp.asarray(head_of[:, None] == head_of[None, :], BF16)
    tok = pl.BlockSpec((None, tm, hw), lambda b, i: (b, i, 0))
    tok2 = pl.BlockSpec((None, 2, tm, hw), lambda b, i: (b, 0, i, 0))
    halo = tm // SUBLANES

    def full(shape):
        return pl.BlockSpec(shape, lambda b, i: (0,) * len(shape))

    r, v, kk, g, bonus, logw, kd, bb = pl.pallas_call(
        _rwkv_prep_kernel,
        grid=(bsz, nblk),
        in_specs=[pl.BlockSpec((None, tm, zw), lambda b, i: (b, i, 0)),
                  pl.BlockSpec((None, SUBLANES, zw), lambda b, i: (b, jnp.maximum(i * halo - 1, 0), 0)),
                  pl.BlockSpec((None, SUBLANES, zw),
                               lambda b, i: (b, jnp.minimum((i + 1) * halo, s // SUBLANES - 1), 0)),
                  full((2, RWKV_IN)), full((2, hw)), full((2 * RWKV_LORA_W, 2 * hw)), full((2, hw)),
                  full((2 * RWKV_LORA_A, 2 * hw)), full((RWKV_LORA_G, hw)), full((1, hw)), full((1, hw)),
                  full((1, hw)), full((hw, hw))],
        out_specs=[tok, tok, tok, tok, tok, tok2, tok2, tok2],
        out_shape=[jax.ShapeDtypeStruct((bsz, s, hw), F32)] * 5 + [jax.ShapeDtypeStruct((bsz, 2, s, hw), F32)] * 3,
        compiler_params=_cparams("parallel", "parallel"),
        name="rwkv_prep",
    )(z, z, z, shift_mu, w0, _block_diag2(w2).astype(BF16), a0, _block_diag2(a2).astype(BF16), g2.astype(BF16),
      k_k.reshape(1, hw), k_a.reshape(1, hw), r_k.reshape(1, hw), e)

    nch = s // RWKV_CHUNK
    step_rows = RWKV_CHUNKS_PER_STEP * RWKV_CHUNK
    ctok = pl.BlockSpec((None, step_rows, hw), lambda b, d, c: (b, c, 0))
    ctok2 = pl.BlockSpec((None, None, step_rows, hw), lambda b, d, c: (b, d, c, 0))
    mat_shape = jax.ShapeDtypeStruct((bsz, 2, nch, RWKV_HEADS, RWKV_CHUNK, 2 * HEAD_DIM), F32)
    mat_spec = pl.BlockSpec((None, None, RWKV_CHUNKS_PER_STEP, RWKV_HEADS, RWKV_CHUNK, 2 * HEAD_DIM),
                            lambda b, d, c: (b, d, c, 0, 0, 0))
    mn, ry = pl.pallas_call(
        _rwkv_chunk_kernel,
        grid=(bsz, 2, nch // RWKV_CHUNKS_PER_STEP),
        in_specs=[ctok, ctok, ctok, ctok2, ctok2, ctok2],
        out_specs=[mat_spec, mat_spec],
        out_shape=[mat_shape, mat_shape],
        compiler_params=_cparams("parallel", "parallel", "parallel"),
        name="rwkv_chunk",
    )(r, v, kk, logw, kd, bb)

    nsb = nch // RWKV_SUPER

    def sb_of(d, j):
        return jnp.where(d == 0, j, jnp.where(j == 0, 0, nsb - j))

    sup_shape = (bsz, 2, nsb, RWKV_SUPER, RWKV_HEADS, RWKV_CHUNK, 2 * HEAD_DIM)
    sup_spec = pl.BlockSpec((None, None, None, RWKV_SUPER, RWKV_HEADS, RWKV_CHUNK, 2 * HEAD_DIM),
                            lambda b, d, j: (b, d, sb_of(d, j), 0, 0, 0, 0))
    y = pl.pallas_call(
        _rwkv_seq_kernel,
        grid=(bsz, 2, nsb),
        in_specs=[sup_spec, sup_spec],
        out_specs=pl.BlockSpec((None, None, tm, hw), lambda b, d, j: (b, d, sb_of(d, j), 0)),
        out_shape=jax.ShapeDtypeStruct((bsz, 2, s, hw), F32),
        scratch_shapes=[pltpu.VMEM((RWKV_HEADS, HEAD_DIM, HEAD_DIM), F32)],
        compiler_params=_cparams("parallel", "parallel", "arbitrary"),
        name="rwkv_seq",
    )(mn.reshape(sup_shape), ry.reshape(sup_shape))

    d = h.shape[-1]
    ka = ya.shape[-1]
    vec = pl.BlockSpec((1, hw), lambda b, i: (0, 0))
    dvec = pl.BlockSpec((1, d), lambda b, i: (0, 0))
    return pl.pallas_call(
        _outproj_ln_rwkv_kernel,
        grid=(bsz, nblk),
        in_specs=[pl.BlockSpec((None, tm, ka), lambda b, i: (b, i, 0)),
                  pl.BlockSpec((None, None, tm, hw), lambda b, i: (b, 0, i, 0)),
                  pl.BlockSpec((None, None, tm, hw), lambda b, i: (b, 1, i, 0)),
                  tok, tok, vec, vec, pl.BlockSpec((hw, hw), lambda b, i: (0, 0)),
                  pl.BlockSpec((None, tm, d), lambda b, i: (b, i, 0)),
                  _tab_spec(which_gate),
                  pl.BlockSpec((ka, d), lambda b, i: (0, 0)),
                  pl.BlockSpec((hw, d), lambda b, i: (0, 0)),
                  dvec, dvec],
        out_specs=pl.BlockSpec((None, tm, d), lambda b, i: (b, i, 0)),
        out_shape=jax.ShapeDtypeStruct((bsz, s, d), F32),
        compiler_params=_cparams("parallel", "parallel"),
        name="outproj_ln_rwkv",
    )(ya, y, y, bonus, g, gn_w.reshape(1, hw), gn_b.reshape(1, hw), e, h, tab,
      w_out[:ka].astype(BF16), w_out[ka:].astype(BF16), ln_g.reshape(1, d), ln_b.reshape(1, d))


def kernel(x, c, ctx, c_ctx,
           l0_w_mod, l0_b_mod, l0_w_in, l0_s5_lam_re, l0_s5_lam_im, l0_s5_log_dt, l0_s5_b_re, l0_s5_b_im,
           l0_s5_c_re, l0_s5_c_im, l0_s5_d, l0_w_glu, l0_b_glu, l0_rwkv_shift, l0_rwkv_w0, l0_rwkv_w2,
           l0_rwkv_a0, l0_rwkv_a2, l0_rwkv_g2, l0_rwkv_k_k, l0_rwkv_k_a, l0_rwkv_r_k, l0_rwkv_gn_w,
           l0_rwkv_gn_b, l0_w_out, l0_ln1_g, l0_ln1_b, l0_peer_wq, l0_peer_keys, l0_peer_u, l0_peer_v,
           l0_ln2_g, l0_ln2_b,
           l1_w_mod, l1_b_mod, l1_w_in, l1_sink, l1_rpb, l1_w_out, l1_ln1_g, l1_ln1_b, l1_peer_wq,
           l1_peer_keys, l1_peer_u, l1_peer_v, l1_ln2_g, l1_ln2_b):
    L = ctx.shape[1]
    h = jnp.concatenate([ctx, x], 1)

    tab = _mod_tables(c, c_ctx, l0_w_mod, l0_b_mod)
    z = _inproj(h, tab, 0, 1, l0_w_in.astype(BF16))
    ya = _s5_mixer(z[..., :S5_WIDTH], L, l0_s5_lam_re, l0_s5_lam_im, l0_s5_log_dt, l0_s5_b_re, l0_s5_b_im,
                   l0_s5_c_re, l0_s5_c_im, l0_s5_d, l0_w_glu, l0_b_glu)
    h = _rwkv_mixer_outproj_ln(z, l0_rwkv_shift, l0_rwkv_w0, l0_rwkv_w2, l0_rwkv_a0, l0_rwkv_a2,
                               l0_rwkv_g2, l0_rwkv_k_k, l0_rwkv_k_a, l0_rwkv_r_k, l0_rwkv_gn_w, l0_rwkv_gn_b,
                               ya, h, tab, 2, l0_w_out, l0_ln1_g, l0_ln1_b)
    h = _peer_ln(h, tab, 0, l0_peer_wq, l0_peer_keys, l0_peer_u, l0_peer_v, l0_ln2_g, l0_ln2_b)

    tab = _mod_tables(c, c_ctx, l1_w_mod, l1_b_mod)
    qc, kc, vc, qd, kd, vd = _inproj_attn(h, tab, 0, 1, l1_w_in.astype(BF16), L)
    oc = _window_gqa(qc, kc, vc, l1_sink, L)
    od = _neighbourhood_attention(qd, kd, vd, l1_rpb, L)
    ctx_blocks = L // ROW_BLOCK
    hx = _outproj_ln(oc, od, h, tab, 2, l1_w_out, l1_ln1_g, l1_ln1_b, ctx_blocks)
    return _peer_ln(hx, tab, ctx_blocks, l1_peer_wq, l1_peer_keys, l1_peer_u, l1_peer_v, l1_ln2_g, l1_ln2_b)
```
